```python
import jax, jax.numpy as jnp
from jax import lax
import numpy as np

D_MODEL = 1024
BATCH = 32
SEQ = 256
DEPTH = 2
DEC_BATCH = 8
DEC_SEQ = 4096
PAST_LEN = 512

GRID_W = 64
N_PC_LAYERS = (DEPTH + 1) // 2
N_NA_LAYERS = DEPTH // 2
POOL_WIDTH = D_MODEL // 2
POOL_WINDOWS = (2, 4, 8, 16)
POOL_GROUP = POOL_WIDTH // len(POOL_WINDOWS)
CONV_WIDTH = D_MODEL // 2
CONV_K = 3
PC_IN = POOL_WIDTH + 3 * CONV_WIDTH
N_HEADS = 16
HEAD_DIM = D_MODEL // N_HEADS
WIN_H = 8
WIN_W = 16
Q_BLOCK_W = 16
K_BLOCK_W = Q_BLOCK_W + WIN_W
CTX_Q_BLOCK = 128
N_EXPERTS = 64
N_EXPERT_GROUPS = 8
EXPERTS_PER_GROUP = N_EXPERTS // N_EXPERT_GROUPS
TOPK_GROUPS = 4
TOPK = 8
D_EXPERT = 256
D_SHARED = 256
ROUTED_SCALE = 2.5
EPS = 1e-6

kernel_name = 'hybrid_pool_conv_natten_moe_diffusion_step'


def rmsnorm(x, g):
    xf = x.astype(jnp.float32)
    xf = xf * lax.rsqrt(jnp.mean(xf * xf, axis=-1, keepdims=True) + EPS)
    return xf.astype(x.dtype) * g


def modulation(cond, w, b):
    m = jax.nn.silu(cond) @ w + b
    return jnp.split(m[:, None, :], 6, axis=-1)


def adaln(x, g, shift, scale):
    return rmsnorm(x, g) * (1 + scale) + shift


def pool_conv_mixer(h, w_in, pool_w, pool_scale, conv_w, w_out):
    S = h.shape[1]
    u = h @ w_in
    ua = u[..., :POOL_WIDTH]
    gate_b, gate_c, val = jnp.split(u[..., POOL_WIDTH:], 3, axis=-1)
    cs = jnp.pad(jnp.cumsum(ua.astype(jnp.float32), axis=1), ((0, 0), (1, 0), (0, 0)))
    t = np.arange(S)
    ys = []
    for g, w in enumerate(POOL_WINDOWS):
        lo = np.clip(t - w // 2, 0, S - 1)
        hi = np.clip(t + (w - w // 2 - 1), 0, S - 1)
        cnt = (hi - lo + 1).astype(np.float32)
        sl = slice(g * POOL_GROUP, (g + 1) * POOL_GROUP)
        csg = cs[..., sl]
        mean = (csg[:, hi + 1] - csg[:, lo]) / cnt[None, :, None]
        ys.append((mean.astype(h.dtype) - ua[..., sl]) @ pool_w[g])
    y_a = jnp.concatenate(ys, axis=-1) * pool_scale
    z = gate_c * val
    zp = jnp.pad(z, ((0, 0), (1, 1), (0, 0)))
    zc = conv_w[0] * zp[:, :-2] + conv_w[1] * zp[:, 1:-1] + conv_w[2] * zp[:, 2:]
    y_b = gate_b * zc
    return jnp.concatenate([y_a, y_b], axis=-1) @ w_out


def na_qkv(h, w_qkv, q_norm, k_norm):
    B, S, _ = h.shape
    qkv = (h @ w_qkv).reshape(B, S, 3, N_HEADS, HEAD_DIM)
    return rmsnorm(qkv[:, :, 0], q_norm), rmsnorm(qkv[:, :, 1], k_norm), qkv[:, :, 2]


def context_attention(q, k, v):
    B, S, H, Dh = q.shape
    nb = S // CTX_Q_BLOCK
    qb = jnp.moveaxis(q.reshape(B, nb, CTX_Q_BLOCK, H, Dh), 1, 0)
    scale = HEAD_DIM ** -0.5

    def block(qi):
        s = jnp.einsum('bqhd,bkhd->bhqk', qi, k).astype(jnp.float32) * scale
        p = jax.nn.softmax(s, axis=-1).astype(v.dtype)
        return jnp.einsum('bhqk,bkhd->bqhd', p, v)

    o = lax.map(block, qb)
    return jnp.moveaxis(o, 0, 1).reshape(B, S, H * Dh)


def neighbourhood_attention(q, k, v, k_ctx, v_ctx, rpb):
    B, S, H, Dh = q.shape
    rows = S // GRID_W
    kh = min(WIN_H, rows)
    n_cb = GRID_W // Q_BLOCK_W
    qg = q.reshape(B, rows, GRID_W, H, Dh)
    kg = k.reshape(B, rows, GRID_W, H, Dh)
    vg = v.reshape(B, rows, GRID_W, H, Dh)
    j = np.arange(n_cb)
    cb_start = np.clip(j * Q_BLOCK_W - WIN_W // 2, 0, GRID_W - K_BLOCK_W)
    key_cols = cb_start[:, None] + np.arange(K_BLOCK_W)
    q_cols = j[:, None] * Q_BLOCK_W + np.arange(Q_BLOCK_W)
    q_start = np.clip(q_cols - WIN_W // 2, 0, GRID_W - WIN_W)
    kc = key_cols[:, None, :]
    col_ok = (kc >= q_start[..., None]) & (kc < q_start[..., None] + WIN_W)
    col_idx = np.clip(kc - q_cols[:, :, None] + WIN_W - 1, 0, 2 * WIN_W - 2)
    n_lat = kh * K_BLOCK_W
    valid = np.broadcast_to(col_ok[:, :, None, :], (n_cb, Q_BLOCK_W, kh, K_BLOCK_W)).reshape(n_cb, Q_BLOCK_W, n_lat)
    scale = HEAD_DIM ** -0.5
    neg = jnp.finfo(jnp.float32).min

    def row_block(r):
        sr = jnp.clip(r - kh // 2, 0, rows - kh)
        qr = lax.dynamic_index_in_dim(qg, r, axis=1, keepdims=False).reshape(B, n_cb, Q_BLOCK_W, H, Dh)
        kr = lax.dynamic_slice_in_dim(kg, sr, kh, axis=1)[:, :, key_cols]
        vr = lax.dynamic_slice_in_dim(vg, sr, kh, axis=1)[:, :, key_cols]
        kr = jnp.swapaxes(kr, 1, 2).reshape(B, n_cb, n_lat, H, Dh)
        vr = jnp.swapaxes(vr, 1, 2).reshape(B, n_cb, n_lat, H, Dh)
        row_idx = sr + jnp.arange(kh) - r + WIN_H - 1
        bias = rpb[:, row_idx][:, :, col_idx]
        bias = jnp.transpose(bias, (0, 2, 3, 1, 4)).reshape(H, n_cb, Q_BLOCK_W, n_lat)
        s_lat = jnp.einsum('bjqhd,bjkhd->bhjqk', qr, kr).astype(jnp.float32) * scale + bias.astype(jnp.float32)
        s_lat = jnp.where(valid, s_lat, neg)
        s_ctx = jnp.einsum('bjqhd,blhd->bhjql', qr, k_ctx).astype(jnp.float32) * scale
        p = jax.nn.softmax(jnp.concatenate([s_lat, s_ctx], axis=-1), axis=-1).astype(v.dtype)
        o = (jnp.einsum('bhjqk,bjkhd->bjqhd', p[..., :n_lat], vr)
             + jnp.einsum('bhjql,blhd->bjqhd', p[..., n_lat:], v_ctx))
        return o.reshape(B, GRID_W, H * Dh)

    o = lax.map(row_block, jnp.arange(rows))
    return jnp.moveaxis(o, 0, 1).reshape(B, S, H * Dh)


def moe(h, w_router, router_bias, w_gate, w_up, w_down, s_gate, s_up, s_down):
    B, S, D = h.shape
    T = B * S
    x = h.reshape(T, D)
    scores = jax.nn.sigmoid((x @ w_router).astype(jnp.float32))
    biased = scores + router_bias.astype(jnp.float32)
    grp = lax.top_k(biased.reshape(T, N_EXPERT_GROUPS, EXPERTS_PER_GROUP), 2)[0].sum(-1)
    _, gidx = lax.top_k(grp, TOPK_GROUPS)
    gsel = jnp.any(gidx[..., None] == jnp.arange(N_EXPERT_GROUPS), axis=-2)
    masked = jnp.where(jnp.repeat(gsel, EXPERTS_PER_GROUP, axis=-1), biased, -jnp.inf)
    _, eidx = lax.top_k(masked, TOPK)
    w = jnp.take_along_axis(scores, eidx, axis=-1)
    w = w / jnp.sum(w, axis=-1, keepdims=True) * ROUTED_SCALE
    gates = jnp.sum(jnp.where(eidx[..., None] == jnp.arange(N_EXPERTS), w[..., None], 0.0), axis=1).astype(h.dtype)
    out = (jax.nn.silu(x @ s_gate) * (x @ s_up)) @ s_down
    for g in range(N_EXPERT_GROUPS):
        sl = slice(g * EXPERTS_PER_GROUP, (g + 1) * EXPERTS_PER_GROUP)
        a = jnp.einsum('td,edf->tef', x, w_gate[sl])
        b = jnp.einsum('td,edf->tef', x, w_up[sl])
        hh = jax.nn.silu(a) * b * gates[:, sl, None]
        out = out + jnp.einsum('tef,efd->td', hh, w_down[sl])
    return out.reshape(B, S, D)


def setup_inputs(seed: int = 0) -> dict:
    key = jax.random.key(seed)
    ks = jax.random.split(key, 32)
    D = D_MODEL

    def nrm(k, shape, scale):
        return jax.random.normal(k, shape, jnp.float32) * scale

    return {
        'x_prompt': nrm(ks[0], (BATCH, SEQ, D), 1.0),
        'x_sample': nrm(ks[1], (DEC_BATCH, DEC_SEQ, D), 1.0),
        'cache_k': nrm(ks[2], (DEC_BATCH, N_NA_LAYERS, PAST_LEN, N_HEADS, HEAD_DIM), 1.0),
        'cache_v': nrm(ks[3], (DEC_BATCH, N_NA_LAYERS, PAST_LEN, N_HEADS, HEAD_DIM), 1.0),
        'c': nrm(ks[4], (DEC_BATCH, D), 1.0),
        'c_ctx': nrm(ks[5], (D,), 1.0),
        'ada_w': nrm(ks[6], (DEPTH, D, 6 * D), 0.5 * D ** -0.5),
        'ada_b': nrm(ks[7], (DEPTH, 6 * D), 0.02),
        'norm_mix': 1.0 + nrm(ks[8], (DEPTH, D), 0.05),
        'norm_ffn': 1.0 + nrm(ks[9], (DEPTH, D), 0.05),
        'pc_w_in': nrm(ks[10], (N_PC_LAYERS, D, PC_IN), D ** -0.5),
        'pc_pool_w': nrm(ks[11], (N_PC_LAYERS, len(POOL_WINDOWS), POOL_GROUP, POOL_GROUP), POOL_GROUP ** -0.5),
        'pc_pool_scale': 1.0 + nrm(ks[12], (N_PC_LAYERS, POOL_WIDTH), 0.1),
        'pc_conv_w': nrm(ks[13], (N_PC_LAYERS, CONV_K, CONV_WIDTH), CONV_K ** -0.5),
        'pc_w_out': nrm(ks[14], (N_PC_LAYERS, D, D), D ** -0.5),
        'na_w_qkv': nrm(ks[15], (N_NA_LAYERS, D, 3 * D), D ** -0.5),
        'na_q_norm': 1.0 + nrm(ks[16], (N_NA_LAYERS, HEAD_DIM), 0.05),
        'na_k_norm': 1.0 + nrm(ks[17], (N_NA_LAYERS, HEAD_DIM), 0.05),
        'na_rpb': nrm(ks[18], (N_NA_LAYERS, N_HEADS, 2 * WIN_H - 1, 2 * WIN_W - 1), 0.1),
        'na_w_out': nrm(ks[19], (N_NA_LAYERS, D, D), D ** -0.5),
        'moe_router': nrm(ks[20], (DEPTH, D, N_EXPERTS), D ** -0.5),
        'moe_router_bias': nrm(ks[21], (DEPTH, N_EXPERTS), 0.01),
        'moe_w_gate': nrm(ks[22], (DEPTH, N_EXPERTS, D, D_EXPERT), D ** -0.5),
        'moe_w_up': nrm(ks[23], (DEPTH, N_EXPERTS, D, D_EXPERT), D ** -0.5),
        'moe_w_down': nrm(ks[24], (DEPTH, N_EXPERTS, D_EXPERT, D), D_EXPERT ** -0.5),
        'moe_shared_gate': nrm(ks[25], (DEPTH, D, D_SHARED), D ** -0.5),
        'moe_shared_up': nrm(ks[26], (DEPTH, D, D_SHARED), D ** -0.5),
        'moe_shared_down': nrm(ks[27], (DEPTH, D_SHARED, D), D_SHARED ** -0.5),
    }


def reference(x_prompt, x_sample, cache_k, cache_v, c, c_ctx, ada_w, ada_b, norm_mix, norm_ffn,
              pc_w_in, pc_pool_w, pc_pool_scale, pc_conv_w, pc_w_out,
              na_w_qkv, na_q_norm, na_k_norm, na_rpb, na_w_out,
              moe_router, moe_router_bias, moe_w_gate, moe_w_up, moe_w_down,
              moe_shared_gate, moe_shared_up, moe_shared_down):
    xp, xs = x_prompt, x_sample
    state_k, state_v = [], []
    for layer in range(DEPTH):
        i = layer // 2
        sh1_p, sc1_p, g1_p, sh2_p, sc2_p, g2_p = modulation(c_ctx[None], ada_w[layer], ada_b[layer])
        sh1_s, sc1_s, g1_s, sh2_s, sc2_s, g2_s = modulation(c, ada_w[layer], ada_b[layer])
        hp = adaln(xp, norm_mix[layer], sh1_p, sc1_p)
        hs = adaln(xs, norm_mix[layer], sh1_s, sc1_s)
        if layer % 2 == 0:
            yp = pool_conv_mixer(hp, pc_w_in[i], pc_pool_w[i], pc_pool_scale[i], pc_conv_w[i], pc_w_out[i])
            ys = pool_conv_mixer(hs, pc_w_in[i], pc_pool_w[i], pc_pool_scale[i], pc_conv_w[i], pc_w_out[i])
        else:
            qp, kp, vp = na_qkv(hp, na_w_qkv[i], na_q_norm[i], na_k_norm[i])
            yp = context_attention(qp, kp, vp) @ na_w_out[i]
            state_k.append(kp)
            state_v.append(vp)
            qs, ks_, vs = na_qkv(hs, na_w_qkv[i], na_q_norm[i], na_k_norm[i])
            ys = neighbourhood_attention(qs, ks_, vs, cache_k[:, i], cache_v[:, i], na_rpb[i]) @ na_w_out[i]
        xp = xp + g1_p * yp
        xs = xs + g1_s * ys
        hp = adaln(xp, norm_ffn[layer], sh2_p, sc2_p)
        hs = adaln(xs, norm_ffn[layer], sh2_s, sc2_s)
        moe_args = (moe_router[layer], moe_router_bias[layer], moe_w_gate[layer], moe_w_up[layer],
                    moe_w_down[layer], moe_shared_gate[layer], moe_shared_up[layer], moe_shared_down[layer])
        xp = xp + g2_p * moe(hp, *moe_args)
        xs = xs + g2_s * moe(hs, *moe_args)
    new_k = jnp.stack(state_k, axis=1)
    new_v = jnp.stack(state_v, axis=1)
    return (xp, xs, new_k, new_v)
```

```python
import functools

import numpy as np
import jax
import jax.numpy as jnp
from jax import lax
from jax.experimental import pallas as pl
from jax.experimental.pallas import tpu as pltpu

F32 = jnp.float32
BF16 = jnp.bfloat16

TM = 256
HALO = 8
POOL_WINDOWS = (2, 4, 8, 16)
N_EXPERTS = 64
N_GROUPS = 8
GROUP_SIZE = N_EXPERTS // N_GROUPS
TOPK_GROUPS = 4
TOPK = 8
ROUTED_SCALE = 2.5
EPS = 1e-6
GRID_W = 64
WIN_H = 8
WIN_W = 16
HEAD_DIM = 64
NEG = float(np.finfo(np.float32).min)
VMEM_LIMIT = 56 * 1024 * 1024
NT_DIMS = (((1,), (1,)), ((), ()))


def _cparams(sem):
    return pltpu.CompilerParams(dimension_semantics=sem, vmem_limit_bytes=VMEM_LIMIT)


def _silu(x):
    return x * jax.nn.sigmoid(x)


def _split_bf16(x):
    hi = x.astype(BF16)
    lo = (x - hi.astype(F32)).astype(BF16)
    return hi, lo


def _adaln(x, g, shift, scale):
    ms = jnp.mean(x * x, axis=-1, keepdims=True)
    return (x * lax.rsqrt(ms + EPS)) * g * (1.0 + scale) + shift


def _mod_kernel(cond_ref, w_ref, b_ref, o_ref):
    c = cond_ref[...]
    a = _silu(c)
    o_ref[0] = jnp.dot(a, w_ref[0], preferred_element_type=F32,
                       precision=lax.Precision.HIGHEST) + b_ref[0]


def _modulation(cond, ada_w, ada_b):
    depth, d, n = ada_w.shape
    rows = cond.shape[0]
    tn = 1536
    return pl.pallas_call(
        _mod_kernel,
        grid=(depth, n // tn),
        in_specs=[
            pl.BlockSpec((rows, d), lambda l, j: (0, 0)),
            pl.BlockSpec((1, d, tn), lambda l, j: (l, 0, j)),
            pl.BlockSpec((1, 1, tn), lambda l, j: (l, 0, j)),
        ],
        out_specs=pl.BlockSpec((1, rows, tn), lambda l, j: (l, 0, j)),
        out_shape=jax.ShapeDtypeStruct((depth, rows, n), F32),
        compiler_params=_cparams(("arbitrary", "arbitrary")),
        name="modulation",
    )(cond, ada_w, ada_b.reshape(depth, 1, n))


def _route(logits_t, bias_col):
    tm = logits_t.shape[1]
    scores = jax.nn.sigmoid(logits_t)
    biased = scores + bias_col
    sub = lax.broadcasted_iota(jnp.int32, (GROUP_SIZE, tm), 0).astype(F32)
    ninf = jnp.float32(-jnp.inf)
    groups, gscore = [], []
    for g in range(N_GROUPS):
        v = biased[g * GROUP_SIZE:(g + 1) * GROUP_SIZE]
        m1 = jnp.max(v, axis=0, keepdims=True)
        first = jnp.min(jnp.where(v == m1, sub, GROUP_SIZE), axis=0, keepdims=True)
        m2 = jnp.max(jnp.where(sub == first, ninf, v), axis=0, keepdims=True)
        groups.append(v)
        gscore.append(m1 + m2)
    masked = []
    for g in range(N_GROUPS):
        rank = jnp.zeros((1, tm), jnp.int32)
        for g2 in range(N_GROUPS):
            if g2 == g:
                continue
            ahead = gscore[g2] > gscore[g]
            if g2 < g:
                ahead = ahead | (gscore[g2] == gscore[g])
            rank = rank + ahead.astype(jnp.int32)
        masked.append(jnp.where(rank < TOPK_GROUPS, groups[g], ninf))
    masked = jnp.concatenate(masked, axis=0)
    eidx = lax.broadcasted_iota(jnp.int32, (N_EXPERTS, tm), 0)
    rank = jnp.zeros((N_EXPERTS, tm), jnp.int32)
    for e2 in range(N_EXPERTS):
        row = masked[e2:e2 + 1]
        ahead = (row > masked) | ((row == masked) & (eidx > e2))
        rank = rank + ahead.astype(jnp.int32)
    w = jnp.where(rank < TOPK, scores, 0.0)
    wsum = jnp.sum(w, axis=0, keepdims=True)
    return w / wsum * ROUTED_SCALE


def _ffn_pre(x1, mod_ref, gffn_ref, wr_hi_ref, wr_lo_ref, rb_ref, h_ref, gates_ref):
    sh2 = mod_ref[0, 3:4, :]
    sc2 = mod_ref[0, 4:5, :]
    h = _adaln(x1, gffn_ref[...], sh2, sc2)
    h_hi, h_lo = _split_bf16(h)
    h_ref[...] = h_hi
    wr_hi = wr_hi_ref[...]
    logits_t = (lax.dot_general(wr_hi, h_hi, NT_DIMS, preferred_element_type=F32)
                + lax.dot_general(wr_lo_ref[...], h_hi, NT_DIMS, preferred_element_type=F32)
                + lax.dot_general(wr_hi, h_lo, NT_DIMS, preferred_element_type=F32))
    gates_t = _route(logits_t[:N_EXPERTS], rb_ref[...])
    gates_t = jnp.concatenate([gates_t, jnp.zeros_like(gates_t)], axis=0)
    gates_ref[...] = gates_t.T


def _mixer_kernel(nbp, bps, sp, ss,
                  xc_ref, xp_ref, xn_ref, mod_ref, gmix_ref, win_ref, pw_ref, ps_ref, cw_ref,
                  wout_ref, gffn_ref, wr_hi_ref, wr_lo_ref, rb_ref,
                  x1_ref, h_ref, gates_ref):
    i = pl.program_id(0)
    is_p = i < nbp
    j = lax.rem(jnp.maximum(i - nbp, 0), bps)
    first = is_p | (j == 0)
    last = is_p | (j == bps - 1)
    base = jnp.where(is_p, 0, j * TM)
    slen = jnp.where(is_p, sp, ss)

    sh1 = mod_ref[0, 0:1, :]
    sc1 = mod_ref[0, 1:2, :]
    g1 = mod_ref[0, 2:3, :]
    xc = xc_ref[...]
    x_ext = jnp.concatenate([xp_ref[...], xc, xn_ref[...]], axis=0)
    h_ext = _adaln(x_ext, gmix_ref[...], sh1, sc1).astype(BF16)
    u = jnp.dot(h_ext, win_ref[...], preferred_element_type=F32)
    next_ = TM + 2 * HALO
    row = lax.broadcasted_iota(jnp.int32, (next_, 1), 0)
    keep = ((row >= HALO) | jnp.logical_not(first)) & ((row < HALO + TM) | jnp.logical_not(last))
    u = jnp.where(keep, u, 0.0)

    dm = u.shape[1] // 4
    ua = u[:, :dm]
    gate_b = u[HALO:HALO + TM, dm:2 * dm]
    z = u[:, 2 * dm:3 * dm] * u[:, 3 * dm:]

    def up(a, k):
        return pltpu.roll(a, next_ - k, 0)

    pos = base + lax.broadcasted_iota(jnp.int32, (TM, 1), 0)
    pg = dm // len(POOL_WINDOWS)
    ya = []
    for g, w in enumerate(POOL_WINDOWS):
        e = ua[:, g * pg:(g + 1) * pg]
        acc = e
        span = 1
        while span < w:
            acc = acc + up(acc, span)
            span *= 2
        off = HALO - w // 2
        wsum = (up(acc, off) if off else acc)[:TM]
        lo = jnp.maximum(pos - w // 2, 0)
        hi = jnp.minimum(pos + (w - w // 2 - 1), slen - 1)
        cnt = (hi - lo + 1).astype(F32)
        diff = wsum / cnt - e[HALO:HALO + TM]
        ya.append(jnp.dot(diff.astype(BF16), pw_ref[g], preferred_element_type=F32))
    y_a = jnp.concatenate(ya, axis=-1) * ps_ref[...]
    zc = (cw_ref[0:1, :] * up(z, HALO - 1)[:TM] + cw_ref[1:2, :] * z[HALO:HALO + TM]
          + cw_ref[2:3, :] * up(z, HALO + 1)[:TM])
    y_b = gate_b * zc
    ycat = jnp.concatenate([y_a, y_b], axis=-1).astype(BF16)
    y = jnp.dot(ycat, wout_ref[...], preferred_element_type=F32)
    x1 = xc + g1 * y
    x1_ref[...] = x1
    _ffn_pre(x1, mod_ref, gffn_ref, wr_hi_ref, wr_lo_ref, rb_ref, h_ref, gates_ref)


def _mod_index(nbp, bps):
    def f(i):
        return jnp.where(i < nbp, 0, 1 + jnp.maximum(i - nbp, 0) // bps)
    return f


def _const_spec(shape):
    nd = len(shape)
    return pl.BlockSpec(shape, lambda i: (0,) * nd)


def _mixer(x, mod, g_mix, w_in, pool_w, pool_scale, conv_w, w_out, g_ffn, wr_hi, wr_lo, rbias,
           nbp, bps, sp, ss):
    t, d = x.shape
    nblk = t // TM
    midx = _mod_index(nbp, bps)
    hpb = TM // HALO
    nh = t // HALO
    in_specs = [
        pl.BlockSpec((TM, d), lambda i: (i, 0)),
        pl.BlockSpec((HALO, d), lambda i: (jnp.maximum(i * hpb - 1, 0), 0)),
        pl.BlockSpec((HALO, d), lambda i: (jnp.minimum((i + 1) * hpb, nh - 1), 0)),
        pl.BlockSpec((1, 6, d), lambda i: (midx(i), 0, 0)),
        _const_spec(g_mix.shape), _const_spec(w_in.shape), _const_spec(pool_w.shape),
        _const_spec(pool_scale.shape), _const_spec(conv_w.shape), _const_spec(w_out.shape),
        _const_spec(g_ffn.shape), _const_spec(wr_hi.shape), _const_spec(wr_lo.shape),
        _const_spec(rbias.shape),
    ]
    out_specs = [
        pl.BlockSpec((TM, d), lambda i: (i, 0)),
        pl.BlockSpec((TM, d), lambda i: (i, 0)),
        pl.BlockSpec((TM, 128), lambda i: (i, 0)),
    ]
    out_shape = [
        jax.ShapeDtypeStruct((t, d), F32),
        jax.ShapeDtypeStruct((t, d), BF16),
        jax.ShapeDtypeStruct((t, 128), F32),
    ]
    return pl.pallas_call(
        functools.partial(_mixer_kernel, nbp, bps, sp, ss),
        grid=(nblk,), in_specs=in_specs, out_specs=out_specs, out_shape=out_shape,
        compiler_params=_cparams(("parallel",)), name="pool_conv_mixer",
    )(x, x, x, mod, g_mix, w_in, pool_w, pool_scale, conv_w, w_out, g_ffn, wr_hi, wr_lo, rbias)


def _moe_kernel(h_ref, gates_ref, x_ref, mod_ref, wgu_ref, wd_ref, sgu_ref, sd_ref,
                o_ref, acc_ref):
    e = pl.program_id(1)
    h = h_ref[...]
    f = wd_ref.shape[1]

    @pl.when(e == 0)
    def _():
        hs = jnp.dot(h, sgu_ref[...], preferred_element_type=F32)
        act = _silu(hs[:, :f]) * hs[:, f:]
        acc_ref[...] = jnp.dot(act.astype(BF16), sd_ref[...], preferred_element_type=F32)

    hh = jnp.dot(h, wgu_ref[0], preferred_element_type=F32)
    gates = gates_ref[...]
    lane = lax.broadcasted_iota(jnp.int32, gates.shape, 1)
    gcol = jnp.sum(jnp.where(lane == e, gates, 0.0), axis=-1, keepdims=True)
    act = _silu(hh[:, :f]) * hh[:, f:] * gcol
    acc_ref[...] += jnp.dot(act.astype(BF16), wd_ref[0], preferred_element_type=F32)

    @pl.when(e == pl.num_programs(1) - 1)
    def _():
        o_ref[...] = x_ref[...] + mod_ref[0, 5:6, :] * acc_ref[...]


def _moe(x, h, gates, mod, wgu, wd, sgu, sd, tm, nbp, bps):
    t, d = x.shape
    ne = wgu.shape[0]
    midx = _mod_index(nbp, bps)
    in_specs = [
        pl.BlockSpec((tm, d), lambda i, e: (i, 0)),
        pl.BlockSpec((tm, 128), lambda i, e: (i, 0)),
        pl.BlockSpec((tm, d), lambda i, e: (i, 0)),
        pl.BlockSpec((1, 6, d), lambda i, e: (midx(i), 0, 0)),
        pl.BlockSpec((1,) + wgu.shape[1:], lambda i, e: (e, 0, 0)),
        pl.BlockSpec((1,) + wd.shape[1:], lambda i, e: (e, 0, 0)),
        pl.BlockSpec(sgu.shape, lambda i, e: (0, 0)),
        pl.BlockSpec(sd.shape, lambda i, e: (0, 0)),
    ]
    return pl.pallas_call(
        _moe_kernel,
        grid=(t // tm, ne), in_specs=in_specs,
        out_specs=pl.BlockSpec((tm, d), lambda i, e: (i, 0)),
        out_shape=jax.ShapeDtypeStruct((t, d), F32),
        scratch_shapes=[pltpu.VMEM((tm, d), F32)],
        compiler_params=_cparams(("parallel", "arbitrary")), name="moe_dense",
    )(h, gates, x, mod, wgu, wd, sgu, sd)


def _head_rms(x, g_row, hm_ref, hmt_ref):
    sq_hi, sq_lo = _split_bf16(x * x)
    hm = hm_ref[...]
    ss = (jnp.dot(sq_hi, hm, preferred_element_type=F32)
          + jnp.dot(sq_lo, hm, preferred_element_type=F32))
    r = lax.rsqrt(ss * (1.0 / HEAD_DIM) + EPS)
    r_hi, r_lo = _split_bf16(r)
    hmt = hmt_ref[...]
    rb = (jnp.dot(r_hi, hmt, preferred_element_type=F32)
          + jnp.dot(r_lo, hmt, preferred_element_type=F32))
    return (x * rb) * g_row


def _qkv_kernel(emit_f32, x_ref, mod_ref, gmix_ref, w_ref, qg_ref, kg_ref, hm_ref, hmt_ref, *outs):
    d = x_ref.shape[1]
    sh1 = mod_ref[0, 0:1, :]
    sc1 = mod_ref[0, 1:2, :]
    h = _adaln(x_ref[...], gmix_ref[...], sh1, sc1).astype(BF16)
    qkv = jnp.dot(h, w_ref[...], preferred_element_type=F32)
    q = _head_rms(qkv[:, :d], qg_ref[...], hm_ref, hmt_ref)
    k = _head_rms(qkv[:, d:2 * d], kg_ref[...], hm_ref, hmt_ref)
    v = qkv[:, 2 * d:]
    outs[0][...] = q.astype(BF16)
    outs[1][...] = k.astype(BF16)
    outs[2][...] = v.astype(BF16)
    if emit_f32:
        outs[3][...] = k
        outs[4][...] = v


def _qkv(x, mod, g_mix, w_qkv, qg, kg, hm, hmt, blk0, nblk, midx, emit_f32):
    t, d = x.shape
    in_specs = [
        pl.BlockSpec((TM, d), lambda i: (i + blk0, 0)),
        pl.BlockSpec((1, 6, d), lambda i: (midx(i + blk0), 0, 0)),
        _const_spec(g_mix.shape), _const_spec(w_qkv.shape), _const_spec(qg.shape),
        _const_spec(kg.shape), _const_spec(hm.shape), _const_spec(hmt.shape),
    ]
    n_out = 5 if emit_f32 else 3
    out_specs = [pl.BlockSpec((TM, d), lambda i: (i, 0)) for _ in range(n_out)]
    out_shape = [jax.ShapeDtypeStruct((nblk * TM, d), BF16 if o < 3 else F32) for o in range(n_out)]
    return pl.pallas_call(
        functools.partial(_qkv_kernel, emit_f32),
        grid=(nblk,), in_specs=in_specs, out_specs=out_specs, out_shape=out_shape,
        compiler_params=_cparams(("parallel",)), name="qkv_f32" if emit_f32 else "qkv",
    )(x, mod, g_mix, w_qkv, qg, kg, hm, hmt)


def _head_masks():
    lane = lax.broadcasted_iota(jnp.int32, (1, 2 * HEAD_DIM), 1)
    return lane < HEAD_DIM


def _ctx_attn_kernel(q_ref, k_ref, v_ref, o_ref):
    q = q_ref[...]
    k = k_ref[...]
    v = v_ref[...]
    lo = _head_masks()
    outs = []
    for hh in range(2):
        msk = lo if hh == 0 else jnp.logical_not(lo)
        qm = jnp.where(msk, q, jnp.zeros_like(q)) * jnp.asarray(HEAD_DIM ** -0.5, BF16)
        s = lax.dot_general(qm, k, NT_DIMS, preferred_element_type=F32)
        m = jnp.max(s, axis=-1, keepdims=True)
        p = jnp.exp(s - m)
        l = jnp.sum(p, axis=-1, keepdims=True)
        o = jnp.dot(p.astype(BF16), v, preferred_element_type=F32)
        outs.append(o / l)
    o_ref[...] = jnp.where(lo, outs[0], outs[1]).astype(BF16)


def _ctx_attn(q, k, v, nb, s):
    t, d = q.shape
    hp = d // (2 * HEAD_DIM)
    spec = pl.BlockSpec((s, 2 * HEAD_DIM), lambda b, h: (b, h))
    return pl.pallas_call(
        _ctx_attn_kernel, grid=(nb, hp), in_specs=[spec, spec, spec], out_specs=spec,
        out_shape=jax.ShapeDtypeStruct((nb * s, d), BF16),
        compiler_params=_cparams(("parallel", "parallel")), name="context_attention",
    )(q, k, v)


NA_QROWS = 8
NA_KROWS = 16


def _na_kernel(rows, q_ref, k_ref, v_ref, kc_ref, vc_ref, bias_ref, o_ref):
    rb = pl.program_id(1)
    kr0 = jnp.clip(rb * NA_QROWS - WIN_H // 2, 0, rows - NA_KROWS)
    start = pl.multiple_of(kr0 * GRID_W, 256)
    nk = NA_KROWS * GRID_W
    q = q_ref[0]
    kw = k_ref[0, pl.ds(start, nk), :]
    vw = v_ref[0, pl.ds(start, nk), :]
    kc = kc_ref[0]
    vc = vc_ref[0]
    lo = _head_masks()
    outs = []
    for hh in range(2):
        msk = lo if hh == 0 else jnp.logical_not(lo)
        qm = jnp.where(msk, q, jnp.zeros_like(q)) * jnp.asarray(HEAD_DIM ** -0.5, BF16)
        s = lax.dot_general(qm, kw, NT_DIMS, preferred_element_type=F32) + bias_ref[0, hh]
        sc = lax.dot_general(qm, kc, NT_DIMS, preferred_element_type=F32)
        m = jnp.maximum(jnp.max(s, axis=-1, keepdims=True), jnp.max(sc, axis=-1, keepdims=True))
        p = jnp.exp(s - m)
        pc = jnp.exp(sc - m)
        l = jnp.sum(p, axis=-1, keepdims=True) + jnp.sum(pc, axis=-1, keepdims=True)
        o = (jnp.dot(p.astype(BF16), vw, preferred_element_type=F32)
             + jnp.dot(pc.astype(BF16), vc, preferred_element_type=F32))
        outs.append(o / l)
    o_ref[0] = jnp.where(lo, outs[0], outs[1]).astype(BF16)


def _na_bias_geometry(rows):
    nrb = rows // NA_QROWS
    classes = (0, 1, nrb - 1)
    ri = np.zeros((3, NA_QROWS * GRID_W, NA_KROWS * GRID_W), np.int32)
    ci = np.zeros_like(ri)
    ok = np.zeros(ri.shape, bool)
    cq = np.arange(GRID_W)
    ck = np.arange(GRID_W)
    q_start = np.clip(cq - WIN_W // 2, 0, GRID_W - WIN_W)
    col_ok = (ck[None, :] >= q_start[:, None]) & (ck[None, :] < q_start[:, None] + WIN_W)
    col_idx = np.clip(ck[None, :] - cq[:, None] + WIN_W - 1, 0, 2 * WIN_W - 2)
    for c, rb in enumerate(classes):
        r0 = rb * NA_QROWS
        kr0 = int(np.clip(r0 - WIN_H // 2, 0, rows - NA_KROWS))
        for rl in range(NA_QROWS):
            r = r0 + rl
            sr = int(np.clip(r - WIN_H // 2, 0, rows - WIN_H))
            for kl in range(NA_KROWS):
                kr = kr0 + kl
                row_ok = sr <= kr < sr + WIN_H
                qs = slice(rl * GRID_W, (rl + 1) * GRID_W)
                ks = slice(kl * GRID_W, (kl + 1) * GRID_W)
                ri[c, qs, ks] = int(np.clip(kr - r + WIN_H - 1, 0, 2 * WIN_H - 2))
                ci[c, qs, ks] = col_idx
                ok[c, qs, ks] = col_ok & row_ok
    return ri, ci, ok


def _na_attn(q, k, v, kc, vc, bias_tab, rows):
    nb, s, d = q.shape
    hp = d // (2 * HEAD_DIM)
    nrb = rows // NA_QROWS
    nq = NA_QROWS * GRID_W
    lc = kc.shape[1]

    def cls(r):
        return jnp.where(r == 0, 0, jnp.where(r == nrb - 1, 2, 1))

    in_specs = [
        pl.BlockSpec((1, nq, 2 * HEAD_DIM), lambda h, r, b: (b, r, h)),
        pl.BlockSpec((1, s, 2 * HEAD_DIM), lambda h, r, b: (b, 0, h)),
        pl.BlockSpec((1, s, 2 * HEAD_DIM), lambda h, r, b: (b, 0, h)),
        pl.BlockSpec((1, lc, 2 * HEAD_DIM), lambda h, r, b: (b, 0, h)),
        pl.BlockSpec((1, lc, 2 * HEAD_DIM), lambda h, r, b: (b, 0, h)),
        pl.BlockSpec((1, 2, nq, NA_KROWS * GRID_W), lambda h, r, b: (cls(r), h, 0, 0)),
    ]
    return pl.pallas_call(
        functools.partial(_na_kernel, rows),
        grid=(hp, nrb, nb), in_specs=in_specs,
        out_specs=pl.BlockSpec((1, nq, 2 * HEAD_DIM), lambda h, r, b: (b, r, h)),
        out_shape=jax.ShapeDtypeStruct((nb, s, d), BF16),
        compiler_params=_cparams(("parallel", "parallel", "parallel")),
        name="neighbourhood_attention",
    )(q, k, v, kc, vc, bias_tab)


def _oproj_kernel(a_ref, x_ref, mod_ref, wout_ref, gffn_ref, wr_hi_ref, wr_lo_ref, rb_ref,
                  x1_ref, h_ref, gates_ref):
    y = jnp.dot(a_ref[...], wout_ref[...], preferred_element_type=F32)
    x1 = x_ref[...] + mod_ref[0, 2:3, :] * y
    x1_ref[...] = x1
    _ffn_pre(x1, mod_ref, gffn_ref, wr_hi_ref, wr_lo_ref, rb_ref, h_ref, gates_ref)


def _oproj(attn, x, mod, w_out, g_ffn, wr_hi, wr_lo, rbias, nbp, bps):
    t, d = x.shape
    midx = _mod_index(nbp, bps)
    in_specs = [
        pl.BlockSpec((TM, d), lambda i: (i, 0)),
        pl.BlockSpec((TM, d), lambda i: (i, 0)),
        pl.BlockSpec((1, 6, d), lambda i: (midx(i), 0, 0)),
        _const_spec(w_out.shape), _const_spec(g_ffn.shape), _const_spec(wr_hi.shape),
        _const_spec(wr_lo.shape), _const_spec(rbias.shape),
    ]
    out_specs = [
        pl.BlockSpec((TM, d), lambda i: (i, 0)),
        pl.BlockSpec((TM, d), lambda i: (i, 0)),
        pl.BlockSpec((TM, 128), lambda i: (i, 0)),
    ]
    out_shape = [
        jax.ShapeDtypeStruct((t, d), F32),
        jax.ShapeDtypeStruct((t, d), BF16),
        jax.ShapeDtypeStruct((t, 128), F32),
    ]
    return pl.pallas_call(
        _oproj_kernel, grid=(t // TM,), in_specs=in_specs, out_specs=out_specs,
        out_shape=out_shape, compiler_params=_cparams(("parallel",)), name="attn_out_proj",
    )(attn, x, mod, w_out, g_ffn, wr_hi, wr_lo, rbias)


def _router_weights(w_router, router_bias):
    d, ne = w_router.shape
    wt = jnp.pad(w_router.T, ((0, 128 - ne), (0, 0)))
    hi = wt.astype(BF16)
    lo = (wt - hi.astype(F32)).astype(BF16)
    return hi, lo, router_bias.reshape(ne, 1)


def _moe_weights(w_gate, w_up, w_down, s_gate, s_up, s_down):
    wgu = jnp.concatenate([w_gate, w_up], axis=-1).astype(BF16)
    sgu = jnp.concatenate([s_gate, s_up], axis=-1).astype(BF16)
    return wgu, w_down.astype(BF16), sgu, s_down.astype(BF16)


def _moe_tile(tp, ts_seq):
    tm = 1024
    while tp % tm or ts_seq % tm:
        tm //= 2
    return tm


def kernel(x_prompt, x_sample, cache_k, cache_v, c, c_ctx, ada_w, ada_b, norm_mix, norm_ffn,
           pc_w_in, pc_pool_w, pc_pool_scale, pc_conv_w, pc_w_out,
           na_w_qkv, na_q_norm, na_k_norm, na_rpb, na_w_out,
           moe_router, moe_router_bias, moe_w_gate, moe_w_up, moe_w_down,
           moe_shared_gate, moe_shared_up, moe_shared_down):
    nb_p, s_p, d = x_prompt.shape
    nb_s, s_s, _ = x_sample.shape
    assert s_p == TM and s_s % TM == 0
    tp, ts = nb_p * s_p, nb_s * s_s
    nbp, bps = tp // TM, s_s // TM
    depth = ada_w.shape[0]
    nh = d // HEAD_DIM
    rows = s_s // GRID_W

    x = jnp.concatenate([x_prompt.reshape(tp, d), x_sample.reshape(ts, d)], axis=0)
    cond = jnp.concatenate([c_ctx[None], c], axis=0)
    cond = jnp.pad(cond, ((0, -cond.shape[0] % 8), (0, 0)))
    mods = _modulation(cond, ada_w, ada_b).reshape(depth, cond.shape[0], 6, d)

    tm_moe = _moe_tile(tp, s_s)
    nbp_moe, bps_moe = tp // tm_moe, s_s // tm_moe
    new_k, new_v = [], []
    for layer in range(depth):
        i = layer // 2
        mod = mods[layer]
        g_mix = norm_mix[layer].reshape(1, d)
        g_ffn = norm_ffn[layer].reshape(1, d)
        wr_hi, wr_lo, rbias = _router_weights(moe_router[layer], moe_router_bias[layer])
        if layer % 2 == 0:
            x, h, gates = _mixer(
                x, mod, g_mix, pc_w_in[i].astype(BF16), pc_pool_w[i].astype(BF16),
                pc_pool_scale[i].reshape(1, -1), pc_conv_w[i], pc_w_out[i].astype(BF16),
                g_ffn, wr_hi, wr_lo, rbias, nbp, bps, s_p, s_s)
        else:
            head_of = np.arange(d) // HEAD_DIM
            hm = jnp.asarray(head_of[:, None] == np.arange(128)[None, :], BF16)
            hmt = jnp.asarray(np.arange(128)[:, None] == head_of[None, :], BF16)
            qg = jnp.tile(na_q_norm[i], nh).reshape(1, d)
            kg = jnp.tile(na_k_norm[i], nh).reshape(1, d)
            w_qkv = na_w_qkv[i].astype(BF16)
            midx = _mod_index(nbp, bps)
            qp, kp, vp, kp32, vp32 = _qkv(x, mod, g_mix, w_qkv, qg, kg, hm, hmt, 0, nbp, midx, True)
            qs, ks, vs = _qkv(x, mod, g_mix, w_qkv, qg, kg, hm, hmt, nbp, ts // TM, midx, False)
            new_k.append(kp32.reshape(nb_p, s_p, nh, HEAD_DIM))
            new_v.append(vp32.reshape(nb_p, s_p, nh, HEAD_DIM))
            ap = _ctx_attn(qp, kp, vp, nb_p, s_p)
            ri, ci, ok = _na_bias_geometry(rows)
            bias_tab = jnp.where(ok[:, None], jnp.moveaxis(na_rpb[i][:, ri, ci], 0, 1), NEG)
            lc = cache_k.shape[2]
            kc = cache_k[:, i].reshape(nb_s, lc, d).astype(BF16)
            vc = cache_v[:, i].reshape(nb_s, lc, d).astype(BF16)
            a_s = _na_attn(qs.reshape(nb_s, s_s, d), ks.reshape(nb_s, s_s, d),
                           vs.reshape(nb_s, s_s, d), kc, vc, bias_tab, rows)
            attn = jnp.concatenate([ap, a_s.reshape(ts, d)], axis=0)
            x, h, gates = _oproj(attn, x, mod, na_w_out[i].astype(BF16), g_ffn, wr_hi, wr_lo,
                                 rbias, nbp, bps)
        wgu, wd, sgu, sd = _moe_weights(moe_w_gate[layer], moe_w_up[layer], moe_w_down[layer],
                                        moe_shared_gate[layer], moe_shared_up[layer],
                                        moe_shared_down[layer])
        x = _moe(x, h, gates, mod, wgu, wd, sgu, sd, tm_moe, nbp_moe, bps_moe)
    y_p = x[:tp].reshape(nb_p, s_p, d)
    y_s = x[tp:].reshape(nb_s, s_s, d)
    return (y_p, y_s, jnp.stack(new_k, axis=1), jnp.stack(new_v, axis=1))
```

```python
import functools

import numpy as np
import jax
import jax.numpy as jnp
from jax import lax
from jax.experimental import pallas as pl
from jax.experimental.pallas import tpu as pltpu
from jax.experimental.pallas import tpu_sc as plsc

F32 = jnp.float32
BF16 = jnp.bfloat16
I32 = jnp.int32
U32 = jnp.uint32

TM = 256
HALO = 8
POOL_WINDOWS = (2, 4, 8, 16)
N_EXPERTS = 64
N_GROUPS = 8
GROUP_SIZE = N_EXPERTS // N_GROUPS
TOPK_GROUPS = 4
TOPK = 8
ROUTED_SCALE = 2.5
EPS = 1e-6
GRID_W = 64
WIN_H = 8
WIN_W = 16
HEAD_DIM = 64
NEG = float(np.finfo(np.float32).min)
VMEM_LIMIT = 56 * 1024 * 1024
NT_DIMS = (((1,), (1,)), ((), ()))
TG = 512
SC_WINDOW = 64
SC_WORKERS = 32


def _cparams(sem):
    return pltpu.CompilerParams(dimension_semantics=sem, vmem_limit_bytes=VMEM_LIMIT)


def _silu(x):
    return x * jax.nn.sigmoid(x)


def _split_bf16(x):
    hi = x.astype(BF16)
    lo = (x - hi.astype(F32)).astype(BF16)
    return hi, lo


def _adaln(x, g, shift, scale):
    ms = jnp.mean(x * x, axis=-1, keepdims=True)
    return (x * lax.rsqrt(ms + EPS)) * g * (1.0 + scale) + shift


def _pack_pair(x):
    w = x.shape[1] // 2
    lo = lax.bitcast_convert_type(x[:, :w].astype(BF16).astype(F32), U32) >> 16
    hi = lax.bitcast_convert_type(x[:, w:].astype(BF16).astype(F32), U32)
    return lax.bitcast_convert_type(lo | hi, I32)


def _unpack_pair(p):
    u = lax.bitcast_convert_type(p, U32)
    lo = lax.bitcast_convert_type(u << 16, F32)
    hi = lax.bitcast_convert_type(u & jnp.uint32(0xFFFF0000), F32)
    return jnp.concatenate([lo, hi], axis=-1)


def _mod_kernel(cond_ref, w_ref, b_ref, o_ref):
    c = cond_ref[...]
    a = _silu(c)
    o_ref[0] = jnp.dot(a, w_ref[0], preferred_element_type=F32,
                       precision=lax.Precision.HIGHEST) + b_ref[0]


def _modulation(cond, ada_w, ada_b):
    depth, d, n = ada_w.shape
    rows = cond.shape[0]
    tn = 1536
    return pl.pallas_call(
        _mod_kernel,
        grid=(depth, n // tn),
        in_specs=[
            pl.BlockSpec((rows, d), lambda l, j: (0, 0)),
            pl.BlockSpec((1, d, tn), lambda l, j: (l, 0, j)),
            pl.BlockSpec((1, 1, tn), lambda l, j: (l, 0, j)),
        ],
        out_specs=pl.BlockSpec((1, rows, tn), lambda l, j: (l, 0, j)),
        out_shape=jax.ShapeDtypeStruct((depth, rows, n), F32),
        compiler_params=_cparams(("arbitrary", "arbitrary")),
        name="modulation",
    )(cond, ada_w, ada_b.reshape(depth, 1, n))


def _route(logits_t, bias_col):
    tm = logits_t.shape[1]
    scores = jax.nn.sigmoid(logits_t)
    biased = scores + bias_col
    sub = lax.broadcasted_iota(I32, (GROUP_SIZE, tm), 0).astype(F32)
    ninf = jnp.float32(-jnp.inf)
    groups, gscore = [], []
    for g in range(N_GROUPS):
        v = biased[g * GROUP_SIZE:(g + 1) * GROUP_SIZE]
        m1 = jnp.max(v, axis=0, keepdims=True)
        first = jnp.min(jnp.where(v == m1, sub, float(GROUP_SIZE)), axis=0, keepdims=True)
        m2 = jnp.max(jnp.where(sub == first, ninf, v), axis=0, keepdims=True)
        groups.append(v)
        gscore.append(m1 + m2)
    masked = []
    for g in range(N_GROUPS):
        rank = jnp.zeros((1, tm), I32)
        for g2 in range(N_GROUPS):
            if g2 == g:
                continue
            ahead = gscore[g2] > gscore[g]
            if g2 < g:
                ahead = ahead | (gscore[g2] == gscore[g])
            rank = rank + ahead.astype(I32)
        masked.append(jnp.where(rank < TOPK_GROUPS, groups[g], ninf))
    masked = jnp.concatenate(masked, axis=0)
    eidx = lax.broadcasted_iota(I32, (N_EXPERTS, tm), 0)
    rank = jnp.zeros((N_EXPERTS, tm), I32)
    for e2 in range(N_EXPERTS):
        row = masked[e2:e2 + 1]
        ahead = (row > masked) | ((row == masked) & (eidx > e2))
        rank = rank + ahead.astype(I32)
    sel = rank < TOPK
    w = jnp.where(sel, scores, 0.0)
    wsum = jnp.sum(w, axis=0, keepdims=True)
    return w / wsum * ROUTED_SCALE, sel


def _ffn_pre(x1, mod_ref, gffn_ref, wr_hi_ref, wr_lo_ref, rb_ref,
             h_ref, gates_ref, sel_ref, cnt_ref):
    sh2 = mod_ref[0, 3:4, :]
    sc2 = mod_ref[0, 4:5, :]
    h = _adaln(x1, gffn_ref[...], sh2, sc2)
    h_hi, h_lo = _split_bf16(h)
    h_ref[...] = _pack_pair(h_hi)
    wr_hi = wr_hi_ref[...]
    logits_t = (lax.dot_general(wr_hi, h_hi, NT_DIMS, preferred_element_type=F32)
                + lax.dot_general(wr_lo_ref[...], h_hi, NT_DIMS, preferred_element_type=F32)
                + lax.dot_general(wr_hi, h_lo, NT_DIMS, preferred_element_type=F32))
    gates_t, sel = _route(logits_t[:N_EXPERTS], rb_ref[...])
    gates_ref[...] = gates_t
    sel_b = sel.astype(F32).astype(BF16)
    sel_ref[...] = sel_b
    ones = jnp.ones((8, sel_b.shape[1]), BF16)
    cnt_ref[0] = lax.dot_general(ones, sel_b, NT_DIMS, preferred_element_type=F32)


def _pre_out_specs(t, d):
    specs = [
        pl.BlockSpec((TM, d), lambda i: (i, 0)),
        pl.BlockSpec((TM, d // 2), lambda i: (i, 0)),
        pl.BlockSpec((N_EXPERTS, TM), lambda i: (0, i)),
        pl.BlockSpec((N_EXPERTS, TM), lambda i: (0, i)),
        pl.BlockSpec((1, 8, N_EXPERTS), lambda i: (i, 0, 0)),
    ]
    shapes = [
        jax.ShapeDtypeStruct((t, d), F32),
        jax.ShapeDtypeStruct((t, d // 2), I32),
        jax.ShapeDtypeStruct((N_EXPERTS, t), F32),
        jax.ShapeDtypeStruct((N_EXPERTS, t), BF16),
        jax.ShapeDtypeStruct((t // TM, 8, N_EXPERTS), F32),
    ]
    return specs, shapes


def _mixer_kernel(nbp, bps, sp, ss,
                  xc_ref, xp_ref, xn_ref, mod_ref, gmix_ref, win_ref, pw_ref, ps_ref, cw_ref,
                  wout_ref, gffn_ref, wr_hi_ref, wr_lo_ref, rb_ref,
                  x1_ref, h_ref, gates_ref, sel_ref, cnt_ref):
    i = pl.program_id(0)
    is_p = i < nbp
    j = lax.rem(jnp.maximum(i - nbp, 0), bps)
    first = is_p | (j == 0)
    last = is_p | (j == bps - 1)
    base = jnp.where(is_p, 0, j * TM)
    slen = jnp.where(is_p, sp, ss)

    sh1 = mod_ref[0, 0:1, :]
    sc1 = mod_ref[0, 1:2, :]
    g1 = mod_ref[0, 2:3, :]
    xc = xc_ref[...]
    x_ext = jnp.concatenate([xp_ref[...], xc, xn_ref[...]], axis=0)
    h_ext = _adaln(x_ext, gmix_ref[...], sh1, sc1).astype(BF16)
    u = jnp.dot(h_ext, win_ref[...], preferred_element_type=F32)
    next_ = TM + 2 * HALO
    row = lax.broadcasted_iota(I32, (next_, 1), 0)
    keep = ((row >= HALO) | jnp.logical_not(first)) & ((row < HALO + TM) | jnp.logical_not(last))
    u = jnp.where(keep, u, 0.0)

    dm = u.shape[1] // 4
    ua = u[:, :dm]
    gate_b = u[HALO:HALO + TM, dm:2 * dm]
    z = u[:, 2 * dm:3 * dm] * u[:, 3 * dm:]

    def up(a, k):
        return pltpu.roll(a, next_ - k, 0)

    pos = base + lax.broadcasted_iota(I32, (TM, 1), 0)
    pg = dm // len(POOL_WINDOWS)
    ya = []
    for g, w in enumerate(POOL_WINDOWS):
        e = ua[:, g * pg:(g + 1) * pg]
        acc = e
        span = 1
        while span < w:
            acc = acc + up(acc, span)
            span *= 2
        off = HALO - w // 2
        wsum = (up(acc, off) if off else acc)[:TM]
        lo = jnp.maximum(pos - w // 2, 0)
        hi = jnp.minimum(pos + (w - w // 2 - 1), slen - 1)
        cnt = (hi - lo + 1).astype(F32)
        diff = wsum / cnt - e[HALO:HALO + TM]
        ya.append(jnp.dot(diff.astype(BF16), pw_ref[g], preferred_element_type=F32))
    y_a = jnp.concatenate(ya, axis=-1) * ps_ref[...]
    zc = (cw_ref[0:1, :] * up(z, HALO - 1)[:TM] + cw_ref[1:2, :] * z[HALO:HALO + TM]
          + cw_ref[2:3, :] * up(z, HALO + 1)[:TM])
    y_b = gate_b * zc
    ycat = jnp.concatenate([y_a, y_b], axis=-1).astype(BF16)
    y = jnp.dot(ycat, wout_ref[...], preferred_element_type=F32)
    x1 = xc + g1 * y
    x1_ref[...] = x1
    _ffn_pre(x1, mod_ref, gffn_ref, wr_hi_ref, wr_lo_ref, rb_ref,
             h_ref, gates_ref, sel_ref, cnt_ref)


def _mod_index(nbp, bps):
    def f(i):
        return jnp.where(i < nbp, 0, 1 + jnp.maximum(i - nbp, 0) // bps)
    return f


def _const_spec(shape):
    nd = len(shape)
    return pl.BlockSpec(shape, lambda i: (0,) * nd)


def _mixer(x, mod, g_mix, w_in, pool_w, pool_scale, conv_w, w_out, g_ffn, wr_hi, wr_lo, rbias,
           nbp, bps, sp, ss):
    t, d = x.shape
    nblk = t // TM
    midx = _mod_index(nbp, bps)
    hpb = TM // HALO
    nh = t // HALO
    in_specs = [
        pl.BlockSpec((TM, d), lambda i: (i, 0)),
        pl.BlockSpec((HALO, d), lambda i: (jnp.maximum(i * hpb - 1, 0), 0)),
        pl.BlockSpec((HALO, d), lambda i: (jnp.minimum((i + 1) * hpb, nh - 1), 0)),
        pl.BlockSpec((1, 6, d), lambda i: (midx(i), 0, 0)),
        _const_spec(g_mix.shape), _const_spec(w_in.shape), _const_spec(pool_w.shape),
        _const_spec(pool_scale.shape), _const_spec(conv_w.shape), _const_spec(w_out.shape),
        _const_spec(g_ffn.shape), _const_spec(wr_hi.shape), _const_spec(wr_lo.shape),
        _const_spec(rbias.shape),
    ]
    out_specs, out_shape = _pre_out_specs(t, d)
    return pl.pallas_call(
        functools.partial(_mixer_kernel, nbp, bps, sp, ss),
        grid=(nblk,), in_specs=in_specs, out_specs=out_specs, out_shape=out_shape,
        compiler_params=_cparams(("parallel",)), name="pool_conv_mixer",
    )(x, x, x, mod, g_mix, w_in, pool_w, pool_scale, conv_w, w_out, g_ffn, wr_hi, wr_lo, rbias)


def _slots_kernel(sel_ref, gates_ref, base_ref, pos_ref, w_ref):
    sel = sel_ref[...]
    tm = sel.shape[1]
    r = lax.broadcasted_iota(I32, (tm, tm), 0)
    c = lax.broadcasted_iota(I32, (tm, tm), 1)
    before = (r < c).astype(F32).astype(BF16)
    rank_tok = jnp.dot(sel, before, preferred_element_type=F32)
    er = lax.broadcasted_iota(I32, (N_EXPERTS, N_EXPERTS), 0)
    ec = lax.broadcasted_iota(I32, (N_EXPERTS, N_EXPERTS), 1)
    lower = (ec < er).astype(F32).astype(BF16)
    rank_exp = jnp.dot(lower, sel, preferred_element_type=F32)
    slot = base_ref[0] + rank_tok
    chosen = sel > 0
    gates = gates_ref[...]
    sub = lax.broadcasted_iota(I32, (TOPK, tm), 0)
    pos8 = jnp.zeros((TOPK, tm), F32)
    w8 = jnp.zeros((TOPK, tm), F32)
    for k in range(TOPK):
        mk = chosen & (rank_exp == float(k))
        pk = jnp.sum(jnp.where(mk, slot, 0.0), axis=0, keepdims=True)
        wk = jnp.sum(jnp.where(mk, gates, 0.0), axis=0, keepdims=True)
        pos8 = jnp.where(sub == k, pk, pos8)
        w8 = jnp.where(sub == k, wk, w8)
    pos_ref[...] = pos8.astype(I32)
    w8 = jnp.concatenate([w8, jnp.zeros((128 - TOPK, tm), F32)], axis=0)
    w_ref[...] = w8.T


def _slots(sel_t, gates_t, base):
    ne, t = sel_t.shape
    return pl.pallas_call(
        _slots_kernel, grid=(t // TM,),
        in_specs=[
            pl.BlockSpec((ne, TM), lambda i: (0, i)),
            pl.BlockSpec((ne, TM), lambda i: (0, i)),
            pl.BlockSpec((1, ne, 1), lambda i: (i, 0, 0)),
        ],
        out_specs=[pl.BlockSpec((TOPK, TM), lambda i: (0, i)),
                   pl.BlockSpec((TM, 128), lambda i: (i, 0))],
        out_shape=[jax.ShapeDtypeStruct((TOPK, t), I32), jax.ShapeDtypeStruct((t, 128), F32)],
        compiler_params=_cparams(("parallel",)), name="moe_slots",
    )(sel_t, gates_t, base)


def _sc_gather(table, idx):
    m = idx.shape[0]
    d = table.shape[1]
    assert m % (SC_WINDOW * SC_WORKERS) == 0
    mesh = plsc.VectorSubcoreMesh(core_axis_name="core", subcore_axis_name="subcore")

    @pl.kernel(out_type=jax.ShapeDtypeStruct((m, d), table.dtype), mesh=mesh)
    def gather_rows(x_hbm, i_hbm, o_hbm):
        def body(i_vmem, o_vmem):
            pltpu.sync_copy(x_hbm.at[i_vmem.at[0]], o_vmem)

        pltpu.emit_pipeline(
            body, grid=(m // SC_WINDOW,),
            in_specs=[pl.BlockSpec((1, SC_WINDOW), index_map=lambda i: (i, 0))],
            out_specs=[pl.BlockSpec((SC_WINDOW, d), index_map=lambda i: (i, 0))],
            core_axis_name=("core", "subcore"),
            dimension_semantics=(pltpu.PARALLEL,),
        )(i_hbm, o_hbm)

    return gather_rows(table, idx.reshape(m // SC_WINDOW, SC_WINDOW))


def _expert_ffn_kernel(te_ref, nv_ref, x_ref, wgu_ref, wd_ref, y_ref):
    j = pl.program_id(0)

    @pl.when(j < nv_ref[0])
    def _():
        f = wd_ref.shape[1]
        x = _unpack_pair(x_ref[...]).astype(BF16)
        hh = jnp.dot(x, wgu_ref[0], preferred_element_type=F32)
        act = (_silu(hh[:, :f]) * hh[:, f:]).astype(BF16)
        y_ref[...] = _pack_pair(jnp.dot(act, wd_ref[0], preferred_element_type=F32))


def _expert_ffn(xs, tile_expert, n_valid, wgu, wd):
    n, dh = xs.shape
    last = lambda j, te, nv: jnp.minimum(j, nv[0] - 1)
    grid_spec = pltpu.PrefetchScalarGridSpec(
        num_scalar_prefetch=2, grid=(n // TG,),
        in_specs=[
            pl.BlockSpec((TG, dh), lambda j, te, nv: (last(j, te, nv), 0)),
            pl.BlockSpec((1,) + wgu.shape[1:], lambda j, te, nv: (te[j], 0, 0)),
            pl.BlockSpec((1,) + wd.shape[1:], lambda j, te, nv: (te[j], 0, 0)),
        ],
        out_specs=pl.BlockSpec((TG, dh), lambda j, te, nv: (last(j, te, nv), 0)),
    )
    return pl.pallas_call(
        _expert_ffn_kernel, grid_spec=grid_spec,
        out_shape=jax.ShapeDtypeStruct((n, dh), I32),
        compiler_params=_cparams(("arbitrary",)), name="expert_ffn",
    )(tile_expert, n_valid, xs, wgu, wd)


def _combine_kernel(yg_ref, w_ref, h_ref, x_ref, mod_ref, sgu_ref, sd_ref, o_ref):
    f = sd_ref.shape[0]
    h = _unpack_pair(h_ref[...]).astype(BF16)
    hs = jnp.dot(h, sgu_ref[...], preferred_element_type=F32)
    act = (_silu(hs[:, :f]) * hs[:, f:]).astype(BF16)
    acc = jnp.dot(act, sd_ref[...], preferred_element_type=F32)
    w = w_ref[...]
    for k in range(TOPK):
        acc = acc + w[:, k:k + 1] * _unpack_pair(yg_ref[k])
    o_ref[...] = x_ref[...] + mod_ref[0, 5:6, :] * acc


def _combine(yg, w8, h, x, mod, sgu, sd, nbp, bps):
    t, d = x.shape
    midx = _mod_index(nbp, bps)
    in_specs = [
        pl.BlockSpec((TOPK, TM, d // 2), lambda i: (0, i, 0)),
        pl.BlockSpec((TM, 128), lambda i: (i, 0)),
        pl.BlockSpec((TM, d // 2), lambda i: (i, 0)),
        pl.BlockSpec((TM, d), lambda i: (i, 0)),
        pl.BlockSpec((1, 6, d), lambda i: (midx(i), 0, 0)),
        _const_spec(sgu.shape), _const_spec(sd.shape),
    ]
    return pl.pallas_call(
        _combine_kernel, grid=(t // TM,), in_specs=in_specs,
        out_specs=pl.BlockSpec((TM, d), lambda i: (i, 0)),
        out_shape=jax.ShapeDtypeStruct((t, d), F32),
        compiler_params=_cparams(("parallel",)), name="moe_combine",
    )(yg, w8, h, x, mod, sgu, sd)


def _sparse_moe(x, h, gates_t, sel_t, cnt, mod, wgu, wd, sgu, sd, nbp, bps):
    t, d = x.shape
    ne = N_EXPERTS
    n_slots = t * TOPK + ne * TG
    counts = cnt[:, 0, :].astype(I32)
    per_expert = jnp.sum(counts, axis=0)
    padded = (per_expert + TG - 1) // TG * TG
    ends = jnp.cumsum(padded)
    starts = ends - padded
    block_off = jnp.cumsum(counts, axis=0) - counts
    base = (starts[None, :] + block_off).astype(F32)[:, :, None]
    n_valid = (ends[-1] // TG).astype(I32).reshape(1)
    tile_start = jnp.minimum(jnp.arange(n_slots // TG, dtype=I32), n_valid[0] - 1) * TG
    tile_expert = jnp.sum((ends[None, :] <= tile_start[:, None]).astype(I32), axis=1)

    pos8, w8 = _slots(sel_t, gates_t, base)
    pos_flat = pos8.reshape(-1)
    tok = jnp.tile(jnp.arange(t, dtype=I32), TOPK)
    src = jnp.zeros((n_slots,), I32).at[pos_flat].set(tok, unique_indices=True)
    xs = _sc_gather(h, src)
    ys = _expert_ffn(xs, tile_expert, n_valid, wgu, wd)
    yg = _sc_gather(ys, pos_flat).reshape(TOPK, t, d // 2)
    return _combine(yg, w8, h, x, mod, sgu, sd, nbp, bps)


def _head_rms(x, g_row, hm_ref, hmt_ref):
    sq_hi, sq_lo = _split_bf16(x * x)
    hm = hm_ref[...]
    ss = (jnp.dot(sq_hi, hm, preferred_element_type=F32)
          + jnp.dot(sq_lo, hm, preferred_element_type=F32))
    r = lax.rsqrt(ss * (1.0 / HEAD_DIM) + EPS)
    r_hi, r_lo = _split_bf16(r)
    hmt = hmt_ref[...]
    rb = (jnp.dot(r_hi, hmt, preferred_element_type=F32)
          + jnp.dot(r_lo, hmt, preferred_element_type=F32))
    return (x * rb) * g_row


def _qkv_kernel(emit_f32, x_ref, mod_ref, gmix_ref, w_ref, qg_ref, kg_ref, hm_ref, hmt_ref, *outs):
    d = x_ref.shape[1]
    sh1 = mod_ref[0, 0:1, :]
    sc1 = mod_ref[0, 1:2, :]
    h = _adaln(x_ref[...], gmix_ref[...], sh1, sc1).astype(BF16)
    qkv = jnp.dot(h, w_ref[...], preferred_element_type=F32)
    q = _head_rms(qkv[:, :d], qg_ref[...], hm_ref, hmt_ref)
    k = _head_rms(qkv[:, d:2 * d], kg_ref[...], hm_ref, hmt_ref)
    v = qkv[:, 2 * d:]
    outs[0][...] = q.astype(BF16)
    outs[1][...] = k.astype(BF16)
    outs[2][...] = v.astype(BF16)
    if emit_f32:
        outs[3][...] = k
        outs[4][...] = v


def _qkv(x, mod, g_mix, w_qkv, qg, kg, hm, hmt, blk0, nblk, midx, emit_f32):
    t, d = x.shape
    in_specs = [
        pl.BlockSpec((TM, d), lambda i: (i + blk0, 0)),
        pl.BlockSpec((1, 6, d), lambda i: (midx(i + blk0), 0, 0)),
        _const_spec(g_mix.shape), _const_spec(w_qkv.shape), _const_spec(qg.shape),
        _const_spec(kg.shape), _const_spec(hm.shape), _const_spec(hmt.shape),
    ]
    n_out = 5 if emit_f32 else 3
    out_specs = [pl.BlockSpec((TM, d), lambda i: (i, 0)) for _ in range(n_out)]
    out_shape = [jax.ShapeDtypeStruct((nblk * TM, d), BF16 if o < 3 else F32) for o in range(n_out)]
    return pl.pallas_call(
        functools.partial(_qkv_kernel, emit_f32),
        grid=(nblk,), in_specs=in_specs, out_specs=out_specs, out_shape=out_shape,
        compiler_params=_cparams(("parallel",)), name="qkv_f32" if emit_f32 else "qkv",
    )(x, mod, g_mix, w_qkv, qg, kg, hm, hmt)


def _head_masks():
    lane = lax.broadcasted_iota(I32, (1, 2 * HEAD_DIM), 1)
    return lane < HEAD_DIM


def _ctx_attn_kernel(q_ref, k_ref, v_ref, o_ref):
    q = q_ref[...]
    k = k_ref[...]
    v = v_ref[...]
    lo = _head_masks()
    outs = []
    for hh in range(2):
        msk = lo if hh == 0 else jnp.logical_not(lo)
        qm = jnp.where(msk, q, jnp.zeros_like(q)) * jnp.asarray(HEAD_DIM ** -0.5, BF16)
        s = lax.dot_general(qm, k, NT_DIMS, preferred_element_type=F32)
        m = jnp.max(s, axis=-1, keepdims=True)
        p = jnp.exp(s - m)
        l = jnp.sum(p, axis=-1, keepdims=True)
        o = jnp.dot(p.astype(BF16), v, preferred_element_type=F32)
        outs.append(o / l)
    o_ref[...] = jnp.where(lo, outs[0], outs[1]).astype(BF16)


def _ctx_attn(q, k, v, nb, s):
    t, d = q.shape
    hp = d // (2 * HEAD_DIM)
    spec = pl.BlockSpec((s, 2 * HEAD_DIM), lambda b, h: (b, h))
    return pl.pallas_call(
        _ctx_attn_kernel, grid=(nb, hp), in_specs=[spec, spec, spec], out_specs=spec,
        out_shape=jax.ShapeDtypeStruct((nb * s, d), BF16),
        compiler_params=_cparams(("parallel", "parallel")), name="context_attention",
    )(q, k, v)


NA_QROWS = 8
NA_KROWS = 16


def _na_kernel(rows, q_ref, k_ref, v_ref, kc_ref, vc_ref, bias_ref, o_ref):
    rb = pl.program_id(1)
    kr0 = jnp.clip(rb * NA_QROWS - WIN_H // 2, 0, rows - NA_KROWS)
    start = pl.multiple_of(kr0 * GRID_W, 256)
    nk = NA_KROWS * GRID_W
    q = q_ref[0]
    kw = k_ref[0, pl.ds(start, nk), :]
    vw = v_ref[0, pl.ds(start, nk), :]
    kc = kc_ref[0]
    vc = vc_ref[0]
    lo = _head_masks()
    outs = []
    for hh in range(2):
        msk = lo if hh == 0 else jnp.logical_not(lo)
        qm = jnp.where(msk, q, jnp.zeros_like(q)) * jnp.asarray(HEAD_DIM ** -0.5, BF16)
        s = lax.dot_general(qm, kw, NT_DIMS, preferred_element_type=F32) + bias_ref[0, hh]
        sc = lax.dot_general(qm, kc, NT_DIMS, preferred_element_type=F32)
        m = jnp.maximum(jnp.max(s, axis=-1, keepdims=True), jnp.max(sc, axis=-1, keepdims=True))
        p = jnp.exp(s - m)
        pc = jnp.exp(sc - m)
        l = jnp.sum(p, axis=-1, keepdims=True) + jnp.sum(pc, axis=-1, keepdims=True)
        o = (jnp.dot(p.astype(BF16), vw, preferred_element_type=F32)
             + jnp.dot(pc.astype(BF16), vc, preferred_element_type=F32))
        outs.append(o / l)
    o_ref[0] = jnp.where(lo, outs[0], outs[1]).astype(BF16)


def _na_row_classes(rows):
    nrb = rows // NA_QROWS
    out = []
    for rb in (0, 1, nrb - 1):
        r0 = rb * NA_QROWS
        kr0 = int(np.clip(r0 - WIN_H // 2, 0, rows - NA_KROWS))
        table = []
        for rl in range(NA_QROWS):
            r = r0 + rl
            sr = int(np.clip(r - WIN_H // 2, 0, rows - WIN_H))
            table.append([(kr0 + kl - r + WIN_H - 1) if sr <= kr0 + kl < sr + WIN_H else None
                          for kl in range(NA_KROWS)])
        out.append(table)
    return out


def _na_bias_kernel(rows, rpb_ref, o_ref):
    h = pl.program_id(0)
    n_dr, n_dc = 2 * WIN_H - 1, 2 * WIN_W - 1
    cq = lax.broadcasted_iota(I32, (GRID_W, 2 * GRID_W), 0)
    lane = lax.broadcasted_iota(I32, (GRID_W, 2 * GRID_W), 1)
    ck = lane & (GRID_W - 1)
    q_start = jnp.clip(cq - WIN_W // 2, 0, GRID_W - WIN_W)
    col_ok = (ck >= q_start) & (ck < q_start + WIN_W)
    dc = ck - cq + (WIN_W - 1)
    neg = jnp.full((GRID_W, 2 * GRID_W), NEG, F32)
    tiles = []
    for i in range(n_dr):
        t = jnp.zeros((GRID_W, 2 * GRID_W), F32)
        for jj in range(n_dc):
            t = jnp.where(dc == jj, rpb_ref[h * (n_dr * n_dc) + i * n_dc + jj], t)
        tiles.append(jnp.where(col_ok, t, neg))
    first_half = lane < GRID_W
    for c, table in enumerate(_na_row_classes(rows)):
        for rl in range(NA_QROWS):
            for m in range(NA_KROWS // 2):
                ia, ib = table[rl][2 * m], table[rl][2 * m + 1]
                ta = neg if ia is None else tiles[ia]
                tb = neg if ib is None else tiles[ib]
                blk = ta if ia == ib else jnp.where(first_half, ta, tb)
                o_ref[c, 0, rl * GRID_W:(rl + 1) * GRID_W, m * 2 * GRID_W:(m + 1) * 2 * GRID_W] = blk


def _na_bias(rpb, rows):
    nh = rpb.shape[0]
    nq, nk = NA_QROWS * GRID_W, NA_KROWS * GRID_W
    return pl.pallas_call(
        functools.partial(_na_bias_kernel, rows), grid=(nh,),
        in_specs=[pl.BlockSpec(memory_space=pltpu.SMEM)],
        out_specs=pl.BlockSpec((3, 1, nq, nk), lambda h: (0, h, 0, 0)),
        out_shape=jax.ShapeDtypeStruct((3, nh, nq, nk), F32),
        compiler_params=_cparams(("parallel",)), name="na_bias",
    )(rpb.reshape(-1))


def _na_attn(q, k, v, kc, vc, bias_tab, rows):
    nb, s, d = q.shape
    hp = d // (2 * HEAD_DIM)
    nrb = rows // NA_QROWS
    nq = NA_QROWS * GRID_W
    lc = kc.shape[1]

    def cls(r):
        return jnp.where(r == 0, 0, jnp.where(r == nrb - 1, 2, 1))

    in_specs = [
        pl.BlockSpec((1, nq, 2 * HEAD_DIM), lambda h, r, b: (b, r, h)),
        pl.BlockSpec((1, s, 2 * HEAD_DIM), lambda h, r, b: (b, 0, h)),
        pl.BlockSpec((1, s, 2 * HEAD_DIM), lambda h, r, b: (b, 0, h)),
        pl.BlockSpec((1, lc, 2 * HEAD_DIM), lambda h, r, b: (b, 0, h)),
        pl.BlockSpec((1, lc, 2 * HEAD_DIM), lambda h, r, b: (b, 0, h)),
        pl.BlockSpec((1, 2, nq, NA_KROWS * GRID_W), lambda h, r, b: (cls(r), h, 0, 0)),
    ]
    return pl.pallas_call(
        functools.partial(_na_kernel, rows),
        grid=(hp, nrb, nb), in_specs=in_specs,
        out_specs=pl.BlockSpec((1, nq, 2 * HEAD_DIM), lambda h, r, b: (b, r, h)),
        out_shape=jax.ShapeDtypeStruct((nb, s, d), BF16),
        compiler_params=_cparams(("parallel", "parallel", "parallel")),
        name="neighbourhood_attention",
    )(q, k, v, kc, vc, bias_tab)


def _oproj_kernel(a_ref, x_ref, mod_ref, wout_ref, gffn_ref, wr_hi_ref, wr_lo_ref, rb_ref,
                  x1_ref, h_ref, gates_ref, sel_ref, cnt_ref):
    y = jnp.dot(a_ref[...], wout_ref[...], preferred_element_type=F32)
    x1 = x_ref[...] + mod_ref[0, 2:3, :] * y
    x1_ref[...] = x1
    _ffn_pre(x1, mod_ref, gffn_ref, wr_hi_ref, wr_lo_ref, rb_ref,
             h_ref, gates_ref, sel_ref, cnt_ref)


def _oproj(attn, x, mod, w_out, g_ffn, wr_hi, wr_lo, rbias, nbp, bps):
    t, d = x.shape
    midx = _mod_index(nbp, bps)
    in_specs = [
        pl.BlockSpec((TM, d), lambda i: (i, 0)),
        pl.BlockSpec((TM, d), lambda i: (i, 0)),
        pl.BlockSpec((1, 6, d), lambda i: (midx(i), 0, 0)),
        _const_spec(w_out.shape), _const_spec(g_ffn.shape), _const_spec(wr_hi.shape),
        _const_spec(wr_lo.shape), _const_spec(rbias.shape),
    ]
    out_specs, out_shape = _pre_out_specs(t, d)
    return pl.pallas_call(
        _oproj_kernel, grid=(t // TM,), in_specs=in_specs, out_specs=out_specs,
        out_shape=out_shape, compiler_params=_cparams(("parallel",)), name="attn_out_proj",
    )(attn, x, mod, w_out, g_ffn, wr_hi, wr_lo, rbias)


def _router_weights(w_router, router_bias):
    d, ne = w_router.shape
    wt = jnp.pad(w_router.T, ((0, 128 - ne), (0, 0)))
    hi = wt.astype(BF16)
    lo = (wt - hi.astype(F32)).astype(BF16)
    return hi, lo, router_bias.reshape(ne, 1)


def _moe_weights(w_gate, w_up, w_down, s_gate, s_up, s_down):
    wgu = jnp.concatenate([w_gate, w_up], axis=-1).astype(BF16)
    sgu = jnp.concatenate([s_gate, s_up], axis=-1).astype(BF16)
    return wgu, w_down.astype(BF16), sgu, s_down.astype(BF16)


def kernel(x_prompt, x_sample, cache_k, cache_v, c, c_ctx, ada_w, ada_b, norm_mix, norm_ffn,
           pc_w_in, pc_pool_w, pc_pool_scale, pc_conv_w, pc_w_out,
           na_w_qkv, na_q_norm, na_k_norm, na_rpb, na_w_out,
           moe_router, moe_router_bias, moe_w_gate, moe_w_up, moe_w_down,
           moe_shared_gate, moe_shared_up, moe_shared_down):
    nb_p, s_p, d = x_prompt.shape
    nb_s, s_s, _ = x_sample.shape
    assert s_p == TM and s_s % TM == 0
    tp, ts = nb_p * s_p, nb_s * s_s
    nbp, bps = tp // TM, s_s // TM
    depth = ada_w.shape[0]
    nh = d // HEAD_DIM
    rows = s_s // GRID_W

    x = jnp.concatenate([x_prompt.reshape(tp, d), x_sample.reshape(ts, d)], axis=0)
    cond = jnp.concatenate([c_ctx[None], c], axis=0)
    cond = jnp.pad(cond, ((0, -cond.shape[0] % 8), (0, 0)))
    mods = _modulation(cond, ada_w, ada_b).reshape(depth, cond.shape[0], 6, d)

    new_k, new_v = [], []
    for layer in range(depth):
        i = layer // 2
        mod = mods[layer]
        g_mix = norm_mix[layer].reshape(1, d)
        g_ffn = norm_ffn[layer].reshape(1, d)
        wr_hi, wr_lo, rbias = _router_weights(moe_router[layer], moe_router_bias[layer])
        if layer % 2 == 0:
            x, h, gates_t, sel_t, cnt = _mixer(
                x, mod, g_mix, pc_w_in[i].astype(BF16), pc_pool_w[i].astype(BF16),
                pc_pool_scale[i].reshape(1, -1), pc_conv_w[i], pc_w_out[i].astype(BF16),
                g_ffn, wr_hi, wr_lo, rbias, nbp, bps, s_p, s_s)
        else:
            head_of = np.arange(d) // HEAD_DIM
            hm = jnp.asarray(head_of[:, None] == np.arange(128)[None, :], BF16)
            hmt = jnp.asarray(np.arange(128)[:, None] == head_of[None, :], BF16)
            qg = jnp.tile(na_q_norm[i], nh).reshape(1, d)
            kg = jnp.tile(na_k_norm[i], nh).reshape(1, d)
            w_qkv = na_w_qkv[i].astype(BF16)
            midx = _mod_index(nbp, bps)
            qp, kp, vp, kp32, vp32 = _qkv(x, mod, g_mix, w_qkv, qg, kg, hm, hmt, 0, nbp, midx, True)
            qs, ks, vs = _qkv(x, mod, g_mix, w_qkv, qg, kg, hm, hmt, nbp, ts // TM, midx, False)
            new_k.append(kp32.reshape(nb_p, s_p, nh, HEAD_DIM))
            new_v.append(vp32.reshape(nb_p, s_p, nh, HEAD_DIM))
            ap = _ctx_attn(qp, kp, vp, nb_p, s_p)
            bias_tab = _na_bias(na_rpb[i], rows)
            lc = cache_k.shape[2]
            kc = cache_k[:, i].reshape(nb_s, lc, d).astype(BF16)
            vc = cache_v[:, i].reshape(nb_s, lc, d).astype(BF16)
            a_s = _na_attn(qs.reshape(nb_s, s_s, d), ks.reshape(nb_s, s_s, d),
                           vs.reshape(nb_s, s_s, d), kc, vc, bias_tab, rows)
            attn = jnp.concatenate([ap, a_s.reshape(ts, d)], axis=0)
            x, h, gates_t, sel_t, cnt = _oproj(attn, x, mod, na_w_out[i].astype(BF16), g_ffn,
                                               wr_hi, wr_lo, rbias, nbp, bps)
        wgu, wd, sgu, sd = _moe_weights(moe_w_gate[layer], moe_w_up[layer], moe_w_down[layer],
                                        moe_shared_gate[layer], moe_shared_up[layer],
                                        moe_shared_down[layer])
        x = _sparse_moe(x, h, gates_t, sel_t, cnt, mod, wgu, wd, sgu, sd, nbp, bps)
    y_p = x[:tp].reshape(nb_p, s_p, d)
    y_s = x[tp:].reshape(nb_s, s_s, d)
    return (y_p, y_s, jnp.stack(new_k, axis=1), jnp.stack(new_v, axis=1))
```

```python
import functools

import numpy as np
import jax
import jax.numpy as jnp
from jax import lax
from jax.experimental import pallas as pl
from jax.experimental.pallas import tpu as pltpu
from jax.experimental.pallas import tpu_sc as plsc

F32 = jnp.float32
BF16 = jnp.bfloat16
I32 = jnp.int32
U32 = jnp.uint32

TM = 256
HALO = 8
POOL_WINDOWS = (2, 4, 8, 16)
N_EXPERTS = 64
N_GROUPS = 8
GROUP_SIZE = N_EXPERTS // N_GROUPS
TOPK_GROUPS = 4
TOPK = 8
ROUTED_SCALE = 2.5
EPS = 1e-6
GRID_W = 64
WIN_H = 8
WIN_W = 16
HEAD_DIM = 64
NEG = float(np.finfo(np.float32).min)
VMEM_LIMIT = 56 * 1024 * 1024
NT_DIMS = (((1,), (1,)), ((), ()))
TG = 512
SC_WINDOW = 64
SC_WORKERS = 32


def _cparams(sem):
    return pltpu.CompilerParams(dimension_semantics=sem, vmem_limit_bytes=VMEM_LIMIT)


def _silu(x):
    return x * jax.nn.sigmoid(x)


def _split_bf16(x):
    hi = x.astype(BF16)
    lo = (x - hi.astype(F32)).astype(BF16)
    return hi, lo


def _adaln(x, g, shift, scale):
    ms = jnp.mean(x * x, axis=-1, keepdims=True)
    return (x * lax.rsqrt(ms + EPS)) * g * (1.0 + scale) + shift


def _pack_pair(x):
    w = x.shape[1] // 2
    lo = lax.bitcast_convert_type(x[:, :w].astype(BF16).astype(F32), U32) >> 16
    hi = lax.bitcast_convert_type(x[:, w:].astype(BF16).astype(F32), U32)
    return lax.bitcast_convert_type(lo | hi, I32)


def _unpack_pair(p):
    u = lax.bitcast_convert_type(p, U32)
    lo = lax.bitcast_convert_type(u << 16, F32)
    hi = lax.bitcast_convert_type(u & jnp.uint32(0xFFFF0000), F32)
    return jnp.concatenate([lo, hi], axis=-1)


def _mod_kernel(cond_ref, w_ref, b_ref, o_ref):
    c = cond_ref[...]
    a = _silu(c)
    o_ref[0] = jnp.dot(a, w_ref[0], preferred_element_type=F32,
                       precision=lax.Precision.HIGHEST) + b_ref[0]


def _modulation(cond, ada_w, ada_b):
    depth, d, n = ada_w.shape
    rows = cond.shape[0]
    tn = 1536
    return pl.pallas_call(
        _mod_kernel,
        grid=(depth, n // tn),
        in_specs=[
            pl.BlockSpec((rows, d), lambda l, j: (0, 0)),
            pl.BlockSpec((1, d, tn), lambda l, j: (l, 0, j)),
            pl.BlockSpec((1, 1, tn), lambda l, j: (l, 0, j)),
        ],
        out_specs=pl.BlockSpec((1, rows, tn), lambda l, j: (l, 0, j)),
        out_shape=jax.ShapeDtypeStruct((depth, rows, n), F32),
        compiler_params=_cparams(("arbitrary", "arbitrary")),
        name="modulation",
    )(cond, ada_w, ada_b.reshape(depth, 1, n))


def _route(logits_t, bias_col):
    tm = logits_t.shape[1]
    scores = jax.nn.sigmoid(logits_t)
    biased = scores + bias_col
    sub = lax.broadcasted_iota(I32, (GROUP_SIZE, tm), 0).astype(F32)
    ninf = jnp.float32(-jnp.inf)
    groups, gscore = [], []
    for g in range(N_GROUPS):
        v = biased[g * GROUP_SIZE:(g + 1) * GROUP_SIZE]
        m1 = jnp.max(v, axis=0, keepdims=True)
        first = jnp.min(jnp.where(v == m1, sub, float(GROUP_SIZE)), axis=0, keepdims=True)
        m2 = jnp.max(jnp.where(sub == first, ninf, v), axis=0, keepdims=True)
        groups.append(v)
        gscore.append(m1 + m2)
    masked = []
    for g in range(N_GROUPS):
        rank = jnp.zeros((1, tm), I32)
        for g2 in range(N_GROUPS):
            if g2 == g:
                continue
            ahead = gscore[g2] > gscore[g]
            if g2 < g:
                ahead = ahead | (gscore[g2] == gscore[g])
            rank = rank + ahead.astype(I32)
        masked.append(jnp.where(rank < TOPK_GROUPS, groups[g], ninf))
    masked = jnp.concatenate(masked, axis=0)
    eidx = lax.broadcasted_iota(I32, (N_EXPERTS, tm), 0)
    rank = jnp.zeros((N_EXPERTS, tm), I32)
    for e2 in range(N_EXPERTS):
        row = masked[e2:e2 + 1]
        ahead = (row > masked) | ((row == masked) & (eidx > e2))
        rank = rank + ahead.astype(I32)
    sel = rank < TOPK
    w = jnp.where(sel, scores, 0.0)
    wsum = jnp.sum(w, axis=0, keepdims=True)
    return w / wsum * ROUTED_SCALE, sel


def _ffn_pre(x1, mod_ref, gffn_ref, wr_hi_ref, wr_lo_ref, rb_ref,
             h_ref, gates_ref, sel_ref, cnt_ref):
    sh2 = mod_ref[0, 3:4, :]
    sc2 = mod_ref[0, 4:5, :]
    h = _adaln(x1, gffn_ref[...], sh2, sc2)
    h_hi, h_lo = _split_bf16(h)
    h_ref[...] = _pack_pair(h_hi)
    wr_hi = wr_hi_ref[...]
    logits_t = (lax.dot_general(wr_hi, h_hi, NT_DIMS, preferred_element_type=F32)
                + lax.dot_general(wr_lo_ref[...], h_hi, NT_DIMS, preferred_element_type=F32)
                + lax.dot_general(wr_hi, h_lo, NT_DIMS, preferred_element_type=F32))
    gates_t, sel = _route(logits_t[:N_EXPERTS], rb_ref[...])
    gates_ref[...] = gates_t
    sel_b = sel.astype(F32).astype(BF16)
    sel_ref[...] = sel_b
    ones = jnp.ones((8, sel_b.shape[1]), BF16)
    cnt_ref[0] = lax.dot_general(ones, sel_b, NT_DIMS, preferred_element_type=F32)


def _pre_out_specs(t, d):
    specs = [
        pl.BlockSpec((TM, d), lambda i: (i, 0)),
        pl.BlockSpec((TM, d // 2), lambda i: (i, 0)),
        pl.BlockSpec((N_EXPERTS, TM), lambda i: (0, i)),
        pl.BlockSpec((N_EXPERTS, TM), lambda i: (0, i)),
        pl.BlockSpec((1, 8, N_EXPERTS), lambda i: (i, 0, 0)),
    ]
    shapes = [
        jax.ShapeDtypeStruct((t, d), F32),
        jax.ShapeDtypeStruct((t, d // 2), I32),
        jax.ShapeDtypeStruct((N_EXPERTS, t), F32),
        jax.ShapeDtypeStruct((N_EXPERTS, t), BF16),
        jax.ShapeDtypeStruct((t // TM, 8, N_EXPERTS), F32),
    ]
    return specs, shapes


def _mixer_kernel(nbp, bps, sp, ss,
                  xc_ref, xp_ref, xn_ref, mod_ref, gmix_ref, win_ref, pw_ref, ps_ref, cw_ref,
                  wout_ref, gffn_ref, wr_hi_ref, wr_lo_ref, rb_ref,
                  x1_ref, h_ref, gates_ref, sel_ref, cnt_ref):
    i = pl.program_id(0)
    is_p = i < nbp
    j = lax.rem(jnp.maximum(i - nbp, 0), bps)
    first = is_p | (j == 0)
    last = is_p | (j == bps - 1)
    base = jnp.where(is_p, 0, j * TM)
    slen = jnp.where(is_p, sp, ss)

    sh1 = mod_ref[0, 0:1, :]
    sc1 = mod_ref[0, 1:2, :]
    g1 = mod_ref[0, 2:3, :]
    xc = xc_ref[...]
    x_ext = jnp.concatenate([xp_ref[...], xc, xn_ref[...]], axis=0)
    h_ext = _adaln(x_ext, gmix_ref[...], sh1, sc1).astype(BF16)
    u = jnp.dot(h_ext, win_ref[...], preferred_element_type=F32)
    next_ = TM + 2 * HALO
    row = lax.broadcasted_iota(I32, (next_, 1), 0)
    keep = ((row >= HALO) | jnp.logical_not(first)) & ((row < HALO + TM) | jnp.logical_not(last))
    u = jnp.where(keep, u, 0.0)

    dm = u.shape[1] // 4
    ua = u[:, :dm]
    gate_b = u[HALO:HALO + TM, dm:2 * dm]
    z = u[:, 2 * dm:3 * dm] * u[:, 3 * dm:]

    def up(a, k):
        return pltpu.roll(a, next_ - k, 0)

    pos = base + lax.broadcasted_iota(I32, (TM, 1), 0)
    pg = dm // len(POOL_WINDOWS)
    ya = []
    for g, w in enumerate(POOL_WINDOWS):
        e = ua[:, g * pg:(g + 1) * pg]
        acc = e
        span = 1
        while span < w:
            acc = acc + up(acc, span)
            span *= 2
        off = HALO - w // 2
        wsum = (up(acc, off) if off else acc)[:TM]
        lo = jnp.maximum(pos - w // 2, 0)
        hi = jnp.minimum(pos + (w - w // 2 - 1), slen - 1)
        cnt = (hi - lo + 1).astype(F32)
        diff = wsum / cnt - e[HALO:HALO + TM]
        ya.append(jnp.dot(diff.astype(BF16), pw_ref[g], preferred_element_type=F32))
    y_a = jnp.concatenate(ya, axis=-1) * ps_ref[...]
    zc = (cw_ref[0:1, :] * up(z, HALO - 1)[:TM] + cw_ref[1:2, :] * z[HALO:HALO + TM]
          + cw_ref[2:3, :] * up(z, HALO + 1)[:TM])
    y_b = gate_b * zc
    ycat = jnp.concatenate([y_a, y_b], axis=-1).astype(BF16)
    y = jnp.dot(ycat, wout_ref[...], preferred_element_type=F32)
    x1 = xc + g1 * y
    x1_ref[...] = x1
    _ffn_pre(x1, mod_ref, gffn_ref, wr_hi_ref, wr_lo_ref, rb_ref,
             h_ref, gates_ref, sel_ref, cnt_ref)


def _mod_index(nbp, bps):
    def f(i):
        return jnp.where(i < nbp, 0, 1 + jnp.maximum(i - nbp, 0) // bps)
    return f


def _const_spec(shape):
    nd = len(shape)
    return pl.BlockSpec(shape, lambda i: (0,) * nd)


def _mixer(x, mod, g_mix, w_in, pool_w, pool_scale, conv_w, w_out, g_ffn, wr_hi, wr_lo, rbias,
           nbp, bps, sp, ss):
    t, d = x.shape
    nblk = t // TM
    midx = _mod_index(nbp, bps)
    hpb = TM // HALO
    nh = t // HALO
    in_specs = [
        pl.BlockSpec((TM, d), lambda i: (i, 0)),
        pl.BlockSpec((HALO, d), lambda i: (jnp.maximum(i * hpb - 1, 0), 0)),
        pl.BlockSpec((HALO, d), lambda i: (jnp.minimum((i + 1) * hpb, nh - 1), 0)),
        pl.BlockSpec((1, 6, d), lambda i: (midx(i), 0, 0)),
        _const_spec(g_mix.shape), _const_spec(w_in.shape), _const_spec(pool_w.shape),
        _const_spec(pool_scale.shape), _const_spec(conv_w.shape), _const_spec(w_out.shape),
        _const_spec(g_ffn.shape), _const_spec(wr_hi.shape), _const_spec(wr_lo.shape),
        _const_spec(rbias.shape),
    ]
    out_specs, out_shape = _pre_out_specs(t, d)
    return pl.pallas_call(
        functools.partial(_mixer_kernel, nbp, bps, sp, ss),
        grid=(nblk,), in_specs=in_specs, out_specs=out_specs, out_shape=out_shape,
        compiler_params=_cparams(("parallel",)), name="pool_conv_mixer",
    )(x, x, x, mod, g_mix, w_in, pool_w, pool_scale, conv_w, w_out, g_ffn, wr_hi, wr_lo, rbias)


def _slots_kernel(sel_ref, gates_ref, base_ref, pos_ref, w_ref):
    sel = sel_ref[...]
    tm = sel.shape[1]
    r = lax.broadcasted_iota(I32, (tm, tm), 0)
    c = lax.broadcasted_iota(I32, (tm, tm), 1)
    before = (r < c).astype(F32).astype(BF16)
    rank_tok = jnp.dot(sel, before, preferred_element_type=F32)
    er = lax.broadcasted_iota(I32, (N_EXPERTS, N_EXPERTS), 0)
    ec = lax.broadcasted_iota(I32, (N_EXPERTS, N_EXPERTS), 1)
    lower = (ec < er).astype(F32).astype(BF16)
    rank_exp = jnp.dot(lower, sel, preferred_element_type=F32)
    slot = base_ref[0] + rank_tok
    chosen = sel > 0
    gates = gates_ref[...]
    sub = lax.broadcasted_iota(I32, (TOPK, tm), 0)
    pos8 = jnp.zeros((TOPK, tm), F32)
    w8 = jnp.zeros((TOPK, tm), F32)
    for k in range(TOPK):
        mk = chosen & (rank_exp == float(k))
        pk = jnp.sum(jnp.where(mk, slot, 0.0), axis=0, keepdims=True)
        wk = jnp.sum(jnp.where(mk, gates, 0.0), axis=0, keepdims=True)
        pos8 = jnp.where(sub == k, pk, pos8)
        w8 = jnp.where(sub == k, wk, w8)
    pos_ref[...] = pos8.astype(I32)
    w8 = jnp.concatenate([w8, jnp.zeros((128 - TOPK, tm), F32)], axis=0)
    w_ref[...] = w8.T


def _slots(sel_t, gates_t, base):
    ne, t = sel_t.shape
    return pl.pallas_call(
        _slots_kernel, grid=(t // TM,),
        in_specs=[
            pl.BlockSpec((ne, TM), lambda i: (0, i)),
            pl.BlockSpec((ne, TM), lambda i: (0, i)),
            pl.BlockSpec((1, ne, 1), lambda i: (i, 0, 0)),
        ],
        out_specs=[pl.BlockSpec((TOPK, TM), lambda i: (0, i)),
                   pl.BlockSpec((TM, 128), lambda i: (i, 0))],
        out_shape=[jax.ShapeDtypeStruct((TOPK, t), I32), jax.ShapeDtypeStruct((t, 128), F32)],
        compiler_params=_cparams(("parallel",)), name="moe_slots",
    )(sel_t, gates_t, base)


def _sc_gather(table, idx):
    m = idx.shape[0]
    d = table.shape[1]
    assert m % (SC_WINDOW * SC_WORKERS) == 0
    mesh = plsc.VectorSubcoreMesh(core_axis_name="core", subcore_axis_name="subcore")

    @pl.kernel(out_type=jax.ShapeDtypeStruct((m, d), table.dtype), mesh=mesh)
    def gather_rows(x_hbm, i_hbm, o_hbm):
        def body(i_vmem, o_vmem):
            pltpu.sync_copy(x_hbm.at[i_vmem.at[0]], o_vmem)

        pltpu.emit_pipeline(
            body, grid=(m // SC_WINDOW,),
            in_specs=[pl.BlockSpec((1, SC_WINDOW), index_map=lambda i: (i, 0))],
            out_specs=[pl.BlockSpec((SC_WINDOW, d), index_map=lambda i: (i, 0))],
            core_axis_name=("core", "subcore"),
            dimension_semantics=(pltpu.PARALLEL,),
        )(i_hbm, o_hbm)

    return gather_rows(table, idx.reshape(m // SC_WINDOW, SC_WINDOW))


def _sc_scatter(rows, idx, n_out):
    nk, t = idx.shape
    d = rows.shape[1]
    assert t % (SC_WINDOW * SC_WORKERS) == 0
    mesh = plsc.VectorSubcoreMesh(core_axis_name="core", subcore_axis_name="subcore")

    @pl.kernel(out_type=jax.ShapeDtypeStruct((n_out, d), rows.dtype), mesh=mesh)
    def scatter_rows(x_hbm, i_hbm, o_hbm):
        def body(x_vmem, i_vmem):
            for k in range(nk):
                pltpu.sync_copy(x_vmem, o_hbm.at[i_vmem.at[k, 0]])

        pltpu.emit_pipeline(
            body, grid=(t // SC_WINDOW,),
            in_specs=[pl.BlockSpec((SC_WINDOW, d), index_map=lambda i: (i, 0)),
                      pl.BlockSpec((nk, 1, SC_WINDOW), index_map=lambda i: (0, i, 0))],
            out_specs=[],
            core_axis_name=("core", "subcore"),
            dimension_semantics=(pltpu.PARALLEL,),
        )(x_hbm, i_hbm)

    return scatter_rows(rows, idx.reshape(nk, t // SC_WINDOW, SC_WINDOW))


def _expert_ffn_kernel(te_ref, fresh_ref, nv_ref, x_ref, wg_ref, wu_ref, wd_ref, y_ref,
                       wg_s, wu_s, wd_s):
    j = pl.program_id(0)

    @pl.when(j < nv_ref[0])
    def _():
        @pl.when(fresh_ref[j] == 1)
        def _():
            wg_s[...] = wg_ref[0, 0].astype(BF16)
            wu_s[...] = wu_ref[0, 0].astype(BF16)
            wd_s[...] = wd_ref[0, 0].astype(BF16)

        x = _unpack_pair(x_ref[...]).astype(BF16)
        a = jnp.dot(x, wg_s[...], preferred_element_type=F32)
        b = jnp.dot(x, wu_s[...], preferred_element_type=F32)
        act = (_silu(a) * b).astype(BF16)
        y_ref[...] = _pack_pair(jnp.dot(act, wd_s[...], preferred_element_type=F32))


def _expert_ffn(xs, n_tiles, tile_expert, fresh, n_valid, layer, w_gate, w_up, w_down):
    dh = xs.shape[1]
    last = lambda j, te, fr, nv: jnp.minimum(j, nv[0] - 1)
    wspec = lambda w: pl.BlockSpec((1, 1) + w.shape[2:],
                                   lambda j, te, fr, nv: (layer, te[j], 0, 0))
    grid_spec = pltpu.PrefetchScalarGridSpec(
        num_scalar_prefetch=3, grid=(n_tiles,),
        in_specs=[
            pl.BlockSpec((TG, dh), lambda j, te, fr, nv: (last(j, te, fr, nv), 0)),
            wspec(w_gate), wspec(w_up), wspec(w_down),
        ],
        out_specs=pl.BlockSpec((TG, dh), lambda j, te, fr, nv: (last(j, te, fr, nv), 0)),
        scratch_shapes=[pltpu.VMEM(w_gate.shape[2:], BF16), pltpu.VMEM(w_up.shape[2:], BF16),
                        pltpu.VMEM(w_down.shape[2:], BF16)],
    )
    return pl.pallas_call(
        _expert_ffn_kernel, grid_spec=grid_spec,
        out_shape=jax.ShapeDtypeStruct((n_tiles * TG, dh), I32),
        compiler_params=_cparams(("arbitrary",)), name="expert_ffn",
    )(tile_expert, fresh, n_valid, xs, w_gate, w_up, w_down)


def _combine_kernel(yg_ref, w_ref, h_ref, x_ref, mod_ref, sgu_ref, sd_ref, o_ref):
    f = sd_ref.shape[0]
    h = _unpack_pair(h_ref[...]).astype(BF16)
    hs = jnp.dot(h, sgu_ref[...], preferred_element_type=F32)
    act = (_silu(hs[:, :f]) * hs[:, f:]).astype(BF16)
    acc = jnp.dot(act, sd_ref[...], preferred_element_type=F32)
    w = w_ref[...]
    for k in range(TOPK):
        acc = acc + w[:, k:k + 1] * _unpack_pair(yg_ref[k])
    o_ref[...] = x_ref[...] + mod_ref[0, 5:6, :] * acc


def _combine(yg, w8, h, x, mod, sgu, sd, nbp, bps):
    t, d = x.shape
    midx = _mod_index(nbp, bps)
    in_specs = [
        pl.BlockSpec((TOPK, TM, d // 2), lambda i: (0, i, 0)),
        pl.BlockSpec((TM, 128), lambda i: (i, 0)),
        pl.BlockSpec((TM, d // 2), lambda i: (i, 0)),
        pl.BlockSpec((TM, d), lambda i: (i, 0)),
        pl.BlockSpec((1, 6, d), lambda i: (midx(i), 0, 0)),
        _const_spec(sgu.shape), _const_spec(sd.shape),
    ]
    return pl.pallas_call(
        _combine_kernel, grid=(t // TM,), in_specs=in_specs,
        out_specs=pl.BlockSpec((TM, d), lambda i: (i, 0)),
        out_shape=jax.ShapeDtypeStruct((t, d), F32),
        compiler_params=_cparams(("parallel",)), name="moe_combine",
    )(yg, w8, h, x, mod, sgu, sd)


def _sparse_moe(x, h, gates_t, sel_t, cnt, mod, layer, w_gate, w_up, w_down, sgu, sd, nbp, bps):
    t, d = x.shape
    ne = N_EXPERTS
    n_pad = ne * TG
    n_tiles = (t * TOPK + n_pad) // TG
    n_slots = n_tiles * TG
    counts = cnt[:, 0, :].astype(I32)
    per_expert = jnp.sum(counts, axis=0)
    padded = (per_expert + TG - 1) // TG * TG
    ends = jnp.cumsum(padded)
    starts = ends - padded
    block_off = jnp.cumsum(counts, axis=0) - counts
    base = (starts[None, :] + block_off).astype(F32)[:, :, None]
    n_valid = (ends[-1] // TG).astype(I32).reshape(1)
    tile_start = jnp.minimum(jnp.arange(n_tiles, dtype=I32), n_valid[0] - 1) * TG
    tile_expert = jnp.sum((ends[None, :] <= tile_start[:, None]).astype(I32), axis=1)
    fresh = jnp.concatenate([jnp.ones((1,), I32),
                             (tile_expert[1:] != tile_expert[:-1]).astype(I32)])

    pos8, w8 = _slots(sel_t, gates_t, base)
    n_extra = -(-n_pad // t)
    cand = (starts + per_expert)[:, None] + jnp.arange(TG, dtype=I32)[None, :]
    spare = n_slots + jnp.arange(n_pad, dtype=I32)
    fill = jnp.where(cand < ends[:, None], cand, spare.reshape(ne, TG)).reshape(-1)
    rest = n_slots + jnp.arange(n_pad, n_extra * t, dtype=I32) % n_pad
    dest = jnp.concatenate([pos8, jnp.concatenate([fill, rest]).reshape(n_extra, t)], axis=0)
    xs = _sc_scatter(h, dest, n_slots + n_pad)
    ys = _expert_ffn(xs, n_tiles, tile_expert, fresh, n_valid, layer, w_gate, w_up, w_down)
    yg = _sc_gather(ys, pos8.reshape(-1)).reshape(TOPK, t, d // 2)
    return _combine(yg, w8, h, x, mod, sgu, sd, nbp, bps)


def _head_rms(x, g_row, hm_ref, hmt_ref):
    sq_hi, sq_lo = _split_bf16(x * x)
    hm = hm_ref[...]
    ss = (jnp.dot(sq_hi, hm, preferred_element_type=F32)
          + jnp.dot(sq_lo, hm, preferred_element_type=F32))
    r = lax.rsqrt(ss * (1.0 / HEAD_DIM) + EPS)
    r_hi, r_lo = _split_bf16(r)
    hmt = hmt_ref[...]
    rb = (jnp.dot(r_hi, hmt, preferred_element_type=F32)
          + jnp.dot(r_lo, hmt, preferred_element_type=F32))
    return (x * rb) * g_row


def _qkv_kernel(emit_f32, x_ref, mod_ref, gmix_ref, w_ref, qg_ref, kg_ref, hm_ref, hmt_ref, *outs):
    d = x_ref.shape[1]
    sh1 = mod_ref[0, 0:1, :]
    sc1 = mod_ref[0, 1:2, :]
    h = _adaln(x_ref[...], gmix_ref[...], sh1, sc1).astype(BF16)
    qkv = jnp.dot(h, w_ref[...], preferred_element_type=F32)
    q = _head_rms(qkv[:, :d], qg_ref[...], hm_ref, hmt_ref)
    k = _head_rms(qkv[:, d:2 * d], kg_ref[...], hm_ref, hmt_ref)
    v = qkv[:, 2 * d:]
    outs[0][...] = q.astype(BF16)
    outs[1][...] = k.astype(BF16)
    outs[2][...] = v.astype(BF16)
    if emit_f32:
        outs[3][...] = k
        outs[4][...] = v


def _qkv(x, mod, g_mix, w_qkv, qg, kg, hm, hmt, blk0, nblk, midx, emit_f32):
    t, d = x.shape
    in_specs = [
        pl.BlockSpec((TM, d), lambda i: (i + blk0, 0)),
        pl.BlockSpec((1, 6, d), lambda i: (midx(i + blk0), 0, 0)),
        _const_spec(g_mix.shape), _const_spec(w_qkv.shape), _const_spec(qg.shape),
        _const_spec(kg.shape), _const_spec(hm.shape), _const_spec(hmt.shape),
    ]
    n_out = 5 if emit_f32 else 3
    out_specs = [pl.BlockSpec((TM, d), lambda i: (i, 0)) for _ in range(n_out)]
    out_shape = [jax.ShapeDtypeStruct((nblk * TM, d), BF16 if o < 3 else F32) for o in range(n_out)]
    return pl.pallas_call(
        functools.partial(_qkv_kernel, emit_f32),
        grid=(nblk,), in_specs=in_specs, out_specs=out_specs, out_shape=out_shape,
        compiler_params=_cparams(("parallel",)), name="qkv_f32" if emit_f32 else "qkv",
    )(x, mod, g_mix, w_qkv, qg, kg, hm, hmt)


def _head_masks():
    lane = lax.broadcasted_iota(I32, (1, 2 * HEAD_DIM), 1)
    return lane < HEAD_DIM


def _ctx_attn_kernel(q_ref, k_ref, v_ref, o_ref):
    q = q_ref[...]
    k = k_ref[...]
    v = v_ref[...]
    lo = _head_masks()
    outs = []
    for hh in range(2):
        msk = lo if hh == 0 else jnp.logical_not(lo)
        qm = jnp.where(msk, q, jnp.zeros_like(q)) * jnp.asarray(HEAD_DIM ** -0.5, BF16)
        s = lax.dot_general(qm, k, NT_DIMS, preferred_element_type=F32)
        m = jnp.max(s, axis=-1, keepdims=True)
        p = jnp.exp(s - m)
        l = jnp.sum(p, axis=-1, keepdims=True)
        o = jnp.dot(p.astype(BF16), v, preferred_element_type=F32)
        outs.append(o / l)
    o_ref[...] = jnp.where(lo, outs[0], outs[1]).astype(BF16)


def _ctx_attn(q, k, v, nb, s):
    t, d = q.shape
    hp = d // (2 * HEAD_DIM)
    spec = pl.BlockSpec((s, 2 * HEAD_DIM), lambda b, h: (b, h))
    return pl.pallas_call(
        _ctx_attn_kernel, grid=(nb, hp), in_specs=[spec, spec, spec], out_specs=spec,
        out_shape=jax.ShapeDtypeStruct((nb * s, d), BF16),
        compiler_params=_cparams(("parallel", "parallel")), name="context_attention",
    )(q, k, v)


NA_QROWS = 8
NA_KROWS = 16


def _na_kernel(rows, q_ref, k_ref, v_ref, kc_ref, vc_ref, bias_ref, o_ref):
    rb = pl.program_id(1)
    kr0 = jnp.clip(rb * NA_QROWS - WIN_H // 2, 0, rows - NA_KROWS)
    start = pl.multiple_of(kr0 * GRID_W, 256)
    nk = NA_KROWS * GRID_W
    q = q_ref[0]
    kw = k_ref[0, pl.ds(start, nk), :]
    vw = v_ref[0, pl.ds(start, nk), :]
    kc = kc_ref[0]
    vc = vc_ref[0]
    lo = _head_masks()
    outs = []
    for hh in range(2):
        msk = lo if hh == 0 else jnp.logical_not(lo)
        qm = jnp.where(msk, q, jnp.zeros_like(q)) * jnp.asarray(HEAD_DIM ** -0.5, BF16)
        s = lax.dot_general(qm, kw, NT_DIMS, preferred_element_type=F32) + bias_ref[0, hh]
        sc = lax.dot_general(qm, kc, NT_DIMS, preferred_element_type=F32)
        m = jnp.maximum(jnp.max(s, axis=-1, keepdims=True), jnp.max(sc, axis=-1, keepdims=True))
        p = jnp.exp(s - m)
        pc = jnp.exp(sc - m)
        l = jnp.sum(p, axis=-1, keepdims=True) + jnp.sum(pc, axis=-1, keepdims=True)
        o = (jnp.dot(p.astype(BF16), vw, preferred_element_type=F32)
             + jnp.dot(pc.astype(BF16), vc, preferred_element_type=F32))
        outs.append(o / l)
    o_ref[0] = jnp.where(lo, outs[0], outs[1]).astype(BF16)


def _na_row_classes(rows):
    nrb = rows // NA_QROWS
    out = []
    for rb in (0, 1, nrb - 1):
        r0 = rb * NA_QROWS
        kr0 = int(np.clip(r0 - WIN_H // 2, 0, rows - NA_KROWS))
        table = []
        for rl in range(NA_QROWS):
            r = r0 + rl
            sr = int(np.clip(r - WIN_H // 2, 0, rows - WIN_H))
            table.append([(kr0 + kl - r + WIN_H - 1) if sr <= kr0 + kl < sr + WIN_H else None
                          for kl in range(NA_KROWS)])
        out.append(table)
    return out


def _na_bias_kernel(rows, rpb_ref, o_ref):
    h = pl.program_id(0)
    n_dr, n_dc = 2 * WIN_H - 1, 2 * WIN_W - 1
    cq = lax.broadcasted_iota(I32, (GRID_W, 2 * GRID_W), 0)
    lane = lax.broadcasted_iota(I32, (GRID_W, 2 * GRID_W), 1)
    ck = lane & (GRID_W - 1)
    q_start = jnp.clip(cq - WIN_W // 2, 0, GRID_W - WIN_W)
    col_ok = (ck >= q_start) & (ck < q_start + WIN_W)
    dc = ck - cq + (WIN_W - 1)
    neg = jnp.full((GRID_W, 2 * GRID_W), NEG, F32)
    tiles = []
    for i in range(n_dr):
        t = jnp.zeros((GRID_W, 2 * GRID_W), F32)
        for jj in range(n_dc):
            t = jnp.where(dc == jj, rpb_ref[h * (n_dr * n_dc) + i * n_dc + jj], t)
        tiles.append(jnp.where(col_ok, t, neg))
    first_half = lane < GRID_W
    for c, table in enumerate(_na_row_classes(rows)):
        for rl in range(NA_QROWS):
            for m in range(NA_KROWS // 2):
                ia, ib = table[rl][2 * m], table[rl][2 * m + 1]
                ta = neg if ia is None else tiles[ia]
                tb = neg if ib is None else tiles[ib]
                blk = ta if ia == ib else jnp.where(first_half, ta, tb)
                o_ref[c, 0, rl * GRID_W:(rl + 1) * GRID_W, m * 2 * GRID_W:(m + 1) * 2 * GRID_W] = blk


def _na_bias(rpb, rows):
    nh = rpb.shape[0]
    nq, nk = NA_QROWS * GRID_W, NA_KROWS * GRID_W
    return pl.pallas_call(
        functools.partial(_na_bias_kernel, rows), grid=(nh,),
        in_specs=[pl.BlockSpec(memory_space=pltpu.SMEM)],
        out_specs=pl.BlockSpec((3, 1, nq, nk), lambda h: (0, h, 0, 0)),
        out_shape=jax.ShapeDtypeStruct((3, nh, nq, nk), F32),
        compiler_params=_cparams(("parallel",)), name="na_bias",
    )(rpb.reshape(-1))


def _na_attn(q, k, v, kc, vc, bias_tab, rows):
    nb, s, d = q.shape
    hp = d // (2 * HEAD_DIM)
    nrb = rows // NA_QROWS
    nq = NA_QROWS * GRID_W
    lc = kc.shape[1]

    def cls(r):
        return jnp.where(r == 0, 0, jnp.where(r == nrb - 1, 2, 1))

    in_specs = [
        pl.BlockSpec((1, nq, 2 * HEAD_DIM), lambda h, r, b: (b, r, h)),
        pl.BlockSpec((1, s, 2 * HEAD_DIM), lambda h, r, b: (b, 0, h)),
        pl.BlockSpec((1, s, 2 * HEAD_DIM), lambda h, r, b: (b, 0, h)),
        pl.BlockSpec((1, lc, 2 * HEAD_DIM), lambda h, r, b: (b, 0, h)),
        pl.BlockSpec((1, lc, 2 * HEAD_DIM), lambda h, r, b: (b, 0, h)),
        pl.BlockSpec((1, 2, nq, NA_KROWS * GRID_W), lambda h, r, b: (cls(r), h, 0, 0)),
    ]
    return pl.pallas_call(
        functools.partial(_na_kernel, rows),
        grid=(hp, nrb, nb), in_specs=in_specs,
        out_specs=pl.BlockSpec((1, nq, 2 * HEAD_DIM), lambda h, r, b: (b, r, h)),
        out_shape=jax.ShapeDtypeStruct((nb, s, d), BF16),
        compiler_params=_cparams(("parallel", "parallel", "parallel")),
        name="neighbourhood_attention",
    )(q, k, v, kc, vc, bias_tab)


def _oproj_kernel(a_ref, x_ref, mod_ref, wout_ref, gffn_ref, wr_hi_ref, wr_lo_ref, rb_ref,
                  x1_ref, h_ref, gates_ref, sel_ref, cnt_ref):
    y = jnp.dot(a_ref[...], wout_ref[...], preferred_element_type=F32)
    x1 = x_ref[...] + mod_ref[0, 2:3, :] * y
    x1_ref[...] = x1
    _ffn_pre(x1, mod_ref, gffn_ref, wr_hi_ref, wr_lo_ref, rb_ref,
             h_ref, gates_ref, sel_ref, cnt_ref)


def _oproj(attn, x, mod, w_out, g_ffn, wr_hi, wr_lo, rbias, nbp, bps):
    t, d = x.shape
    midx = _mod_index(nbp, bps)
    in_specs = [
        pl.BlockSpec((TM, d), lambda i: (i, 0)),
        pl.BlockSpec((TM, d), lambda i: (i, 0)),
        pl.BlockSpec((1, 6, d), lambda i: (midx(i), 0, 0)),
        _const_spec(w_out.shape), _const_spec(g_ffn.shape), _const_spec(wr_hi.shape),
        _const_spec(wr_lo.shape), _const_spec(rbias.shape),
    ]
    out_specs, out_shape = _pre_out_specs(t, d)
    return pl.pallas_call(
        _oproj_kernel, grid=(t // TM,), in_specs=in_specs, out_specs=out_specs,
        out_shape=out_shape, compiler_params=_cparams(("parallel",)), name="attn_out_proj",
    )(attn, x, mod, w_out, g_ffn, wr_hi, wr_lo, rbias)


def _router_weights(w_router, router_bias):
    d, ne = w_router.shape
    wt = jnp.pad(w_router.T, ((0, 128 - ne), (0, 0)))
    hi = wt.astype(BF16)
    lo = (wt - hi.astype(F32)).astype(BF16)
    return hi, lo, router_bias.reshape(ne, 1)


def _shared_weights(s_gate, s_up, s_down):
    return jnp.concatenate([s_gate, s_up], axis=-1).astype(BF16), s_down.astype(BF16)


def kernel(x_prompt, x_sample, cache_k, cache_v, c, c_ctx, ada_w, ada_b, norm_mix, norm_ffn,
           pc_w_in, pc_pool_w, pc_pool_scale, pc_conv_w, pc_w_out,
           na_w_qkv, na_q_norm, na_k_norm, na_rpb, na_w_out,
           moe_router, moe_router_bias, moe_w_gate, moe_w_up, moe_w_down,
           moe_shared_gate, moe_shared_up, moe_shared_down):
    nb_p, s_p, d = x_prompt.shape
    nb_s, s_s, _ = x_sample.shape
    assert s_p == TM and s_s % TM == 0
    tp, ts = nb_p * s_p, nb_s * s_s
    nbp, bps = tp // TM, s_s // TM
    depth = ada_w.shape[0]
    nh = d // HEAD_DIM
    rows = s_s // GRID_W

    x = jnp.concatenate([x_prompt.reshape(tp, d), x_sample.reshape(ts, d)], axis=0)
    cond = jnp.concatenate([c_ctx[None], c], axis=0)
    cond = jnp.pad(cond, ((0, -cond.shape[0] % 8), (0, 0)))
    mods = _modulation(cond, ada_w, ada_b).reshape(depth, cond.shape[0], 6, d)

    new_k, new_v = [], []
    for layer in range(depth):
        i = layer // 2
        mod = mods[layer]
        g_mix = norm_mix[layer].reshape(1, d)
        g_ffn = norm_ffn[layer].reshape(1, d)
        wr_hi, wr_lo, rbias = _router_weights(moe_router[layer], moe_router_bias[layer])
        if layer % 2 == 0:
            x, h, gates_t, sel_t, cnt = _mixer(
                x, mod, g_mix, pc_w_in[i].astype(BF16), pc_pool_w[i].astype(BF16),
                pc_pool_scale[i].reshape(1, -1), pc_conv_w[i], pc_w_out[i].astype(BF16),
                g_ffn, wr_hi, wr_lo, rbias, nbp, bps, s_p, s_s)
        else:
            head_of = np.arange(d) // HEAD_DIM
            hm = jnp.asarray(head_of[:, None] == np.arange(128)[None, :], BF16)
            hmt = jnp.asarray(np.arange(128)[:, None] == head_of[None, :], BF16)
            qg = jnp.tile(na_q_norm[i], nh).reshape(1, d)
            kg = jnp.tile(na_k_norm[i], nh).reshape(1, d)
            w_qkv = na_w_qkv[i].astype(BF16)
            midx = _mod_index(nbp, bps)
            qp, kp, vp, kp32, vp32 = _qkv(x, mod, g_mix, w_qkv, qg, kg, hm, hmt, 0, nbp, midx, True)
            qs, ks, vs = _qkv(x, mod, g_mix, w_qkv, qg, kg, hm, hmt, nbp, ts // TM, midx, False)
            new_k.append(kp32.reshape(nb_p, s_p, nh, HEAD_DIM))
            new_v.append(vp32.reshape(nb_p, s_p, nh, HEAD_DIM))
            ap = _ctx_attn(qp, kp, vp, nb_p, s_p)
            bias_tab = _na_bias(na_rpb[i], rows)
            lc = cache_k.shape[2]
            kc = cache_k[:, i].reshape(nb_s, lc, d).astype(BF16)
            vc = cache_v[:, i].reshape(nb_s, lc, d).astype(BF16)
            a_s = _na_attn(qs.reshape(nb_s, s_s, d), ks.reshape(nb_s, s_s, d),
                           vs.reshape(nb_s, s_s, d), kc, vc, bias_tab, rows)
            attn = jnp.concatenate([ap, a_s.reshape(ts, d)], axis=0)
            x, h, gates_t, sel_t, cnt = _oproj(attn, x, mod, na_w_out[i].astype(BF16), g_ffn,
                                               wr_hi, wr_lo, rbias, nbp, bps)
        sgu, sd = _shared_weights(moe_shared_gate[layer], moe_shared_up[layer],
                                  moe_shared_down[layer])
        x = _sparse_moe(x, h, gates_t, sel_t, cnt, mod, layer, moe_w_gate, moe_w_up, moe_w_down,
                        sgu, sd, nbp, bps)
    y_p = x[:tp].reshape(nb_p, s_p, d)
    y_s = x[tp:].reshape(nb_s, s_s, d)
    return (y_p, y_s, jnp.stack(new_k, axis=1), jnp.stack(new_v, axis=1))
```

```python
import functools

import numpy as np
import jax
import jax.numpy as jnp
from jax import lax
from jax.experimental import pallas as pl
from jax.experimental.pallas import tpu as pltpu
from jax.experimental.pallas import tpu_sc as plsc

F32 = jnp.float32
BF16 = jnp.bfloat16
I32 = jnp.int32
U32 = jnp.uint32

TM = 256
HALO = 8
POOL_WINDOWS = (2, 4, 8, 16)
N_EXPERTS = 64
N_GROUPS = 8
GROUP_SIZE = N_EXPERTS // N_GROUPS
TOPK_GROUPS = 4
TOPK = 8
ROUTED_SCALE = 2.5
EPS = 1e-6
GRID_W = 64
WIN_H = 8
WIN_W = 16
HEAD_DIM = 64
NEG = float(np.finfo(np.float32).min)
VMEM_LIMIT = 56 * 1024 * 1024
NT_DIMS = (((1,), (1,)), ((), ()))
TG = 1024
SC_WINDOW = 64
SC_WORKERS = 32


def _cparams(sem):
    return pltpu.CompilerParams(dimension_semantics=sem, vmem_limit_bytes=VMEM_LIMIT)


def _silu(x):
    return x * jax.nn.sigmoid(x)


def _split_bf16(x):
    hi = x.astype(BF16)
    lo = (x - hi.astype(F32)).astype(BF16)
    return hi, lo


def _adaln(x, g, shift, scale):
    ms = jnp.mean(x * x, axis=-1, keepdims=True)
    return (x * lax.rsqrt(ms + EPS)) * g * (1.0 + scale) + shift


def _pack_pair(x):
    w = x.shape[1] // 2
    lo = lax.bitcast_convert_type(x[:, :w].astype(BF16).astype(F32), U32) >> 16
    hi = lax.bitcast_convert_type(x[:, w:].astype(BF16).astype(F32), U32)
    return lax.bitcast_convert_type(lo | hi, I32)


def _unpack_pair(p):
    u = lax.bitcast_convert_type(p, U32)
    lo = lax.bitcast_convert_type(u << 16, F32)
    hi = lax.bitcast_convert_type(u & jnp.uint32(0xFFFF0000), F32)
    return jnp.concatenate([lo, hi], axis=-1)


def _mod_kernel(cond_ref, w_ref, b_ref, o_ref):
    c = cond_ref[...]
    a = _silu(c)
    o_ref[0] = jnp.dot(a, w_ref[0], preferred_element_type=F32,
                       precision=lax.Precision.HIGHEST) + b_ref[0]


def _modulation(cond, ada_w, ada_b):
    depth, d, n = ada_w.shape
    rows = cond.shape[0]
    tn = 1536
    return pl.pallas_call(
        _mod_kernel,
        grid=(depth, n // tn),
        in_specs=[
            pl.BlockSpec((rows, d), lambda l, j: (0, 0)),
            pl.BlockSpec((1, d, tn), lambda l, j: (l, 0, j)),
            pl.BlockSpec((1, 1, tn), lambda l, j: (l, 0, j)),
        ],
        out_specs=pl.BlockSpec((1, rows, tn), lambda l, j: (l, 0, j)),
        out_shape=jax.ShapeDtypeStruct((depth, rows, n), F32),
        compiler_params=_cparams(("arbitrary", "arbitrary")),
        name="modulation",
    )(cond, ada_w, ada_b.reshape(depth, 1, n))


def _route(logits_t, bias_col):
    tm = logits_t.shape[1]
    scores = jax.nn.sigmoid(logits_t)
    biased = scores + bias_col
    sub = lax.broadcasted_iota(I32, (GROUP_SIZE, tm), 0).astype(F32)
    ninf = jnp.float32(-jnp.inf)
    groups, gscore = [], []
    for g in range(N_GROUPS):
        v = biased[g * GROUP_SIZE:(g + 1) * GROUP_SIZE]
        m1 = jnp.max(v, axis=0, keepdims=True)
        first = jnp.min(jnp.where(v == m1, sub, float(GROUP_SIZE)), axis=0, keepdims=True)
        m2 = jnp.max(jnp.where(sub == first, ninf, v), axis=0, keepdims=True)
        groups.append(v)
        gscore.append(m1 + m2)
    masked = []
    for g in range(N_GROUPS):
        rank = jnp.zeros((1, tm), I32)
        for g2 in range(N_GROUPS):
            if g2 == g:
                continue
            ahead = gscore[g2] > gscore[g]
            if g2 < g:
                ahead = ahead | (gscore[g2] == gscore[g])
            rank = rank + ahead.astype(I32)
        masked.append(jnp.where(rank < TOPK_GROUPS, groups[g], ninf))
    masked = jnp.concatenate(masked, axis=0)
    eidx = lax.broadcasted_iota(I32, (N_EXPERTS, tm), 0).astype(F32)
    sel = jnp.zeros((N_EXPERTS, tm), jnp.bool_)
    for _ in range(TOPK):
        best = jnp.max(masked, axis=0, keepdims=True)
        first = jnp.min(jnp.where(masked == best, eidx, float(N_EXPERTS)), axis=0, keepdims=True)
        hit = eidx == first
        sel = sel | hit
        masked = jnp.where(hit, ninf, masked)
    w = jnp.where(sel, scores, 0.0)
    wsum = jnp.sum(w, axis=0, keepdims=True)
    return w / wsum * ROUTED_SCALE, sel


def _ffn_pre(x1, mod_ref, gffn_ref, wr_hi_ref, wr_lo_ref, rb_ref,
             h_ref, gates_ref, sel_ref, cnt_ref):
    sh2 = mod_ref[0, 3:4, :]
    sc2 = mod_ref[0, 4:5, :]
    h = _adaln(x1, gffn_ref[...], sh2, sc2)
    h_hi, h_lo = _split_bf16(h)
    h_ref[...] = _pack_pair(h_hi)
    wr_hi = wr_hi_ref[...]
    logits = (jnp.dot(h_hi, wr_hi, preferred_element_type=F32)
              + jnp.dot(h_hi, wr_lo_ref[...], preferred_element_type=F32)
              + jnp.dot(h_lo, wr_hi, preferred_element_type=F32))
    gates_t, sel = _route(logits.T[:N_EXPERTS], rb_ref[...])
    gates_ref[...] = gates_t
    sel_b = sel.astype(F32).astype(BF16)
    sel_ref[...] = sel_b
    ones = jnp.ones((8, sel_b.shape[1]), BF16)
    cnt_ref[0] = lax.dot_general(ones, sel_b, NT_DIMS, preferred_element_type=F32)


def _pre_out_specs(t, d):
    specs = [
        pl.BlockSpec((TM, d), lambda i: (i, 0)),
        pl.BlockSpec((TM, d // 2), lambda i: (i, 0)),
        pl.BlockSpec((N_EXPERTS, TM), lambda i: (0, i)),
        pl.BlockSpec((N_EXPERTS, TM), lambda i: (0, i)),
        pl.BlockSpec((1, 8, N_EXPERTS), lambda i: (i, 0, 0)),
    ]
    shapes = [
        jax.ShapeDtypeStruct((t, d), F32),
        jax.ShapeDtypeStruct((t, d // 2), I32),
        jax.ShapeDtypeStruct((N_EXPERTS, t), F32),
        jax.ShapeDtypeStruct((N_EXPERTS, t), BF16),
        jax.ShapeDtypeStruct((t // TM, 8, N_EXPERTS), F32),
    ]
    return specs, shapes


def _mixer_kernel(nbp, bps, sp, ss,
                  xc_ref, xp_ref, xn_ref, mod_ref, gmix_ref, win_ref, pw_ref, ps_ref, cw_ref,
                  wout_ref, gffn_ref, wr_hi_ref, wr_lo_ref, rb_ref,
                  x1_ref, h_ref, gates_ref, sel_ref, cnt_ref):
    i = pl.program_id(0)
    is_p = i < nbp
    j = lax.rem(jnp.maximum(i - nbp, 0), bps)
    first = is_p | (j == 0)
    last = is_p | (j == bps - 1)
    base = jnp.where(is_p, 0, j * TM)
    slen = jnp.where(is_p, sp, ss)

    sh1 = mod_ref[0, 0:1, :]
    sc1 = mod_ref[0, 1:2, :]
    g1 = mod_ref[0, 2:3, :]
    xc = xc_ref[...]
    x_ext = jnp.concatenate([xp_ref[...], xc, xn_ref[...]], axis=0)
    h_ext = _adaln(x_ext, gmix_ref[...], sh1, sc1).astype(BF16)
    u = jnp.dot(h_ext, win_ref[...], preferred_element_type=F32)
    next_ = TM + 2 * HALO
    row = lax.broadcasted_iota(I32, (next_, 1), 0)
    keep = ((row >= HALO) | jnp.logical_not(first)) & ((row < HALO + TM) | jnp.logical_not(last))
    u = jnp.where(keep, u, 0.0)

    dm = u.shape[1] // 4
    ua = u[:, :dm]
    gate_b = u[HALO:HALO + TM, dm:2 * dm]
    z = u[:, 2 * dm:3 * dm] * u[:, 3 * dm:]

    def up(a, k):
        return pltpu.roll(a, next_ - k, 0)

    pos = base + lax.broadcasted_iota(I32, (TM, 1), 0)
    pg = dm // len(POOL_WINDOWS)
    ya = []
    for g, w in enumerate(POOL_WINDOWS):
        e = ua[:, g * pg:(g + 1) * pg]
        acc = e
        span = 1
        while span < w:
            acc = acc + up(acc, span)
            span *= 2
        off = HALO - w // 2
        wsum = (up(acc, off) if off else acc)[:TM]
        lo = jnp.maximum(pos - w // 2, 0)
        hi = jnp.minimum(pos + (w - w // 2 - 1), slen - 1)
        cnt = (hi - lo + 1).astype(F32)
        diff = wsum / cnt - e[HALO:HALO + TM]
        ya.append(jnp.dot(diff.astype(BF16), pw_ref[g], preferred_element_type=F32))
    y_a = jnp.concatenate(ya, axis=-1) * ps_ref[...]
    zc = (cw_ref[0:1, :] * up(z, HALO - 1)[:TM] + cw_ref[1:2, :] * z[HALO:HALO + TM]
          + cw_ref[2:3, :] * up(z, HALO + 1)[:TM])
    y_b = gate_b * zc
    ycat = jnp.concatenate([y_a, y_b], axis=-1).astype(BF16)
    y = jnp.dot(ycat, wout_ref[...], preferred_element_type=F32)
    x1 = xc + g1 * y
    x1_ref[...] = x1
    _ffn_pre(x1, mod_ref, gffn_ref, wr_hi_ref, wr_lo_ref, rb_ref,
             h_ref, gates_ref, sel_ref, cnt_ref)


def _mod_index(nbp, bps):
    def f(i):
        return jnp.where(i < nbp, 0, 1 + jnp.maximum(i - nbp, 0) // bps)
    return f


def _const_spec(shape):
    nd = len(shape)
    return pl.BlockSpec(shape, lambda i: (0,) * nd)


def _mixer(x, mod, g_mix, w_in, pool_w, pool_scale, conv_w, w_out, g_ffn, wr_hi, wr_lo, rbias,
           nbp, bps, sp, ss):
    t, d = x.shape
    nblk = t // TM
    midx = _mod_index(nbp, bps)
    hpb = TM // HALO
    nh = t // HALO
    in_specs = [
        pl.BlockSpec((TM, d), lambda i: (i, 0)),
        pl.BlockSpec((HALO, d), lambda i: (jnp.maximum(i * hpb - 1, 0), 0)),
        pl.BlockSpec((HALO, d), lambda i: (jnp.minimum((i + 1) * hpb, nh - 1), 0)),
        pl.BlockSpec((1, 6, d), lambda i: (midx(i), 0, 0)),
        _const_spec(g_mix.shape), _const_spec(w_in.shape), _const_spec(pool_w.shape),
        _const_spec(pool_scale.shape), _const_spec(conv_w.shape), _const_spec(w_out.shape),
        _const_spec(g_ffn.shape), _const_spec(wr_hi.shape), _const_spec(wr_lo.shape),
        _const_spec(rbias.shape),
    ]
    out_specs, out_shape = _pre_out_specs(t, d)
    return pl.pallas_call(
        functools.partial(_mixer_kernel, nbp, bps, sp, ss),
        grid=(nblk,), in_specs=in_specs, out_specs=out_specs, out_shape=out_shape,
        compiler_params=_cparams(("parallel",)), name="pool_conv_mixer",
    )(x, x, x, mod, g_mix, w_in, pool_w, pool_scale, conv_w, w_out, g_ffn, wr_hi, wr_lo, rbias)


def _slots_kernel(sel_ref, gates_ref, base_ref, pos_ref, w_ref):
    sel = sel_ref[...]
    tm = sel.shape[1]
    r = lax.broadcasted_iota(I32, (tm, tm), 0)
    c = lax.broadcasted_iota(I32, (tm, tm), 1)
    before = (r < c).astype(F32).astype(BF16)
    rank_tok = jnp.dot(sel, before, preferred_element_type=F32)
    er = lax.broadcasted_iota(I32, (N_EXPERTS, N_EXPERTS), 0)
    ec = lax.broadcasted_iota(I32, (N_EXPERTS, N_EXPERTS), 1)
    lower = (ec < er).astype(F32).astype(BF16)
    rank_exp = jnp.dot(lower, sel, preferred_element_type=F32)
    slot = base_ref[0] + rank_tok
    chosen = sel > 0
    gates = gates_ref[...]
    sub = lax.broadcasted_iota(I32, (TOPK, tm), 0)
    pos8 = jnp.zeros((TOPK, tm), F32)
    w8 = jnp.zeros((TOPK, tm), F32)
    for k in range(TOPK):
        mk = chosen & (rank_exp == float(k))
        pk = jnp.sum(jnp.where(mk, slot, 0.0), axis=0, keepdims=True)
        wk = jnp.sum(jnp.where(mk, gates, 0.0), axis=0, keepdims=True)
        pos8 = jnp.where(sub == k, pk, pos8)
        w8 = jnp.where(sub == k, wk, w8)
    pos_ref[...] = pos8.astype(I32)
    w8 = jnp.concatenate([w8, jnp.zeros((128 - TOPK, tm), F32)], axis=0)
    w_ref[...] = w8.T


def _slots(sel_t, gates_t, base):
    ne, t = sel_t.shape
    return pl.pallas_call(
        _slots_kernel, grid=(t // TM,),
        in_specs=[
            pl.BlockSpec((ne, TM), lambda i: (0, i)),
            pl.BlockSpec((ne, TM), lambda i: (0, i)),
            pl.BlockSpec((1, ne, 1), lambda i: (i, 0, 0)),
        ],
        out_specs=[pl.BlockSpec((TOPK, TM), lambda i: (0, i)),
                   pl.BlockSpec((TM, 128), lambda i: (i, 0))],
        out_shape=[jax.ShapeDtypeStruct((TOPK, t), I32), jax.ShapeDtypeStruct((t, 128), F32)],
        compiler_params=_cparams(("parallel",)), name="moe_slots",
    )(sel_t, gates_t, base)


def _sc_gather(table, idx):
    m = idx.shape[0]
    d = table.shape[1]
    assert m % (SC_WINDOW * SC_WORKERS) == 0
    mesh = plsc.VectorSubcoreMesh(core_axis_name="core", subcore_axis_name="subcore")

    @pl.kernel(out_type=jax.ShapeDtypeStruct((m, d), table.dtype), mesh=mesh)
    def gather_rows(x_hbm, i_hbm, o_hbm):
        def body(i_vmem, o_vmem):
            pltpu.sync_copy(x_hbm.at[i_vmem.at[0]], o_vmem)

        pltpu.emit_pipeline(
            body, grid=(m // SC_WINDOW,),
            in_specs=[pl.BlockSpec((1, SC_WINDOW), index_map=lambda i: (i, 0))],
            out_specs=[pl.BlockSpec((SC_WINDOW, d), index_map=lambda i: (i, 0))],
            core_axis_name=("core", "subcore"),
            dimension_semantics=(pltpu.PARALLEL,),
        )(i_hbm, o_hbm)

    return gather_rows(table, idx.reshape(m // SC_WINDOW, SC_WINDOW))


def _sc_scatter(rows, idx, n_out):
    nk, t = idx.shape
    d = rows.shape[1]
    assert t % (SC_WINDOW * SC_WORKERS) == 0
    mesh = plsc.VectorSubcoreMesh(core_axis_name="core", subcore_axis_name="subcore")

    @pl.kernel(out_type=jax.ShapeDtypeStruct((n_out, d), rows.dtype), mesh=mesh)
    def scatter_rows(x_hbm, i_hbm, o_hbm):
        def body(x_vmem, i_vmem):
            for k in range(nk):
                pltpu.sync_copy(x_vmem, o_hbm.at[i_vmem.at[k, 0]])

        pltpu.emit_pipeline(
            body, grid=(t // SC_WINDOW,),
            in_specs=[pl.BlockSpec((SC_WINDOW, d), index_map=lambda i: (i, 0)),
                      pl.BlockSpec((nk, 1, SC_WINDOW), index_map=lambda i: (0, i, 0))],
            out_specs=[],
            core_axis_name=("core", "subcore"),
            dimension_semantics=(pltpu.PARALLEL,),
        )(x_hbm, i_hbm)

    return scatter_rows(rows, idx.reshape(nk, t // SC_WINDOW, SC_WINDOW))


def _expert_ffn_kernel(te_ref, fresh_ref, nv_ref, x_ref, wg_ref, wu_ref, wd_ref, y_ref,
                       wg_s, wu_s, wd_s):
    j = pl.program_id(0)

    @pl.when(j < nv_ref[0])
    def _():
        @pl.when(fresh_ref[j] == 1)
        def _():
            wg_s[...] = wg_ref[0, 0].astype(BF16)
            wu_s[...] = wu_ref[0, 0].astype(BF16)
            wd_s[...] = wd_ref[0, 0].astype(BF16)

        x = _unpack_pair(x_ref[...]).astype(BF16)
        a = jnp.dot(x, wg_s[...], preferred_element_type=F32)
        b = jnp.dot(x, wu_s[...], preferred_element_type=F32)
        act = (_silu(a) * b).astype(BF16)
        y_ref[...] = _pack_pair(jnp.dot(act, wd_s[...], preferred_element_type=F32))


def _expert_ffn(xs, n_tiles, tile_expert, fresh, n_valid, layer, w_gate, w_up, w_down):
    dh = xs.shape[1]
    last = lambda j, te, fr, nv: jnp.minimum(j, nv[0] - 1)
    wspec = lambda w: pl.BlockSpec((1, 1) + w.shape[2:],
                                   lambda j, te, fr, nv: (layer, te[j], 0, 0))
    grid_spec = pltpu.PrefetchScalarGridSpec(
        num_scalar_prefetch=3, grid=(n_tiles,),
        in_specs=[
            pl.BlockSpec((TG, dh), lambda j, te, fr, nv: (last(j, te, fr, nv), 0)),
            wspec(w_gate), wspec(w_up), wspec(w_down),
        ],
        out_specs=pl.BlockSpec((TG, dh), lambda j, te, fr, nv: (last(j, te, fr, nv), 0)),
        scratch_shapes=[pltpu.VMEM(w_gate.shape[2:], BF16), pltpu.VMEM(w_up.shape[2:], BF16),
                        pltpu.VMEM(w_down.shape[2:], BF16)],
    )
    return pl.pallas_call(
        _expert_ffn_kernel, grid_spec=grid_spec,
        out_shape=jax.ShapeDtypeStruct((n_tiles * TG, dh), I32),
        compiler_params=_cparams(("arbitrary",)), name="expert_ffn",
    )(tile_expert, fresh, n_valid, xs, w_gate, w_up, w_down)


def _combine_kernel(yg_ref, w_ref, h_ref, x_ref, mod_ref, sgu_ref, sd_ref, o_ref):
    f = sd_ref.shape[0]
    h = _unpack_pair(h_ref[...]).astype(BF16)
    hs = jnp.dot(h, sgu_ref[...], preferred_element_type=F32)
    act = (_silu(hs[:, :f]) * hs[:, f:]).astype(BF16)
    acc = jnp.dot(act, sd_ref[...], preferred_element_type=F32)
    w = w_ref[...]
    for k in range(TOPK):
        acc = acc + w[:, k:k + 1] * _unpack_pair(yg_ref[k])
    o_ref[...] = x_ref[...] + mod_ref[0, 5:6, :] * acc


def _combine(yg, w8, h, x, mod, sgu, sd, nbp, bps):
    t, d = x.shape
    midx = _mod_index(nbp, bps)
    in_specs = [
        pl.BlockSpec((TOPK, TM, d // 2), lambda i: (0, i, 0)),
        pl.BlockSpec((TM, 128), lambda i: (i, 0)),
        pl.BlockSpec((TM, d // 2), lambda i: (i, 0)),
        pl.BlockSpec((TM, d), lambda i: (i, 0)),
        pl.BlockSpec((1, 6, d), lambda i: (midx(i), 0, 0)),
        _const_spec(sgu.shape), _const_spec(sd.shape),
    ]
    return pl.pallas_call(
        _combine_kernel, grid=(t // TM,), in_specs=in_specs,
        out_specs=pl.BlockSpec((TM, d), lambda i: (i, 0)),
        out_shape=jax.ShapeDtypeStruct((t, d), F32),
        compiler_params=_cparams(("parallel",)), name="moe_combine",
    )(yg, w8, h, x, mod, sgu, sd)


def _sparse_moe(x, h, gates_t, sel_t, cnt, mod, layer, w_gate, w_up, w_down, sgu, sd, nbp, bps):
    t, d = x.shape
    ne = N_EXPERTS
    n_pad = ne * TG
    n_tiles = (t * TOPK + n_pad) // TG
    n_slots = n_tiles * TG
    counts = cnt[:, 0, :].astype(I32)
    per_expert = jnp.sum(counts, axis=0)
    padded = (per_expert + TG - 1) // TG * TG
    ends = jnp.cumsum(padded)
    starts = ends - padded
    block_off = jnp.cumsum(counts, axis=0) - counts
    base = (starts[None, :] + block_off).astype(F32)[:, :, None]
    n_valid = (ends[-1] // TG).astype(I32).reshape(1)
    tile_start = jnp.minimum(jnp.arange(n_tiles, dtype=I32), n_valid[0] - 1) * TG
    tile_expert = jnp.sum((ends[None, :] <= tile_start[:, None]).astype(I32), axis=1)
    fresh = jnp.concatenate([jnp.ones((1,), I32),
                             (tile_expert[1:] != tile_expert[:-1]).astype(I32)])

    pos8, w8 = _slots(sel_t, gates_t, base)
    n_extra = -(-n_pad // t)
    cand = (starts + per_expert)[:, None] + jnp.arange(TG, dtype=I32)[None, :]
    spare = n_slots + jnp.arange(n_pad, dtype=I32)
    fill = jnp.where(cand < ends[:, None], cand, spare.reshape(ne, TG)).reshape(-1)
    rest = n_slots + jnp.arange(n_pad, n_extra * t, dtype=I32) % n_pad
    dest = jnp.concatenate([pos8, jnp.concatenate([fill, rest]).reshape(n_extra, t)], axis=0)
    xs = _sc_scatter(h, dest, n_slots + n_pad)
    ys = _expert_ffn(xs, n_tiles, tile_expert, fresh, n_valid, layer, w_gate, w_up, w_down)
    yg = _sc_gather(ys, pos8.reshape(-1)).reshape(TOPK, t, d // 2)
    return _combine(yg, w8, h, x, mod, sgu, sd, nbp, bps)


def _head_rms(x, g_row, hm_ref, hmt_ref):
    sq_hi, sq_lo = _split_bf16(x * x)
    hm = hm_ref[...]
    ss = (jnp.dot(sq_hi, hm, preferred_element_type=F32)
          + jnp.dot(sq_lo, hm, preferred_element_type=F32))
    r = lax.rsqrt(ss * (1.0 / HEAD_DIM) + EPS)
    r_hi, r_lo = _split_bf16(r)
    hmt = hmt_ref[...]
    rb = (jnp.dot(r_hi, hmt, preferred_element_type=F32)
          + jnp.dot(r_lo, hmt, preferred_element_type=F32))
    return (x * rb) * g_row


def _qkv_kernel(emit_f32, x_ref, mod_ref, gmix_ref, w_ref, qg_ref, kg_ref, hm_ref, hmt_ref, *outs):
    d = x_ref.shape[1]
    sh1 = mod_ref[0, 0:1, :]
    sc1 = mod_ref[0, 1:2, :]
    h = _adaln(x_ref[...], gmix_ref[...], sh1, sc1).astype(BF16)
    qkv = jnp.dot(h, w_ref[...], preferred_element_type=F32)
    q = _head_rms(qkv[:, :d], qg_ref[...], hm_ref, hmt_ref)
    k = _head_rms(qkv[:, d:2 * d], kg_ref[...], hm_ref, hmt_ref)
    v = qkv[:, 2 * d:]
    outs[0][...] = q.astype(BF16)
    outs[1][...] = k.astype(BF16)
    outs[2][...] = v.astype(BF16)
    if emit_f32:
        outs[3][...] = k
        outs[4][...] = v


def _qkv(x, mod, g_mix, w_qkv, qg, kg, hm, hmt, blk0, nblk, midx, emit_f32):
    t, d = x.shape
    in_specs = [
        pl.BlockSpec((TM, d), lambda i: (i + blk0, 0)),
        pl.BlockSpec((1, 6, d), lambda i: (midx(i + blk0), 0, 0)),
        _const_spec(g_mix.shape), _const_spec(w_qkv.shape), _const_spec(qg.shape),
        _const_spec(kg.shape), _const_spec(hm.shape), _const_spec(hmt.shape),
    ]
    n_out = 5 if emit_f32 else 3
    out_specs = [pl.BlockSpec((TM, d), lambda i: (i, 0)) for _ in range(n_out)]
    out_shape = [jax.ShapeDtypeStruct((nblk * TM, d), BF16 if o < 3 else F32) for o in range(n_out)]
    return pl.pallas_call(
        functools.partial(_qkv_kernel, emit_f32),
        grid=(nblk,), in_specs=in_specs, out_specs=out_specs, out_shape=out_shape,
        compiler_params=_cparams(("parallel",)), name="qkv_f32" if emit_f32 else "qkv",
    )(x, mod, g_mix, w_qkv, qg, kg, hm, hmt)


def _head_masks():
    lane = lax.broadcasted_iota(I32, (1, 2 * HEAD_DIM), 1)
    return lane < HEAD_DIM


def _ctx_attn_kernel(q_ref, k_ref, v_ref, o_ref):
    q = q_ref[...]
    k = k_ref[...]
    v = v_ref[...]
    lo = _head_masks()
    outs = []
    for hh in range(2):
        msk = lo if hh == 0 else jnp.logical_not(lo)
        qm = jnp.where(msk, q, jnp.zeros_like(q)) * jnp.asarray(HEAD_DIM ** -0.5, BF16)
        s = lax.dot_general(qm, k, NT_DIMS, preferred_element_type=F32)
        m = jnp.max(s, axis=-1, keepdims=True)
        p = jnp.exp(s - m)
        l = jnp.sum(p, axis=-1, keepdims=True)
        o = jnp.dot(p.astype(BF16), v, preferred_element_type=F32)
        outs.append(o / l)
    o_ref[...] = jnp.where(lo, outs[0], outs[1]).astype(BF16)


def _ctx_attn(q, k, v, nb, s):
    t, d = q.shape
    hp = d // (2 * HEAD_DIM)
    spec = pl.BlockSpec((s, 2 * HEAD_DIM), lambda b, h: (b, h))
    return pl.pallas_call(
        _ctx_attn_kernel, grid=(nb, hp), in_specs=[spec, spec, spec], out_specs=spec,
        out_shape=jax.ShapeDtypeStruct((nb * s, d), BF16),
        compiler_params=_cparams(("parallel", "parallel")), name="context_attention",
    )(q, k, v)


NA_QROWS = 8
NA_KROWS = 16


def _na_kernel(rows, q_ref, k_ref, v_ref, kc_ref, vc_ref, bias_ref, o_ref):
    rb = pl.program_id(1)
    kr0 = jnp.clip(rb * NA_QROWS - WIN_H // 2, 0, rows - NA_KROWS)
    start = pl.multiple_of(kr0 * GRID_W, 256)
    nk = NA_KROWS * GRID_W
    q = q_ref[0]
    kw = k_ref[0, pl.ds(start, nk), :]
    vw = v_ref[0, pl.ds(start, nk), :]
    kc = kc_ref[0]
    vc = vc_ref[0]
    lo = _head_masks()
    outs = []
    for hh in range(2):
        msk = lo if hh == 0 else jnp.logical_not(lo)
        qm = jnp.where(msk, q, jnp.zeros_like(q)) * jnp.asarray(HEAD_DIM ** -0.5, BF16)
        s = lax.dot_general(qm, kw, NT_DIMS, preferred_element_type=F32) + bias_ref[0, hh]
        sc = lax.dot_general(qm, kc, NT_DIMS, preferred_element_type=F32)
        m = jnp.maximum(jnp.max(s, axis=-1, keepdims=True), jnp.max(sc, axis=-1, keepdims=True))
        p = jnp.exp(s - m)
        pc = jnp.exp(sc - m)
        l = jnp.sum(p, axis=-1, keepdims=True) + jnp.sum(pc, axis=-1, keepdims=True)
        o = (jnp.dot(p.astype(BF16), vw, preferred_element_type=F32)
             + jnp.dot(pc.astype(BF16), vc, preferred_element_type=F32))
        outs.append(o / l)
    o_ref[0] = jnp.where(lo, outs[0], outs[1]).astype(BF16)


def _na_row_classes(rows):
    nrb = rows // NA_QROWS
    out = []
    for rb in (0, 1, nrb - 1):
        r0 = rb * NA_QROWS
        kr0 = int(np.clip(r0 - WIN_H // 2, 0, rows - NA_KROWS))
        table = []
        for rl in range(NA_QROWS):
            r = r0 + rl
            sr = int(np.clip(r - WIN_H // 2, 0, rows - WIN_H))
            table.append([(kr0 + kl - r + WIN_H - 1) if sr <= kr0 + kl < sr + WIN_H else None
                          for kl in range(NA_KROWS)])
        out.append(table)
    return out


def _na_bias_kernel(rows, rpb_ref, o_ref):
    h = pl.program_id(0)
    n_dr, n_dc = 2 * WIN_H - 1, 2 * WIN_W - 1
    cq = lax.broadcasted_iota(I32, (GRID_W, 2 * GRID_W), 0)
    lane = lax.broadcasted_iota(I32, (GRID_W, 2 * GRID_W), 1)
    ck = lane & (GRID_W - 1)
    q_start = jnp.clip(cq - WIN_W // 2, 0, GRID_W - WIN_W)
    col_ok = (ck >= q_start) & (ck < q_start + WIN_W)
    dc = ck - cq + (WIN_W - 1)
    neg = jnp.full((GRID_W, 2 * GRID_W), NEG, F32)
    tiles = []
    for i in range(n_dr):
        t = jnp.zeros((GRID_W, 2 * GRID_W), F32)
        for jj in range(n_dc):
            t = jnp.where(dc == jj, rpb_ref[h * (n_dr * n_dc) + i * n_dc + jj], t)
        tiles.append(jnp.where(col_ok, t, neg))
    first_half = lane < GRID_W
    for c, table in enumerate(_na_row_classes(rows)):
        for rl in range(NA_QROWS):
            for m in range(NA_KROWS // 2):
                ia, ib = table[rl][2 * m], table[rl][2 * m + 1]
                ta = neg if ia is None else tiles[ia]
                tb = neg if ib is None else tiles[ib]
                blk = ta if ia == ib else jnp.where(first_half, ta, tb)
                o_ref[c, 0, rl * GRID_W:(rl + 1) * GRID_W, m * 2 * GRID_W:(m + 1) * 2 * GRID_W] = blk


def _na_bias(rpb, rows):
    nh = rpb.shape[0]
    nq, nk = NA_QROWS * GRID_W, NA_KROWS * GRID_W
    return pl.pallas_call(
        functools.partial(_na_bias_kernel, rows), grid=(nh,),
        in_specs=[pl.BlockSpec(memory_space=pltpu.SMEM)],
        out_specs=pl.BlockSpec((3, 1, nq, nk), lambda h: (0, h, 0, 0)),
        out_shape=jax.ShapeDtypeStruct((3, nh, nq, nk), F32),
        compiler_params=_cparams(("parallel",)), name="na_bias",
    )(rpb.reshape(-1))


def _na_attn(q, k, v, kc, vc, bias_tab, rows):
    nb, s, d = q.shape
    hp = d // (2 * HEAD_DIM)
    nrb = rows // NA_QROWS
    nq = NA_QROWS * GRID_W
    lc = kc.shape[1]

    def cls(r):
        return jnp.where(r == 0, 0, jnp.where(r == nrb - 1, 2, 1))

    in_specs = [
        pl.BlockSpec((1, nq, 2 * HEAD_DIM), lambda h, r, b: (b, r, h)),
        pl.BlockSpec((1, s, 2 * HEAD_DIM), lambda h, r, b: (b, 0, h)),
        pl.BlockSpec((1, s, 2 * HEAD_DIM), lambda h, r, b: (b, 0, h)),
        pl.BlockSpec((1, lc, 2 * HEAD_DIM), lambda h, r, b: (b, 0, h)),
        pl.BlockSpec((1, lc, 2 * HEAD_DIM), lambda h, r, b: (b, 0, h)),
        pl.BlockSpec((1, 2, nq, NA_KROWS * GRID_W), lambda h, r, b: (cls(r), h, 0, 0)),
    ]
    return pl.pallas_call(
        functools.partial(_na_kernel, rows),
        grid=(hp, nrb, nb), in_specs=in_specs,
        out_specs=pl.BlockSpec((1, nq, 2 * HEAD_DIM), lambda h, r, b: (b, r, h)),
        out_shape=jax.ShapeDtypeStruct((nb, s, d), BF16),
        compiler_params=_cparams(("parallel", "parallel", "parallel")),
        name="neighbourhood_attention",
    )(q, k, v, kc, vc, bias_tab)


def _oproj_kernel(a_ref, x_ref, mod_ref, wout_ref, gffn_ref, wr_hi_ref, wr_lo_ref, rb_ref,
                  x1_ref, h_ref, gates_ref, sel_ref, cnt_ref):
    y = jnp.dot(a_ref[...], wout_ref[...], preferred_element_type=F32)
    x1 = x_ref[...] + mod_ref[0, 2:3, :] * y
    x1_ref[...] = x1
    _ffn_pre(x1, mod_ref, gffn_ref, wr_hi_ref, wr_lo_ref, rb_ref,
             h_ref, gates_ref, sel_ref, cnt_ref)


def _oproj(attn, x, mod, w_out, g_ffn, wr_hi, wr_lo, rbias, nbp, bps):
    t, d = x.shape
    midx = _mod_index(nbp, bps)
    in_specs = [
        pl.BlockSpec((TM, d), lambda i: (i, 0)),
        pl.BlockSpec((TM, d), lambda i: (i, 0)),
        pl.BlockSpec((1, 6, d), lambda i: (midx(i), 0, 0)),
        _const_spec(w_out.shape), _const_spec(g_ffn.shape), _const_spec(wr_hi.shape),
        _const_spec(wr_lo.shape), _const_spec(rbias.shape),
    ]
    out_specs, out_shape = _pre_out_specs(t, d)
    return pl.pallas_call(
        _oproj_kernel, grid=(t // TM,), in_specs=in_specs, out_specs=out_specs,
        out_shape=out_shape, compiler_params=_cparams(("parallel",)), name="attn_out_proj",
    )(attn, x, mod, w_out, g_ffn, wr_hi, wr_lo, rbias)


def _router_weights(w_router, router_bias):
    d, ne = w_router.shape
    w = jnp.pad(w_router, ((0, 0), (0, 128 - ne)))
    hi = w.astype(BF16)
    lo = (w - hi.astype(F32)).astype(BF16)
    return hi, lo, router_bias.reshape(ne, 1)


def _shared_weights(s_gate, s_up, s_down):
    return jnp.concatenate([s_gate, s_up], axis=-1).astype(BF16), s_down.astype(BF16)


def kernel(x_prompt, x_sample, cache_k, cache_v, c, c_ctx, ada_w, ada_b, norm_mix, norm_ffn,
           pc_w_in, pc_pool_w, pc_pool_scale, pc_conv_w, pc_w_out,
           na_w_qkv, na_q_norm, na_k_norm, na_rpb, na_w_out,
           moe_router, moe_router_bias, moe_w_gate, moe_w_up, moe_w_down,
           moe_shared_gate, moe_shared_up, moe_shared_down):
    nb_p, s_p, d = x_prompt.shape
    nb_s, s_s, _ = x_sample.shape
    assert s_p == TM and s_s % TM == 0
    tp, ts = nb_p * s_p, nb_s * s_s
    nbp, bps = tp // TM, s_s // TM
    depth = ada_w.shape[0]
    nh = d // HEAD_DIM
    rows = s_s // GRID_W

    x = jnp.concatenate([x_prompt.reshape(tp, d), x_sample.reshape(ts, d)], axis=0)
    cond = jnp.concatenate([c_ctx[None], c], axis=0)
    cond = jnp.pad(cond, ((0, -cond.shape[0] % 8), (0, 0)))
    mods = _modulation(cond, ada_w, ada_b).reshape(depth, cond.shape[0], 6, d)

    new_k, new_v = [], []
    for layer in range(depth):
        i = layer // 2
        mod = mods[layer]
        g_mix = norm_mix[layer].reshape(1, d)
        g_ffn = norm_ffn[layer].reshape(1, d)
        wr_hi, wr_lo, rbias = _router_weights(moe_router[layer], moe_router_bias[layer])
        if layer % 2 == 0:
            x, h, gates_t, sel_t, cnt = _mixer(
                x, mod, g_mix, pc_w_in[i].astype(BF16), pc_pool_w[i].astype(BF16),
                pc_pool_scale[i].reshape(1, -1), pc_conv_w[i], pc_w_out[i].astype(BF16),
                g_ffn, wr_hi, wr_lo, rbias, nbp, bps, s_p, s_s)
        else:
            head_of = np.arange(d) // HEAD_DIM
            hm = jnp.asarray(head_of[:, None] == np.arange(128)[None, :], BF16)
            hmt = jnp.asarray(np.arange(128)[:, None] == head_of[None, :], BF16)
            qg = jnp.tile(na_q_norm[i], nh).reshape(1, d)
            kg = jnp.tile(na_k_norm[i], nh).reshape(1, d)
            w_qkv = na_w_qkv[i].astype(BF16)
            midx = _mod_index(nbp, bps)
            qp, kp, vp, kp32, vp32 = _qkv(x, mod, g_mix, w_qkv, qg, kg, hm, hmt, 0, nbp, midx, True)
            qs, ks, vs = _qkv(x, mod, g_mix, w_qkv, qg, kg, hm, hmt, nbp, ts // TM, midx, False)
            new_k.append(kp32.reshape(nb_p, s_p, nh, HEAD_DIM))
            new_v.append(vp32.reshape(nb_p, s_p, nh, HEAD_DIM))
            ap = _ctx_attn(qp, kp, vp, nb_p, s_p)
            bias_tab = _na_bias(na_rpb[i], rows)
            lc = cache_k.shape[2]
            kc = cache_k[:, i].reshape(nb_s, lc, d).astype(BF16)
            vc = cache_v[:, i].reshape(nb_s, lc, d).astype(BF16)
            a_s = _na_attn(qs.reshape(nb_s, s_s, d), ks.reshape(nb_s, s_s, d),
                           vs.reshape(nb_s, s_s, d), kc, vc, bias_tab, rows)
            attn = jnp.concatenate([ap, a_s.reshape(ts, d)], axis=0)
            x, h, gates_t, sel_t, cnt = _oproj(attn, x, mod, na_w_out[i].astype(BF16), g_ffn,
                                               wr_hi, wr_lo, rbias, nbp, bps)
        sgu, sd = _shared_weights(moe_shared_gate[layer], moe_shared_up[layer],
                                  moe_shared_down[layer])
        x = _sparse_moe(x, h, gates_t, sel_t, cnt, mod, layer, moe_w_gate, moe_w_up, moe_w_down,
                        sgu, sd, nbp, bps)
    y_p = x[:tp].reshape(nb_p, s_p, d)
    y_s = x[tp:].reshape(nb_s, s_s, d)
    return (y_p, y_s, jnp.stack(new_k, axis=1), jnp.stack(new_v, axis=1))
```

```python
import functools

import numpy as np
import jax
import jax.numpy as jnp
from jax import lax
from jax.experimental import pallas as pl
from jax.experimental.pallas import tpu as pltpu
from jax.experimental.pallas import tpu_sc as plsc

F32 = jnp.float32
BF16 = jnp.bfloat16
I32 = jnp.int32
U32 = jnp.uint32

TM = 256
HALO = 8
POOL_WINDOWS = (2, 4, 8, 16)
N_EXPERTS = 64
N_GROUPS = 8
GROUP_SIZE = N_EXPERTS // N_GROUPS
TOPK_GROUPS = 4
TOPK = 8
ROUTED_SCALE = 2.5
EPS = 1e-6
GRID_W = 64
WIN_H = 8
WIN_W = 16
HEAD_DIM = 64
NEG = float(np.finfo(np.float32).min)
VMEM_LIMIT = 56 * 1024 * 1024
NT_DIMS = (((1,), (1,)), ((), ()))
TG = 1024
SC_WINDOW = 64
SC_WORKERS = 32


def _cparams(sem):
    return pltpu.CompilerParams(dimension_semantics=sem, vmem_limit_bytes=VMEM_LIMIT)


def _silu(x):
    return x * jax.nn.sigmoid(x)


def _split_bf16(x):
    hi = x.astype(BF16)
    lo = (x - hi.astype(F32)).astype(BF16)
    return hi, lo


def _adaln(x, g, shift, scale):
    ms = jnp.mean(x * x, axis=-1, keepdims=True)
    return (x * lax.rsqrt(ms + EPS)) * g * (1.0 + scale) + shift


def _pack_pair(x):
    w = x.shape[1] // 2
    lo = lax.bitcast_convert_type(x[:, :w].astype(BF16).astype(F32), U32) >> 16
    hi = lax.bitcast_convert_type(x[:, w:].astype(BF16).astype(F32), U32)
    return lax.bitcast_convert_type(lo | hi, I32)


def _unpack_pair(p):
    u = lax.bitcast_convert_type(p, U32)
    lo = lax.bitcast_convert_type(u << 16, F32)
    hi = lax.bitcast_convert_type(u & jnp.uint32(0xFFFF0000), F32)
    return jnp.concatenate([lo, hi], axis=-1)


def _mod_kernel(cond_ref, w_ref, b_ref, o_ref):
    c = cond_ref[...]
    a = _silu(c)
    o_ref[0] = jnp.dot(a, w_ref[0], preferred_element_type=F32,
                       precision=lax.Precision.HIGHEST) + b_ref[0]


def _modulation(cond, ada_w, ada_b):
    depth, d, n = ada_w.shape
    rows = cond.shape[0]
    tn = 1536
    return pl.pallas_call(
        _mod_kernel,
        grid=(depth, n // tn),
        in_specs=[
            pl.BlockSpec((rows, d), lambda l, j: (0, 0)),
            pl.BlockSpec((1, d, tn), lambda l, j: (l, 0, j)),
            pl.BlockSpec((1, 1, tn), lambda l, j: (l, 0, j)),
        ],
        out_specs=pl.BlockSpec((1, rows, tn), lambda l, j: (l, 0, j)),
        out_shape=jax.ShapeDtypeStruct((depth, rows, n), F32),
        compiler_params=_cparams(("arbitrary", "arbitrary")),
        name="modulation",
    )(cond, ada_w, ada_b.reshape(depth, 1, n))


def _route(logits_t, bias_col):
    tm = logits_t.shape[1]
    scores = jax.nn.sigmoid(logits_t)
    biased = scores + bias_col
    sub = lax.broadcasted_iota(I32, (GROUP_SIZE, tm), 0).astype(F32)
    ninf = jnp.float32(-jnp.inf)
    groups, gscore = [], []
    for g in range(N_GROUPS):
        v = biased[g * GROUP_SIZE:(g + 1) * GROUP_SIZE]
        m1 = jnp.max(v, axis=0, keepdims=True)
        first = jnp.min(jnp.where(v == m1, sub, float(GROUP_SIZE)), axis=0, keepdims=True)
        m2 = jnp.max(jnp.where(sub == first, ninf, v), axis=0, keepdims=True)
        groups.append(v)
        gscore.append(m1 + m2)
    masked = []
    for g in range(N_GROUPS):
        rank = jnp.zeros((1, tm), I32)
        for g2 in range(N_GROUPS):
            if g2 == g:
                continue
            ahead = gscore[g2] > gscore[g]
            if g2 < g:
                ahead = ahead | (gscore[g2] == gscore[g])
            rank = rank + ahead.astype(I32)
        masked.append(jnp.where(rank < TOPK_GROUPS, groups[g], ninf))
    masked = jnp.concatenate(masked, axis=0)
    eidx = lax.broadcasted_iota(I32, (N_EXPERTS, tm), 0).astype(F32)
    sel = jnp.zeros((N_EXPERTS, tm), jnp.bool_)
    for _ in range(TOPK):
        best = jnp.max(masked, axis=0, keepdims=True)
        first = jnp.min(jnp.where(masked == best, eidx, float(N_EXPERTS)), axis=0, keepdims=True)
        hit = eidx == first
        sel = sel | hit
        masked = jnp.where(hit, ninf, masked)
    w = jnp.where(sel, scores, 0.0)
    wsum = jnp.sum(w, axis=0, keepdims=True)
    return w / wsum * ROUTED_SCALE, sel


def _ffn_pre(x1, mod_ref, gffn_ref, wr_hi_ref, wr_lo_ref, rb_ref,
             h_ref, gates_ref, sel_ref, cnt_ref):
    sh2 = mod_ref[0, 3:4, :]
    sc2 = mod_ref[0, 4:5, :]
    h = _adaln(x1, gffn_ref[...], sh2, sc2)
    h_hi, h_lo = _split_bf16(h)
    h_ref[...] = _pack_pair(h_hi)
    wr_hi = wr_hi_ref[...]
    logits = (jnp.dot(h_hi, wr_hi, preferred_element_type=F32)
              + jnp.dot(h_hi, wr_lo_ref[...], preferred_element_type=F32)
              + jnp.dot(h_lo, wr_hi, preferred_element_type=F32))
    gates_t, sel = _route(logits.T[:N_EXPERTS], rb_ref[...])
    gates_ref[...] = gates_t
    sel_b = sel.astype(F32).astype(BF16)
    sel_ref[...] = sel_b
    ones = jnp.ones((8, sel_b.shape[1]), BF16)
    cnt_ref[0] = lax.dot_general(ones, sel_b, NT_DIMS, preferred_element_type=F32)


def _pre_out_specs(t, d):
    specs = [
        pl.BlockSpec((TM, d), lambda i: (i, 0)),
        pl.BlockSpec((TM, d // 2), lambda i: (i, 0)),
        pl.BlockSpec((N_EXPERTS, TM), lambda i: (0, i)),
        pl.BlockSpec((N_EXPERTS, TM), lambda i: (0, i)),
        pl.BlockSpec((1, 8, N_EXPERTS), lambda i: (i, 0, 0)),
    ]
    shapes = [
        jax.ShapeDtypeStruct((t, d), F32),
        jax.ShapeDtypeStruct((t, d // 2), I32),
        jax.ShapeDtypeStruct((N_EXPERTS, t), F32),
        jax.ShapeDtypeStruct((N_EXPERTS, t), BF16),
        jax.ShapeDtypeStruct((t // TM, 8, N_EXPERTS), F32),
    ]
    return specs, shapes


def _mixer_kernel(nbp, bps, sp, ss,
                  xc_ref, xp_ref, xn_ref, mod_ref, gmix_ref, win_ref, pw_ref, ps_ref, cw_ref,
                  wout_ref, gffn_ref, wr_hi_ref, wr_lo_ref, rb_ref,
                  x1_ref, h_ref, gates_ref, sel_ref, cnt_ref):
    i = pl.program_id(0)
    is_p = i < nbp
    j = lax.rem(jnp.maximum(i - nbp, 0), bps)
    first = is_p | (j == 0)
    last = is_p | (j == bps - 1)
    base = jnp.where(is_p, 0, j * TM)
    slen = jnp.where(is_p, sp, ss)

    sh1 = mod_ref[0, 0:1, :]
    sc1 = mod_ref[0, 1:2, :]
    g1 = mod_ref[0, 2:3, :]
    xc = xc_ref[...]
    x_ext = jnp.concatenate([xp_ref[...], xc, xn_ref[...]], axis=0)
    h_ext = _adaln(x_ext, gmix_ref[...], sh1, sc1).astype(BF16)
    u = jnp.dot(h_ext, win_ref[...], preferred_element_type=F32)
    next_ = TM + 2 * HALO
    row = lax.broadcasted_iota(I32, (next_, 1), 0)
    keep = ((row >= HALO) | jnp.logical_not(first)) & ((row < HALO + TM) | jnp.logical_not(last))
    u = jnp.where(keep, u, 0.0)

    dm = u.shape[1] // 4
    ua = u[:, :dm]
    gate_b = u[HALO:HALO + TM, dm:2 * dm]
    z = u[:, 2 * dm:3 * dm] * u[:, 3 * dm:]

    def up(a, k):
        return pltpu.roll(a, next_ - k, 0)

    pos = base + lax.broadcasted_iota(I32, (TM, 1), 0)
    pg = dm // len(POOL_WINDOWS)
    ya = []
    for g, w in enumerate(POOL_WINDOWS):
        e = ua[:, g * pg:(g + 1) * pg]
        acc = e
        span = 1
        while span < w:
            acc = acc + up(acc, span)
            span *= 2
        off = HALO - w // 2
        wsum = (up(acc, off) if off else acc)[:TM]
        lo = jnp.maximum(pos - w // 2, 0)
        hi = jnp.minimum(pos + (w - w // 2 - 1), slen - 1)
        cnt = (hi - lo + 1).astype(F32)
        diff = wsum / cnt - e[HALO:HALO + TM]
        ya.append(jnp.dot(diff.astype(BF16), pw_ref[g], preferred_element_type=F32))
    y_a = jnp.concatenate(ya, axis=-1) * ps_ref[...]
    zc = (cw_ref[0:1, :] * up(z, HALO - 1)[:TM] + cw_ref[1:2, :] * z[HALO:HALO + TM]
          + cw_ref[2:3, :] * up(z, HALO + 1)[:TM])
    y_b = gate_b * zc
    ycat = jnp.concatenate([y_a, y_b], axis=-1).astype(BF16)
    y = jnp.dot(ycat, wout_ref[...], preferred_element_type=F32)
    x1 = xc + g1 * y
    x1_ref[...] = x1
    _ffn_pre(x1, mod_ref, gffn_ref, wr_hi_ref, wr_lo_ref, rb_ref,
             h_ref, gates_ref, sel_ref, cnt_ref)


def _mod_index(nbp, bps):
    def f(i):
        return jnp.where(i < nbp, 0, 1 + jnp.maximum(i - nbp, 0) // bps)
    return f


def _const_spec(shape):
    nd = len(shape)
    return pl.BlockSpec(shape, lambda i: (0,) * nd)


def _mixer(x, mod, g_mix, w_in, pool_w, pool_scale, conv_w, w_out, g_ffn, wr_hi, wr_lo, rbias,
           nbp, bps, sp, ss):
    t, d = x.shape
    nblk = t // TM
    midx = _mod_index(nbp, bps)
    hpb = TM // HALO
    nh = t // HALO
    in_specs = [
        pl.BlockSpec((TM, d), lambda i: (i, 0)),
        pl.BlockSpec((HALO, d), lambda i: (jnp.maximum(i * hpb - 1, 0), 0)),
        pl.BlockSpec((HALO, d), lambda i: (jnp.minimum((i + 1) * hpb, nh - 1), 0)),
        pl.BlockSpec((1, 6, d), lambda i: (midx(i), 0, 0)),
        _const_spec(g_mix.shape), _const_spec(w_in.shape), _const_spec(pool_w.shape),
        _const_spec(pool_scale.shape), _const_spec(conv_w.shape), _const_spec(w_out.shape),
        _const_spec(g_ffn.shape), _const_spec(wr_hi.shape), _const_spec(wr_lo.shape),
        _const_spec(rbias.shape),
    ]
    out_specs, out_shape = _pre_out_specs(t, d)
    return pl.pallas_call(
        functools.partial(_mixer_kernel, nbp, bps, sp, ss),
        grid=(nblk,), in_specs=in_specs, out_specs=out_specs, out_shape=out_shape,
        compiler_params=_cparams(("parallel",)), name="pool_conv_mixer",
    )(x, x, x, mod, g_mix, w_in, pool_w, pool_scale, conv_w, w_out, g_ffn, wr_hi, wr_lo, rbias)


def _slots_kernel(sel_ref, gates_ref, base_ref, pos_ref, w_ref):
    sel = sel_ref[...]
    tm = sel.shape[1]
    r = lax.broadcasted_iota(I32, (tm, tm), 0)
    c = lax.broadcasted_iota(I32, (tm, tm), 1)
    before = (r < c).astype(F32).astype(BF16)
    rank_tok = jnp.dot(sel, before, preferred_element_type=F32)
    er = lax.broadcasted_iota(I32, (N_EXPERTS, N_EXPERTS), 0)
    ec = lax.broadcasted_iota(I32, (N_EXPERTS, N_EXPERTS), 1)
    lower = (ec < er).astype(F32).astype(BF16)
    rank_exp = jnp.dot(lower, sel, preferred_element_type=F32)
    slot = base_ref[0] + rank_tok
    chosen = sel > 0
    gates = gates_ref[...]
    sub = lax.broadcasted_iota(I32, (TOPK, tm), 0)
    pos8 = jnp.zeros((TOPK, tm), F32)
    w8 = jnp.zeros((TOPK, tm), F32)
    for k in range(TOPK):
        mk = chosen & (rank_exp == float(k))
        pk = jnp.sum(jnp.where(mk, slot, 0.0), axis=0, keepdims=True)
        wk = jnp.sum(jnp.where(mk, gates, 0.0), axis=0, keepdims=True)
        pos8 = jnp.where(sub == k, pk, pos8)
        w8 = jnp.where(sub == k, wk, w8)
    pos_ref[...] = pos8.astype(I32)
    w8 = jnp.concatenate([w8, jnp.zeros((128 - TOPK, tm), F32)], axis=0)
    w_ref[...] = w8.T


def _slots(sel_t, gates_t, base):
    ne, t = sel_t.shape
    return pl.pallas_call(
        _slots_kernel, grid=(t // TM,),
        in_specs=[
            pl.BlockSpec((ne, TM), lambda i: (0, i)),
            pl.BlockSpec((ne, TM), lambda i: (0, i)),
            pl.BlockSpec((1, ne, 1), lambda i: (i, 0, 0)),
        ],
        out_specs=[pl.BlockSpec((TOPK, TM), lambda i: (0, i)),
                   pl.BlockSpec((TM, 128), lambda i: (i, 0))],
        out_shape=[jax.ShapeDtypeStruct((TOPK, t), I32), jax.ShapeDtypeStruct((t, 128), F32)],
        compiler_params=_cparams(("parallel",)), name="moe_slots",
    )(sel_t, gates_t, base)


def _sc_gather(table, idx):
    m = idx.shape[0]
    d = table.shape[1]
    assert m % (SC_WINDOW * SC_WORKERS) == 0
    mesh = plsc.VectorSubcoreMesh(core_axis_name="core", subcore_axis_name="subcore")

    @pl.kernel(out_type=jax.ShapeDtypeStruct((m, d), table.dtype), mesh=mesh)
    def gather_rows(x_hbm, i_hbm, o_hbm):
        def body(i_vmem, o_vmem):
            pltpu.sync_copy(x_hbm.at[i_vmem.at[0]], o_vmem)

        pltpu.emit_pipeline(
            body, grid=(m // SC_WINDOW,),
            in_specs=[pl.BlockSpec((1, SC_WINDOW), index_map=lambda i: (i, 0))],
            out_specs=[pl.BlockSpec((SC_WINDOW, d), index_map=lambda i: (i, 0))],
            core_axis_name=("core", "subcore"),
            dimension_semantics=(pltpu.PARALLEL,),
        )(i_hbm, o_hbm)

    return gather_rows(table, idx.reshape(m // SC_WINDOW, SC_WINDOW))


def _sc_scatter(rows, idx, n_out):
    nk, t = idx.shape
    d = rows.shape[1]
    assert t % (SC_WINDOW * SC_WORKERS) == 0
    mesh = plsc.VectorSubcoreMesh(core_axis_name="core", subcore_axis_name="subcore")

    @pl.kernel(out_type=jax.ShapeDtypeStruct((n_out, d), rows.dtype), mesh=mesh)
    def scatter_rows(x_hbm, i_hbm, o_hbm):
        def body(x_vmem, i_vmem):
            for k in range(nk):
                pltpu.sync_copy(x_vmem, o_hbm.at[i_vmem.at[k, 0]])

        pltpu.emit_pipeline(
            body, grid=(t // SC_WINDOW,),
            in_specs=[pl.BlockSpec((SC_WINDOW, d), index_map=lambda i: (i, 0)),
                      pl.BlockSpec((nk, 1, SC_WINDOW), index_map=lambda i: (0, i, 0))],
            out_specs=[],
            core_axis_name=("core", "subcore"),
            dimension_semantics=(pltpu.PARALLEL,),
        )(x_hbm, i_hbm)

    return scatter_rows(rows, idx.reshape(nk, t // SC_WINDOW, SC_WINDOW))


def _expert_ffn_kernel(te_ref, fresh_ref, nv_ref, x_ref, wg_ref, wu_ref, wd_ref, y_ref,
                       wg_s, wu_s, wd_s):
    j = pl.program_id(0)

    @pl.when(j < nv_ref[0])
    def _():
        @pl.when(fresh_ref[j] == 1)
        def _():
            wg_s[...] = wg_ref[0, 0].astype(BF16)
            wu_s[...] = wu_ref[0, 0].astype(BF16)
            wd_s[...] = wd_ref[0, 0].astype(BF16)

        x = _unpack_pair(x_ref[...]).astype(BF16)
        a = jnp.dot(x, wg_s[...], preferred_element_type=F32)
        b = jnp.dot(x, wu_s[...], preferred_element_type=F32)
        act = (_silu(a) * b).astype(BF16)
        y_ref[...] = _pack_pair(jnp.dot(act, wd_s[...], preferred_element_type=F32))


def _expert_ffn(xs, n_tiles, tile_expert, fresh, n_valid, layer, w_gate, w_up, w_down):
    dh = xs.shape[1]
    last = lambda j, te, fr, nv: jnp.minimum(j, nv[0] - 1)
    wspec = lambda w: pl.BlockSpec((1, 1) + w.shape[2:],
                                   lambda j, te, fr, nv: (layer, te[j], 0, 0))
    grid_spec = pltpu.PrefetchScalarGridSpec(
        num_scalar_prefetch=3, grid=(n_tiles,),
        in_specs=[
            pl.BlockSpec((TG, dh), lambda j, te, fr, nv: (last(j, te, fr, nv), 0)),
            wspec(w_gate), wspec(w_up), wspec(w_down),
        ],
        out_specs=pl.BlockSpec((TG, dh), lambda j, te, fr, nv: (last(j, te, fr, nv), 0)),
        scratch_shapes=[pltpu.VMEM(w_gate.shape[2:], BF16), pltpu.VMEM(w_up.shape[2:], BF16),
                        pltpu.VMEM(w_down.shape[2:], BF16)],
    )
    return pl.pallas_call(
        _expert_ffn_kernel, grid_spec=grid_spec,
        out_shape=jax.ShapeDtypeStruct((n_tiles * TG, dh), I32),
        compiler_params=_cparams(("arbitrary",)), name="expert_ffn",
    )(tile_expert, fresh, n_valid, xs, w_gate, w_up, w_down)


def _combine_kernel(yg_ref, w_ref, h_ref, x_ref, mod_ref, sgu_ref, sd_ref, o_ref):
    f = sd_ref.shape[0]
    h = _unpack_pair(h_ref[...]).astype(BF16)
    hs = jnp.dot(h, sgu_ref[...], preferred_element_type=F32)
    act = (_silu(hs[:, :f]) * hs[:, f:]).astype(BF16)
    acc = jnp.dot(act, sd_ref[...], preferred_element_type=F32)
    w = w_ref[...]
    for k in range(TOPK):
        acc = acc + w[:, k:k + 1] * _unpack_pair(yg_ref[k])
    o_ref[...] = x_ref[...] + mod_ref[0, 5:6, :] * acc


def _combine(yg, w8, h, x, mod, sgu, sd, nbp, bps):
    t, d = x.shape
    midx = _mod_index(nbp, bps)
    in_specs = [
        pl.BlockSpec((TOPK, TM, d // 2), lambda i: (0, i, 0)),
        pl.BlockSpec((TM, 128), lambda i: (i, 0)),
        pl.BlockSpec((TM, d // 2), lambda i: (i, 0)),
        pl.BlockSpec((TM, d), lambda i: (i, 0)),
        pl.BlockSpec((1, 6, d), lambda i: (midx(i), 0, 0)),
        _const_spec(sgu.shape), _const_spec(sd.shape),
    ]
    return pl.pallas_call(
        _combine_kernel, grid=(t // TM,), in_specs=in_specs,
        out_specs=pl.BlockSpec((TM, d), lambda i: (i, 0)),
        out_shape=jax.ShapeDtypeStruct((t, d), F32),
        compiler_params=_cparams(("parallel",)), name="moe_combine",
    )(yg, w8, h, x, mod, sgu, sd)


def _sparse_moe(x, h, gates_t, sel_t, cnt, mod, layer, w_gate, w_up, w_down, sgu, sd, nbp, bps):
    t, d = x.shape
    ne = N_EXPERTS
    n_pad = ne * TG
    n_tiles = (t * TOPK + n_pad) // TG
    n_slots = n_tiles * TG
    counts = cnt[:, 0, :].astype(I32)
    per_expert = jnp.sum(counts, axis=0)
    padded = (per_expert + TG - 1) // TG * TG
    ends = jnp.cumsum(padded)
    starts = ends - padded
    block_off = jnp.cumsum(counts, axis=0) - counts
    base = (starts[None, :] + block_off).astype(F32)[:, :, None]
    n_valid = (ends[-1] // TG).astype(I32).reshape(1)
    tile_start = jnp.minimum(jnp.arange(n_tiles, dtype=I32), n_valid[0] - 1) * TG
    tile_expert = jnp.sum((ends[None, :] <= tile_start[:, None]).astype(I32), axis=1)
    fresh = jnp.concatenate([jnp.ones((1,), I32),
                             (tile_expert[1:] != tile_expert[:-1]).astype(I32)])

    pos8, w8 = _slots(sel_t, gates_t, base)
    n_extra = -(-n_pad // t)
    cand = (starts + per_expert)[:, None] + jnp.arange(TG, dtype=I32)[None, :]
    spare = n_slots + jnp.arange(n_pad, dtype=I32)
    fill = jnp.where(cand < ends[:, None], cand, spare.reshape(ne, TG)).reshape(-1)
    rest = n_slots + jnp.arange(n_pad, n_extra * t, dtype=I32) % n_pad
    dest = jnp.concatenate([pos8, jnp.concatenate([fill, rest]).reshape(n_extra, t)], axis=0)
    xs = _sc_scatter(h, dest, n_slots + n_pad)
    ys = _expert_ffn(xs, n_tiles, tile_expert, fresh, n_valid, layer, w_gate, w_up, w_down)
    yg = _sc_gather(ys, pos8.reshape(-1)).reshape(TOPK, t, d // 2)
    return _combine(yg, w8, h, x, mod, sgu, sd, nbp, bps)


def _head_rms(x, g_row, hm_ref, hmt_ref):
    sq_hi, sq_lo = _split_bf16(x * x)
    hm = hm_ref[...]
    ss = (jnp.dot(sq_hi, hm, preferred_element_type=F32)
          + jnp.dot(sq_lo, hm, preferred_element_type=F32))
    r = lax.rsqrt(ss * (1.0 / HEAD_DIM) + EPS)
    r_hi, r_lo = _split_bf16(r)
    hmt = hmt_ref[...]
    rb = (jnp.dot(r_hi, hmt, preferred_element_type=F32)
          + jnp.dot(r_lo, hmt, preferred_element_type=F32))
    return (x * rb) * g_row


def _qkv_kernel(emit_f32, x_ref, mod_ref, gmix_ref, w_ref, qg_ref, kg_ref, hm_ref, hmt_ref, *outs):
    d = x_ref.shape[1]
    sh1 = mod_ref[0, 0:1, :]
    sc1 = mod_ref[0, 1:2, :]
    h = _adaln(x_ref[...], gmix_ref[...], sh1, sc1).astype(BF16)
    qkv = jnp.dot(h, w_ref[...], preferred_element_type=F32)
    q = _head_rms(qkv[:, :d], qg_ref[...], hm_ref, hmt_ref)
    k = _head_rms(qkv[:, d:2 * d], kg_ref[...], hm_ref, hmt_ref)
    v = qkv[:, 2 * d:]
    outs[0][...] = q.astype(BF16)
    outs[1][...] = k.astype(BF16)
    outs[2][...] = v.astype(BF16)
    if emit_f32:
        outs[3][...] = k
        outs[4][...] = v


def _qkv(x, mod, g_mix, w_qkv, qg, kg, hm, hmt, blk0, nblk, midx, emit_f32):
    t, d = x.shape
    in_specs = [
        pl.BlockSpec((TM, d), lambda i: (i + blk0, 0)),
        pl.BlockSpec((1, 6, d), lambda i: (midx(i + blk0), 0, 0)),
        _const_spec(g_mix.shape), _const_spec(w_qkv.shape), _const_spec(qg.shape),
        _const_spec(kg.shape), _const_spec(hm.shape), _const_spec(hmt.shape),
    ]
    n_out = 5 if emit_f32 else 3
    out_specs = [pl.BlockSpec((TM, d), lambda i: (i, 0)) for _ in range(n_out)]
    out_shape = [jax.ShapeDtypeStruct((nblk * TM, d), BF16 if o < 3 else F32) for o in range(n_out)]
    return pl.pallas_call(
        functools.partial(_qkv_kernel, emit_f32),
        grid=(nblk,), in_specs=in_specs, out_specs=out_specs, out_shape=out_shape,
        compiler_params=_cparams(("parallel",)), name="qkv_f32" if emit_f32 else "qkv",
    )(x, mod, g_mix, w_qkv, qg, kg, hm, hmt)


def _head_masks():
    lane = lax.broadcasted_iota(I32, (1, 2 * HEAD_DIM), 1)
    return lane < HEAD_DIM


def _ctx_attn_kernel(q_ref, k_ref, v_ref, o_ref):
    lo = _head_masks()
    pw = 2 * HEAD_DIM
    for hp in range(q_ref.shape[1] // pw):
        cols = slice(hp * pw, (hp + 1) * pw)
        q = q_ref[:, cols]
        k = k_ref[:, cols]
        v = v_ref[:, cols]
        outs = []
        for hh in range(2):
            msk = lo if hh == 0 else jnp.logical_not(lo)
            qm = jnp.where(msk, q, jnp.zeros_like(q)) * jnp.asarray(HEAD_DIM ** -0.5, BF16)
            s = lax.dot_general(qm, k, NT_DIMS, preferred_element_type=F32)
            m = jnp.max(s, axis=-1, keepdims=True)
            p = jnp.exp(s - m)
            l = jnp.sum(p, axis=-1, keepdims=True)
            o = jnp.dot(p.astype(BF16), v, preferred_element_type=F32)
            outs.append(o / l)
        o_ref[:, cols] = jnp.where(lo, outs[0], outs[1]).astype(BF16)


def _ctx_attn(q, k, v, nb, s):
    t, d = q.shape
    spec = pl.BlockSpec((s, d), lambda b: (b, 0))
    return pl.pallas_call(
        _ctx_attn_kernel, grid=(nb,), in_specs=[spec, spec, spec], out_specs=spec,
        out_shape=jax.ShapeDtypeStruct((nb * s, d), BF16),
        compiler_params=_cparams(("parallel",)), name="context_attention",
    )(q, k, v)


NA_QROWS = 8
NA_KROWS = 16


def _na_kernel(rows, q_ref, k_ref, v_ref, kc_ref, vc_ref, bias_ref, o_ref):
    rb = pl.program_id(1)
    kr0 = jnp.clip(rb * NA_QROWS - WIN_H // 2, 0, rows - NA_KROWS)
    start = pl.multiple_of(kr0 * GRID_W, 256)
    nk = NA_KROWS * GRID_W
    q = q_ref[0]
    kw = k_ref[0, pl.ds(start, nk), :]
    vw = v_ref[0, pl.ds(start, nk), :]
    kc = kc_ref[0]
    vc = vc_ref[0]
    lo = _head_masks()
    outs = []
    for hh in range(2):
        msk = lo if hh == 0 else jnp.logical_not(lo)
        qm = jnp.where(msk, q, jnp.zeros_like(q)) * jnp.asarray(HEAD_DIM ** -0.5, BF16)
        s = lax.dot_general(qm, kw, NT_DIMS, preferred_element_type=F32) + bias_ref[0, hh]
        sc = lax.dot_general(qm, kc, NT_DIMS, preferred_element_type=F32)
        m = jnp.maximum(jnp.max(s, axis=-1, keepdims=True), jnp.max(sc, axis=-1, keepdims=True))
        p = jnp.exp(s - m)
        pc = jnp.exp(sc - m)
        l = jnp.sum(p, axis=-1, keepdims=True) + jnp.sum(pc, axis=-1, keepdims=True)
        o = (jnp.dot(p.astype(BF16), vw, preferred_element_type=F32)
             + jnp.dot(pc.astype(BF16), vc, preferred_element_type=F32))
        outs.append(o / l)
    o_ref[0] = jnp.where(lo, outs[0], outs[1]).astype(BF16)


def _na_row_classes(rows):
    nrb = rows // NA_QROWS
    out = []
    for rb in (0, 1, nrb - 1):
        r0 = rb * NA_QROWS
        kr0 = int(np.clip(r0 - WIN_H // 2, 0, rows - NA_KROWS))
        table = []
        for rl in range(NA_QROWS):
            r = r0 + rl
            sr = int(np.clip(r - WIN_H // 2, 0, rows - WIN_H))
            table.append([(kr0 + kl - r + WIN_H - 1) if sr <= kr0 + kl < sr + WIN_H else None
                          for kl in range(NA_KROWS)])
        out.append(table)
    return out


def _na_bias_kernel(rows, rpb_ref, o_ref):
    h = pl.program_id(0)
    n_dr, n_dc = 2 * WIN_H - 1, 2 * WIN_W - 1
    cq = lax.broadcasted_iota(I32, (GRID_W, 2 * GRID_W), 0)
    lane = lax.broadcasted_iota(I32, (GRID_W, 2 * GRID_W), 1)
    ck = lane & (GRID_W - 1)
    q_start = jnp.clip(cq - WIN_W // 2, 0, GRID_W - WIN_W)
    col_ok = (ck >= q_start) & (ck < q_start + WIN_W)
    dc = ck - cq + (WIN_W - 1)
    neg = jnp.full((GRID_W, 2 * GRID_W), NEG, F32)
    tiles = []
    for i in range(n_dr):
        t = jnp.zeros((GRID_W, 2 * GRID_W), F32)
        for jj in range(n_dc):
            t = jnp.where(dc == jj, rpb_ref[h * (n_dr * n_dc) + i * n_dc + jj], t)
        tiles.append(jnp.where(col_ok, t, neg))
    first_half = lane < GRID_W
    for c, table in enumerate(_na_row_classes(rows)):
        for rl in range(NA_QROWS):
            for m in range(NA_KROWS // 2):
                ia, ib = table[rl][2 * m], table[rl][2 * m + 1]
                ta = neg if ia is None else tiles[ia]
                tb = neg if ib is None else tiles[ib]
                blk = ta if ia == ib else jnp.where(first_half, ta, tb)
                o_ref[c, 0, rl * GRID_W:(rl + 1) * GRID_W, m * 2 * GRID_W:(m + 1) * 2 * GRID_W] = blk


def _na_bias(rpb, rows):
    nh = rpb.shape[0]
    nq, nk = NA_QROWS * GRID_W, NA_KROWS * GRID_W
    return pl.pallas_call(
        functools.partial(_na_bias_kernel, rows), grid=(nh,),
        in_specs=[pl.BlockSpec(memory_space=pltpu.SMEM)],
        out_specs=pl.BlockSpec((3, 1, nq, nk), lambda h: (0, h, 0, 0)),
        out_shape=jax.ShapeDtypeStruct((3, nh, nq, nk), F32),
        compiler_params=_cparams(("parallel",)), name="na_bias",
    )(rpb.reshape(-1))


def _na_attn(q, k, v, kc, vc, bias_tab, rows):
    nb, s, d = q.shape
    hp = d // (2 * HEAD_DIM)
    nrb = rows // NA_QROWS
    nq = NA_QROWS * GRID_W
    lc = kc.shape[1]

    def cls(r):
        return jnp.where(r == 0, 0, jnp.where(r == nrb - 1, 2, 1))

    in_specs = [
        pl.BlockSpec((1, nq, 2 * HEAD_DIM), lambda h, r, b: (b, r, h)),
        pl.BlockSpec((1, s, 2 * HEAD_DIM), lambda h, r, b: (b, 0, h)),
        pl.BlockSpec((1, s, 2 * HEAD_DIM), lambda h, r, b: (b, 0, h)),
        pl.BlockSpec((1, lc, 2 * HEAD_DIM), lambda h, r, b: (b, 0, h)),
        pl.BlockSpec((1, lc, 2 * HEAD_DIM), lambda h, r, b: (b, 0, h)),
        pl.BlockSpec((1, 2, nq, NA_KROWS * GRID_W), lambda h, r, b: (cls(r), h, 0, 0)),
    ]
    return pl.pallas_call(
        functools.partial(_na_kernel, rows),
        grid=(hp, nrb, nb), in_specs=in_specs,
        out_specs=pl.BlockSpec((1, nq, 2 * HEAD_DIM), lambda h, r, b: (b, r, h)),
        out_shape=jax.ShapeDtypeStruct((nb, s, d), BF16),
        compiler_params=_cparams(("parallel", "parallel", "parallel")),
        name="neighbourhood_attention",
    )(q, k, v, kc, vc, bias_tab)


def _oproj_kernel(a_ref, x_ref, mod_ref, wout_ref, gffn_ref, wr_hi_ref, wr_lo_ref, rb_ref,
                  x1_ref, h_ref, gates_ref, sel_ref, cnt_ref):
    y = jnp.dot(a_ref[...], wout_ref[...], preferred_element_type=F32)
    x1 = x_ref[...] + mod_ref[0, 2:3, :] * y
    x1_ref[...] = x1
    _ffn_pre(x1, mod_ref, gffn_ref, wr_hi_ref, wr_lo_ref, rb_ref,
             h_ref, gates_ref, sel_ref, cnt_ref)


def _oproj(attn, x, mod, w_out, g_ffn, wr_hi, wr_lo, rbias, nbp, bps):
    t, d = x.shape
    midx = _mod_index(nbp, bps)
    in_specs = [
        pl.BlockSpec((TM, d), lambda i: (i, 0)),
        pl.BlockSpec((TM, d), lambda i: (i, 0)),
        pl.BlockSpec((1, 6, d), lambda i: (midx(i), 0, 0)),
        _const_spec(w_out.shape), _const_spec(g_ffn.shape), _const_spec(wr_hi.shape),
        _const_spec(wr_lo.shape), _const_spec(rbias.shape),
    ]
    out_specs, out_shape = _pre_out_specs(t, d)
    return pl.pallas_call(
        _oproj_kernel, grid=(t // TM,), in_specs=in_specs, out_specs=out_specs,
        out_shape=out_shape, compiler_params=_cparams(("parallel",)), name="attn_out_proj",
    )(attn, x, mod, w_out, g_ffn, wr_hi, wr_lo, rbias)


def _router_weights(w_router, router_bias):
    d, ne = w_router.shape
    w = jnp.pad(w_router, ((0, 0), (0, 128 - ne)))
    hi = w.astype(BF16)
    lo = (w - hi.astype(F32)).astype(BF16)
    return hi, lo, router_bias.reshape(ne, 1)


def _shared_weights(s_gate, s_up, s_down):
    return jnp.concatenate([s_gate, s_up], axis=-1).astype(BF16), s_down.astype(BF16)


def _layer_weights(layer, d, norm_mix, norm_ffn, pc, na, moe):
    i = layer // 2
    w = dict(g_mix=norm_mix[layer].reshape(1, d), g_ffn=norm_ffn[layer].reshape(1, d))
    w["wr_hi"], w["wr_lo"], w["rbias"] = _router_weights(moe["router"][layer],
                                                         moe["router_bias"][layer])
    w["sgu"], w["sd"] = _shared_weights(moe["s_gate"][layer], moe["s_up"][layer],
                                        moe["s_down"][layer])
    if layer % 2 == 0:
        w.update(w_in=pc["w_in"][i].astype(BF16), pool_w=pc["pool_w"][i].astype(BF16),
                 pool_scale=pc["pool_scale"][i].reshape(1, -1), conv_w=pc["conv_w"][i],
                 w_out=pc["w_out"][i].astype(BF16))
    else:
        nh = d // HEAD_DIM
        head_of = np.arange(d) // HEAD_DIM
        w.update(hm=jnp.asarray(head_of[:, None] == np.arange(128)[None, :], BF16),
                 hmt=jnp.asarray(np.arange(128)[:, None] == head_of[None, :], BF16),
                 qg=jnp.tile(na["q_norm"][i], nh).reshape(1, d),
                 kg=jnp.tile(na["k_norm"][i], nh).reshape(1, d),
                 w_qkv=na["w_qkv"][i].astype(BF16), w_out=na["w_out"][i].astype(BF16))
    return w


def _trunk(x, mods, nb_p, s_p, nb_s, s_s, cache_k, cache_v, bias_tabs, weights, moe):
    t, d = x.shape
    tp, ts = nb_p * s_p, nb_s * s_s
    nbp, bps = tp // TM, s_s // TM
    nh = d // HEAD_DIM
    rows = s_s // GRID_W
    new_k, new_v = [], []
    for layer, w in enumerate(weights):
        i = layer // 2
        mod = mods[layer]
        if layer % 2 == 0:
            x, h, gates_t, sel_t, cnt = _mixer(
                x, mod, w["g_mix"], w["w_in"], w["pool_w"], w["pool_scale"], w["conv_w"],
                w["w_out"], w["g_ffn"], w["wr_hi"], w["wr_lo"], w["rbias"], nbp, bps, s_p, s_s)
        else:
            midx = _mod_index(nbp, bps)
            qkv_args = (x, mod, w["g_mix"], w["w_qkv"], w["qg"], w["kg"], w["hm"], w["hmt"])
            parts = []
            if nb_p:
                qp, kp, vp, kp32, vp32 = _qkv(*qkv_args, 0, nbp, midx, True)
                new_k.append(kp32.reshape(nb_p, s_p, nh, HEAD_DIM))
                new_v.append(vp32.reshape(nb_p, s_p, nh, HEAD_DIM))
                parts.append(_ctx_attn(qp, kp, vp, nb_p, s_p))
            qs, ks, vs = _qkv(*qkv_args, nbp, ts // TM, midx, False)
            lc = cache_k.shape[2]
            kc = cache_k[:, i].reshape(nb_s, lc, d).astype(BF16)
            vc = cache_v[:, i].reshape(nb_s, lc, d).astype(BF16)
            a_s = _na_attn(qs.reshape(nb_s, s_s, d), ks.reshape(nb_s, s_s, d),
                           vs.reshape(nb_s, s_s, d), kc, vc, bias_tabs[i], rows)
            parts.append(a_s.reshape(ts, d))
            attn = jnp.concatenate(parts, axis=0) if len(parts) > 1 else parts[0]
            x, h, gates_t, sel_t, cnt = _oproj(attn, x, mod, w["w_out"], w["g_ffn"], w["wr_hi"],
                                               w["wr_lo"], w["rbias"], nbp, bps)
        x = _sparse_moe(x, h, gates_t, sel_t, cnt, mod, layer, moe["w_gate"], moe["w_up"],
                        moe["w_down"], w["sgu"], w["sd"], nbp, bps)
    return x, new_k, new_v


def kernel(x_prompt, x_sample, cache_k, cache_v, c, c_ctx, ada_w, ada_b, norm_mix, norm_ffn,
           pc_w_in, pc_pool_w, pc_pool_scale, pc_conv_w, pc_w_out,
           na_w_qkv, na_q_norm, na_k_norm, na_rpb, na_w_out,
           moe_router, moe_router_bias, moe_w_gate, moe_w_up, moe_w_down,
           moe_shared_gate, moe_shared_up, moe_shared_down):
    nb_p, s_p, d = x_prompt.shape
    nb_s, s_s, _ = x_sample.shape
    assert s_p == TM and s_s % TM == 0
    tp = nb_p * s_p
    depth = ada_w.shape[0]
    rows = s_s // GRID_W

    cond = jnp.concatenate([c_ctx[None], c], axis=0)
    cond = jnp.pad(cond, ((0, -cond.shape[0] % 8), (0, 0)))
    mods = _modulation(cond, ada_w, ada_b).reshape(depth, cond.shape[0], 6, d)
    pc = dict(w_in=pc_w_in, pool_w=pc_pool_w, pool_scale=pc_pool_scale, conv_w=pc_conv_w,
              w_out=pc_w_out)
    na = dict(w_qkv=na_w_qkv, q_norm=na_q_norm, k_norm=na_k_norm, w_out=na_w_out)
    moe = dict(router=moe_router, router_bias=moe_router_bias, w_gate=moe_w_gate, w_up=moe_w_up,
               w_down=moe_w_down, s_gate=moe_shared_gate, s_up=moe_shared_up,
               s_down=moe_shared_down)
    weights = [_layer_weights(l, d, norm_mix, norm_ffn, pc, na, moe) for l in range(depth)]
    bias_tabs = [_na_bias(na_rpb[i], rows) for i in range(na_rpb.shape[0])]

    split = max(0, min(nb_s, (nb_s * s_s - tp) // (2 * s_s)))
    xp = x_prompt.reshape(tp, d)
    chains = [(nb_p, 0, split), (0, split, nb_s)] if 0 < split < nb_s else [(nb_p, 0, nb_s)]
    outs = []
    for cp, b0, b1 in chains:
        xs_part = x_sample[b0:b1].reshape((b1 - b0) * s_s, d)
        x = jnp.concatenate([xp, xs_part], axis=0) if cp else xs_part
        mod_rows = np.concatenate([[0], 1 + np.arange(b0, b1)])
        outs.append(_trunk(x, mods[:, mod_rows], cp, s_p, b1 - b0, s_s, cache_k[b0:b1],
                           cache_v[b0:b1], bias_tabs, weights, moe))
    x_a, new_k, new_v = outs[0]
    y_p = x_a[:tp].reshape(nb_p, s_p, d)
    y_s = jnp.concatenate([x_a[tp:]] + [o[0] for o in outs[1:]], axis=0).reshape(nb_s, s_s, d)
    return (y_p, y_s, jnp.stack(new_k, axis=1), jnp.stack(new_v, axis=1))
```

```python
import functools

import numpy as np
import jax
import jax.numpy as jnp
from jax import lax
from jax.experimental import pallas as pl
from jax.experimental.pallas import tpu as pltpu
from jax.experimental.pallas import tpu_sc as plsc

F32 = jnp.float32
BF16 = jnp.bfloat16
I32 = jnp.int32
U32 = jnp.uint32

TM = 256
HALO = 8
POOL_WINDOWS = (2, 4, 8, 16)
N_EXPERTS = 64
N_GROUPS = 8
GROUP_SIZE = N_EXPERTS // N_GROUPS
TOPK_GROUPS = 4
TOPK = 8
ROUTED_SCALE = 2.5
EPS = 1e-6
GRID_W = 64
WIN_H = 8
WIN_W = 16
HEAD_DIM = 64
NEG = float(np.finfo(np.float32).min)
VMEM_LIMIT = 56 * 1024 * 1024
NT_DIMS = (((1,), (1,)), ((), ()))
TG = 1024
SC_WINDOW = 64
SC_WORKERS = 32


def _cparams(sem):
    return pltpu.CompilerParams(dimension_semantics=sem, vmem_limit_bytes=VMEM_LIMIT)


def _silu(x):
    return x * jax.nn.sigmoid(x)


def _split_bf16(x):
    hi = x.astype(BF16)
    lo = (x - hi.astype(F32)).astype(BF16)
    return hi, lo


def _adaln(x, g, shift, scale):
    ms = jnp.mean(x * x, axis=-1, keepdims=True)
    return (x * lax.rsqrt(ms + EPS)) * g * (1.0 + scale) + shift


def _pack_pair(x):
    w = x.shape[1] // 2
    lo = lax.bitcast_convert_type(x[:, :w].astype(BF16).astype(F32), U32) >> 16
    hi = lax.bitcast_convert_type(x[:, w:].astype(BF16).astype(F32), U32)
    return lax.bitcast_convert_type(lo | hi, I32)


def _unpack_pair(p):
    u = lax.bitcast_convert_type(p, U32)
    lo = lax.bitcast_convert_type(u << 16, F32)
    hi = lax.bitcast_convert_type(u & jnp.uint32(0xFFFF0000), F32)
    return jnp.concatenate([lo, hi], axis=-1)


def _mod_kernel(cond_ref, w_ref, b_ref, o_ref):
    c = cond_ref[...]
    a = _silu(c)
    o_ref[0] = jnp.dot(a, w_ref[0], preferred_element_type=F32,
                       precision=lax.Precision.HIGHEST) + b_ref[0]


def _modulation(cond, ada_w, ada_b):
    depth, d, n = ada_w.shape
    rows = cond.shape[0]
    tn = 1536
    return pl.pallas_call(
        _mod_kernel,
        grid=(depth, n // tn),
        in_specs=[
            pl.BlockSpec((rows, d), lambda l, j: (0, 0)),
            pl.BlockSpec((1, d, tn), lambda l, j: (l, 0, j)),
            pl.BlockSpec((1, 1, tn), lambda l, j: (l, 0, j)),
        ],
        out_specs=pl.BlockSpec((1, rows, tn), lambda l, j: (l, 0, j)),
        out_shape=jax.ShapeDtypeStruct((depth, rows, n), F32),
        compiler_params=_cparams(("arbitrary", "arbitrary")),
        name="modulation",
    )(cond, ada_w, ada_b.reshape(depth, 1, n))


def _route(logits_t, bias_col):
    tm = logits_t.shape[1]
    scores = jax.nn.sigmoid(logits_t)
    biased = scores + bias_col
    sub = lax.broadcasted_iota(I32, (GROUP_SIZE, tm), 0).astype(F32)
    ninf = jnp.float32(-jnp.inf)
    groups, gscore = [], []
    for g in range(N_GROUPS):
        v = biased[g * GROUP_SIZE:(g + 1) * GROUP_SIZE]
        m1 = jnp.max(v, axis=0, keepdims=True)
        first = jnp.min(jnp.where(v == m1, sub, float(GROUP_SIZE)), axis=0, keepdims=True)
        m2 = jnp.max(jnp.where(sub == first, ninf, v), axis=0, keepdims=True)
        groups.append(v)
        gscore.append(m1 + m2)
    masked = []
    for g in range(N_GROUPS):
        rank = jnp.zeros((1, tm), I32)
        for g2 in range(N_GROUPS):
            if g2 == g:
                continue
            ahead = gscore[g2] > gscore[g]
            if g2 < g:
                ahead = ahead | (gscore[g2] == gscore[g])
            rank = rank + ahead.astype(I32)
        masked.append(jnp.where(rank < TOPK_GROUPS, groups[g], ninf))
    masked = jnp.concatenate(masked, axis=0)
    eidx = lax.broadcasted_iota(I32, (N_EXPERTS, tm), 0).astype(F32)
    sel = jnp.zeros((N_EXPERTS, tm), jnp.bool_)
    for _ in range(TOPK):
        best = jnp.max(masked, axis=0, keepdims=True)
        first = jnp.min(jnp.where(masked == best, eidx, float(N_EXPERTS)), axis=0, keepdims=True)
        hit = eidx == first
        sel = sel | hit
        masked = jnp.where(hit, ninf, masked)
    w = jnp.where(sel, scores, 0.0)
    wsum = jnp.sum(w, axis=0, keepdims=True)
    return w / wsum * ROUTED_SCALE, sel


def _ffn_pre(x1, mod_ref, gffn_ref, wr_hi_ref, wr_lo_ref, rb_ref,
             h_ref, gates_ref, sel_ref, cnt_ref):
    sh2 = mod_ref[0, 3:4, :]
    sc2 = mod_ref[0, 4:5, :]
    h = _adaln(x1, gffn_ref[...], sh2, sc2)
    h_hi, h_lo = _split_bf16(h)
    h_ref[...] = _pack_pair(h_hi)
    wr_hi = wr_hi_ref[...]
    logits = (jnp.dot(h_hi, wr_hi, preferred_element_type=F32)
              + jnp.dot(h_hi, wr_lo_ref[...], preferred_element_type=F32)
              + jnp.dot(h_lo, wr_hi, preferred_element_type=F32))
    gates_t, sel = _route(logits.T[:N_EXPERTS], rb_ref[...])
    gates_ref[...] = gates_t
    sel_b = sel.astype(F32).astype(BF16)
    sel_ref[...] = sel_b
    ones = jnp.ones((8, sel_b.shape[1]), BF16)
    cnt_ref[0] = lax.dot_general(ones, sel_b, NT_DIMS, preferred_element_type=F32)


def _pre_out_specs(t, d):
    specs = [
        pl.BlockSpec((TM, d), lambda i: (i, 0)),
        pl.BlockSpec((TM, d // 2), lambda i: (i, 0)),
        pl.BlockSpec((N_EXPERTS, TM), lambda i: (0, i)),
        pl.BlockSpec((N_EXPERTS, TM), lambda i: (0, i)),
        pl.BlockSpec((1, 8, N_EXPERTS), lambda i: (i, 0, 0)),
    ]
    shapes = [
        jax.ShapeDtypeStruct((t, d), F32),
        jax.ShapeDtypeStruct((t, d // 2), I32),
        jax.ShapeDtypeStruct((N_EXPERTS, t), F32),
        jax.ShapeDtypeStruct((N_EXPERTS, t), BF16),
        jax.ShapeDtypeStruct((t // TM, 8, N_EXPERTS), F32),
    ]
    return specs, shapes


def _mixer_kernel(nbp, bps, sp, ss,
                  xpc_ref, xsc_ref, xprev_ref, xnext_ref, mod_ref, gmix_ref, win_ref, pw_ref, ps_ref,
                  cw_ref,
                  wout_ref, gffn_ref, wr_hi_ref, wr_lo_ref, rb_ref,
                  x1_ref, h_ref, gates_ref, sel_ref, cnt_ref):
    i = pl.program_id(0)
    is_p = i < nbp
    j = lax.rem(jnp.maximum(i - nbp, 0), bps)
    first = is_p | (j == 0)
    last = is_p | (j == bps - 1)
    base = jnp.where(is_p, 0, j * TM)
    slen = jnp.where(is_p, sp, ss)

    sh1 = mod_ref[0, 0:1, :]
    sc1 = mod_ref[0, 1:2, :]
    g1 = mod_ref[0, 2:3, :]
    xc = jnp.where(is_p, xpc_ref[...], xsc_ref[...])
    x_ext = jnp.concatenate([xprev_ref[...], xc, xnext_ref[...]], axis=0)
    h_ext = _adaln(x_ext, gmix_ref[...], sh1, sc1).astype(BF16)
    u = jnp.dot(h_ext, win_ref[...], preferred_element_type=F32)
    next_ = TM + 2 * HALO
    row = lax.broadcasted_iota(I32, (next_, 1), 0)
    keep = ((row >= HALO) | jnp.logical_not(first)) & ((row < HALO + TM) | jnp.logical_not(last))
    u = jnp.where(keep, u, 0.0)

    dm = u.shape[1] // 4
    ua = u[:, :dm]
    gate_b = u[HALO:HALO + TM, dm:2 * dm]
    z = u[:, 2 * dm:3 * dm] * u[:, 3 * dm:]

    def up(a, k):
        return pltpu.roll(a, next_ - k, 0)

    pos = base + lax.broadcasted_iota(I32, (TM, 1), 0)
    pg = dm // len(POOL_WINDOWS)
    ya = []
    for g, w in enumerate(POOL_WINDOWS):
        e = ua[:, g * pg:(g + 1) * pg]
        acc = e
        span = 1
        while span < w:
            acc = acc + up(acc, span)
            span *= 2
        off = HALO - w // 2
        wsum = (up(acc, off) if off else acc)[:TM]
        lo = jnp.maximum(pos - w // 2, 0)
        hi = jnp.minimum(pos + (w - w // 2 - 1), slen - 1)
        cnt = (hi - lo + 1).astype(F32)
        diff = wsum / cnt - e[HALO:HALO + TM]
        ya.append(jnp.dot(diff.astype(BF16), pw_ref[g], preferred_element_type=F32))
    y_a = jnp.concatenate(ya, axis=-1) * ps_ref[...]
    zc = (cw_ref[0:1, :] * up(z, HALO - 1)[:TM] + cw_ref[1:2, :] * z[HALO:HALO + TM]
          + cw_ref[2:3, :] * up(z, HALO + 1)[:TM])
    y_b = gate_b * zc
    ycat = jnp.concatenate([y_a, y_b], axis=-1).astype(BF16)
    y = jnp.dot(ycat, wout_ref[...], preferred_element_type=F32)
    x1 = xc + g1 * y
    x1_ref[...] = x1
    _ffn_pre(x1, mod_ref, gffn_ref, wr_hi_ref, wr_lo_ref, rb_ref,
             h_ref, gates_ref, sel_ref, cnt_ref)


def _mod_index(nbp, bps):
    def f(i):
        return jnp.where(i < nbp, 0, 1 + jnp.maximum(i - nbp, 0) // bps)
    return f


def _const_spec(shape):
    nd = len(shape)
    return pl.BlockSpec(shape, lambda i: (0,) * nd)


def _mixer(xp, xs, mod, g_mix, w_in, pool_w, pool_scale, conv_w, w_out, g_ffn, wr_hi, wr_lo, rbias,
           nbp, bps, sp, ss):
    d = xp.shape[1]
    t = xp.shape[0] + xs.shape[0]
    nblk = t // TM
    midx = _mod_index(nbp, bps)
    hpb = TM // HALO
    nh = xs.shape[0] // HALO
    in_specs = [
        pl.BlockSpec((TM, d), lambda i: (jnp.minimum(i, nbp - 1), 0)),
        pl.BlockSpec((TM, d), lambda i: (jnp.maximum(i - nbp, 0), 0)),
        pl.BlockSpec((HALO, d), lambda i: (jnp.maximum((i - nbp) * hpb - 1, 0), 0)),
        pl.BlockSpec((HALO, d), lambda i: (jnp.clip((i - nbp + 1) * hpb, 0, nh - 1), 0)),
        pl.BlockSpec((1, 6, d), lambda i: (midx(i), 0, 0)),
        _const_spec(g_mix.shape), _const_spec(w_in.shape), _const_spec(pool_w.shape),
        _const_spec(pool_scale.shape), _const_spec(conv_w.shape), _const_spec(w_out.shape),
        _const_spec(g_ffn.shape), _const_spec(wr_hi.shape), _const_spec(wr_lo.shape),
        _const_spec(rbias.shape),
    ]
    out_specs, out_shape = _pre_out_specs(t, d)
    return pl.pallas_call(
        functools.partial(_mixer_kernel, nbp, bps, sp, ss),
        grid=(nblk,), in_specs=in_specs, out_specs=out_specs, out_shape=out_shape,
        compiler_params=_cparams(("parallel",)), name="pool_conv_mixer",
    )(xp, xs, xs, xs, mod, g_mix, w_in, pool_w, pool_scale, conv_w, w_out, g_ffn, wr_hi, wr_lo,
      rbias)


def _slots_kernel(sel_ref, gates_ref, base_ref, pos_ref, w_ref):
    sel = sel_ref[...]
    tm = sel.shape[1]
    r = lax.broadcasted_iota(I32, (tm, tm), 0)
    c = lax.broadcasted_iota(I32, (tm, tm), 1)
    before = (r < c).astype(F32).astype(BF16)
    rank_tok = jnp.dot(sel, before, preferred_element_type=F32)
    er = lax.broadcasted_iota(I32, (N_EXPERTS, N_EXPERTS), 0)
    ec = lax.broadcasted_iota(I32, (N_EXPERTS, N_EXPERTS), 1)
    lower = (ec < er).astype(F32).astype(BF16)
    rank_exp = jnp.dot(lower, sel, preferred_element_type=F32)
    slot = base_ref[0] + rank_tok
    chosen = sel > 0
    gates = gates_ref[...]
    sub = lax.broadcasted_iota(I32, (TOPK, tm), 0)
    pos8 = jnp.zeros((TOPK, tm), F32)
    w8 = jnp.zeros((TOPK, tm), F32)
    for k in range(TOPK):
        mk = chosen & (rank_exp == float(k))
        pk = jnp.sum(jnp.where(mk, slot, 0.0), axis=0, keepdims=True)
        wk = jnp.sum(jnp.where(mk, gates, 0.0), axis=0, keepdims=True)
        pos8 = jnp.where(sub == k, pk, pos8)
        w8 = jnp.where(sub == k, wk, w8)
    pos_ref[...] = pos8.astype(I32)
    w8 = jnp.concatenate([w8, jnp.zeros((128 - TOPK, tm), F32)], axis=0)
    w_ref[...] = w8.T


def _slots(sel_t, gates_t, base):
    ne, t = sel_t.shape
    return pl.pallas_call(
        _slots_kernel, grid=(t // TM,),
        in_specs=[
            pl.BlockSpec((ne, TM), lambda i: (0, i)),
            pl.BlockSpec((ne, TM), lambda i: (0, i)),
            pl.BlockSpec((1, ne, 1), lambda i: (i, 0, 0)),
        ],
        out_specs=[pl.BlockSpec((TOPK, TM), lambda i: (0, i)),
                   pl.BlockSpec((TM, 128), lambda i: (i, 0))],
        out_shape=[jax.ShapeDtypeStruct((TOPK, t), I32), jax.ShapeDtypeStruct((t, 128), F32)],
        compiler_params=_cparams(("parallel",)), name="moe_slots",
    )(sel_t, gates_t, base)


def _sc_gather(table, idx):
    m = idx.shape[0]
    d = table.shape[1]
    assert m % (SC_WINDOW * SC_WORKERS) == 0
    mesh = plsc.VectorSubcoreMesh(core_axis_name="core", subcore_axis_name="subcore")

    @pl.kernel(out_type=jax.ShapeDtypeStruct((m, d), table.dtype), mesh=mesh)
    def gather_rows(x_hbm, i_hbm, o_hbm):
        def body(i_vmem, o_vmem):
            pltpu.sync_copy(x_hbm.at[i_vmem.at[0]], o_vmem)

        pltpu.emit_pipeline(
            body, grid=(m // SC_WINDOW,),
            in_specs=[pl.BlockSpec((1, SC_WINDOW), index_map=lambda i: (i, 0))],
            out_specs=[pl.BlockSpec((SC_WINDOW, d), index_map=lambda i: (i, 0))],
            core_axis_name=("core", "subcore"),
            dimension_semantics=(pltpu.PARALLEL,),
        )(i_hbm, o_hbm)

    return gather_rows(table, idx.reshape(m // SC_WINDOW, SC_WINDOW))


def _sc_scatter(rows, idx, n_out):
    nk, t = idx.shape
    d = rows.shape[1]
    assert t % (SC_WINDOW * SC_WORKERS) == 0
    mesh = plsc.VectorSubcoreMesh(core_axis_name="core", subcore_axis_name="subcore")

    @pl.kernel(out_type=jax.ShapeDtypeStruct((n_out, d), rows.dtype), mesh=mesh)
    def scatter_rows(x_hbm, i_hbm, o_hbm):
        def body(x_vmem, i_vmem):
            for k in range(nk):
                pltpu.sync_copy(x_vmem, o_hbm.at[i_vmem.at[k, 0]])

        pltpu.emit_pipeline(
            body, grid=(t // SC_WINDOW,),
            in_specs=[pl.BlockSpec((SC_WINDOW, d), index_map=lambda i: (i, 0)),
                      pl.BlockSpec((nk, 1, SC_WINDOW), index_map=lambda i: (0, i, 0))],
            out_specs=[],
            core_axis_name=("core", "subcore"),
            dimension_semantics=(pltpu.PARALLEL,),
        )(x_hbm, i_hbm)

    return scatter_rows(rows, idx.reshape(nk, t // SC_WINDOW, SC_WINDOW))


def _expert_ffn_kernel(te_ref, fresh_ref, nv_ref, x_ref, wg_ref, wu_ref, wd_ref, y_ref,
                       wg_s, wu_s, wd_s):
    j = pl.program_id(0)

    @pl.when(j < nv_ref[0])
    def _():
        @pl.when(fresh_ref[j] == 1)
        def _():
            wg_s[...] = wg_ref[0, 0].astype(BF16)
            wu_s[...] = wu_ref[0, 0].astype(BF16)
            wd_s[...] = wd_ref[0, 0].astype(BF16)

        x = _unpack_pair(x_ref[...]).astype(BF16)
        a = jnp.dot(x, wg_s[...], preferred_element_type=F32)
        b = jnp.dot(x, wu_s[...], preferred_element_type=F32)
        act = (_silu(a) * b).astype(BF16)
        y_ref[...] = _pack_pair(jnp.dot(act, wd_s[...], preferred_element_type=F32))


def _expert_ffn(xs, n_tiles, tile_expert, fresh, n_valid, layer, w_gate, w_up, w_down):
    dh = xs.shape[1]
    last = lambda j, te, fr, nv: jnp.minimum(j, nv[0] - 1)
    wspec = lambda w: pl.BlockSpec((1, 1) + w.shape[2:],
                                   lambda j, te, fr, nv: (layer, te[j], 0, 0))
    grid_spec = pltpu.PrefetchScalarGridSpec(
        num_scalar_prefetch=3, grid=(n_tiles,),
        in_specs=[
            pl.BlockSpec((TG, dh), lambda j, te, fr, nv: (last(j, te, fr, nv), 0)),
            wspec(w_gate), wspec(w_up), wspec(w_down),
        ],
        out_specs=pl.BlockSpec((TG, dh), lambda j, te, fr, nv: (last(j, te, fr, nv), 0)),
        scratch_shapes=[pltpu.VMEM(w_gate.shape[2:], BF16), pltpu.VMEM(w_up.shape[2:], BF16),
                        pltpu.VMEM(w_down.shape[2:], BF16)],
    )
    return pl.pallas_call(
        _expert_ffn_kernel, grid_spec=grid_spec,
        out_shape=jax.ShapeDtypeStruct((n_tiles * TG, dh), I32),
        compiler_params=_cparams(("arbitrary",)), name="expert_ffn",
    )(tile_expert, fresh, n_valid, xs, w_gate, w_up, w_down)


def _combine_kernel(nbp, yg_ref, w_ref, h_ref, x_ref, mod_ref, sgu_ref, sd_ref, *o_refs):
    f = sd_ref.shape[0]
    h = _unpack_pair(h_ref[...]).astype(BF16)
    hs = jnp.dot(h, sgu_ref[...], preferred_element_type=F32)
    act = (_silu(hs[:, :f]) * hs[:, f:]).astype(BF16)
    acc = jnp.dot(act, sd_ref[...], preferred_element_type=F32)
    w = w_ref[...]
    for k in range(TOPK):
        acc = acc + w[:, k:k + 1] * _unpack_pair(yg_ref[k])
    out = x_ref[...] + mod_ref[0, 5:6, :] * acc
    if len(o_refs) == 1:
        o_refs[0][...] = out
    else:
        i = pl.program_id(0)

        @pl.when(i < nbp)
        def _():
            o_refs[0][...] = out

        @pl.when(i >= nbp)
        def _():
            o_refs[1][...] = out


def _combine(yg, w8, h, x, mod, sgu, sd, nbp, bps, split_out):
    t, d = x.shape
    midx = _mod_index(nbp, bps)
    if split_out:
        out_specs = [pl.BlockSpec((TM, d), lambda i: (jnp.minimum(i, nbp - 1), 0)),
                     pl.BlockSpec((TM, d), lambda i: (jnp.maximum(i - nbp, 0), 0))]
        out_shape = [jax.ShapeDtypeStruct((nbp * TM, d), F32),
                     jax.ShapeDtypeStruct((t - nbp * TM, d), F32)]
    else:
        out_specs = pl.BlockSpec((TM, d), lambda i: (i, 0))
        out_shape = jax.ShapeDtypeStruct((t, d), F32)
    in_specs = [
        pl.BlockSpec((TOPK, TM, d // 2), lambda i: (0, i, 0)),
        pl.BlockSpec((TM, 128), lambda i: (i, 0)),
        pl.BlockSpec((TM, d // 2), lambda i: (i, 0)),
        pl.BlockSpec((TM, d), lambda i: (i, 0)),
        pl.BlockSpec((1, 6, d), lambda i: (midx(i), 0, 0)),
        _const_spec(sgu.shape), _const_spec(sd.shape),
    ]
    return pl.pallas_call(
        functools.partial(_combine_kernel, nbp), grid=(t // TM,), in_specs=in_specs,
        out_specs=out_specs, out_shape=out_shape,
        compiler_params=_cparams(("arbitrary",)), name="moe_combine",
    )(yg, w8, h, x, mod, sgu, sd)


def _sparse_moe(x, h, gates_t, sel_t, cnt, mod, layer, w_gate, w_up, w_down, sgu, sd, nbp, bps,
                split_out):
    t, d = x.shape
    ne = N_EXPERTS
    n_pad = ne * TG
    n_tiles = (t * TOPK + n_pad) // TG
    n_slots = n_tiles * TG
    counts = cnt[:, 0, :].astype(I32)
    per_expert = jnp.sum(counts, axis=0)
    padded = (per_expert + TG - 1) // TG * TG
    ends = jnp.cumsum(padded)
    starts = ends - padded
    block_off = jnp.cumsum(counts, axis=0) - counts
    base = (starts[None, :] + block_off).astype(F32)[:, :, None]
    n_valid = (ends[-1] // TG).astype(I32).reshape(1)
    tile_start = jnp.minimum(jnp.arange(n_tiles, dtype=I32), n_valid[0] - 1) * TG
    tile_expert = jnp.sum((ends[None, :] <= tile_start[:, None]).astype(I32), axis=1)
    fresh = jnp.concatenate([jnp.ones((1,), I32),
                             (tile_expert[1:] != tile_expert[:-1]).astype(I32)])

    pos8, w8 = _slots(sel_t, gates_t, base)
    n_extra = -(-n_pad // t)
    cand = (starts + per_expert)[:, None] + jnp.arange(TG, dtype=I32)[None, :]
    spare = n_slots + jnp.arange(n_pad, dtype=I32)
    fill = jnp.where(cand < ends[:, None], cand, spare.reshape(ne, TG)).reshape(-1)
    rest = n_slots + jnp.arange(n_pad, n_extra * t, dtype=I32) % n_pad
    dest = jnp.concatenate([pos8, jnp.concatenate([fill, rest]).reshape(n_extra, t)], axis=0)
    xs = _sc_scatter(h, dest, n_slots + n_pad)
    ys = _expert_ffn(xs, n_tiles, tile_expert, fresh, n_valid, layer, w_gate, w_up, w_down)
    yg = _sc_gather(ys, pos8.reshape(-1)).reshape(TOPK, t, d // 2)
    return _combine(yg, w8, h, x, mod, sgu, sd, nbp, bps, split_out)


def _head_rms(x, g_row, hm_ref, hmt_ref):
    sq_hi, sq_lo = _split_bf16(x * x)
    hm = hm_ref[...]
    ss = (jnp.dot(sq_hi, hm, preferred_element_type=F32)
          + jnp.dot(sq_lo, hm, preferred_element_type=F32))
    r = lax.rsqrt(ss * (1.0 / HEAD_DIM) + EPS)
    r_hi, r_lo = _split_bf16(r)
    hmt = hmt_ref[...]
    rb = (jnp.dot(r_hi, hmt, preferred_element_type=F32)
          + jnp.dot(r_lo, hmt, preferred_element_type=F32))
    return (x * rb) * g_row


def _qkv_kernel(emit_f32, x_ref, mod_ref, gmix_ref, w_ref, qg_ref, kg_ref, hm_ref, hmt_ref, *outs):
    d = x_ref.shape[1]
    sh1 = mod_ref[0, 0:1, :]
    sc1 = mod_ref[0, 1:2, :]
    h = _adaln(x_ref[...], gmix_ref[...], sh1, sc1).astype(BF16)
    qkv = jnp.dot(h, w_ref[...], preferred_element_type=F32)
    q = _head_rms(qkv[:, :d], qg_ref[...], hm_ref, hmt_ref)
    k = _head_rms(qkv[:, d:2 * d], kg_ref[...], hm_ref, hmt_ref)
    v = qkv[:, 2 * d:]
    outs[0][...] = q.astype(BF16)
    outs[1][...] = k.astype(BF16)
    outs[2][...] = v.astype(BF16)
    if emit_f32:
        outs[3][...] = k
        outs[4][...] = v


def _qkv(x, mod, g_mix, w_qkv, qg, kg, hm, hmt, blk0, nblk, midx, emit_f32):
    t, d = x.shape
    in_specs = [
        pl.BlockSpec((TM, d), lambda i: (i + blk0, 0)),
        pl.BlockSpec((1, 6, d), lambda i: (midx(i + blk0), 0, 0)),
        _const_spec(g_mix.shape), _const_spec(w_qkv.shape), _const_spec(qg.shape),
        _const_spec(kg.shape), _const_spec(hm.shape), _const_spec(hmt.shape),
    ]
    n_out = 5 if emit_f32 else 3
    out_specs = [pl.BlockSpec((TM, d), lambda i: (i, 0)) for _ in range(n_out)]
    out_shape = [jax.ShapeDtypeStruct((nblk * TM, d), BF16 if o < 3 else F32) for o in range(n_out)]
    return pl.pallas_call(
        functools.partial(_qkv_kernel, emit_f32),
        grid=(nblk,), in_specs=in_specs, out_specs=out_specs, out_shape=out_shape,
        compiler_params=_cparams(("parallel",)), name="qkv_f32" if emit_f32 else "qkv",
    )(x, mod, g_mix, w_qkv, qg, kg, hm, hmt)


def _head_masks():
    lane = lax.broadcasted_iota(I32, (1, 2 * HEAD_DIM), 1)
    return lane < HEAD_DIM


def _ctx_attn_kernel(q_ref, k_ref, v_ref, o_ref):
    lo = _head_masks()
    pw = 2 * HEAD_DIM
    for hp in range(q_ref.shape[1] // pw):
        cols = slice(hp * pw, (hp + 1) * pw)
        q = q_ref[:, cols]
        k = k_ref[:, cols]
        v = v_ref[:, cols]
        outs = []
        for hh in range(2):
            msk = lo if hh == 0 else jnp.logical_not(lo)
            qm = jnp.where(msk, q, jnp.zeros_like(q)) * jnp.asarray(HEAD_DIM ** -0.5, BF16)
            s = lax.dot_general(qm, k, NT_DIMS, preferred_element_type=F32)
            m = jnp.max(s, axis=-1, keepdims=True)
            p = jnp.exp(s - m)
            l = jnp.sum(p, axis=-1, keepdims=True)
            o = jnp.dot(p.astype(BF16), v, preferred_element_type=F32)
            outs.append(o / l)
        o_ref[:, cols] = jnp.where(lo, outs[0], outs[1]).astype(BF16)


def _ctx_attn(q, k, v, nb, s):
    t, d = q.shape
    spec = pl.BlockSpec((s, d), lambda b: (b, 0))
    return pl.pallas_call(
        _ctx_attn_kernel, grid=(nb,), in_specs=[spec, spec, spec], out_specs=spec,
        out_shape=jax.ShapeDtypeStruct((nb * s, d), BF16),
        compiler_params=_cparams(("parallel",)), name="context_attention",
    )(q, k, v)


NA_QROWS = 8
NA_KROWS = 16


def _na_kernel(rows, q_ref, k_ref, v_ref, kc_ref, vc_ref, bias_ref, o_ref):
    rb = pl.program_id(1)
    kr0 = jnp.clip(rb * NA_QROWS - WIN_H // 2, 0, rows - NA_KROWS)
    start = pl.multiple_of(kr0 * GRID_W, 256)
    nk = NA_KROWS * GRID_W
    q = q_ref[0]
    kw = k_ref[0, pl.ds(start, nk), :]
    vw = v_ref[0, pl.ds(start, nk), :]
    kc = kc_ref[0]
    vc = vc_ref[0]
    lo = _head_masks()
    outs = []
    for hh in range(2):
        msk = lo if hh == 0 else jnp.logical_not(lo)
        qm = jnp.where(msk, q, jnp.zeros_like(q)) * jnp.asarray(HEAD_DIM ** -0.5, BF16)
        s = lax.dot_general(qm, kw, NT_DIMS, preferred_element_type=F32) + bias_ref[0, hh]
        sc = lax.dot_general(qm, kc, NT_DIMS, preferred_element_type=F32)
        m = jnp.maximum(jnp.max(s, axis=-1, keepdims=True), jnp.max(sc, axis=-1, keepdims=True))
        p = jnp.exp(s - m)
        pc = jnp.exp(sc - m)
        l = jnp.sum(p, axis=-1, keepdims=True) + jnp.sum(pc, axis=-1, keepdims=True)
        o = (jnp.dot(p.astype(BF16), vw, preferred_element_type=F32)
             + jnp.dot(pc.astype(BF16), vc, preferred_element_type=F32))
        outs.append(o / l)
    o_ref[0] = jnp.where(lo, outs[0], outs[1]).astype(BF16)


def _na_row_classes(rows):
    nrb = rows // NA_QROWS
    out = []
    for rb in (0, 1, nrb - 1):
        r0 = rb * NA_QROWS
        kr0 = int(np.clip(r0 - WIN_H // 2, 0, rows - NA_KROWS))
        table = []
        for rl in range(NA_QROWS):
            r = r0 + rl
            sr = int(np.clip(r - WIN_H // 2, 0, rows - WIN_H))
            table.append([(kr0 + kl - r + WIN_H - 1) if sr <= kr0 + kl < sr + WIN_H else None
                          for kl in range(NA_KROWS)])
        out.append(table)
    return out


def _na_bias_kernel(rows, rpb_ref, o_ref):
    h = pl.program_id(0)
    n_dr, n_dc = 2 * WIN_H - 1, 2 * WIN_W - 1
    cq = lax.broadcasted_iota(I32, (GRID_W, 2 * GRID_W), 0)
    lane = lax.broadcasted_iota(I32, (GRID_W, 2 * GRID_W), 1)
    ck = lane & (GRID_W - 1)
    q_start = jnp.clip(cq - WIN_W // 2, 0, GRID_W - WIN_W)
    col_ok = (ck >= q_start) & (ck < q_start + WIN_W)
    dc = ck - cq + (WIN_W - 1)
    neg = jnp.full((GRID_W, 2 * GRID_W), NEG, F32)
    tiles = []
    for i in range(n_dr):
        t = jnp.zeros((GRID_W, 2 * GRID_W), F32)
        for jj in range(n_dc):
            t = jnp.where(dc == jj, rpb_ref[h * (n_dr * n_dc) + i * n_dc + jj], t)
        tiles.append(jnp.where(col_ok, t, neg))
    first_half = lane < GRID_W
    for c, table in enumerate(_na_row_classes(rows)):
        for rl in range(NA_QROWS):
            for m in range(NA_KROWS // 2):
                ia, ib = table[rl][2 * m], table[rl][2 * m + 1]
                ta = neg if ia is None else tiles[ia]
                tb = neg if ib is None else tiles[ib]
                blk = ta if ia == ib else jnp.where(first_half, ta, tb)
                o_ref[c, 0, rl * GRID_W:(rl + 1) * GRID_W, m * 2 * GRID_W:(m + 1) * 2 * GRID_W] = blk


def _na_bias(rpb, rows):
    nh = rpb.shape[0]
    nq, nk = NA_QROWS * GRID_W, NA_KROWS * GRID_W
    return pl.pallas_call(
        functools.partial(_na_bias_kernel, rows), grid=(nh,),
        in_specs=[pl.BlockSpec(memory_space=pltpu.SMEM)],
        out_specs=pl.BlockSpec((3, 1, nq, nk), lambda h: (0, h, 0, 0)),
        out_shape=jax.ShapeDtypeStruct((3, nh, nq, nk), F32),
        compiler_params=_cparams(("parallel",)), name="na_bias",
    )(rpb.reshape(-1))


def _na_attn(q, k, v, kc, vc, bias_tab, rows):
    nb, s, d = q.shape
    hp = d // (2 * HEAD_DIM)
    nrb = rows // NA_QROWS
    nq = NA_QROWS * GRID_W
    lc = kc.shape[1]

    def cls(r):
        return jnp.where(r == 0, 0, jnp.where(r == nrb - 1, 2, 1))

    in_specs = [
        pl.BlockSpec((1, nq, 2 * HEAD_DIM), lambda h, r, b: (b, r, h)),
        pl.BlockSpec((1, s, 2 * HEAD_DIM), lambda h, r, b: (b, 0, h)),
        pl.BlockSpec((1, s, 2 * HEAD_DIM), lambda h, r, b: (b, 0, h)),
        pl.BlockSpec((1, lc, 2 * HEAD_DIM), lambda h, r, b: (b, 0, h)),
        pl.BlockSpec((1, lc, 2 * HEAD_DIM), lambda h, r, b: (b, 0, h)),
        pl.BlockSpec((1, 2, nq, NA_KROWS * GRID_W), lambda h, r, b: (cls(r), h, 0, 0)),
    ]
    return pl.pallas_call(
        functools.partial(_na_kernel, rows),
        grid=(hp, nrb, nb), in_specs=in_specs,
        out_specs=pl.BlockSpec((1, nq, 2 * HEAD_DIM), lambda h, r, b: (b, r, h)),
        out_shape=jax.ShapeDtypeStruct((nb, s, d), BF16),
        compiler_params=_cparams(("parallel", "parallel", "parallel")),
        name="neighbourhood_attention",
    )(q, k, v, kc, vc, bias_tab)


def _oproj_kernel(nbp, ap_ref, as_ref, x_ref, mod_ref, wout_ref, gffn_ref, wr_hi_ref, wr_lo_ref,
                  rb_ref, x1_ref, h_ref, gates_ref, sel_ref, cnt_ref):
    a = jnp.where(pl.program_id(0) < nbp, ap_ref[...], as_ref[...])
    y = jnp.dot(a, wout_ref[...], preferred_element_type=F32)
    x1 = x_ref[...] + mod_ref[0, 2:3, :] * y
    x1_ref[...] = x1
    _ffn_pre(x1, mod_ref, gffn_ref, wr_hi_ref, wr_lo_ref, rb_ref,
             h_ref, gates_ref, sel_ref, cnt_ref)


def _oproj(attn_p, attn_s, x, mod, w_out, g_ffn, wr_hi, wr_lo, rbias, nbp, bps):
    t, d = x.shape
    midx = _mod_index(nbp, bps)
    in_specs = [
        pl.BlockSpec((TM, d), lambda i: (jnp.minimum(i, nbp - 1), 0)),
        pl.BlockSpec((TM, d), lambda i: (jnp.maximum(i - nbp, 0), 0)),
        pl.BlockSpec((TM, d), lambda i: (i, 0)),
        pl.BlockSpec((1, 6, d), lambda i: (midx(i), 0, 0)),
        _const_spec(w_out.shape), _const_spec(g_ffn.shape), _const_spec(wr_hi.shape),
        _const_spec(wr_lo.shape), _const_spec(rbias.shape),
    ]
    out_specs, out_shape = _pre_out_specs(t, d)
    return pl.pallas_call(
        functools.partial(_oproj_kernel, nbp), grid=(t // TM,), in_specs=in_specs,
        out_specs=out_specs, out_shape=out_shape, compiler_params=_cparams(("parallel",)),
        name="attn_out_proj",
    )(attn_p, attn_s, x, mod, w_out, g_ffn, wr_hi, wr_lo, rbias)


def _router_weights(w_router, router_bias):
    d, ne = w_router.shape
    w = jnp.pad(w_router, ((0, 0), (0, 128 - ne)))
    hi = w.astype(BF16)
    lo = (w - hi.astype(F32)).astype(BF16)
    return hi, lo, router_bias.reshape(ne, 1)


def _shared_weights(s_gate, s_up, s_down):
    return jnp.concatenate([s_gate, s_up], axis=-1).astype(BF16), s_down.astype(BF16)


def _layer_weights(layer, d, norm_mix, norm_ffn, pc, na, moe):
    i = layer // 2
    w = dict(g_mix=norm_mix[layer].reshape(1, d), g_ffn=norm_ffn[layer].reshape(1, d))
    w["wr_hi"], w["wr_lo"], w["rbias"] = _router_weights(moe["router"][layer],
                                                         moe["router_bias"][layer])
    w["sgu"], w["sd"] = _shared_weights(moe["s_gate"][layer], moe["s_up"][layer],
                                        moe["s_down"][layer])
    if layer % 2 == 0:
        w.update(w_in=pc["w_in"][i].astype(BF16), pool_w=pc["pool_w"][i].astype(BF16),
                 pool_scale=pc["pool_scale"][i].reshape(1, -1), conv_w=pc["conv_w"][i],
                 w_out=pc["w_out"][i].astype(BF16))
    else:
        nh = d // HEAD_DIM
        head_of = np.arange(d) // HEAD_DIM
        w.update(hm=jnp.asarray(head_of[:, None] == np.arange(128)[None, :], BF16),
                 hmt=jnp.asarray(np.arange(128)[:, None] == head_of[None, :], BF16),
                 qg=jnp.tile(na["q_norm"][i], nh).reshape(1, d),
                 kg=jnp.tile(na["k_norm"][i], nh).reshape(1, d),
                 w_qkv=na["w_qkv"][i].astype(BF16), w_out=na["w_out"][i].astype(BF16))
    return w


def kernel(x_prompt, x_sample, cache_k, cache_v, c, c_ctx, ada_w, ada_b, norm_mix, norm_ffn,
           pc_w_in, pc_pool_w, pc_pool_scale, pc_conv_w, pc_w_out,
           na_w_qkv, na_q_norm, na_k_norm, na_rpb, na_w_out,
           moe_router, moe_router_bias, moe_w_gate, moe_w_up, moe_w_down,
           moe_shared_gate, moe_shared_up, moe_shared_down):
    nb_p, s_p, d = x_prompt.shape
    nb_s, s_s, _ = x_sample.shape
    assert s_p == TM and s_s % TM == 0 and nb_p > 0 and nb_s > 0
    tp, ts = nb_p * s_p, nb_s * s_s
    nbp, bps = tp // TM, s_s // TM
    depth = ada_w.shape[0]
    nh = d // HEAD_DIM
    rows = s_s // GRID_W

    cond = jnp.concatenate([c_ctx[None], c], axis=0)
    cond = jnp.pad(cond, ((0, -cond.shape[0] % 8), (0, 0)))
    mods = _modulation(cond, ada_w, ada_b).reshape(depth, cond.shape[0], 6, d)
    pc = dict(w_in=pc_w_in, pool_w=pc_pool_w, pool_scale=pc_pool_scale, conv_w=pc_conv_w,
              w_out=pc_w_out)
    na = dict(w_qkv=na_w_qkv, q_norm=na_q_norm, k_norm=na_k_norm, w_out=na_w_out)
    moe = dict(router=moe_router, router_bias=moe_router_bias, s_gate=moe_shared_gate,
               s_up=moe_shared_up, s_down=moe_shared_down)

    xp, xs = x_prompt.reshape(tp, d), x_sample.reshape(ts, d)
    x = None
    new_k, new_v = [], []
    for layer in range(depth):
        i = layer // 2
        w = _layer_weights(layer, d, norm_mix, norm_ffn, pc, na, moe)
        mod = mods[layer]
        if layer % 2 == 0:
            if x is not None:
                xp, xs = x[:tp], x[tp:]
            x, h, gates_t, sel_t, cnt = _mixer(
                xp, xs, mod, w["g_mix"], w["w_in"], w["pool_w"], w["pool_scale"], w["conv_w"],
                w["w_out"], w["g_ffn"], w["wr_hi"], w["wr_lo"], w["rbias"], nbp, bps, s_p, s_s)
        else:
            midx = _mod_index(nbp, bps)
            qkv_args = (x, mod, w["g_mix"], w["w_qkv"], w["qg"], w["kg"], w["hm"], w["hmt"])
            qp, kp, vp, kp32, vp32 = _qkv(*qkv_args, 0, nbp, midx, True)
            qs, ks, vs = _qkv(*qkv_args, nbp, ts // TM, midx, False)
            new_k.append(kp32.reshape(nb_p, s_p, nh, HEAD_DIM))
            new_v.append(vp32.reshape(nb_p, s_p, nh, HEAD_DIM))
            a_p = _ctx_attn(qp, kp, vp, nb_p, s_p)
            lc = cache_k.shape[2]
            kc = cache_k[:, i].reshape(nb_s, lc, d).astype(BF16)
            vc = cache_v[:, i].reshape(nb_s, lc, d).astype(BF16)
            a_s = _na_attn(qs.reshape(nb_s, s_s, d), ks.reshape(nb_s, s_s, d),
                           vs.reshape(nb_s, s_s, d), kc, vc, _na_bias(na_rpb[i], rows), rows)
            x, h, gates_t, sel_t, cnt = _oproj(a_p, a_s.reshape(ts, d), x, mod, w["w_out"],
                                               w["g_ffn"], w["wr_hi"], w["wr_lo"], w["rbias"],
                                               nbp, bps)
        x = _sparse_moe(x, h, gates_t, sel_t, cnt, mod, layer, moe_w_gate, moe_w_up, moe_w_down,
                        w["sgu"], w["sd"], nbp, bps, split_out=layer == depth - 1)
    y_p, y_s = x
    return (y_p.reshape(nb_p, s_p, d), y_s.reshape(nb_s, s_s, d),
            jnp.stack(new_k, axis=1), jnp.stack(new_v, axis=1))
```

```python
import functools

import numpy as np
import jax
import jax.numpy as jnp
from jax import lax
from jax.experimental import pallas as pl
from jax.experimental.pallas import tpu as pltpu
from jax.experimental.pallas import tpu_sc as plsc

F32 = jnp.float32
BF16 = jnp.bfloat16
I32 = jnp.int32
U32 = jnp.uint32

TM = 256
HALO = 8
POOL_WINDOWS = (2, 4, 8, 16)
N_EXPERTS = 64
N_GROUPS = 8
GROUP_SIZE = N_EXPERTS // N_GROUPS
TOPK_GROUPS = 4
TOPK = 8
ROUTED_SCALE = 2.5
EPS = 1e-6
GRID_W = 64
WIN_H = 8
WIN_W = 16
HEAD_DIM = 64
NEG = float(np.finfo(np.float32).min)
VMEM_LIMIT = 56 * 1024 * 1024
NT_DIMS = (((1,), (1,)), ((), ()))
TG = 1024
SC_WINDOW = 64
SC_WORKERS = 32


def _cparams(sem):
    return pltpu.CompilerParams(dimension_semantics=sem, vmem_limit_bytes=VMEM_LIMIT)


def _silu(x):
    return x * jax.nn.sigmoid(x)


def _split_bf16(x):
    hi = x.astype(BF16)
    lo = (x - hi.astype(F32)).astype(BF16)
    return hi, lo


def _adaln(x, g, shift, scale):
    ms = jnp.mean(x * x, axis=-1, keepdims=True)
    return (x * lax.rsqrt(ms + EPS)) * g * (1.0 + scale) + shift


def _pack_pair(x):
    w = x.shape[1] // 2
    lo = lax.bitcast_convert_type(x[:, :w].astype(BF16).astype(F32), U32) >> 16
    hi = lax.bitcast_convert_type(x[:, w:].astype(BF16).astype(F32), U32)
    return lax.bitcast_convert_type(lo | hi, I32)


def _unpack_pair(p):
    u = lax.bitcast_convert_type(p, U32)
    lo = lax.bitcast_convert_type(u << 16, F32)
    hi = lax.bitcast_convert_type(u & jnp.uint32(0xFFFF0000), F32)
    return jnp.concatenate([lo, hi], axis=-1)


def _mod_kernel(cond_ref, w_ref, b_ref, o_ref):
    c = cond_ref[...]
    a = _silu(c)
    o_ref[0] = jnp.dot(a, w_ref[0], preferred_element_type=F32,
                       precision=lax.Precision.HIGHEST) + b_ref[0]


def _modulation(cond, ada_w, ada_b):
    depth, d, n = ada_w.shape
    rows = cond.shape[0]
    tn = 1536
    return pl.pallas_call(
        _mod_kernel,
        grid=(depth, n // tn),
        in_specs=[
            pl.BlockSpec((rows, d), lambda l, j: (0, 0)),
            pl.BlockSpec((1, d, tn), lambda l, j: (l, 0, j)),
            pl.BlockSpec((1, 1, tn), lambda l, j: (l, 0, j)),
        ],
        out_specs=pl.BlockSpec((1, rows, tn), lambda l, j: (l, 0, j)),
        out_shape=jax.ShapeDtypeStruct((depth, rows, n), F32),
        compiler_params=_cparams(("arbitrary", "arbitrary")),
        name="modulation",
    )(cond, ada_w, ada_b.reshape(depth, 1, n))


def _route(logits_t, bias_col):
    tm = logits_t.shape[1]
    scores = jax.nn.sigmoid(logits_t)
    biased = scores + bias_col
    sub = lax.broadcasted_iota(I32, (GROUP_SIZE, tm), 0).astype(F32)
    ninf = jnp.float32(-jnp.inf)
    groups, gscore = [], []
    for g in range(N_GROUPS):
        v = biased[g * GROUP_SIZE:(g + 1) * GROUP_SIZE]
        m1 = jnp.max(v, axis=0, keepdims=True)
        first = jnp.min(jnp.where(v == m1, sub, float(GROUP_SIZE)), axis=0, keepdims=True)
        m2 = jnp.max(jnp.where(sub == first, ninf, v), axis=0, keepdims=True)
        groups.append(v)
        gscore.append(m1 + m2)
    masked = []
    for g in range(N_GROUPS):
        rank = jnp.zeros((1, tm), I32)
        for g2 in range(N_GROUPS):
            if g2 == g:
                continue
            ahead = gscore[g2] > gscore[g]
            if g2 < g:
                ahead = ahead | (gscore[g2] == gscore[g])
            rank = rank + ahead.astype(I32)
        masked.append(jnp.where(rank < TOPK_GROUPS, groups[g], ninf))
    masked = jnp.concatenate(masked, axis=0)
    eidx = lax.broadcasted_iota(I32, (N_EXPERTS, tm), 0).astype(F32)
    sel = jnp.zeros((N_EXPERTS, tm), jnp.bool_)
    for _ in range(TOPK):
        best = jnp.max(masked, axis=0, keepdims=True)
        first = jnp.min(jnp.where(masked == best, eidx, float(N_EXPERTS)), axis=0, keepdims=True)
        hit = eidx == first
        sel = sel | hit
        masked = jnp.where(hit, ninf, masked)
    w = jnp.where(sel, scores, 0.0)
    wsum = jnp.sum(w, axis=0, keepdims=True)
    return w / wsum * ROUTED_SCALE, sel


def _ffn_pre(x1, mod_ref, gffn_ref, wr_hi_ref, wr_lo_ref, rb_ref,
             h_ref, gates_ref, sel_ref, cnt_ref):
    sh2 = mod_ref[0, 3:4, :]
    sc2 = mod_ref[0, 4:5, :]
    h = _adaln(x1, gffn_ref[...], sh2, sc2)
    h_hi, h_lo = _split_bf16(h)
    h_ref[...] = _pack_pair(h_hi)
    wr_hi = wr_hi_ref[...]
    logits = (jnp.dot(h_hi, wr_hi, preferred_element_type=F32)
              + jnp.dot(h_hi, wr_lo_ref[...], preferred_element_type=F32)
              + jnp.dot(h_lo, wr_hi, preferred_element_type=F32))
    gates_t, sel = _route(logits.T[:N_EXPERTS], rb_ref[...])
    gates_ref[...] = gates_t
    sel_b = sel.astype(F32).astype(BF16)
    sel_ref[...] = sel_b
    ones = jnp.ones((8, sel_b.shape[1]), BF16)
    cnt_ref[0] = lax.dot_general(ones, sel_b, NT_DIMS, preferred_element_type=F32)


def _pre_out_specs(t, d):
    specs = [
        pl.BlockSpec((TM, d), lambda i: (i, 0)),
        pl.BlockSpec((TM, d // 2), lambda i: (i, 0)),
        pl.BlockSpec((N_EXPERTS, TM), lambda i: (0, i)),
        pl.BlockSpec((N_EXPERTS, TM), lambda i: (0, i)),
        pl.BlockSpec((1, 8, N_EXPERTS), lambda i: (i, 0, 0)),
    ]
    shapes = [
        jax.ShapeDtypeStruct((t, d), F32),
        jax.ShapeDtypeStruct((t, d // 2), I32),
        jax.ShapeDtypeStruct((N_EXPERTS, t), F32),
        jax.ShapeDtypeStruct((N_EXPERTS, t), BF16),
        jax.ShapeDtypeStruct((t // TM, 8, N_EXPERTS), F32),
    ]
    return specs, shapes


def _mixer_kernel(nbp, bps, sp, ss,
                  xpc_ref, xsc_ref, xprev_ref, xnext_ref, mod_ref, gmix_ref, win_ref, pw_ref, ps_ref,
                  cw_ref,
                  wout_ref, gffn_ref, wr_hi_ref, wr_lo_ref, rb_ref,
                  x1_ref, h_ref, gates_ref, sel_ref, cnt_ref):
    i = pl.program_id(0)
    is_p = i < nbp
    j = lax.rem(jnp.maximum(i - nbp, 0), bps)
    first = is_p | (j == 0)
    last = is_p | (j == bps - 1)
    base = jnp.where(is_p, 0, j * TM)
    slen = jnp.where(is_p, sp, ss)

    sh1 = mod_ref[0, 0:1, :]
    sc1 = mod_ref[0, 1:2, :]
    g1 = mod_ref[0, 2:3, :]
    xc = jnp.where(is_p, xpc_ref[...], xsc_ref[...])
    x_ext = jnp.concatenate([xprev_ref[...], xc, xnext_ref[...]], axis=0)
    h_ext = _adaln(x_ext, gmix_ref[...], sh1, sc1).astype(BF16)
    u = jnp.dot(h_ext, win_ref[...], preferred_element_type=F32)
    next_ = TM + 2 * HALO
    row = lax.broadcasted_iota(I32, (next_, 1), 0)
    keep = ((row >= HALO) | jnp.logical_not(first)) & ((row < HALO + TM) | jnp.logical_not(last))
    u = jnp.where(keep, u, 0.0)

    dm = u.shape[1] // 4
    ua = u[:, :dm]
    gate_b = u[HALO:HALO + TM, dm:2 * dm]
    z = u[:, 2 * dm:3 * dm] * u[:, 3 * dm:]

    def up(a, k):
        return pltpu.roll(a, next_ - k, 0)

    pos = base + lax.broadcasted_iota(I32, (TM, 1), 0)
    pg = dm // len(POOL_WINDOWS)
    ya = []
    for g, w in enumerate(POOL_WINDOWS):
        e = ua[:, g * pg:(g + 1) * pg]
        acc = e
        span = 1
        while span < w:
            acc = acc + up(acc, span)
            span *= 2
        off = HALO - w // 2
        wsum = (up(acc, off) if off else acc)[:TM]
        lo = jnp.maximum(pos - w // 2, 0)
        hi = jnp.minimum(pos + (w - w // 2 - 1), slen - 1)
        cnt = (hi - lo + 1).astype(F32)
        diff = wsum / cnt - e[HALO:HALO + TM]
        ya.append(jnp.dot(diff.astype(BF16), pw_ref[g], preferred_element_type=F32))
    y_a = jnp.concatenate(ya, axis=-1) * ps_ref[...]
    zc = (cw_ref[0:1, :] * up(z, HALO - 1)[:TM] + cw_ref[1:2, :] * z[HALO:HALO + TM]
          + cw_ref[2:3, :] * up(z, HALO + 1)[:TM])
    y_b = gate_b * zc
    ycat = jnp.concatenate([y_a, y_b], axis=-1).astype(BF16)
    y = jnp.dot(ycat, wout_ref[...], preferred_element_type=F32)
    x1 = xc + g1 * y
    x1_ref[...] = x1
    _ffn_pre(x1, mod_ref, gffn_ref, wr_hi_ref, wr_lo_ref, rb_ref,
             h_ref, gates_ref, sel_ref, cnt_ref)


def _mod_index(nbp, bps):
    def f(i):
        return jnp.where(i < nbp, 0, 1 + jnp.maximum(i - nbp, 0) // bps)
    return f


def _const_spec(shape):
    nd = len(shape)
    return pl.BlockSpec(shape, lambda i: (0,) * nd)


def _mixer(xp, xs, mod, g_mix, w_in, pool_w, pool_scale, conv_w, w_out, g_ffn, wr_hi, wr_lo, rbias,
           nbp, bps, sp, ss):
    d = xp.shape[1]
    t = xp.shape[0] + xs.shape[0]
    nblk = t // TM
    midx = _mod_index(nbp, bps)
    hpb = TM // HALO
    nh = xs.shape[0] // HALO
    in_specs = [
        pl.BlockSpec((TM, d), lambda i: (jnp.minimum(i, nbp - 1), 0)),
        pl.BlockSpec((TM, d), lambda i: (jnp.maximum(i - nbp, 0), 0)),
        pl.BlockSpec((HALO, d), lambda i: (jnp.maximum((i - nbp) * hpb - 1, 0), 0)),
        pl.BlockSpec((HALO, d), lambda i: (jnp.clip((i - nbp + 1) * hpb, 0, nh - 1), 0)),
        pl.BlockSpec((1, 6, d), lambda i: (midx(i), 0, 0)),
        _const_spec(g_mix.shape), _const_spec(w_in.shape), _const_spec(pool_w.shape),
        _const_spec(pool_scale.shape), _const_spec(conv_w.shape), _const_spec(w_out.shape),
        _const_spec(g_ffn.shape), _const_spec(wr_hi.shape), _const_spec(wr_lo.shape),
        _const_spec(rbias.shape),
    ]
    out_specs, out_shape = _pre_out_specs(t, d)
    return pl.pallas_call(
        functools.partial(_mixer_kernel, nbp, bps, sp, ss),
        grid=(nblk,), in_specs=in_specs, out_specs=out_specs, out_shape=out_shape,
        compiler_params=_cparams(("parallel",)), name="pool_conv_mixer",
    )(xp, xs, xs, xs, mod, g_mix, w_in, pool_w, pool_scale, conv_w, w_out, g_ffn, wr_hi, wr_lo,
      rbias)


def _slots_kernel(sel_ref, gates_ref, base_ref, pos_ref, w_ref):
    sel = sel_ref[...]
    tm = sel.shape[1]
    r = lax.broadcasted_iota(I32, (tm, tm), 0)
    c = lax.broadcasted_iota(I32, (tm, tm), 1)
    before = (r < c).astype(F32).astype(BF16)
    rank_tok = jnp.dot(sel, before, preferred_element_type=F32)
    er = lax.broadcasted_iota(I32, (N_EXPERTS, N_EXPERTS), 0)
    ec = lax.broadcasted_iota(I32, (N_EXPERTS, N_EXPERTS), 1)
    lower = (ec < er).astype(F32).astype(BF16)
    rank_exp = jnp.dot(lower, sel, preferred_element_type=F32)
    slot = base_ref[0] + rank_tok
    chosen = sel > 0
    gates = gates_ref[...]
    sub = lax.broadcasted_iota(I32, (TOPK, tm), 0)
    pos8 = jnp.zeros((TOPK, tm), F32)
    w8 = jnp.zeros((TOPK, tm), F32)
    for k in range(TOPK):
        mk = chosen & (rank_exp == float(k))
        pk = jnp.sum(jnp.where(mk, slot, 0.0), axis=0, keepdims=True)
        wk = jnp.sum(jnp.where(mk, gates, 0.0), axis=0, keepdims=True)
        pos8 = jnp.where(sub == k, pk, pos8)
        w8 = jnp.where(sub == k, wk, w8)
    pos_ref[...] = pos8.astype(I32)
    w8 = jnp.concatenate([w8, jnp.zeros((128 - TOPK, tm), F32)], axis=0)
    w_ref[...] = w8.T


def _slots(sel_t, gates_t, base):
    ne, t = sel_t.shape
    return pl.pallas_call(
        _slots_kernel, grid=(t // TM,),
        in_specs=[
            pl.BlockSpec((ne, TM), lambda i: (0, i)),
            pl.BlockSpec((ne, TM), lambda i: (0, i)),
            pl.BlockSpec((1, ne, 1), lambda i: (i, 0, 0)),
        ],
        out_specs=[pl.BlockSpec((TOPK, TM), lambda i: (0, i)),
                   pl.BlockSpec((TM, 128), lambda i: (i, 0))],
        out_shape=[jax.ShapeDtypeStruct((TOPK, t), I32), jax.ShapeDtypeStruct((t, 128), F32)],
        compiler_params=_cparams(("parallel",)), name="moe_slots",
    )(sel_t, gates_t, base)


def _sc_gather(table, idx):
    m = idx.shape[0]
    d = table.shape[1]
    assert m % (SC_WINDOW * SC_WORKERS) == 0
    mesh = plsc.VectorSubcoreMesh(core_axis_name="core", subcore_axis_name="subcore")

    @pl.kernel(out_type=jax.ShapeDtypeStruct((m, d), table.dtype), mesh=mesh)
    def gather_rows(x_hbm, i_hbm, o_hbm):
        def body(i_vmem, o_vmem):
            pltpu.sync_copy(x_hbm.at[i_vmem.at[0]], o_vmem)

        pltpu.emit_pipeline(
            body, grid=(m // SC_WINDOW,),
            in_specs=[pl.BlockSpec((1, SC_WINDOW), index_map=lambda i: (i, 0))],
            out_specs=[pl.BlockSpec((SC_WINDOW, d), index_map=lambda i: (i, 0))],
            core_axis_name=("core", "subcore"),
            dimension_semantics=(pltpu.PARALLEL,),
        )(i_hbm, o_hbm)

    return gather_rows(table, idx.reshape(m // SC_WINDOW, SC_WINDOW))


def _sc_scatter(rows, idx, n_out):
    nk, t = idx.shape
    d = rows.shape[1]
    assert t % (SC_WINDOW * SC_WORKERS) == 0
    mesh = plsc.VectorSubcoreMesh(core_axis_name="core", subcore_axis_name="subcore")

    @pl.kernel(out_type=jax.ShapeDtypeStruct((n_out, d), rows.dtype), mesh=mesh)
    def scatter_rows(x_hbm, i_hbm, o_hbm):
        def body(x_vmem, i_vmem):
            for k in range(nk):
                pltpu.sync_copy(x_vmem, o_hbm.at[i_vmem.at[k, 0]])

        pltpu.emit_pipeline(
            body, grid=(t // SC_WINDOW,),
            in_specs=[pl.BlockSpec((SC_WINDOW, d), index_map=lambda i: (i, 0)),
                      pl.BlockSpec((nk, 1, SC_WINDOW), index_map=lambda i: (0, i, 0))],
            out_specs=[],
            core_axis_name=("core", "subcore"),
            dimension_semantics=(pltpu.PARALLEL,),
        )(x_hbm, i_hbm)

    return scatter_rows(rows, idx.reshape(nk, t // SC_WINDOW, SC_WINDOW))


def _expert_ffn_kernel(te_ref, fresh_ref, nv_ref, x_ref, wg_ref, wu_ref, wd_ref, y_ref,
                       wg_s, wu_s, wd_s):
    j = pl.program_id(0)

    @pl.when(j < nv_ref[0])
    def _():
        @pl.when(fresh_ref[j] == 1)
        def _():
            wg_s[...] = wg_ref[0, 0].astype(BF16)
            wu_s[...] = wu_ref[0, 0].astype(BF16)
            wd_s[...] = wd_ref[0, 0].astype(BF16)

        x = _unpack_pair(x_ref[...]).astype(BF16)
        a = jnp.dot(x, wg_s[...], preferred_element_type=F32)
        b = jnp.dot(x, wu_s[...], preferred_element_type=F32)
        act = (_silu(a) * b).astype(BF16)
        y_ref[...] = _pack_pair(jnp.dot(act, wd_s[...], preferred_element_type=F32))


def _expert_ffn(xs, n_tiles, tile_expert, fresh, n_valid, layer, w_gate, w_up, w_down):
    dh = xs.shape[1]
    last = lambda j, te, fr, nv: jnp.minimum(j, nv[0] - 1)
    wspec = lambda w: pl.BlockSpec((1, 1) + w.shape[2:],
                                   lambda j, te, fr, nv: (layer, te[j], 0, 0))
    grid_spec = pltpu.PrefetchScalarGridSpec(
        num_scalar_prefetch=3, grid=(n_tiles,),
        in_specs=[
            pl.BlockSpec((TG, dh), lambda j, te, fr, nv: (last(j, te, fr, nv), 0)),
            wspec(w_gate), wspec(w_up), wspec(w_down),
        ],
        out_specs=pl.BlockSpec((TG, dh), lambda j, te, fr, nv: (last(j, te, fr, nv), 0)),
        scratch_shapes=[pltpu.VMEM(w_gate.shape[2:], BF16), pltpu.VMEM(w_up.shape[2:], BF16),
                        pltpu.VMEM(w_down.shape[2:], BF16)],
    )
    return pl.pallas_call(
        _expert_ffn_kernel, grid_spec=grid_spec,
        out_shape=jax.ShapeDtypeStruct((n_tiles * TG, dh), I32),
        compiler_params=_cparams(("arbitrary",)), name="expert_ffn",
    )(tile_expert, fresh, n_valid, xs, w_gate, w_up, w_down)


def _combine_kernel(nbp, yg_ref, w_ref, h_ref, x_ref, mod_ref, sgu_ref, sd_ref, *o_refs):
    f = sd_ref.shape[0]
    h = _unpack_pair(h_ref[...]).astype(BF16)
    hs = jnp.dot(h, sgu_ref[...], preferred_element_type=F32)
    act = (_silu(hs[:, :f]) * hs[:, f:]).astype(BF16)
    acc = jnp.dot(act, sd_ref[...], preferred_element_type=F32)
    w = w_ref[...]
    for k in range(TOPK):
        acc = acc + w[:, k:k + 1] * _unpack_pair(yg_ref[k])
    out = x_ref[...] + mod_ref[0, 5:6, :] * acc
    if len(o_refs) == 1:
        o_refs[0][...] = out
    else:
        i = pl.program_id(0)

        @pl.when(i < nbp)
        def _():
            o_refs[0][...] = out

        @pl.when(i >= nbp)
        def _():
            o_refs[1][...] = out


def _combine(yg, w8, h, x, mod, sgu, sd, nbp, bps, split_out):
    t, d = x.shape
    midx = _mod_index(nbp, bps)
    if split_out:
        out_specs = [pl.BlockSpec((TM, d), lambda i: (jnp.minimum(i, nbp - 1), 0)),
                     pl.BlockSpec((TM, d), lambda i: (jnp.maximum(i - nbp, 0), 0))]
        out_shape = [jax.ShapeDtypeStruct((nbp * TM, d), F32),
                     jax.ShapeDtypeStruct((t - nbp * TM, d), F32)]
    else:
        out_specs = pl.BlockSpec((TM, d), lambda i: (i, 0))
        out_shape = jax.ShapeDtypeStruct((t, d), F32)
    in_specs = [
        pl.BlockSpec((TOPK, TM, d // 2), lambda i: (0, i, 0)),
        pl.BlockSpec((TM, 128), lambda i: (i, 0)),
        pl.BlockSpec((TM, d // 2), lambda i: (i, 0)),
        pl.BlockSpec((TM, d), lambda i: (i, 0)),
        pl.BlockSpec((1, 6, d), lambda i: (midx(i), 0, 0)),
        _const_spec(sgu.shape), _const_spec(sd.shape),
    ]
    return pl.pallas_call(
        functools.partial(_combine_kernel, nbp), grid=(t // TM,), in_specs=in_specs,
        out_specs=out_specs, out_shape=out_shape,
        compiler_params=_cparams(("arbitrary",)), name="moe_combine",
    )(yg, w8, h, x, mod, sgu, sd)


def _sparse_moe(x, h, gates_t, sel_t, cnt, mod, layer, w_gate, w_up, w_down, sgu, sd, nbp, bps,
                split_out):
    t, d = x.shape
    ne = N_EXPERTS
    n_pad = ne * TG
    n_tiles = (t * TOPK + n_pad) // TG
    n_slots = n_tiles * TG
    counts = cnt[:, 0, :].astype(I32)
    per_expert = jnp.sum(counts, axis=0)
    padded = (per_expert + TG - 1) // TG * TG
    ends = jnp.cumsum(padded)
    starts = ends - padded
    block_off = jnp.cumsum(counts, axis=0) - counts
    base = (starts[None, :] + block_off).astype(F32)[:, :, None]
    n_valid = (ends[-1] // TG).astype(I32).reshape(1)
    tile_start = jnp.minimum(jnp.arange(n_tiles, dtype=I32), n_valid[0] - 1) * TG
    tile_expert = jnp.sum((ends[None, :] <= tile_start[:, None]).astype(I32), axis=1)
    fresh = jnp.concatenate([jnp.ones((1,), I32),
                             (tile_expert[1:] != tile_expert[:-1]).astype(I32)])

    pos8, w8 = _slots(sel_t, gates_t, base)
    n_extra = -(-n_pad // t)
    cand = (starts + per_expert)[:, None] + jnp.arange(TG, dtype=I32)[None, :]
    spare = n_slots + jnp.arange(n_pad, dtype=I32)
    fill = jnp.where(cand < ends[:, None], cand, spare.reshape(ne, TG)).reshape(-1)
    rest = n_slots + jnp.arange(n_pad, n_extra * t, dtype=I32) % n_pad
    dest = jnp.concatenate([pos8, jnp.concatenate([fill, rest]).reshape(n_extra, t)], axis=0)
    xs = _sc_scatter(h, dest, n_slots + n_pad)
    ys = _expert_ffn(xs, n_tiles, tile_expert, fresh, n_valid, layer, w_gate, w_up, w_down)
    yg = _sc_gather(ys, pos8.reshape(-1)).reshape(TOPK, t, d // 2)
    return _combine(yg, w8, h, x, mod, sgu, sd, nbp, bps, split_out)


def _head_rms(x, g_row, hm_ref, hmt_ref):
    sq_hi, sq_lo = _split_bf16(x * x)
    hm = hm_ref[...]
    ss = (jnp.dot(sq_hi, hm, preferred_element_type=F32)
          + jnp.dot(sq_lo, hm, preferred_element_type=F32))
    r = lax.rsqrt(ss * (1.0 / HEAD_DIM) + EPS)
    r_hi, r_lo = _split_bf16(r)
    hmt = hmt_ref[...]
    rb = (jnp.dot(r_hi, hmt, preferred_element_type=F32)
          + jnp.dot(r_lo, hmt, preferred_element_type=F32))
    return (x * rb) * g_row


def _qkv_kernel(emit_f32, x_ref, mod_ref, gmix_ref, w_ref, qg_ref, kg_ref, hm_ref, hmt_ref, *outs):
    d = x_ref.shape[1]
    sh1 = mod_ref[0, 0:1, :]
    sc1 = mod_ref[0, 1:2, :]
    h = _adaln(x_ref[...], gmix_ref[...], sh1, sc1).astype(BF16)
    qkv = jnp.dot(h, w_ref[...], preferred_element_type=F32)
    q = _head_rms(qkv[:, :d], qg_ref[...], hm_ref, hmt_ref)
    k = _head_rms(qkv[:, d:2 * d], kg_ref[...], hm_ref, hmt_ref)
    v = qkv[:, 2 * d:]
    outs[0][...] = q.astype(BF16)
    outs[1][...] = k.astype(BF16)
    outs[2][...] = v.astype(BF16)
    if emit_f32:
        outs[3][...] = k
        outs[4][...] = v


def _qkv(x, mod, g_mix, w_qkv, qg, kg, hm, hmt, blk0, nblk, midx, emit_f32):
    t, d = x.shape
    in_specs = [
        pl.BlockSpec((TM, d), lambda i: (i + blk0, 0)),
        pl.BlockSpec((1, 6, d), lambda i: (midx(i + blk0), 0, 0)),
        _const_spec(g_mix.shape), _const_spec(w_qkv.shape), _const_spec(qg.shape),
        _const_spec(kg.shape), _const_spec(hm.shape), _const_spec(hmt.shape),
    ]
    n_out = 5 if emit_f32 else 3
    out_specs = [pl.BlockSpec((TM, d), lambda i: (i, 0)) for _ in range(n_out)]
    out_shape = [jax.ShapeDtypeStruct((nblk * TM, d), BF16 if o < 3 else F32) for o in range(n_out)]
    return pl.pallas_call(
        functools.partial(_qkv_kernel, emit_f32),
        grid=(nblk,), in_specs=in_specs, out_specs=out_specs, out_shape=out_shape,
        compiler_params=_cparams(("parallel",)), name="qkv_f32" if emit_f32 else "qkv",
    )(x, mod, g_mix, w_qkv, qg, kg, hm, hmt)


def _head_masks():
    lane = lax.broadcasted_iota(I32, (1, 2 * HEAD_DIM), 1)
    return lane < HEAD_DIM


def _ctx_attn_kernel(q_ref, k_ref, v_ref, o_ref):
    lo = _head_masks()
    pw = 2 * HEAD_DIM
    for hp in range(q_ref.shape[1] // pw):
        cols = slice(hp * pw, (hp + 1) * pw)
        q = q_ref[:, cols]
        k = k_ref[:, cols]
        v = v_ref[:, cols]
        outs = []
        for hh in range(2):
            msk = lo if hh == 0 else jnp.logical_not(lo)
            qm = jnp.where(msk, q, jnp.zeros_like(q)) * jnp.asarray(HEAD_DIM ** -0.5, BF16)
            s = lax.dot_general(qm, k, NT_DIMS, preferred_element_type=F32)
            m = jnp.max(s, axis=-1, keepdims=True)
            p = jnp.exp(s - m)
            l = jnp.sum(p, axis=-1, keepdims=True)
            o = jnp.dot(p.astype(BF16), v, preferred_element_type=F32)
            outs.append(o / l)
        o_ref[:, cols] = jnp.where(lo, outs[0], outs[1]).astype(BF16)


def _ctx_attn(q, k, v, nb, s):
    t, d = q.shape
    spec = pl.BlockSpec((s, d), lambda b: (b, 0))
    return pl.pallas_call(
        _ctx_attn_kernel, grid=(nb,), in_specs=[spec, spec, spec], out_specs=spec,
        out_shape=jax.ShapeDtypeStruct((nb * s, d), BF16),
        compiler_params=_cparams(("parallel",)), name="context_attention",
    )(q, k, v)


NA_QROWS = 8
NA_KROWS = 16
NA_HPAIRS = 2


def _na_kernel(rows, q_ref, k_ref, v_ref, kc_ref, vc_ref, bias_ref, o_ref):
    rb = pl.program_id(1)
    kr0 = jnp.clip(rb * NA_QROWS - WIN_H // 2, 0, rows - NA_KROWS)
    start = pl.multiple_of(kr0 * GRID_W, 256)
    nk = NA_KROWS * GRID_W
    lo = _head_masks()
    pw = 2 * HEAD_DIM
    for pp in range(NA_HPAIRS):
        cols = slice(pp * pw, (pp + 1) * pw)
        q = q_ref[0, :, cols]
        kw = k_ref[0, pl.ds(start, nk), cols]
        vw = v_ref[0, pl.ds(start, nk), cols]
        kc = kc_ref[0, :, cols]
        vc = vc_ref[0, :, cols]
        outs = []
        for hh in range(2):
            msk = lo if hh == 0 else jnp.logical_not(lo)
            qm = jnp.where(msk, q, jnp.zeros_like(q)) * jnp.asarray(HEAD_DIM ** -0.5, BF16)
            s = (lax.dot_general(qm, kw, NT_DIMS, preferred_element_type=F32)
                 + bias_ref[0, 2 * pp + hh])
            sc = lax.dot_general(qm, kc, NT_DIMS, preferred_element_type=F32)
            m = jnp.maximum(jnp.max(s, axis=-1, keepdims=True),
                            jnp.max(sc, axis=-1, keepdims=True))
            p = jnp.exp(s - m)
            pc = jnp.exp(sc - m)
            l = jnp.sum(p, axis=-1, keepdims=True) + jnp.sum(pc, axis=-1, keepdims=True)
            o = (jnp.dot(p.astype(BF16), vw, preferred_element_type=F32)
                 + jnp.dot(pc.astype(BF16), vc, preferred_element_type=F32))
            outs.append(o / l)
        o_ref[0, :, cols] = jnp.where(lo, outs[0], outs[1]).astype(BF16)


def _na_row_classes(rows):
    nrb = rows // NA_QROWS
    out = []
    for rb in (0, 1, nrb - 1):
        r0 = rb * NA_QROWS
        kr0 = int(np.clip(r0 - WIN_H // 2, 0, rows - NA_KROWS))
        table = []
        for rl in range(NA_QROWS):
            r = r0 + rl
            sr = int(np.clip(r - WIN_H // 2, 0, rows - WIN_H))
            table.append([(kr0 + kl - r + WIN_H - 1) if sr <= kr0 + kl < sr + WIN_H else None
                          for kl in range(NA_KROWS)])
        out.append(table)
    return out


def _na_bias_kernel(rows, rpb_ref, o_ref):
    h = pl.program_id(0)
    n_dr, n_dc = 2 * WIN_H - 1, 2 * WIN_W - 1
    cq = lax.broadcasted_iota(I32, (GRID_W, 2 * GRID_W), 0)
    lane = lax.broadcasted_iota(I32, (GRID_W, 2 * GRID_W), 1)
    ck = lane & (GRID_W - 1)
    q_start = jnp.clip(cq - WIN_W // 2, 0, GRID_W - WIN_W)
    col_ok = (ck >= q_start) & (ck < q_start + WIN_W)
    dc = ck - cq + (WIN_W - 1)
    neg = jnp.full((GRID_W, 2 * GRID_W), NEG, F32)
    tiles = []
    for i in range(n_dr):
        t = jnp.zeros((GRID_W, 2 * GRID_W), F32)
        for jj in range(n_dc):
            t = jnp.where(dc == jj, rpb_ref[h * (n_dr * n_dc) + i * n_dc + jj], t)
        tiles.append(jnp.where(col_ok, t, neg))
    first_half = lane < GRID_W
    for c, table in enumerate(_na_row_classes(rows)):
        for rl in range(NA_QROWS):
            for m in range(NA_KROWS // 2):
                ia, ib = table[rl][2 * m], table[rl][2 * m + 1]
                ta = neg if ia is None else tiles[ia]
                tb = neg if ib is None else tiles[ib]
                blk = ta if ia == ib else jnp.where(first_half, ta, tb)
                o_ref[c, 0, rl * GRID_W:(rl + 1) * GRID_W, m * 2 * GRID_W:(m + 1) * 2 * GRID_W] = blk


def _na_bias(rpb, rows):
    nh = rpb.shape[0]
    nq, nk = NA_QROWS * GRID_W, NA_KROWS * GRID_W
    return pl.pallas_call(
        functools.partial(_na_bias_kernel, rows), grid=(nh,),
        in_specs=[pl.BlockSpec(memory_space=pltpu.SMEM)],
        out_specs=pl.BlockSpec((3, 1, nq, nk), lambda h: (0, h, 0, 0)),
        out_shape=jax.ShapeDtypeStruct((3, nh, nq, nk), F32),
        compiler_params=_cparams(("parallel",)), name="na_bias",
    )(rpb.reshape(-1))


def _na_attn(q, k, v, kc, vc, bias_tab, rows):
    nb, s, d = q.shape
    lw = NA_HPAIRS * 2 * HEAD_DIM
    hp = d // lw
    nrb = rows // NA_QROWS
    nq = NA_QROWS * GRID_W
    lc = kc.shape[1]

    def cls(r):
        return jnp.where(r == 0, 0, jnp.where(r == nrb - 1, 2, 1))

    in_specs = [
        pl.BlockSpec((1, nq, lw), lambda h, r, b: (b, r, h)),
        pl.BlockSpec((1, s, lw), lambda h, r, b: (b, 0, h)),
        pl.BlockSpec((1, s, lw), lambda h, r, b: (b, 0, h)),
        pl.BlockSpec((1, lc, lw), lambda h, r, b: (b, 0, h)),
        pl.BlockSpec((1, lc, lw), lambda h, r, b: (b, 0, h)),
        pl.BlockSpec((1, 2 * NA_HPAIRS, nq, NA_KROWS * GRID_W),
                     lambda h, r, b: (cls(r), h, 0, 0)),
    ]
    return pl.pallas_call(
        functools.partial(_na_kernel, rows),
        grid=(hp, nrb, nb), in_specs=in_specs,
        out_specs=pl.BlockSpec((1, nq, lw), lambda h, r, b: (b, r, h)),
        out_shape=jax.ShapeDtypeStruct((nb, s, d), BF16),
        compiler_params=_cparams(("parallel", "parallel", "parallel")),
        name="neighbourhood_attention",
    )(q, k, v, kc, vc, bias_tab)


def _oproj_kernel(nbp, ap_ref, as_ref, x_ref, mod_ref, wout_ref, gffn_ref, wr_hi_ref, wr_lo_ref,
                  rb_ref, x1_ref, h_ref, gates_ref, sel_ref, cnt_ref):
    a = jnp.where(pl.program_id(0) < nbp, ap_ref[...], as_ref[...])
    y = jnp.dot(a, wout_ref[...], preferred_element_type=F32)
    x1 = x_ref[...] + mod_ref[0, 2:3, :] * y
    x1_ref[...] = x1
    _ffn_pre(x1, mod_ref, gffn_ref, wr_hi_ref, wr_lo_ref, rb_ref,
             h_ref, gates_ref, sel_ref, cnt_ref)


def _oproj(attn_p, attn_s, x, mod, w_out, g_ffn, wr_hi, wr_lo, rbias, nbp, bps):
    t, d = x.shape
    midx = _mod_index(nbp, bps)
    in_specs = [
        pl.BlockSpec((TM, d), lambda i: (jnp.minimum(i, nbp - 1), 0)),
        pl.BlockSpec((TM, d), lambda i: (jnp.maximum(i - nbp, 0), 0)),
        pl.BlockSpec((TM, d), lambda i: (i, 0)),
        pl.BlockSpec((1, 6, d), lambda i: (midx(i), 0, 0)),
        _const_spec(w_out.shape), _const_spec(g_ffn.shape), _const_spec(wr_hi.shape),
        _const_spec(wr_lo.shape), _const_spec(rbias.shape),
    ]
    out_specs, out_shape = _pre_out_specs(t, d)
    return pl.pallas_call(
        functools.partial(_oproj_kernel, nbp), grid=(t // TM,), in_specs=in_specs,
        out_specs=out_specs, out_shape=out_shape, compiler_params=_cparams(("parallel",)),
        name="attn_out_proj",
    )(attn_p, attn_s, x, mod, w_out, g_ffn, wr_hi, wr_lo, rbias)


def _router_weights(w_router, router_bias):
    d, ne = w_router.shape
    w = jnp.pad(w_router, ((0, 0), (0, 128 - ne)))
    hi = w.astype(BF16)
    lo = (w - hi.astype(F32)).astype(BF16)
    return hi, lo, router_bias.reshape(ne, 1)


def _shared_weights(s_gate, s_up, s_down):
    return jnp.concatenate([s_gate, s_up], axis=-1).astype(BF16), s_down.astype(BF16)


def _layer_weights(layer, d, norm_mix, norm_ffn, pc, na, moe):
    i = layer // 2
    w = dict(g_mix=norm_mix[layer].reshape(1, d), g_ffn=norm_ffn[layer].reshape(1, d))
    w["wr_hi"], w["wr_lo"], w["rbias"] = _router_weights(moe["router"][layer],
                                                         moe["router_bias"][layer])
    w["sgu"], w["sd"] = _shared_weights(moe["s_gate"][layer], moe["s_up"][layer],
                                        moe["s_down"][layer])
    if layer % 2 == 0:
        w.update(w_in=pc["w_in"][i].astype(BF16), pool_w=pc["pool_w"][i].astype(BF16),
                 pool_scale=pc["pool_scale"][i].reshape(1, -1), conv_w=pc["conv_w"][i],
                 w_out=pc["w_out"][i].astype(BF16))
    else:
        nh = d // HEAD_DIM
        head_of = np.arange(d) // HEAD_DIM
        w.update(hm=jnp.asarray(head_of[:, None] == np.arange(128)[None, :], BF16),
                 hmt=jnp.asarray(np.arange(128)[:, None] == head_of[None, :], BF16),
                 qg=jnp.tile(na["q_norm"][i], nh).reshape(1, d),
                 kg=jnp.tile(na["k_norm"][i], nh).reshape(1, d),
                 w_qkv=na["w_qkv"][i].astype(BF16), w_out=na["w_out"][i].astype(BF16))
    return w


def kernel(x_prompt, x_sample, cache_k, cache_v, c, c_ctx, ada_w, ada_b, norm_mix, norm_ffn,
           pc_w_in, pc_pool_w, pc_pool_scale, pc_conv_w, pc_w_out,
           na_w_qkv, na_q_norm, na_k_norm, na_rpb, na_w_out,
           moe_router, moe_router_bias, moe_w_gate, moe_w_up, moe_w_down,
           moe_shared_gate, moe_shared_up, moe_shared_down):
    nb_p, s_p, d = x_prompt.shape
    nb_s, s_s, _ = x_sample.shape
    assert s_p == TM and s_s % TM == 0 and nb_p > 0 and nb_s > 0
    tp, ts = nb_p * s_p, nb_s * s_s
    nbp, bps = tp // TM, s_s // TM
    depth = ada_w.shape[0]
    nh = d // HEAD_DIM
    rows = s_s // GRID_W

    cond = jnp.concatenate([c_ctx[None], c], axis=0)
    cond = jnp.pad(cond, ((0, -cond.shape[0] % 8), (0, 0)))
    mods = _modulation(cond, ada_w, ada_b).reshape(depth, cond.shape[0], 6, d)
    pc = dict(w_in=pc_w_in, pool_w=pc_pool_w, pool_scale=pc_pool_scale, conv_w=pc_conv_w,
              w_out=pc_w_out)
    na = dict(w_qkv=na_w_qkv, q_norm=na_q_norm, k_norm=na_k_norm, w_out=na_w_out)
    moe = dict(router=moe_router, router_bias=moe_router_bias, s_gate=moe_shared_gate,
               s_up=moe_shared_up, s_down=moe_shared_down)

    xp, xs = x_prompt.reshape(tp, d), x_sample.reshape(ts, d)
    x = None
    new_k, new_v = [], []
    for layer in range(depth):
        i = layer // 2
        w = _layer_weights(layer, d, norm_mix, norm_ffn, pc, na, moe)
        mod = mods[layer]
        if layer % 2 == 0:
            if x is not None:
                xp, xs = x[:tp], x[tp:]
            x, h, gates_t, sel_t, cnt = _mixer(
                xp, xs, mod, w["g_mix"], w["w_in"], w["pool_w"], w["pool_scale"], w["conv_w"],
                w["w_out"], w["g_ffn"], w["wr_hi"], w["wr_lo"], w["rbias"], nbp, bps, s_p, s_s)
        else:
            midx = _mod_index(nbp, bps)
            qkv_args = (x, mod, w["g_mix"], w["w_qkv"], w["qg"], w["kg"], w["hm"], w["hmt"])
            qp, kp, vp, kp32, vp32 = _qkv(*qkv_args, 0, nbp, midx, True)
            qs, ks, vs = _qkv(*qkv_args, nbp, ts // TM, midx, False)
            new_k.append(kp32.reshape(nb_p, s_p, nh, HEAD_DIM))
            new_v.append(vp32.reshape(nb_p, s_p, nh, HEAD_DIM))
            a_p = _ctx_attn(qp, kp, vp, nb_p, s_p)
            lc = cache_k.shape[2]
            kc = cache_k[:, i].reshape(nb_s, lc, d).astype(BF16)
            vc = cache_v[:, i].reshape(nb_s, lc, d).astype(BF16)
            a_s = _na_attn(qs.reshape(nb_s, s_s, d), ks.reshape(nb_s, s_s, d),
                           vs.reshape(nb_s, s_s, d), kc, vc, _na_bias(na_rpb[i], rows), rows)
            x, h, gates_t, sel_t, cnt = _oproj(a_p, a_s.reshape(ts, d), x, mod, w["w_out"],
                                               w["g_ffn"], w["wr_hi"], w["wr_lo"], w["rbias"],
                                               nbp, bps)
        x = _sparse_moe(x, h, gates_t, sel_t, cnt, mod, layer, moe_w_gate, moe_w_up, moe_w_down,
                        w["sgu"], w["sd"], nbp, bps, split_out=layer == depth - 1)
    y_p, y_s = x
    return (y_p.reshape(nb_p, s_p, d), y_s.reshape(nb_s, s_s, d),
            jnp.stack(new_k, axis=1), jnp.stack(new_v, axis=1))
```

```python
import functools

import numpy as np
import jax
import jax.numpy as jnp
from jax import lax
from jax.experimental import pallas as pl
from jax.experimental.pallas import tpu as pltpu
from jax.experimental.pallas import tpu_sc as plsc

F32 = jnp.float32
BF16 = jnp.bfloat16
I32 = jnp.int32
U32 = jnp.uint32

TM = 256
HALO = 8
POOL_WINDOWS = (2, 4, 8, 16)
N_EXPERTS = 64
N_GROUPS = 8
GROUP_SIZE = N_EXPERTS // N_GROUPS
TOPK_GROUPS = 4
TOPK = 8
ROUTED_SCALE = 2.5
EPS = 1e-6
GRID_W = 64
WIN_H = 8
WIN_W = 16
HEAD_DIM = 64
NEG = float(np.finfo(np.float32).min)
VMEM_LIMIT = 56 * 1024 * 1024
NT_DIMS = (((1,), (1,)), ((), ()))
TG = 1024
SC_WINDOW = 64
SC_WORKERS = 32


def _cparams(sem):
    return pltpu.CompilerParams(dimension_semantics=sem, vmem_limit_bytes=VMEM_LIMIT)


def _silu(x):
    return x * jax.nn.sigmoid(x)


def _split_bf16(x):
    hi = x.astype(BF16)
    lo = (x - hi.astype(F32)).astype(BF16)
    return hi, lo


def _adaln(x, g, shift, scale):
    ms = jnp.mean(x * x, axis=-1, keepdims=True)
    return (x * lax.rsqrt(ms + EPS)) * g * (1.0 + scale) + shift


def _pack_pair(x):
    w = x.shape[1] // 2
    lo = lax.bitcast_convert_type(x[:, :w].astype(BF16).astype(F32), U32) >> 16
    hi = lax.bitcast_convert_type(x[:, w:].astype(BF16).astype(F32), U32)
    return lax.bitcast_convert_type(lo | hi, I32)


def _unpack_pair(p):
    u = lax.bitcast_convert_type(p, U32)
    lo = lax.bitcast_convert_type(u << 16, F32)
    hi = lax.bitcast_convert_type(u & jnp.uint32(0xFFFF0000), F32)
    return jnp.concatenate([lo, hi], axis=-1)


def _mod_kernel(cond_ref, w_ref, b_ref, o_ref):
    c = cond_ref[...]
    a = _silu(c)
    o_ref[0] = jnp.dot(a, w_ref[0], preferred_element_type=F32,
                       precision=lax.Precision.HIGHEST) + b_ref[0]


def _modulation(cond, ada_w, ada_b):
    depth, d, n = ada_w.shape
    rows = cond.shape[0]
    tn = 1536
    return pl.pallas_call(
        _mod_kernel,
        grid=(depth, n // tn),
        in_specs=[
            pl.BlockSpec((rows, d), lambda l, j: (0, 0)),
            pl.BlockSpec((1, d, tn), lambda l, j: (l, 0, j)),
            pl.BlockSpec((1, 1, tn), lambda l, j: (l, 0, j)),
        ],
        out_specs=pl.BlockSpec((1, rows, tn), lambda l, j: (l, 0, j)),
        out_shape=jax.ShapeDtypeStruct((depth, rows, n), F32),
        compiler_params=_cparams(("arbitrary", "arbitrary")),
        name="modulation",
    )(cond, ada_w, ada_b.reshape(depth, 1, n))


def _route(logits_t, bias_col):
    tm = logits_t.shape[1]
    scores = jax.nn.sigmoid(logits_t)
    biased = scores + bias_col
    sub = lax.broadcasted_iota(I32, (GROUP_SIZE, tm), 0).astype(F32)
    ninf = jnp.float32(-jnp.inf)
    groups, gscore = [], []
    for g in range(N_GROUPS):
        v = biased[g * GROUP_SIZE:(g + 1) * GROUP_SIZE]
        m1 = jnp.max(v, axis=0, keepdims=True)
        first = jnp.min(jnp.where(v == m1, sub, float(GROUP_SIZE)), axis=0, keepdims=True)
        m2 = jnp.max(jnp.where(sub == first, ninf, v), axis=0, keepdims=True)
        groups.append(v)
        gscore.append(m1 + m2)
    masked = []
    for g in range(N_GROUPS):
        rank = jnp.zeros((1, tm), I32)
        for g2 in range(N_GROUPS):
            if g2 == g:
                continue
            ahead = gscore[g2] > gscore[g]
            if g2 < g:
                ahead = ahead | (gscore[g2] == gscore[g])
            rank = rank + ahead.astype(I32)
        masked.append(jnp.where(rank < TOPK_GROUPS, groups[g], ninf))
    masked = jnp.concatenate(masked, axis=0)
    eidx = lax.broadcasted_iota(I32, (N_EXPERTS, tm), 0).astype(F32)
    sel = jnp.zeros((N_EXPERTS, tm), jnp.bool_)
    for _ in range(TOPK):
        best = jnp.max(masked, axis=0, keepdims=True)
        first = jnp.min(jnp.where(masked == best, eidx, float(N_EXPERTS)), axis=0, keepdims=True)
        hit = eidx == first
        sel = sel | hit
        masked = jnp.where(hit, ninf, masked)
    w = jnp.where(sel, scores, 0.0)
    wsum = jnp.sum(w, axis=0, keepdims=True)
    return w / wsum * ROUTED_SCALE, sel


def _ffn_pre(x1, mod_ref, gffn_ref, wr_hi_ref, wr_lo_ref, rb_ref,
             h_ref, gates_ref, sel_ref, cnt_ref):
    sh2 = mod_ref[0, 3:4, :]
    sc2 = mod_ref[0, 4:5, :]
    h = _adaln(x1, gffn_ref[...], sh2, sc2)
    h_hi, h_lo = _split_bf16(h)
    h_ref[...] = _pack_pair(h_hi)
    wr_hi = wr_hi_ref[...]
    logits = (jnp.dot(h_hi, wr_hi, preferred_element_type=F32)
              + jnp.dot(h_hi, wr_lo_ref[...], preferred_element_type=F32)
              + jnp.dot(h_lo, wr_hi, preferred_element_type=F32))
    gates_t, sel = _route(logits.T[:N_EXPERTS], rb_ref[...])
    gates_ref[...] = gates_t
    sel_b = sel.astype(F32).astype(BF16)
    sel_ref[...] = sel_b
    ones = jnp.ones((8, sel_b.shape[1]), BF16)
    cnt_ref[0] = lax.dot_general(ones, sel_b, NT_DIMS, preferred_element_type=F32)


def _pre_out_specs(t, d):
    specs = [
        pl.BlockSpec((TM, d), lambda i: (i, 0)),
        pl.BlockSpec((TM, d // 2), lambda i: (i, 0)),
        pl.BlockSpec((N_EXPERTS, TM), lambda i: (0, i)),
        pl.BlockSpec((N_EXPERTS, TM), lambda i: (0, i)),
        pl.BlockSpec((1, 8, N_EXPERTS), lambda i: (i, 0, 0)),
    ]
    shapes = [
        jax.ShapeDtypeStruct((t, d), F32),
        jax.ShapeDtypeStruct((t, d // 2), I32),
        jax.ShapeDtypeStruct((N_EXPERTS, t), F32),
        jax.ShapeDtypeStruct((N_EXPERTS, t), BF16),
        jax.ShapeDtypeStruct((t // TM, 8, N_EXPERTS), F32),
    ]
    return specs, shapes


def _mixer_kernel(nbp, bps, sp, ss,
                  xpc_ref, xsc_ref, xprev_ref, xnext_ref, mod_ref, gmix_ref, win_ref, pw_ref, ps_ref,
                  cw_ref,
                  wout_ref, gffn_ref, wr_hi_ref, wr_lo_ref, rb_ref,
                  x1_ref, h_ref, gates_ref, sel_ref, cnt_ref):
    i = pl.program_id(0)
    is_p = i < nbp
    j = lax.rem(jnp.maximum(i - nbp, 0), bps)
    first = is_p | (j == 0)
    last = is_p | (j == bps - 1)
    base = jnp.where(is_p, 0, j * TM)
    slen = jnp.where(is_p, sp, ss)

    sh1 = mod_ref[0, 0:1, :]
    sc1 = mod_ref[0, 1:2, :]
    g1 = mod_ref[0, 2:3, :]
    xc = jnp.where(is_p, xpc_ref[...], xsc_ref[...])
    x_ext = jnp.concatenate([xprev_ref[...], xc, xnext_ref[...]], axis=0)
    h_ext = _adaln(x_ext, gmix_ref[...], sh1, sc1).astype(BF16)
    u = jnp.dot(h_ext, win_ref[...], preferred_element_type=F32)
    next_ = TM + 2 * HALO
    row = lax.broadcasted_iota(I32, (next_, 1), 0)
    keep = ((row >= HALO) | jnp.logical_not(first)) & ((row < HALO + TM) | jnp.logical_not(last))
    u = jnp.where(keep, u, 0.0)

    dm = u.shape[1] // 4
    ua = u[:, :dm]
    gate_b = u[HALO:HALO + TM, dm:2 * dm]
    z = u[:, 2 * dm:3 * dm] * u[:, 3 * dm:]

    def up(a, k):
        return pltpu.roll(a, next_ - k, 0)

    pos = base + lax.broadcasted_iota(I32, (TM, 1), 0)
    pg = dm // len(POOL_WINDOWS)
    ya = []
    for g, w in enumerate(POOL_WINDOWS):
        e = ua[:, g * pg:(g + 1) * pg]
        acc = e
        span = 1
        while span < w:
            acc = acc + up(acc, span)
            span *= 2
        off = HALO - w // 2
        wsum = (up(acc, off) if off else acc)[:TM]
        lo = jnp.maximum(pos - w // 2, 0)
        hi = jnp.minimum(pos + (w - w // 2 - 1), slen - 1)
        cnt = (hi - lo + 1).astype(F32)
        diff = wsum / cnt - e[HALO:HALO + TM]
        ya.append(jnp.dot(diff.astype(BF16), pw_ref[g], preferred_element_type=F32))
    y_a = jnp.concatenate(ya, axis=-1) * ps_ref[...]
    zc = (cw_ref[0:1, :] * up(z, HALO - 1)[:TM] + cw_ref[1:2, :] * z[HALO:HALO + TM]
          + cw_ref[2:3, :] * up(z, HALO + 1)[:TM])
    y_b = gate_b * zc
    ycat = jnp.concatenate([y_a, y_b], axis=-1).astype(BF16)
    y = jnp.dot(ycat, wout_ref[...], preferred_element_type=F32)
    x1 = xc + g1 * y
    x1_ref[...] = x1
    _ffn_pre(x1, mod_ref, gffn_ref, wr_hi_ref, wr_lo_ref, rb_ref,
             h_ref, gates_ref, sel_ref, cnt_ref)


def _mod_index(nbp, bps):
    def f(i):
        return jnp.where(i < nbp, 0, 1 + jnp.maximum(i - nbp, 0) // bps)
    return f


def _const_spec(shape):
    nd = len(shape)
    return pl.BlockSpec(shape, lambda i: (0,) * nd)


def _mixer(xp, xs, mod, g_mix, w_in, pool_w, pool_scale, conv_w, w_out, g_ffn, wr_hi, wr_lo, rbias,
           nbp, bps, sp, ss):
    d = xp.shape[1]
    t = xp.shape[0] + xs.shape[0]
    nblk = t // TM
    midx = _mod_index(nbp, bps)
    hpb = TM // HALO
    nh = xs.shape[0] // HALO
    in_specs = [
        pl.BlockSpec((TM, d), lambda i: (jnp.minimum(i, nbp - 1), 0)),
        pl.BlockSpec((TM, d), lambda i: (jnp.maximum(i - nbp, 0), 0)),
        pl.BlockSpec((HALO, d), lambda i: (jnp.maximum((i - nbp) * hpb - 1, 0), 0)),
        pl.BlockSpec((HALO, d), lambda i: (jnp.clip((i - nbp + 1) * hpb, 0, nh - 1), 0)),
        pl.BlockSpec((1, 6, d), lambda i: (midx(i), 0, 0)),
        _const_spec(g_mix.shape), _const_spec(w_in.shape), _const_spec(pool_w.shape),
        _const_spec(pool_scale.shape), _const_spec(conv_w.shape), _const_spec(w_out.shape),
        _const_spec(g_ffn.shape), _const_spec(wr_hi.shape), _const_spec(wr_lo.shape),
        _const_spec(rbias.shape),
    ]
    out_specs, out_shape = _pre_out_specs(t, d)
    return pl.pallas_call(
        functools.partial(_mixer_kernel, nbp, bps, sp, ss),
        grid=(nblk,), in_specs=in_specs, out_specs=out_specs, out_shape=out_shape,
        compiler_params=_cparams(("parallel",)), name="pool_conv_mixer",
    )(xp, xs, xs, xs, mod, g_mix, w_in, pool_w, pool_scale, conv_w, w_out, g_ffn, wr_hi, wr_lo,
      rbias)


def _slots_kernel(sel_ref, gates_ref, base_ref, pos_ref, w_ref):
    sel = sel_ref[...]
    tm = sel.shape[1]
    r = lax.broadcasted_iota(I32, (tm, tm), 0)
    c = lax.broadcasted_iota(I32, (tm, tm), 1)
    before = (r < c).astype(F32).astype(BF16)
    rank_tok = jnp.dot(sel, before, preferred_element_type=F32)
    er = lax.broadcasted_iota(I32, (N_EXPERTS, N_EXPERTS), 0)
    ec = lax.broadcasted_iota(I32, (N_EXPERTS, N_EXPERTS), 1)
    lower = (ec < er).astype(F32).astype(BF16)
    rank_exp = jnp.dot(lower, sel, preferred_element_type=F32)
    slot = base_ref[0] + rank_tok
    chosen = sel > 0
    gates = gates_ref[...]
    sub = lax.broadcasted_iota(I32, (TOPK, tm), 0)
    pos8 = jnp.zeros((TOPK, tm), F32)
    w8 = jnp.zeros((TOPK, tm), F32)
    for k in range(TOPK):
        mk = chosen & (rank_exp == float(k))
        pk = jnp.sum(jnp.where(mk, slot, 0.0), axis=0, keepdims=True)
        wk = jnp.sum(jnp.where(mk, gates, 0.0), axis=0, keepdims=True)
        pos8 = jnp.where(sub == k, pk, pos8)
        w8 = jnp.where(sub == k, wk, w8)
    pos_ref[...] = pos8.astype(I32)
    w8 = jnp.concatenate([w8, jnp.zeros((128 - TOPK, tm), F32)], axis=0)
    w_ref[...] = w8.T


def _slots(sel_t, gates_t, base):
    ne, t = sel_t.shape
    return pl.pallas_call(
        _slots_kernel, grid=(t // TM,),
        in_specs=[
            pl.BlockSpec((ne, TM), lambda i: (0, i)),
            pl.BlockSpec((ne, TM), lambda i: (0, i)),
            pl.BlockSpec((1, ne, 1), lambda i: (i, 0, 0)),
        ],
        out_specs=[pl.BlockSpec((TOPK, TM), lambda i: (0, i)),
                   pl.BlockSpec((TM, 128), lambda i: (i, 0))],
        out_shape=[jax.ShapeDtypeStruct((TOPK, t), I32), jax.ShapeDtypeStruct((t, 128), F32)],
        compiler_params=_cparams(("parallel",)), name="moe_slots",
    )(sel_t, gates_t, base)


def _sc_gather(table, idx):
    m = idx.shape[0]
    d = table.shape[1]
    assert m % (SC_WINDOW * SC_WORKERS) == 0
    mesh = plsc.VectorSubcoreMesh(core_axis_name="core", subcore_axis_name="subcore")

    @pl.kernel(out_type=jax.ShapeDtypeStruct((m, d), table.dtype), mesh=mesh)
    def gather_rows(x_hbm, i_hbm, o_hbm):
        def body(i_vmem, o_vmem):
            pltpu.sync_copy(x_hbm.at[i_vmem.at[0]], o_vmem)

        pltpu.emit_pipeline(
            body, grid=(m // SC_WINDOW,),
            in_specs=[pl.BlockSpec((1, SC_WINDOW), index_map=lambda i: (i, 0))],
            out_specs=[pl.BlockSpec((SC_WINDOW, d), index_map=lambda i: (i, 0))],
            core_axis_name=("core", "subcore"),
            dimension_semantics=(pltpu.PARALLEL,),
        )(i_hbm, o_hbm)

    return gather_rows(table, idx.reshape(m // SC_WINDOW, SC_WINDOW))


def _sc_scatter(rows, idx, n_out):
    nk, t = idx.shape
    d = rows.shape[1]
    assert t % (SC_WINDOW * SC_WORKERS) == 0
    mesh = plsc.VectorSubcoreMesh(core_axis_name="core", subcore_axis_name="subcore")

    @pl.kernel(out_type=jax.ShapeDtypeStruct((n_out, d), rows.dtype), mesh=mesh)
    def scatter_rows(x_hbm, i_hbm, o_hbm):
        def body(x_vmem, i_vmem):
            for k in range(nk):
                pltpu.sync_copy(x_vmem, o_hbm.at[i_vmem.at[k, 0]])

        pltpu.emit_pipeline(
            body, grid=(t // SC_WINDOW,),
            in_specs=[pl.BlockSpec((SC_WINDOW, d), index_map=lambda i: (i, 0)),
                      pl.BlockSpec((nk, 1, SC_WINDOW), index_map=lambda i: (0, i, 0))],
            out_specs=[],
            core_axis_name=("core", "subcore"),
            dimension_semantics=(pltpu.PARALLEL,),
        )(x_hbm, i_hbm)

    return scatter_rows(rows, idx.reshape(nk, t // SC_WINDOW, SC_WINDOW))


def _expert_ffn_kernel(te_ref, fresh_ref, nv_ref, x_ref, wg_ref, wu_ref, wd_ref, y_ref,
                       wg_s, wu_s, wd_s):
    j = pl.program_id(0)

    @pl.when(j < nv_ref[0])
    def _():
        @pl.when(fresh_ref[j] == 1)
        def _():
            wg_s[...] = wg_ref[0, 0].astype(BF16)
            wu_s[...] = wu_ref[0, 0].astype(BF16)
            wd_s[...] = wd_ref[0, 0].astype(BF16)

        x = _unpack_pair(x_ref[...]).astype(BF16)
        a = jnp.dot(x, wg_s[...], preferred_element_type=F32)
        b = jnp.dot(x, wu_s[...], preferred_element_type=F32)
        act = (_silu(a) * b).astype(BF16)
        y_ref[...] = _pack_pair(jnp.dot(act, wd_s[...], preferred_element_type=F32))


def _expert_ffn(xs, n_tiles, tile_expert, fresh, n_valid, layer, w_gate, w_up, w_down):
    dh = xs.shape[1]
    last = lambda j, te, fr, nv: jnp.minimum(j, nv[0] - 1)
    wspec = lambda w: pl.BlockSpec((1, 1) + w.shape[2:],
                                   lambda j, te, fr, nv: (layer, te[j], 0, 0))
    grid_spec = pltpu.PrefetchScalarGridSpec(
        num_scalar_prefetch=3, grid=(n_tiles,),
        in_specs=[
            pl.BlockSpec((TG, dh), lambda j, te, fr, nv: (last(j, te, fr, nv), 0)),
            wspec(w_gate), wspec(w_up), wspec(w_down),
        ],
        out_specs=pl.BlockSpec((TG, dh), lambda j, te, fr, nv: (last(j, te, fr, nv), 0)),
        scratch_shapes=[pltpu.VMEM(w_gate.shape[2:], BF16), pltpu.VMEM(w_up.shape[2:], BF16),
                        pltpu.VMEM(w_down.shape[2:], BF16)],
    )
    return pl.pallas_call(
        _expert_ffn_kernel, grid_spec=grid_spec,
        out_shape=jax.ShapeDtypeStruct((n_tiles * TG, dh), I32),
        compiler_params=_cparams(("arbitrary",)), name="expert_ffn",
    )(tile_expert, fresh, n_valid, xs, w_gate, w_up, w_down)


def _combine_kernel(nbp, yg_ref, w_ref, h_ref, x_ref, mod_ref, sgu_ref, sd_ref, *o_refs):
    f = sd_ref.shape[0]
    h = _unpack_pair(h_ref[...]).astype(BF16)
    hs = jnp.dot(h, sgu_ref[...], preferred_element_type=F32)
    act = (_silu(hs[:, :f]) * hs[:, f:]).astype(BF16)
    acc = jnp.dot(act, sd_ref[...], preferred_element_type=F32)
    w = w_ref[...]
    for k in range(TOPK):
        acc = acc + w[:, k:k + 1] * _unpack_pair(yg_ref[k])
    out = x_ref[...] + mod_ref[0, 5:6, :] * acc
    if len(o_refs) == 1:
        o_refs[0][...] = out
    else:
        i = pl.program_id(0)

        @pl.when(i < nbp)
        def _():
            o_refs[0][...] = out

        @pl.when(i >= nbp)
        def _():
            o_refs[1][...] = out


def _combine(yg, w8, h, x, mod, sgu, sd, nbp, bps, split_out):
    t, d = x.shape
    tc = 2 * TM
    assert (nbp * TM) % tc == 0 and (bps * TM) % tc == 0
    nbp, bps = nbp * TM // tc, bps * TM // tc
    midx = _mod_index(nbp, bps)
    if split_out:
        out_specs = [pl.BlockSpec((tc, d), lambda i: (jnp.minimum(i, nbp - 1), 0)),
                     pl.BlockSpec((tc, d), lambda i: (jnp.maximum(i - nbp, 0), 0))]
        out_shape = [jax.ShapeDtypeStruct((nbp * tc, d), F32),
                     jax.ShapeDtypeStruct((t - nbp * tc, d), F32)]
    else:
        out_specs = pl.BlockSpec((tc, d), lambda i: (i, 0))
        out_shape = jax.ShapeDtypeStruct((t, d), F32)
    in_specs = [
        pl.BlockSpec((TOPK, tc, d // 2), lambda i: (0, i, 0)),
        pl.BlockSpec((tc, 128), lambda i: (i, 0)),
        pl.BlockSpec((tc, d // 2), lambda i: (i, 0)),
        pl.BlockSpec((tc, d), lambda i: (i, 0)),
        pl.BlockSpec((1, 6, d), lambda i: (midx(i), 0, 0)),
        _const_spec(sgu.shape), _const_spec(sd.shape),
    ]
    return pl.pallas_call(
        functools.partial(_combine_kernel, nbp), grid=(t // tc,), in_specs=in_specs,
        out_specs=out_specs, out_shape=out_shape,
        compiler_params=_cparams(("arbitrary",)), name="moe_combine",
    )(yg, w8, h, x, mod, sgu, sd)


def _sparse_moe(x, h, gates_t, sel_t, cnt, mod, layer, w_gate, w_up, w_down, sgu, sd, nbp, bps,
                split_out):
    t, d = x.shape
    ne = N_EXPERTS
    n_pad = ne * TG
    n_tiles = (t * TOPK + n_pad) // TG
    n_slots = n_tiles * TG
    counts = cnt[:, 0, :].astype(I32)
    per_expert = jnp.sum(counts, axis=0)
    padded = (per_expert + TG - 1) // TG * TG
    ends = jnp.cumsum(padded)
    starts = ends - padded
    block_off = jnp.cumsum(counts, axis=0) - counts
    base = (starts[None, :] + block_off).astype(F32)[:, :, None]
    n_valid = (ends[-1] // TG).astype(I32).reshape(1)
    tile_start = jnp.minimum(jnp.arange(n_tiles, dtype=I32), n_valid[0] - 1) * TG
    tile_expert = jnp.sum((ends[None, :] <= tile_start[:, None]).astype(I32), axis=1)
    fresh = jnp.concatenate([jnp.ones((1,), I32),
                             (tile_expert[1:] != tile_expert[:-1]).astype(I32)])

    pos8, w8 = _slots(sel_t, gates_t, base)
    n_extra = -(-n_pad // t)
    cand = (starts + per_expert)[:, None] + jnp.arange(TG, dtype=I32)[None, :]
    spare = n_slots + jnp.arange(n_pad, dtype=I32)
    fill = jnp.where(cand < ends[:, None], cand, spare.reshape(ne, TG)).reshape(-1)
    rest = n_slots + jnp.arange(n_pad, n_extra * t, dtype=I32) % n_pad
    dest = jnp.concatenate([pos8, jnp.concatenate([fill, rest]).reshape(n_extra, t)], axis=0)
    xs = _sc_scatter(h, dest, n_slots + n_pad)
    ys = _expert_ffn(xs, n_tiles, tile_expert, fresh, n_valid, layer, w_gate, w_up, w_down)
    yg = _sc_gather(ys, pos8.reshape(-1)).reshape(TOPK, t, d // 2)
    return _combine(yg, w8, h, x, mod, sgu, sd, nbp, bps, split_out)


def _head_rms(x, g_row, hm_ref, hmt_ref):
    sq_hi, sq_lo = _split_bf16(x * x)
    hm = hm_ref[...]
    ss = (jnp.dot(sq_hi, hm, preferred_element_type=F32)
          + jnp.dot(sq_lo, hm, preferred_element_type=F32))
    r = lax.rsqrt(ss * (1.0 / HEAD_DIM) + EPS)
    r_hi, r_lo = _split_bf16(r)
    hmt = hmt_ref[...]
    rb = (jnp.dot(r_hi, hmt, preferred_element_type=F32)
          + jnp.dot(r_lo, hmt, preferred_element_type=F32))
    return (x * rb) * g_row


def _qkv_kernel(emit_f32, x_ref, mod_ref, gmix_ref, w_ref, qg_ref, kg_ref, hm_ref, hmt_ref, *outs):
    d = x_ref.shape[1]
    sh1 = mod_ref[0, 0:1, :]
    sc1 = mod_ref[0, 1:2, :]
    h = _adaln(x_ref[...], gmix_ref[...], sh1, sc1).astype(BF16)
    qkv = jnp.dot(h, w_ref[...], preferred_element_type=F32)
    q = _head_rms(qkv[:, :d], qg_ref[...], hm_ref, hmt_ref)
    k = _head_rms(qkv[:, d:2 * d], kg_ref[...], hm_ref, hmt_ref)
    v = qkv[:, 2 * d:]
    outs[0][...] = q.astype(BF16)
    outs[1][...] = k.astype(BF16)
    outs[2][...] = v.astype(BF16)
    if emit_f32:
        outs[3][...] = k
        outs[4][...] = v


def _qkv(x, mod, g_mix, w_qkv, qg, kg, hm, hmt, blk0, nblk, midx, emit_f32):
    t, d = x.shape
    in_specs = [
        pl.BlockSpec((TM, d), lambda i: (i + blk0, 0)),
        pl.BlockSpec((1, 6, d), lambda i: (midx(i + blk0), 0, 0)),
        _const_spec(g_mix.shape), _const_spec(w_qkv.shape), _const_spec(qg.shape),
        _const_spec(kg.shape), _const_spec(hm.shape), _const_spec(hmt.shape),
    ]
    n_out = 5 if emit_f32 else 3
    out_specs = [pl.BlockSpec((TM, d), lambda i: (i, 0)) for _ in range(n_out)]
    out_shape = [jax.ShapeDtypeStruct((nblk * TM, d), BF16 if o < 3 else F32) for o in range(n_out)]
    return pl.pallas_call(
        functools.partial(_qkv_kernel, emit_f32),
        grid=(nblk,), in_specs=in_specs, out_specs=out_specs, out_shape=out_shape,
        compiler_params=_cparams(("parallel",)), name="qkv_f32" if emit_f32 else "qkv",
    )(x, mod, g_mix, w_qkv, qg, kg, hm, hmt)


def _head_masks():
    lane = lax.broadcasted_iota(I32, (1, 2 * HEAD_DIM), 1)
    return lane < HEAD_DIM


def _ctx_attn_kernel(q_ref, k_ref, v_ref, o_ref):
    lo = _head_masks()
    pw = 2 * HEAD_DIM
    for hp in range(q_ref.shape[1] // pw):
        cols = slice(hp * pw, (hp + 1) * pw)
        q = q_ref[:, cols]
        k = k_ref[:, cols]
        v = v_ref[:, cols]
        outs = []
        for hh in range(2):
            msk = lo if hh == 0 else jnp.logical_not(lo)
            qm = jnp.where(msk, q, jnp.zeros_like(q)) * jnp.asarray(HEAD_DIM ** -0.5, BF16)
            s = lax.dot_general(qm, k, NT_DIMS, preferred_element_type=F32)
            m = jnp.max(s, axis=-1, keepdims=True)
            p = jnp.exp(s - m)
            l = jnp.sum(p, axis=-1, keepdims=True)
            o = jnp.dot(p.astype(BF16), v, preferred_element_type=F32)
            outs.append(o / l)
        o_ref[:, cols] = jnp.where(lo, outs[0], outs[1]).astype(BF16)


def _ctx_attn(q, k, v, nb, s):
    t, d = q.shape
    spec = pl.BlockSpec((s, d), lambda b: (b, 0))
    return pl.pallas_call(
        _ctx_attn_kernel, grid=(nb,), in_specs=[spec, spec, spec], out_specs=spec,
        out_shape=jax.ShapeDtypeStruct((nb * s, d), BF16),
        compiler_params=_cparams(("parallel",)), name="context_attention",
    )(q, k, v)


NA_QROWS = 8
NA_GROWS = 4
NA_KROWS = 12
NA_HPAIRS = 2


def _na_kernel(rows, q_ref, k_ref, v_ref, kc_ref, vc_ref, bias_ref, o_ref):
    rb = pl.program_id(1)
    ngrp = NA_QROWS // NA_GROWS
    nq = NA_GROWS * GRID_W
    nk = NA_KROWS * GRID_W
    lo = _head_masks()
    pw = 2 * HEAD_DIM
    for gi in range(ngrp):
        g = rb * ngrp + gi
        kr0 = jnp.clip(g * NA_GROWS - WIN_H // 2, 0, rows - NA_KROWS)
        start = pl.multiple_of(kr0 * GRID_W, 256)
        cls = jnp.where(g == 0, 0, jnp.where(g == rows // NA_GROWS - 1, 2, 1))
        qrows = slice(gi * nq, (gi + 1) * nq)
        for pp in range(NA_HPAIRS):
            cols = slice(pp * pw, (pp + 1) * pw)
            q = q_ref[0, qrows, cols]
            kw = k_ref[0, pl.ds(start, nk), cols]
            vw = v_ref[0, pl.ds(start, nk), cols]
            kc = kc_ref[0, :, cols]
            vc = vc_ref[0, :, cols]
            outs = []
            for hh in range(2):
                msk = lo if hh == 0 else jnp.logical_not(lo)
                qm = jnp.where(msk, q, jnp.zeros_like(q)) * jnp.asarray(HEAD_DIM ** -0.5, BF16)
                s = (lax.dot_general(qm, kw, NT_DIMS, preferred_element_type=F32)
                     + bias_ref[cls, 2 * pp + hh])
                sc = lax.dot_general(qm, kc, NT_DIMS, preferred_element_type=F32)
                m = jnp.maximum(jnp.max(s, axis=-1, keepdims=True),
                                jnp.max(sc, axis=-1, keepdims=True))
                p = jnp.exp(s - m)
                pc = jnp.exp(sc - m)
                l = jnp.sum(p, axis=-1, keepdims=True) + jnp.sum(pc, axis=-1, keepdims=True)
                o = (jnp.dot(p.astype(BF16), vw, preferred_element_type=F32)
                     + jnp.dot(pc.astype(BF16), vc, preferred_element_type=F32))
                outs.append(o / l)
            o_ref[0, qrows, cols] = jnp.where(lo, outs[0], outs[1]).astype(BF16)


def _na_row_classes(rows):
    out = []
    for g in (0, 1, rows // NA_GROWS - 1):
        r0 = g * NA_GROWS
        kr0 = int(np.clip(r0 - WIN_H // 2, 0, rows - NA_KROWS))
        table = []
        for rl in range(NA_GROWS):
            r = r0 + rl
            sr = int(np.clip(r - WIN_H // 2, 0, rows - WIN_H))
            table.append([(kr0 + kl - r + WIN_H - 1) if sr <= kr0 + kl < sr + WIN_H else None
                          for kl in range(NA_KROWS)])
        out.append(table)
    return out


def _na_bias_kernel(rows, rpb_ref, o_ref):
    h = pl.program_id(0)
    n_dr, n_dc = 2 * WIN_H - 1, 2 * WIN_W - 1
    cq = lax.broadcasted_iota(I32, (GRID_W, 2 * GRID_W), 0)
    lane = lax.broadcasted_iota(I32, (GRID_W, 2 * GRID_W), 1)
    ck = lane & (GRID_W - 1)
    q_start = jnp.clip(cq - WIN_W // 2, 0, GRID_W - WIN_W)
    col_ok = (ck >= q_start) & (ck < q_start + WIN_W)
    dc = ck - cq + (WIN_W - 1)
    neg = jnp.full((GRID_W, 2 * GRID_W), NEG, F32)
    tiles = []
    for i in range(n_dr):
        t = jnp.zeros((GRID_W, 2 * GRID_W), F32)
        for jj in range(n_dc):
            t = jnp.where(dc == jj, rpb_ref[h * (n_dr * n_dc) + i * n_dc + jj], t)
        tiles.append(jnp.where(col_ok, t, neg))
    first_half = lane < GRID_W
    for c, table in enumerate(_na_row_classes(rows)):
        for rl in range(NA_GROWS):
            for m in range(NA_KROWS // 2):
                ia, ib = table[rl][2 * m], table[rl][2 * m + 1]
                ta = neg if ia is None else tiles[ia]
                tb = neg if ib is None else tiles[ib]
                blk = ta if ia == ib else jnp.where(first_half, ta, tb)
                o_ref[c, 0, rl * GRID_W:(rl + 1) * GRID_W, m * 2 * GRID_W:(m + 1) * 2 * GRID_W] = blk


def _na_bias(rpb, rows):
    nh = rpb.shape[0]
    nq, nk = NA_GROWS * GRID_W, NA_KROWS * GRID_W
    return pl.pallas_call(
        functools.partial(_na_bias_kernel, rows), grid=(nh,),
        in_specs=[pl.BlockSpec(memory_space=pltpu.SMEM)],
        out_specs=pl.BlockSpec((3, 1, nq, nk), lambda h: (0, h, 0, 0)),
        out_shape=jax.ShapeDtypeStruct((3, nh, nq, nk), F32),
        compiler_params=_cparams(("parallel",)), name="na_bias",
    )(rpb.reshape(-1))


def _na_attn(q, k, v, kc, vc, bias_tab, rows):
    nb, s, d = q.shape
    lw = NA_HPAIRS * 2 * HEAD_DIM
    hp = d // lw
    nrb = rows // NA_QROWS
    nq = NA_QROWS * GRID_W
    lc = kc.shape[1]
    in_specs = [
        pl.BlockSpec((1, nq, lw), lambda h, r, b: (b, r, h)),
        pl.BlockSpec((1, s, lw), lambda h, r, b: (b, 0, h)),
        pl.BlockSpec((1, s, lw), lambda h, r, b: (b, 0, h)),
        pl.BlockSpec((1, lc, lw), lambda h, r, b: (b, 0, h)),
        pl.BlockSpec((1, lc, lw), lambda h, r, b: (b, 0, h)),
        pl.BlockSpec((3, 2 * NA_HPAIRS) + bias_tab.shape[2:], lambda h, r, b: (0, h, 0, 0)),
    ]
    return pl.pallas_call(
        functools.partial(_na_kernel, rows),
        grid=(hp, nrb, nb), in_specs=in_specs,
        out_specs=pl.BlockSpec((1, nq, lw), lambda h, r, b: (b, r, h)),
        out_shape=jax.ShapeDtypeStruct((nb, s, d), BF16),
        compiler_params=_cparams(("parallel", "parallel", "parallel")),
        name="neighbourhood_attention",
    )(q, k, v, kc, vc, bias_tab)


def _oproj_kernel(nbp, ap_ref, as_ref, x_ref, mod_ref, wout_ref, gffn_ref, wr_hi_ref, wr_lo_ref,
                  rb_ref, x1_ref, h_ref, gates_ref, sel_ref, cnt_ref):
    a = jnp.where(pl.program_id(0) < nbp, ap_ref[...], as_ref[...])
    y = jnp.dot(a, wout_ref[...], preferred_element_type=F32)
    x1 = x_ref[...] + mod_ref[0, 2:3, :] * y
    x1_ref[...] = x1
    _ffn_pre(x1, mod_ref, gffn_ref, wr_hi_ref, wr_lo_ref, rb_ref,
             h_ref, gates_ref, sel_ref, cnt_ref)


def _oproj(attn_p, attn_s, x, mod, w_out, g_ffn, wr_hi, wr_lo, rbias, nbp, bps):
    t, d = x.shape
    midx = _mod_index(nbp, bps)
    in_specs = [
        pl.BlockSpec((TM, d), lambda i: (jnp.minimum(i, nbp - 1), 0)),
        pl.BlockSpec((TM, d), lambda i: (jnp.maximum(i - nbp, 0), 0)),
        pl.BlockSpec((TM, d), lambda i: (i, 0)),
        pl.BlockSpec((1, 6, d), lambda i: (midx(i), 0, 0)),
        _const_spec(w_out.shape), _const_spec(g_ffn.shape), _const_spec(wr_hi.shape),
        _const_spec(wr_lo.shape), _const_spec(rbias.shape),
    ]
    out_specs, out_shape = _pre_out_specs(t, d)
    return pl.pallas_call(
        functools.partial(_oproj_kernel, nbp), grid=(t // TM,), in_specs=in_specs,
        out_specs=out_specs, out_shape=out_shape, compiler_params=_cparams(("parallel",)),
        name="attn_out_proj",
    )(attn_p, attn_s, x, mod, w_out, g_ffn, wr_hi, wr_lo, rbias)


def _router_weights(w_router, router_bias):
    d, ne = w_router.shape
    w = jnp.pad(w_router, ((0, 0), (0, 128 - ne)))
    hi = w.astype(BF16)
    lo = (w - hi.astype(F32)).astype(BF16)
    return hi, lo, router_bias.reshape(ne, 1)


def _shared_weights(s_gate, s_up, s_down):
    return jnp.concatenate([s_gate, s_up], axis=-1).astype(BF16), s_down.astype(BF16)


def _layer_weights(layer, d, norm_mix, norm_ffn, pc, na, moe):
    i = layer // 2
    w = dict(g_mix=norm_mix[layer].reshape(1, d), g_ffn=norm_ffn[layer].reshape(1, d))
    w["wr_hi"], w["wr_lo"], w["rbias"] = _router_weights(moe["router"][layer],
                                                         moe["router_bias"][layer])
    w["sgu"], w["sd"] = _shared_weights(moe["s_gate"][layer], moe["s_up"][layer],
                                        moe["s_down"][layer])
    if layer % 2 == 0:
        w.update(w_in=pc["w_in"][i].astype(BF16), pool_w=pc["pool_w"][i].astype(BF16),
                 pool_scale=pc["pool_scale"][i].reshape(1, -1), conv_w=pc["conv_w"][i],
                 w_out=pc["w_out"][i].astype(BF16))
    else:
        nh = d // HEAD_DIM
        head_of = np.arange(d) // HEAD_DIM
        w.update(hm=jnp.asarray(head_of[:, None] == np.arange(128)[None, :], BF16),
                 hmt=jnp.asarray(np.arange(128)[:, None] == head_of[None, :], BF16),
                 qg=jnp.tile(na["q_norm"][i], nh).reshape(1, d),
                 kg=jnp.tile(na["k_norm"][i], nh).reshape(1, d),
                 w_qkv=na["w_qkv"][i].astype(BF16), w_out=na["w_out"][i].astype(BF16))
    return w


def kernel(x_prompt, x_sample, cache_k, cache_v, c, c_ctx, ada_w, ada_b, norm_mix, norm_ffn,
           pc_w_in, pc_pool_w, pc_pool_scale, pc_conv_w, pc_w_out,
           na_w_qkv, na_q_norm, na_k_norm, na_rpb, na_w_out,
           moe_router, moe_router_bias, moe_w_gate, moe_w_up, moe_w_down,
           moe_shared_gate, moe_shared_up, moe_shared_down):
    nb_p, s_p, d = x_prompt.shape
    nb_s, s_s, _ = x_sample.shape
    assert s_p == TM and s_s % TM == 0 and nb_p > 0 and nb_s > 0
    tp, ts = nb_p * s_p, nb_s * s_s
    nbp, bps = tp // TM, s_s // TM
    depth = ada_w.shape[0]
    nh = d // HEAD_DIM
    rows = s_s // GRID_W

    cond = jnp.concatenate([c_ctx[None], c], axis=0)
    cond = jnp.pad(cond, ((0, -cond.shape[0] % 8), (0, 0)))
    mods = _modulation(cond, ada_w, ada_b).reshape(depth, cond.shape[0], 6, d)
    pc = dict(w_in=pc_w_in, pool_w=pc_pool_w, pool_scale=pc_pool_scale, conv_w=pc_conv_w,
              w_out=pc_w_out)
    na = dict(w_qkv=na_w_qkv, q_norm=na_q_norm, k_norm=na_k_norm, w_out=na_w_out)
    moe = dict(router=moe_router, router_bias=moe_router_bias, s_gate=moe_shared_gate,
               s_up=moe_shared_up, s_down=moe_shared_down)

    xp, xs = x_prompt.reshape(tp, d), x_sample.reshape(ts, d)
    x = None
    new_k, new_v = [], []
    for layer in range(depth):
        i = layer // 2
        w = _layer_weights(layer, d, norm_mix, norm_ffn, pc, na, moe)
        mod = mods[layer]
        if layer % 2 == 0:
            if x is not None:
                xp, xs = x[:tp], x[tp:]
            x, h, gates_t, sel_t, cnt = _mixer(
                xp, xs, mod, w["g_mix"], w["w_in"], w["pool_w"], w["pool_scale"], w["conv_w"],
                w["w_out"], w["g_ffn"], w["wr_hi"], w["wr_lo"], w["rbias"], nbp, bps, s_p, s_s)
        else:
            midx = _mod_index(nbp, bps)
            qkv_args = (x, mod, w["g_mix"], w["w_qkv"], w["qg"], w["kg"], w["hm"], w["hmt"])
            qp, kp, vp, kp32, vp32 = _qkv(*qkv_args, 0, nbp, midx, True)
            qs, ks, vs = _qkv(*qkv_args, nbp, ts // TM, midx, False)
            new_k.append(kp32.reshape(nb_p, s_p, nh, HEAD_DIM))
            new_v.append(vp32.reshape(nb_p, s_p, nh, HEAD_DIM))
            a_p = _ctx_attn(qp, kp, vp, nb_p, s_p)
            lc = cache_k.shape[2]
            kc = cache_k[:, i].reshape(nb_s, lc, d).astype(BF16)
            vc = cache_v[:, i].reshape(nb_s, lc, d).astype(BF16)
            a_s = _na_attn(qs.reshape(nb_s, s_s, d), ks.reshape(nb_s, s_s, d),
                           vs.reshape(nb_s, s_s, d), kc, vc, _na_bias(na_rpb[i], rows), rows)
            x, h, gates_t, sel_t, cnt = _oproj(a_p, a_s.reshape(ts, d), x, mod, w["w_out"],
                                               w["g_ffn"], w["wr_hi"], w["wr_lo"], w["rbias"],
                                               nbp, bps)
        x = _sparse_moe(x, h, gates_t, sel_t, cnt, mod, layer, moe_w_gate, moe_w_up, moe_w_down,
                        w["sgu"], w["sd"], nbp, bps, split_out=layer == depth - 1)
    y_p, y_s = x
    return (y_p.reshape(nb_p, s_p, d), y_s.reshape(nb_s, s_s, d),
            jnp.stack(new_k, axis=1), jnp.stack(new_v, axis=1))
```

```python
import functools

import numpy as np
import jax
import jax.numpy as jnp
from jax import lax
from jax.experimental import pallas as pl
from jax.experimental.pallas import tpu as pltpu
from jax.experimental.pallas import tpu_sc as plsc

F32 = jnp.float32
BF16 = jnp.bfloat16
I32 = jnp.int32
U32 = jnp.uint32

TM = 256
HALO = 8
POOL_WINDOWS = (2, 4, 8, 16)
N_EXPERTS = 64
N_GROUPS = 8
GROUP_SIZE = N_EXPERTS // N_GROUPS
TOPK_GROUPS = 4
TOPK = 8
ROUTED_SCALE = 2.5
EPS = 1e-6
GRID_W = 64
WIN_H = 8
WIN_W = 16
HEAD_DIM = 64
NEG = float(np.finfo(np.float32).min)
VMEM_LIMIT = 56 * 1024 * 1024
NT_DIMS = (((1,), (1,)), ((), ()))
TG = 1024
SC_WINDOW = 64
SC_WORKERS = 32


def _cparams(sem):
    return pltpu.CompilerParams(dimension_semantics=sem, vmem_limit_bytes=VMEM_LIMIT)


def _silu(x):
    return x * jax.nn.sigmoid(x)


def _split_bf16(x):
    hi = x.astype(BF16)
    lo = (x - hi.astype(F32)).astype(BF16)
    return hi, lo


def _adaln(x, g, shift, scale):
    ms = jnp.mean(x * x, axis=-1, keepdims=True)
    return (x * lax.rsqrt(ms + EPS)) * g * (1.0 + scale) + shift


def _pack_pair(x):
    w = x.shape[1] // 2
    lo = lax.bitcast_convert_type(x[:, :w].astype(BF16).astype(F32), U32) >> 16
    hi = lax.bitcast_convert_type(x[:, w:].astype(BF16).astype(F32), U32)
    return lax.bitcast_convert_type(lo | hi, I32)


def _unpack_pair(p):
    u = lax.bitcast_convert_type(p, U32)
    lo = lax.bitcast_convert_type(u << 16, F32)
    hi = lax.bitcast_convert_type(u & jnp.uint32(0xFFFF0000), F32)
    return jnp.concatenate([lo, hi], axis=-1)


def _mod_kernel(cond_ref, w_ref, b_ref, o_ref):
    c = cond_ref[...]
    a = _silu(c)
    o_ref[0] = jnp.dot(a, w_ref[0], preferred_element_type=F32,
                       precision=lax.Precision.HIGHEST) + b_ref[0]


def _modulation(cond, ada_w, ada_b):
    depth, d, n = ada_w.shape
    rows = cond.shape[0]
    tn = 1536
    return pl.pallas_call(
        _mod_kernel,
        grid=(depth, n // tn),
        in_specs=[
            pl.BlockSpec((rows, d), lambda l, j: (0, 0)),
            pl.BlockSpec((1, d, tn), lambda l, j: (l, 0, j)),
            pl.BlockSpec((1, 1, tn), lambda l, j: (l, 0, j)),
        ],
        out_specs=pl.BlockSpec((1, rows, tn), lambda l, j: (l, 0, j)),
        out_shape=jax.ShapeDtypeStruct((depth, rows, n), F32),
        compiler_params=_cparams(("arbitrary", "arbitrary")),
        name="modulation",
    )(cond, ada_w, ada_b.reshape(depth, 1, n))


def _route(logits_t, bias_col):
    tm = logits_t.shape[1]
    scores = jax.nn.sigmoid(logits_t)
    biased = scores + bias_col
    sub = lax.broadcasted_iota(I32, (GROUP_SIZE, tm), 0).astype(F32)
    ninf = jnp.float32(-jnp.inf)
    groups, gscore = [], []
    for g in range(N_GROUPS):
        v = biased[g * GROUP_SIZE:(g + 1) * GROUP_SIZE]
        m1 = jnp.max(v, axis=0, keepdims=True)
        first = jnp.min(jnp.where(v == m1, sub, float(GROUP_SIZE)), axis=0, keepdims=True)
        m2 = jnp.max(jnp.where(sub == first, ninf, v), axis=0, keepdims=True)
        groups.append(v)
        gscore.append(m1 + m2)
    masked = []
    for g in range(N_GROUPS):
        rank = jnp.zeros((1, tm), I32)
        for g2 in range(N_GROUPS):
            if g2 == g:
                continue
            ahead = gscore[g2] > gscore[g]
            if g2 < g:
                ahead = ahead | (gscore[g2] == gscore[g])
            rank = rank + ahead.astype(I32)
        masked.append(jnp.where(rank < TOPK_GROUPS, groups[g], ninf))
    masked = jnp.concatenate(masked, axis=0)
    eidx = lax.broadcasted_iota(I32, (N_EXPERTS, tm), 0).astype(F32)
    sel = jnp.zeros((N_EXPERTS, tm), jnp.bool_)
    for _ in range(TOPK):
        best = jnp.max(masked, axis=0, keepdims=True)
        first = jnp.min(jnp.where(masked == best, eidx, float(N_EXPERTS)), axis=0, keepdims=True)
        hit = eidx == first
        sel = sel | hit
        masked = jnp.where(hit, ninf, masked)
    w = jnp.where(sel, scores, 0.0)
    wsum = jnp.sum(w, axis=0, keepdims=True)
    return w / wsum * ROUTED_SCALE, sel


def _ffn_pre(x1, mod_ref, gffn_ref, wr_hi_ref, wr_lo_ref, rb_ref,
             h_ref, gates_ref, sel_ref, cnt_ref):
    sh2 = mod_ref[0, 3:4, :]
    sc2 = mod_ref[0, 4:5, :]
    h = _adaln(x1, gffn_ref[...], sh2, sc2)
    h_hi, h_lo = _split_bf16(h)
    h_ref[...] = _pack_pair(h_hi)
    wr_hi = wr_hi_ref[...]
    logits = (jnp.dot(h_hi, wr_hi, preferred_element_type=F32)
              + jnp.dot(h_hi, wr_lo_ref[...], preferred_element_type=F32)
              + jnp.dot(h_lo, wr_hi, preferred_element_type=F32))
    gates_t, sel = _route(logits.T[:N_EXPERTS], rb_ref[...])
    gates_ref[...] = gates_t
    sel_b = sel.astype(F32).astype(BF16)
    sel_ref[...] = sel_b
    ones = jnp.ones((8, sel_b.shape[1]), BF16)
    cnt_ref[0] = lax.dot_general(ones, sel_b, NT_DIMS, preferred_element_type=F32)


def _pre_out_specs(t, d):
    specs = [
        pl.BlockSpec((TM, d), lambda i: (i, 0)),
        pl.BlockSpec((TM, d // 2), lambda i: (i, 0)),
        pl.BlockSpec((N_EXPERTS, TM), lambda i: (0, i)),
        pl.BlockSpec((N_EXPERTS, TM), lambda i: (0, i)),
        pl.BlockSpec((1, 8, N_EXPERTS), lambda i: (i, 0, 0)),
    ]
    shapes = [
        jax.ShapeDtypeStruct((t, d), F32),
        jax.ShapeDtypeStruct((t, d // 2), I32),
        jax.ShapeDtypeStruct((N_EXPERTS, t), F32),
        jax.ShapeDtypeStruct((N_EXPERTS, t), BF16),
        jax.ShapeDtypeStruct((t // TM, 8, N_EXPERTS), F32),
    ]
    return specs, shapes


def _mixer_kernel(nbp, bps, sp, ss,
                  xpc_ref, xsc_ref, xprev_ref, xnext_ref, mod_ref, gmix_ref, win_ref, pw_ref, ps_ref,
                  cw_ref,
                  wout_ref, gffn_ref, wr_hi_ref, wr_lo_ref, rb_ref,
                  x1_ref, h_ref, gates_ref, sel_ref, cnt_ref):
    i = pl.program_id(0)
    is_p = i < nbp
    j = lax.rem(jnp.maximum(i - nbp, 0), bps)
    first = is_p | (j == 0)
    last = is_p | (j == bps - 1)
    base = jnp.where(is_p, 0, j * TM)
    slen = jnp.where(is_p, sp, ss)

    sh1 = mod_ref[0, 0:1, :]
    sc1 = mod_ref[0, 1:2, :]
    g1 = mod_ref[0, 2:3, :]
    xc = jnp.where(is_p, xpc_ref[...], xsc_ref[...])
    x_ext = jnp.concatenate([xprev_ref[...], xc, xnext_ref[...]], axis=0)
    h_ext = _adaln(x_ext, gmix_ref[...], sh1, sc1).astype(BF16)
    u = jnp.dot(h_ext, win_ref[...], preferred_element_type=F32)
    next_ = TM + 2 * HALO
    row = lax.broadcasted_iota(I32, (next_, 1), 0)
    keep = ((row >= HALO) | jnp.logical_not(first)) & ((row < HALO + TM) | jnp.logical_not(last))
    u = jnp.where(keep, u, 0.0)

    dm = u.shape[1] // 4
    ua = u[:, :dm]
    gate_b = u[HALO:HALO + TM, dm:2 * dm]
    z = u[:, 2 * dm:3 * dm] * u[:, 3 * dm:]

    def up(a, k):
        return pltpu.roll(a, next_ - k, 0)

    pos = base + lax.broadcasted_iota(I32, (TM, 1), 0)
    pg = dm // len(POOL_WINDOWS)
    ya = []
    for g, w in enumerate(POOL_WINDOWS):
        e = ua[:, g * pg:(g + 1) * pg]
        acc = e
        span = 1
        while span < w:
            acc = acc + up(acc, span)
            span *= 2
        off = HALO - w // 2
        wsum = (up(acc, off) if off else acc)[:TM]
        lo = jnp.maximum(pos - w // 2, 0)
        hi = jnp.minimum(pos + (w - w // 2 - 1), slen - 1)
        cnt = (hi - lo + 1).astype(F32)
        diff = wsum / cnt - e[HALO:HALO + TM]
        ya.append(jnp.dot(diff.astype(BF16), pw_ref[g], preferred_element_type=F32))
    y_a = jnp.concatenate(ya, axis=-1) * ps_ref[...]
    zc = (cw_ref[0:1, :] * up(z, HALO - 1)[:TM] + cw_ref[1:2, :] * z[HALO:HALO + TM]
          + cw_ref[2:3, :] * up(z, HALO + 1)[:TM])
    y_b = gate_b * zc
    ycat = jnp.concatenate([y_a, y_b], axis=-1).astype(BF16)
    y = jnp.dot(ycat, wout_ref[...], preferred_element_type=F32)
    x1 = xc + g1 * y
    x1_ref[...] = x1
    _ffn_pre(x1, mod_ref, gffn_ref, wr_hi_ref, wr_lo_ref, rb_ref,
             h_ref, gates_ref, sel_ref, cnt_ref)


def _mod_index(nbp, bps):
    def f(i):
        return jnp.where(i < nbp, 0, 1 + jnp.maximum(i - nbp, 0) // bps)
    return f


def _const_spec(shape):
    nd = len(shape)
    return pl.BlockSpec(shape, lambda i: (0,) * nd)


def _mixer(xp, xs, mod, g_mix, w_in, pool_w, pool_scale, conv_w, w_out, g_ffn, wr_hi, wr_lo, rbias,
           nbp, bps, sp, ss):
    d = xp.shape[1]
    t = xp.shape[0] + xs.shape[0]
    nblk = t // TM
    midx = _mod_index(nbp, bps)
    hpb = TM // HALO
    nh = xs.shape[0] // HALO
    in_specs = [
        pl.BlockSpec((TM, d), lambda i: (jnp.minimum(i, nbp - 1), 0)),
        pl.BlockSpec((TM, d), lambda i: (jnp.maximum(i - nbp, 0), 0)),
        pl.BlockSpec((HALO, d), lambda i: (jnp.maximum((i - nbp) * hpb - 1, 0), 0)),
        pl.BlockSpec((HALO, d), lambda i: (jnp.clip((i - nbp + 1) * hpb, 0, nh - 1), 0)),
        pl.BlockSpec((1, 6, d), lambda i: (midx(i), 0, 0)),
        _const_spec(g_mix.shape), _const_spec(w_in.shape), _const_spec(pool_w.shape),
        _const_spec(pool_scale.shape), _const_spec(conv_w.shape), _const_spec(w_out.shape),
        _const_spec(g_ffn.shape), _const_spec(wr_hi.shape), _const_spec(wr_lo.shape),
        _const_spec(rbias.shape),
    ]
    out_specs, out_shape = _pre_out_specs(t, d)
    return pl.pallas_call(
        functools.partial(_mixer_kernel, nbp, bps, sp, ss),
        grid=(nblk,), in_specs=in_specs, out_specs=out_specs, out_shape=out_shape,
        compiler_params=_cparams(("parallel",)), name="pool_conv_mixer",
    )(xp, xs, xs, xs, mod, g_mix, w_in, pool_w, pool_scale, conv_w, w_out, g_ffn, wr_hi, wr_lo,
      rbias)


def _slots_kernel(sel_ref, gates_ref, base_ref, pos_ref, w_ref):
    sel = sel_ref[...]
    tm = sel.shape[1]
    r = lax.broadcasted_iota(I32, (tm, tm), 0)
    c = lax.broadcasted_iota(I32, (tm, tm), 1)
    before = (r < c).astype(F32).astype(BF16)
    rank_tok = jnp.dot(sel, before, preferred_element_type=F32)
    er = lax.broadcasted_iota(I32, (N_EXPERTS, N_EXPERTS), 0)
    ec = lax.broadcasted_iota(I32, (N_EXPERTS, N_EXPERTS), 1)
    lower = (ec < er).astype(F32).astype(BF16)
    rank_exp = jnp.dot(lower, sel, preferred_element_type=F32)
    slot = base_ref[0] + rank_tok
    chosen = sel > 0
    gates = gates_ref[...]
    sub = lax.broadcasted_iota(I32, (TOPK, tm), 0)
    pos8 = jnp.zeros((TOPK, tm), F32)
    w8 = jnp.zeros((TOPK, tm), F32)
    for k in range(TOPK):
        mk = chosen & (rank_exp == float(k))
        pk = jnp.sum(jnp.where(mk, slot, 0.0), axis=0, keepdims=True)
        wk = jnp.sum(jnp.where(mk, gates, 0.0), axis=0, keepdims=True)
        pos8 = jnp.where(sub == k, pk, pos8)
        w8 = jnp.where(sub == k, wk, w8)
    pos_ref[...] = pos8.astype(I32)
    w8 = jnp.concatenate([w8, jnp.zeros((128 - TOPK, tm), F32)], axis=0)
    w_ref[...] = w8.T


def _slots(sel_t, gates_t, base):
    ne, t = sel_t.shape
    return pl.pallas_call(
        _slots_kernel, grid=(t // TM,),
        in_specs=[
            pl.BlockSpec((ne, TM), lambda i: (0, i)),
            pl.BlockSpec((ne, TM), lambda i: (0, i)),
            pl.BlockSpec((1, ne, 1), lambda i: (i, 0, 0)),
        ],
        out_specs=[pl.BlockSpec((TOPK, TM), lambda i: (0, i)),
                   pl.BlockSpec((TM, 128), lambda i: (i, 0))],
        out_shape=[jax.ShapeDtypeStruct((TOPK, t), I32), jax.ShapeDtypeStruct((t, 128), F32)],
        compiler_params=_cparams(("parallel",)), name="moe_slots",
    )(sel_t, gates_t, base)


def _sc_gather(table, idx):
    m = idx.shape[0]
    d = table.shape[1]
    assert m % (SC_WINDOW * SC_WORKERS) == 0
    mesh = plsc.VectorSubcoreMesh(core_axis_name="core", subcore_axis_name="subcore")

    @pl.kernel(out_type=jax.ShapeDtypeStruct((m, d), table.dtype), mesh=mesh)
    def gather_rows(x_hbm, i_hbm, o_hbm):
        def body(i_vmem, o_vmem):
            pltpu.sync_copy(x_hbm.at[i_vmem.at[0]], o_vmem)

        pltpu.emit_pipeline(
            body, grid=(m // SC_WINDOW,),
            in_specs=[pl.BlockSpec((1, SC_WINDOW), index_map=lambda i: (i, 0))],
            out_specs=[pl.BlockSpec((SC_WINDOW, d), index_map=lambda i: (i, 0))],
            core_axis_name=("core", "subcore"),
            dimension_semantics=(pltpu.PARALLEL,),
        )(i_hbm, o_hbm)

    return gather_rows(table, idx.reshape(m // SC_WINDOW, SC_WINDOW))


def _sc_scatter(rows, idx, n_out):
    nk, t = idx.shape
    d = rows.shape[1]
    assert t % (SC_WINDOW * SC_WORKERS) == 0
    mesh = plsc.VectorSubcoreMesh(core_axis_name="core", subcore_axis_name="subcore")

    @pl.kernel(out_type=jax.ShapeDtypeStruct((n_out, d), rows.dtype), mesh=mesh)
    def scatter_rows(x_hbm, i_hbm, o_hbm):
        def body(x_vmem, i_vmem):
            for k in range(nk):
                pltpu.sync_copy(x_vmem, o_hbm.at[i_vmem.at[k, 0]])

        pltpu.emit_pipeline(
            body, grid=(t // SC_WINDOW,),
            in_specs=[pl.BlockSpec((SC_WINDOW, d), index_map=lambda i: (i, 0)),
                      pl.BlockSpec((nk, 1, SC_WINDOW), index_map=lambda i: (0, i, 0))],
            out_specs=[],
            core_axis_name=("core", "subcore"),
            dimension_semantics=(pltpu.PARALLEL,),
        )(x_hbm, i_hbm)

    return scatter_rows(rows, idx.reshape(nk, t // SC_WINDOW, SC_WINDOW))


def _expert_ffn_kernel(te_ref, fresh_ref, nv_ref, x_ref, wg_ref, wu_ref, wd_ref, y_ref,
                       wg_s, wu_s, wd_s):
    j = pl.program_id(0)

    @pl.when(j < nv_ref[0])
    def _():
        @pl.when(fresh_ref[j] == 1)
        def _():
            wg_s[...] = wg_ref[0, 0].astype(BF16)
            wu_s[...] = wu_ref[0, 0].astype(BF16)
            wd_s[...] = wd_ref[0, 0].astype(BF16)

        x = _unpack_pair(x_ref[...]).astype(BF16)
        a = jnp.dot(x, wg_s[...], preferred_element_type=F32)
        b = jnp.dot(x, wu_s[...], preferred_element_type=F32)
        act = (_silu(a) * b).astype(BF16)
        y_ref[...] = _pack_pair(jnp.dot(act, wd_s[...], preferred_element_type=F32))


def _expert_ffn(xs, n_tiles, tile_expert, fresh, n_valid, layer, w_gate, w_up, w_down):
    dh = xs.shape[1]
    last = lambda j, te, fr, nv: jnp.minimum(j, nv[0] - 1)
    wspec = lambda w: pl.BlockSpec((1, 1) + w.shape[2:],
                                   lambda j, te, fr, nv: (layer, te[j], 0, 0))
    grid_spec = pltpu.PrefetchScalarGridSpec(
        num_scalar_prefetch=3, grid=(n_tiles,),
        in_specs=[
            pl.BlockSpec((TG, dh), lambda j, te, fr, nv: (last(j, te, fr, nv), 0)),
            wspec(w_gate), wspec(w_up), wspec(w_down),
        ],
        out_specs=pl.BlockSpec((TG, dh), lambda j, te, fr, nv: (last(j, te, fr, nv), 0)),
        scratch_shapes=[pltpu.VMEM(w_gate.shape[2:], BF16), pltpu.VMEM(w_up.shape[2:], BF16),
                        pltpu.VMEM(w_down.shape[2:], BF16)],
    )
    return pl.pallas_call(
        _expert_ffn_kernel, grid_spec=grid_spec,
        out_shape=jax.ShapeDtypeStruct((n_tiles * TG, dh), I32),
        compiler_params=_cparams(("arbitrary",)), name="expert_ffn",
    )(tile_expert, fresh, n_valid, xs, w_gate, w_up, w_down)


def _combine_kernel(nbp, yg_ref, w_ref, h_ref, x_ref, mod_ref, sgu_ref, sd_ref, *o_refs):
    f = sd_ref.shape[0]
    h = _unpack_pair(h_ref[...]).astype(BF16)
    hs = jnp.dot(h, sgu_ref[...], preferred_element_type=F32)
    act = (_silu(hs[:, :f]) * hs[:, f:]).astype(BF16)
    acc = jnp.dot(act, sd_ref[...], preferred_element_type=F32)
    w = w_ref[...]
    for k in range(TOPK):
        acc = acc + w[:, k:k + 1] * _unpack_pair(yg_ref[k])
    out = x_ref[...] + mod_ref[0, 5:6, :] * acc
    if len(o_refs) == 1:
        o_refs[0][...] = out
    else:
        i = pl.program_id(0)

        @pl.when(i < nbp)
        def _():
            o_refs[0][...] = out

        @pl.when(i >= nbp)
        def _():
            o_refs[1][...] = out


def _combine(yg, w8, h, x, mod, sgu, sd, nbp, bps, split_out):
    t, d = x.shape
    tc = 2 * TM
    assert (nbp * TM) % tc == 0 and (bps * TM) % tc == 0
    nbp, bps = nbp * TM // tc, bps * TM // tc
    midx = _mod_index(nbp, bps)
    if split_out:
        out_specs = [pl.BlockSpec((tc, d), lambda i: (jnp.minimum(i, nbp - 1), 0)),
                     pl.BlockSpec((tc, d), lambda i: (jnp.maximum(i - nbp, 0), 0))]
        out_shape = [jax.ShapeDtypeStruct((nbp * tc, d), F32),
                     jax.ShapeDtypeStruct((t - nbp * tc, d), F32)]
    else:
        out_specs = pl.BlockSpec((tc, d), lambda i: (i, 0))
        out_shape = jax.ShapeDtypeStruct((t, d), F32)
    in_specs = [
        pl.BlockSpec((TOPK, tc, d // 2), lambda i: (0, i, 0)),
        pl.BlockSpec((tc, 128), lambda i: (i, 0)),
        pl.BlockSpec((tc, d // 2), lambda i: (i, 0)),
        pl.BlockSpec((tc, d), lambda i: (i, 0)),
        pl.BlockSpec((1, 6, d), lambda i: (midx(i), 0, 0)),
        _const_spec(sgu.shape), _const_spec(sd.shape),
    ]
    return pl.pallas_call(
        functools.partial(_combine_kernel, nbp), grid=(t // tc,), in_specs=in_specs,
        out_specs=out_specs, out_shape=out_shape,
        compiler_params=_cparams(("arbitrary",)), name="moe_combine",
    )(yg, w8, h, x, mod, sgu, sd)


def _sparse_moe(x, h, gates_t, sel_t, cnt, mod, layer, w_gate, w_up, w_down, sgu, sd, nbp, bps,
                split_out):
    t, d = x.shape
    ne = N_EXPERTS
    n_pad = ne * TG
    n_tiles = (t * TOPK + n_pad) // TG
    n_slots = n_tiles * TG
    counts = cnt[:, 0, :].astype(I32)
    per_expert = jnp.sum(counts, axis=0)
    padded = (per_expert + TG - 1) // TG * TG
    ends = jnp.cumsum(padded)
    starts = ends - padded
    block_off = jnp.cumsum(counts, axis=0) - counts
    base = (starts[None, :] + block_off).astype(F32)[:, :, None]
    n_valid = (ends[-1] // TG).astype(I32).reshape(1)
    tile_start = jnp.minimum(jnp.arange(n_tiles, dtype=I32), n_valid[0] - 1) * TG
    tile_expert = jnp.sum((ends[None, :] <= tile_start[:, None]).astype(I32), axis=1)
    fresh = jnp.concatenate([jnp.ones((1,), I32),
                             (tile_expert[1:] != tile_expert[:-1]).astype(I32)])

    pos8, w8 = _slots(sel_t, gates_t, base)
    n_extra = -(-n_pad // t)
    cand = (starts + per_expert)[:, None] + jnp.arange(TG, dtype=I32)[None, :]
    spare = n_slots + jnp.arange(n_pad, dtype=I32)
    fill = jnp.where(cand < ends[:, None], cand, spare.reshape(ne, TG)).reshape(-1)
    rest = n_slots + jnp.arange(n_pad, n_extra * t, dtype=I32) % n_pad
    dest = jnp.concatenate([pos8, jnp.concatenate([fill, rest]).reshape(n_extra, t)], axis=0)
    xs = _sc_scatter(h, dest, n_slots + n_pad)
    ys = _expert_ffn(xs, n_tiles, tile_expert, fresh, n_valid, layer, w_gate, w_up, w_down)
    yg = _sc_gather(ys, pos8.reshape(-1)).reshape(TOPK, t, d // 2)
    return _combine(yg, w8, h, x, mod, sgu, sd, nbp, bps, split_out)


def _head_rms(x, g_row, hm_ref, hmt_ref):
    sq_hi, sq_lo = _split_bf16(x * x)
    hm = hm_ref[...]
    ss = (jnp.dot(sq_hi, hm, preferred_element_type=F32)
          + jnp.dot(sq_lo, hm, preferred_element_type=F32))
    r = lax.rsqrt(ss * (1.0 / HEAD_DIM) + EPS)
    r_hi, r_lo = _split_bf16(r)
    hmt = hmt_ref[...]
    rb = (jnp.dot(r_hi, hmt, preferred_element_type=F32)
          + jnp.dot(r_lo, hmt, preferred_element_type=F32))
    return (x * rb) * g_row


def _qkv_kernel(emit_f32, x_ref, mod_ref, gmix_ref, w_ref, qg_ref, kg_ref, hm_ref, hmt_ref, *outs):
    d = x_ref.shape[1]
    sh1 = mod_ref[0, 0:1, :]
    sc1 = mod_ref[0, 1:2, :]
    h = _adaln(x_ref[...], gmix_ref[...], sh1, sc1).astype(BF16)
    qkv = jnp.dot(h, w_ref[...], preferred_element_type=F32)
    q = _head_rms(qkv[:, :d], qg_ref[...], hm_ref, hmt_ref)
    k = _head_rms(qkv[:, d:2 * d], kg_ref[...], hm_ref, hmt_ref)
    v = qkv[:, 2 * d:]
    outs[0][...] = q.astype(BF16)
    outs[1][...] = k.astype(BF16)
    outs[2][...] = v.astype(BF16)
    if emit_f32:
        outs[3][...] = k
        outs[4][...] = v


def _qkv(x, mod, g_mix, w_qkv, qg, kg, hm, hmt, blk0, nblk, midx, emit_f32):
    t, d = x.shape
    in_specs = [
        pl.BlockSpec((TM, d), lambda i: (i + blk0, 0)),
        pl.BlockSpec((1, 6, d), lambda i: (midx(i + blk0), 0, 0)),
        _const_spec(g_mix.shape), _const_spec(w_qkv.shape), _const_spec(qg.shape),
        _const_spec(kg.shape), _const_spec(hm.shape), _const_spec(hmt.shape),
    ]
    n_out = 5 if emit_f32 else 3
    out_specs = [pl.BlockSpec((TM, d), lambda i: (i, 0)) for _ in range(n_out)]
    out_shape = [jax.ShapeDtypeStruct((nblk * TM, d), BF16 if o < 3 else F32) for o in range(n_out)]
    return pl.pallas_call(
        functools.partial(_qkv_kernel, emit_f32),
        grid=(nblk,), in_specs=in_specs, out_specs=out_specs, out_shape=out_shape,
        compiler_params=_cparams(("parallel",)), name="qkv_f32" if emit_f32 else "qkv",
    )(x, mod, g_mix, w_qkv, qg, kg, hm, hmt)


def _head_masks():
    lane = lax.broadcasted_iota(I32, (1, 2 * HEAD_DIM), 1)
    return lane < HEAD_DIM


def _ctx_attn_kernel(q_ref, k_ref, v_ref, o_ref):
    lo = _head_masks()
    pw = 2 * HEAD_DIM
    for hp in range(q_ref.shape[1] // pw):
        cols = slice(hp * pw, (hp + 1) * pw)
        q = q_ref[:, cols]
        k = k_ref[:, cols]
        v = v_ref[:, cols]
        outs = []
        for hh in range(2):
            msk = lo if hh == 0 else jnp.logical_not(lo)
            qm = jnp.where(msk, q, jnp.zeros_like(q)) * jnp.asarray(HEAD_DIM ** -0.5, BF16)
            s = lax.dot_general(qm, k, NT_DIMS, preferred_element_type=F32)
            m = jnp.max(s, axis=-1, keepdims=True)
            p = jnp.exp(s - m)
            l = jnp.sum(p, axis=-1, keepdims=True)
            o = jnp.dot(p.astype(BF16), v, preferred_element_type=F32)
            outs.append(o / l)
        o_ref[:, cols] = jnp.where(lo, outs[0], outs[1]).astype(BF16)


def _ctx_attn(q, k, v, nb, s):
    t, d = q.shape
    spec = pl.BlockSpec((s, d), lambda b: (b, 0))
    return pl.pallas_call(
        _ctx_attn_kernel, grid=(nb,), in_specs=[spec, spec, spec], out_specs=spec,
        out_shape=jax.ShapeDtypeStruct((nb * s, d), BF16),
        compiler_params=_cparams(("parallel",)), name="context_attention",
    )(q, k, v)


NA_QROWS = 8
NA_GROWS = 4
NA_KROWS = 12
NA_HPAIRS = 2


def _na_kernel(rows, q_ref, k_ref, v_ref, kc_ref, vc_ref, bias_ref, o_ref):
    rb = pl.program_id(2)
    ngrp = NA_QROWS // NA_GROWS
    nq = NA_GROWS * GRID_W
    nk = NA_KROWS * GRID_W
    lo = _head_masks()
    pw = 2 * HEAD_DIM
    for gi in range(ngrp):
        g = rb * ngrp + gi
        kr0 = jnp.clip(g * NA_GROWS - WIN_H // 2, 0, rows - NA_KROWS)
        start = pl.multiple_of(kr0 * GRID_W, 256)
        cls = jnp.where(g == 0, 0, jnp.where(g == rows // NA_GROWS - 1, 2, 1))
        qrows = slice(gi * nq, (gi + 1) * nq)
        for pp in range(NA_HPAIRS):
            cols = slice(pp * pw, (pp + 1) * pw)
            q = q_ref[0, qrows, cols]
            kw = k_ref[0, pl.ds(start, nk), cols]
            vw = v_ref[0, pl.ds(start, nk), cols]
            kc = kc_ref[0, :, cols]
            vc = vc_ref[0, :, cols]
            outs = []
            for hh in range(2):
                msk = lo if hh == 0 else jnp.logical_not(lo)
                qm = jnp.where(msk, q, jnp.zeros_like(q)) * jnp.asarray(HEAD_DIM ** -0.5, BF16)
                s = (lax.dot_general(qm, kw, NT_DIMS, preferred_element_type=F32)
                     + bias_ref[cls, 2 * pp + hh])
                sc = lax.dot_general(qm, kc, NT_DIMS, preferred_element_type=F32)
                m = jnp.maximum(jnp.max(s, axis=-1, keepdims=True),
                                jnp.max(sc, axis=-1, keepdims=True))
                p = jnp.exp(s - m)
                pc = jnp.exp(sc - m)
                l = jnp.sum(p, axis=-1, keepdims=True) + jnp.sum(pc, axis=-1, keepdims=True)
                o = (jnp.dot(p.astype(BF16), vw, preferred_element_type=F32)
                     + jnp.dot(pc.astype(BF16), vc, preferred_element_type=F32))
                outs.append(o / l)
            o_ref[0, qrows, cols] = jnp.where(lo, outs[0], outs[1]).astype(BF16)


def _na_row_classes(rows):
    out = []
    for g in (0, 1, rows // NA_GROWS - 1):
        r0 = g * NA_GROWS
        kr0 = int(np.clip(r0 - WIN_H // 2, 0, rows - NA_KROWS))
        table = []
        for rl in range(NA_GROWS):
            r = r0 + rl
            sr = int(np.clip(r - WIN_H // 2, 0, rows - WIN_H))
            table.append([(kr0 + kl - r + WIN_H - 1) if sr <= kr0 + kl < sr + WIN_H else None
                          for kl in range(NA_KROWS)])
        out.append(table)
    return out


def _na_bias_kernel(rows, rpb_ref, o_ref):
    h = pl.program_id(0)
    n_dr, n_dc = 2 * WIN_H - 1, 2 * WIN_W - 1
    cq = lax.broadcasted_iota(I32, (GRID_W, 2 * GRID_W), 0)
    lane = lax.broadcasted_iota(I32, (GRID_W, 2 * GRID_W), 1)
    ck = lane & (GRID_W - 1)
    q_start = jnp.clip(cq - WIN_W // 2, 0, GRID_W - WIN_W)
    col_ok = (ck >= q_start) & (ck < q_start + WIN_W)
    dc = ck - cq + (WIN_W - 1)
    neg = jnp.full((GRID_W, 2 * GRID_W), NEG, F32)
    tiles = []
    for i in range(n_dr):
        t = jnp.zeros((GRID_W, 2 * GRID_W), F32)
        for jj in range(n_dc):
            t = jnp.where(dc == jj, rpb_ref[h * (n_dr * n_dc) + i * n_dc + jj], t)
        tiles.append(jnp.where(col_ok, t, neg))
    first_half = lane < GRID_W
    for c, table in enumerate(_na_row_classes(rows)):
        for rl in range(NA_GROWS):
            for m in range(NA_KROWS // 2):
                ia, ib = table[rl][2 * m], table[rl][2 * m + 1]
                ta = neg if ia is None else tiles[ia]
                tb = neg if ib is None else tiles[ib]
                blk = ta if ia == ib else jnp.where(first_half, ta, tb)
                o_ref[c, 0, rl * GRID_W:(rl + 1) * GRID_W, m * 2 * GRID_W:(m + 1) * 2 * GRID_W] = blk


def _na_bias(rpb, rows):
    nh = rpb.shape[0]
    nq, nk = NA_GROWS * GRID_W, NA_KROWS * GRID_W
    return pl.pallas_call(
        functools.partial(_na_bias_kernel, rows), grid=(nh,),
        in_specs=[pl.BlockSpec(memory_space=pltpu.SMEM)],
        out_specs=pl.BlockSpec((3, 1, nq, nk), lambda h: (0, h, 0, 0)),
        out_shape=jax.ShapeDtypeStruct((3, nh, nq, nk), F32),
        compiler_params=_cparams(("parallel",)), name="na_bias",
    )(rpb.reshape(-1))


def _na_attn(q, k, v, kc, vc, bias_tab, rows):
    nb, s, d = q.shape
    lw = NA_HPAIRS * 2 * HEAD_DIM
    hp = d // lw
    nrb = rows // NA_QROWS
    nq = NA_QROWS * GRID_W
    lc = kc.shape[1]
    in_specs = [
        pl.BlockSpec((1, nq, lw), lambda h, b, r: (b, r, h)),
        pl.BlockSpec((1, s, lw), lambda h, b, r: (b, 0, h)),
        pl.BlockSpec((1, s, lw), lambda h, b, r: (b, 0, h)),
        pl.BlockSpec((1, lc, lw), lambda h, b, r: (b, 0, h)),
        pl.BlockSpec((1, lc, lw), lambda h, b, r: (b, 0, h)),
        pl.BlockSpec((3, 2 * NA_HPAIRS) + bias_tab.shape[2:], lambda h, b, r: (0, h, 0, 0)),
    ]
    return pl.pallas_call(
        functools.partial(_na_kernel, rows),
        grid=(hp, nb, nrb), in_specs=in_specs,
        out_specs=pl.BlockSpec((1, nq, lw), lambda h, b, r: (b, r, h)),
        out_shape=jax.ShapeDtypeStruct((nb, s, d), BF16),
        compiler_params=_cparams(("parallel", "parallel", "parallel")),
        name="neighbourhood_attention",
    )(q, k, v, kc, vc, bias_tab)


def _oproj_kernel(nbp, ap_ref, as_ref, x_ref, mod_ref, wout_ref, gffn_ref, wr_hi_ref, wr_lo_ref,
                  rb_ref, x1_ref, h_ref, gates_ref, sel_ref, cnt_ref):
    a = jnp.where(pl.program_id(0) < nbp, ap_ref[...], as_ref[...])
    y = jnp.dot(a, wout_ref[...], preferred_element_type=F32)
    x1 = x_ref[...] + mod_ref[0, 2:3, :] * y
    x1_ref[...] = x1
    _ffn_pre(x1, mod_ref, gffn_ref, wr_hi_ref, wr_lo_ref, rb_ref,
             h_ref, gates_ref, sel_ref, cnt_ref)


def _oproj(attn_p, attn_s, x, mod, w_out, g_ffn, wr_hi, wr_lo, rbias, nbp, bps):
    t, d = x.shape
    midx = _mod_index(nbp, bps)
    in_specs = [
        pl.BlockSpec((TM, d), lambda i: (jnp.minimum(i, nbp - 1), 0)),
        pl.BlockSpec((TM, d), lambda i: (jnp.maximum(i - nbp, 0), 0)),
        pl.BlockSpec((TM, d), lambda i: (i, 0)),
        pl.BlockSpec((1, 6, d), lambda i: (midx(i), 0, 0)),
        _const_spec(w_out.shape), _const_spec(g_ffn.shape), _const_spec(wr_hi.shape),
        _const_spec(wr_lo.shape), _const_spec(rbias.shape),
    ]
    out_specs, out_shape = _pre_out_specs(t, d)
    return pl.pallas_call(
        functools.partial(_oproj_kernel, nbp), grid=(t // TM,), in_specs=in_specs,
        out_specs=out_specs, out_shape=out_shape, compiler_params=_cparams(("parallel",)),
        name="attn_out_proj",
    )(attn_p, attn_s, x, mod, w_out, g_ffn, wr_hi, wr_lo, rbias)


def _router_weights(w_router, router_bias):
    d, ne = w_router.shape
    w = jnp.pad(w_router, ((0, 0), (0, 128 - ne)))
    hi = w.astype(BF16)
    lo = (w - hi.astype(F32)).astype(BF16)
    return hi, lo, router_bias.reshape(ne, 1)


def _shared_weights(s_gate, s_up, s_down):
    return jnp.concatenate([s_gate, s_up], axis=-1).astype(BF16), s_down.astype(BF16)


def _layer_weights(layer, d, norm_mix, norm_ffn, pc, na, moe):
    i = layer // 2
    w = dict(g_mix=norm_mix[layer].reshape(1, d), g_ffn=norm_ffn[layer].reshape(1, d))
    w["wr_hi"], w["wr_lo"], w["rbias"] = _router_weights(moe["router"][layer],
                                                         moe["router_bias"][layer])
    w["sgu"], w["sd"] = _shared_weights(moe["s_gate"][layer], moe["s_up"][layer],
                                        moe["s_down"][layer])
    if layer % 2 == 0:
        w.update(w_in=pc["w_in"][i].astype(BF16), pool_w=pc["pool_w"][i].astype(BF16),
                 pool_scale=pc["pool_scale"][i].reshape(1, -1), conv_w=pc["conv_w"][i],
                 w_out=pc["w_out"][i].astype(BF16))
    else:
        nh = d // HEAD_DIM
        head_of = np.arange(d) // HEAD_DIM
        w.update(hm=jnp.asarray(head_of[:, None] == np.arange(128)[None, :], BF16),
                 hmt=jnp.asarray(np.arange(128)[:, None] == head_of[None, :], BF16),
                 qg=jnp.tile(na["q_norm"][i], nh).reshape(1, d),
                 kg=jnp.tile(na["k_norm"][i], nh).reshape(1, d),
                 w_qkv=na["w_qkv"][i].astype(BF16), w_out=na["w_out"][i].astype(BF16))
    return w


def kernel(x_prompt, x_sample, cache_k, cache_v, c, c_ctx, ada_w, ada_b, norm_mix, norm_ffn,
           pc_w_in, pc_pool_w, pc_pool_scale, pc_conv_w, pc_w_out,
           na_w_qkv, na_q_norm, na_k_norm, na_rpb, na_w_out,
           moe_router, moe_router_bias, moe_w_gate, moe_w_up, moe_w_down,
           moe_shared_gate, moe_shared_up, moe_shared_down):
    nb_p, s_p, d = x_prompt.shape
    nb_s, s_s, _ = x_sample.shape
    assert s_p == TM and s_s % TM == 0 and nb_p > 0 and nb_s > 0
    tp, ts = nb_p * s_p, nb_s * s_s
    nbp, bps = tp // TM, s_s // TM
    depth = ada_w.shape[0]
    nh = d // HEAD_DIM
    rows = s_s // GRID_W

    cond = jnp.concatenate([c_ctx[None], c], axis=0)
    cond = jnp.pad(cond, ((0, -cond.shape[0] % 8), (0, 0)))
    mods = _modulation(cond, ada_w, ada_b).reshape(depth, cond.shape[0], 6, d)
    pc = dict(w_in=pc_w_in, pool_w=pc_pool_w, pool_scale=pc_pool_scale, conv_w=pc_conv_w,
              w_out=pc_w_out)
    na = dict(w_qkv=na_w_qkv, q_norm=na_q_norm, k_norm=na_k_norm, w_out=na_w_out)
    moe = dict(router=moe_router, router_bias=moe_router_bias, s_gate=moe_shared_gate,
               s_up=moe_shared_up, s_down=moe_shared_down)

    xp, xs = x_prompt.reshape(tp, d), x_sample.reshape(ts, d)
    x = None
    new_k, new_v = [], []
    for layer in range(depth):
        i = layer // 2
        w = _layer_weights(layer, d, norm_mix, norm_ffn, pc, na, moe)
        mod = mods[layer]
        if layer % 2 == 0:
            if x is not None:
                xp, xs = x[:tp], x[tp:]
            x, h, gates_t, sel_t, cnt = _mixer(
                xp, xs, mod, w["g_mix"], w["w_in"], w["pool_w"], w["pool_scale"], w["conv_w"],
                w["w_out"], w["g_ffn"], w["wr_hi"], w["wr_lo"], w["rbias"], nbp, bps, s_p, s_s)
        else:
            midx = _mod_index(nbp, bps)
            qkv_args = (x, mod, w["g_mix"], w["w_qkv"], w["qg"], w["kg"], w["hm"], w["hmt"])
            qp, kp, vp, kp32, vp32 = _qkv(*qkv_args, 0, nbp, midx, True)
            qs, ks, vs = _qkv(*qkv_args, nbp, ts // TM, midx, False)
            new_k.append(kp32.reshape(nb_p, s_p, nh, HEAD_DIM))
            new_v.append(vp32.reshape(nb_p, s_p, nh, HEAD_DIM))
            a_p = _ctx_attn(qp, kp, vp, nb_p, s_p)
            lc = cache_k.shape[2]
            kc = cache_k[:, i].reshape(nb_s, lc, d).astype(BF16)
            vc = cache_v[:, i].reshape(nb_s, lc, d).astype(BF16)
            a_s = _na_attn(qs.reshape(nb_s, s_s, d), ks.reshape(nb_s, s_s, d),
                           vs.reshape(nb_s, s_s, d), kc, vc, _na_bias(na_rpb[i], rows), rows)
            x, h, gates_t, sel_t, cnt = _oproj(a_p, a_s.reshape(ts, d), x, mod, w["w_out"],
                                               w["g_ffn"], w["wr_hi"], w["wr_lo"], w["rbias"],
                                               nbp, bps)
        x = _sparse_moe(x, h, gates_t, sel_t, cnt, mod, layer, moe_w_gate, moe_w_up, moe_w_down,
                        w["sgu"], w["sd"], nbp, bps, split_out=layer == depth - 1)
    y_p, y_s = x
    return (y_p.reshape(nb_p, s_p, d), y_s.reshape(nb_s, s_s, d),
            jnp.stack(new_k, axis=1), jnp.stack(new_v, axis=1))
```

```python
import functools

import numpy as np
import jax
import jax.numpy as jnp
from jax import lax
from jax.experimental import pallas as pl
from jax.experimental.pallas import tpu as pltpu
from jax.experimental.pallas import tpu_sc as plsc

F32 = jnp.float32
BF16 = jnp.bfloat16
I32 = jnp.int32
U32 = jnp.uint32

TM = 256
HALO = 8
POOL_WINDOWS = (2, 4, 8, 16)
N_EXPERTS = 64
N_GROUPS = 8
GROUP_SIZE = N_EXPERTS // N_GROUPS
TOPK_GROUPS = 4
TOPK = 8
ROUTED_SCALE = 2.5
EPS = 1e-6
GRID_W = 64
WIN_H = 8
WIN_W = 16
HEAD_DIM = 64
NEG = float(np.finfo(np.float32).min)
VMEM_LIMIT = 56 * 1024 * 1024
NT_DIMS = (((1,), (1,)), ((), ()))
TG = 1024
SC_WINDOW = 64
SC_WORKERS = 32


def _cparams(sem):
    return pltpu.CompilerParams(dimension_semantics=sem, vmem_limit_bytes=VMEM_LIMIT)


def _silu(x):
    return x * jax.nn.sigmoid(x)


def _split_bf16(x):
    hi = x.astype(BF16)
    lo = (x - hi.astype(F32)).astype(BF16)
    return hi, lo


def _adaln(x, g, shift, scale):
    ms = jnp.mean(x * x, axis=-1, keepdims=True)
    return (x * lax.rsqrt(ms + EPS)) * g * (1.0 + scale) + shift


def _pack_pair(x):
    w = x.shape[1] // 2
    lo = lax.bitcast_convert_type(x[:, :w].astype(BF16).astype(F32), U32) >> 16
    hi = lax.bitcast_convert_type(x[:, w:].astype(BF16).astype(F32), U32)
    return lax.bitcast_convert_type(lo | hi, I32)


def _unpack_pair(p):
    u = lax.bitcast_convert_type(p, U32)
    lo = lax.bitcast_convert_type(u << 16, F32)
    hi = lax.bitcast_convert_type(u & jnp.uint32(0xFFFF0000), F32)
    return jnp.concatenate([lo, hi], axis=-1)


def _mod_kernel(cond_ref, w_ref, b_ref, o_ref):
    c = cond_ref[...]
    a = _silu(c)
    o_ref[0] = jnp.dot(a, w_ref[0], preferred_element_type=F32,
                       precision=lax.Precision.HIGHEST) + b_ref[0]


def _modulation(cond, ada_w, ada_b):
    depth, d, n = ada_w.shape
    rows = cond.shape[0]
    tn = 1536
    return pl.pallas_call(
        _mod_kernel,
        grid=(depth, n // tn),
        in_specs=[
            pl.BlockSpec((rows, d), lambda l, j: (0, 0)),
            pl.BlockSpec((1, d, tn), lambda l, j: (l, 0, j)),
            pl.BlockSpec((1, 1, tn), lambda l, j: (l, 0, j)),
        ],
        out_specs=pl.BlockSpec((1, rows, tn), lambda l, j: (l, 0, j)),
        out_shape=jax.ShapeDtypeStruct((depth, rows, n), F32),
        compiler_params=_cparams(("arbitrary", "arbitrary")),
        name="modulation",
    )(cond, ada_w, ada_b.reshape(depth, 1, n))


def _route(logits_t, bias_col):
    tm = logits_t.shape[1]
    scores = jax.nn.sigmoid(logits_t)
    biased = scores + bias_col
    sub = lax.broadcasted_iota(I32, (GROUP_SIZE, tm), 0).astype(F32)
    ninf = jnp.float32(-jnp.inf)
    groups, gscore = [], []
    for g in range(N_GROUPS):
        v = biased[g * GROUP_SIZE:(g + 1) * GROUP_SIZE]
        m1 = jnp.max(v, axis=0, keepdims=True)
        first = jnp.min(jnp.where(v == m1, sub, float(GROUP_SIZE)), axis=0, keepdims=True)
        m2 = jnp.max(jnp.where(sub == first, ninf, v), axis=0, keepdims=True)
        groups.append(v)
        gscore.append(m1 + m2)
    masked = []
    for g in range(N_GROUPS):
        rank = jnp.zeros((1, tm), I32)
        for g2 in range(N_GROUPS):
            if g2 == g:
                continue
            ahead = gscore[g2] > gscore[g]
            if g2 < g:
                ahead = ahead | (gscore[g2] == gscore[g])
            rank = rank + ahead.astype(I32)
        masked.append(jnp.where(rank < TOPK_GROUPS, groups[g], ninf))
    masked = jnp.concatenate(masked, axis=0)
    eidx = lax.broadcasted_iota(I32, (N_EXPERTS, tm), 0).astype(F32)
    sel = jnp.zeros((N_EXPERTS, tm), jnp.bool_)
    for _ in range(TOPK):
        best = jnp.max(masked, axis=0, keepdims=True)
        first = jnp.min(jnp.where(masked == best, eidx, float(N_EXPERTS)), axis=0, keepdims=True)
        hit = eidx == first
        sel = sel | hit
        masked = jnp.where(hit, ninf, masked)
    w = jnp.where(sel, scores, 0.0)
    wsum = jnp.sum(w, axis=0, keepdims=True)
    return w / wsum * ROUTED_SCALE, sel


def _ffn_pre(x1, mod_ref, gffn_ref, wr_hi_ref, wr_lo_ref, rb_ref,
             h_ref, gates_ref, sel_ref, cnt_ref):
    sh2 = mod_ref[0, 3:4, :]
    sc2 = mod_ref[0, 4:5, :]
    h = _adaln(x1, gffn_ref[...], sh2, sc2)
    h_hi, h_lo = _split_bf16(h)
    h_ref[...] = _pack_pair(h_hi)
    wr_hi = wr_hi_ref[...]
    logits = (jnp.dot(h_hi, wr_hi, preferred_element_type=F32)
              + jnp.dot(h_hi, wr_lo_ref[...], preferred_element_type=F32)
              + jnp.dot(h_lo, wr_hi, preferred_element_type=F32))
    gates_t, sel = _route(logits.T[:N_EXPERTS], rb_ref[...])
    gates_ref[...] = gates_t
    sel_b = sel.astype(F32).astype(BF16)
    sel_ref[...] = sel_b
    ones = jnp.ones((8, sel_b.shape[1]), BF16)
    cnt_ref[0] = lax.dot_general(ones, sel_b, NT_DIMS, preferred_element_type=F32)


def _pre_out_specs(t, d):
    specs = [
        pl.BlockSpec((TM, d), lambda i: (i, 0)),
        pl.BlockSpec((TM, d // 2), lambda i: (i, 0)),
        pl.BlockSpec((N_EXPERTS, TM), lambda i: (0, i)),
        pl.BlockSpec((N_EXPERTS, TM), lambda i: (0, i)),
        pl.BlockSpec((1, 8, N_EXPERTS), lambda i: (i, 0, 0)),
    ]
    shapes = [
        jax.ShapeDtypeStruct((t, d), F32),
        jax.ShapeDtypeStruct((t, d // 2), I32),
        jax.ShapeDtypeStruct((N_EXPERTS, t), F32),
        jax.ShapeDtypeStruct((N_EXPERTS, t), BF16),
        jax.ShapeDtypeStruct((t // TM, 8, N_EXPERTS), F32),
    ]
    return specs, shapes


def _mixer_kernel(nbp, bps, sp, ss,
                  xpc_ref, xsc_ref, xprev_ref, xnext_ref, mod_ref, gmix_ref, win_ref, pw_ref, ps_ref,
                  cw_ref,
                  wout_ref, gffn_ref, wr_hi_ref, wr_lo_ref, rb_ref,
                  x1_ref, h_ref, gates_ref, sel_ref, cnt_ref):
    i = pl.program_id(0)
    is_p = i < nbp
    j = lax.rem(jnp.maximum(i - nbp, 0), bps)
    first = is_p | (j == 0)
    last = is_p | (j == bps - 1)
    base = jnp.where(is_p, 0, j * TM)
    slen = jnp.where(is_p, sp, ss)

    sh1 = mod_ref[0, 0:1, :]
    sc1 = mod_ref[0, 1:2, :]
    g1 = mod_ref[0, 2:3, :]
    xc = jnp.where(is_p, xpc_ref[...], xsc_ref[...])
    x_ext = jnp.concatenate([xprev_ref[...], xc, xnext_ref[...]], axis=0)
    h_ext = _adaln(x_ext, gmix_ref[...], sh1, sc1).astype(BF16)
    u = jnp.dot(h_ext, win_ref[...], preferred_element_type=F32)
    next_ = TM + 2 * HALO
    row = lax.broadcasted_iota(I32, (next_, 1), 0)
    keep = ((row >= HALO) | jnp.logical_not(first)) & ((row < HALO + TM) | jnp.logical_not(last))
    u = jnp.where(keep, u, 0.0)

    dm = u.shape[1] // 4
    ua = u[:, :dm]
    gate_b = u[HALO:HALO + TM, dm:2 * dm]
    z = u[:, 2 * dm:3 * dm] * u[:, 3 * dm:]

    def up(a, k):
        return pltpu.roll(a, next_ - k, 0)

    pos = base + lax.broadcasted_iota(I32, (TM, 1), 0)
    pg = dm // len(POOL_WINDOWS)
    ya = []
    for g, w in enumerate(POOL_WINDOWS):
        e = ua[:, g * pg:(g + 1) * pg]
        acc = e
        span = 1
        while span < w:
            acc = acc + up(acc, span)
            span *= 2
        off = HALO - w // 2
        wsum = (up(acc, off) if off else acc)[:TM]
        lo = jnp.maximum(pos - w // 2, 0)
        hi = jnp.minimum(pos + (w - w // 2 - 1), slen - 1)
        cnt = (hi - lo + 1).astype(F32)
        diff = wsum / cnt - e[HALO:HALO + TM]
        ya.append(jnp.dot(diff.astype(BF16), pw_ref[g], preferred_element_type=F32))
    y_a = jnp.concatenate(ya, axis=-1) * ps_ref[...]
    zc = (cw_ref[0:1, :] * up(z, HALO - 1)[:TM] + cw_ref[1:2, :] * z[HALO:HALO + TM]
          + cw_ref[2:3, :] * up(z, HALO + 1)[:TM])
    y_b = gate_b * zc
    ycat = jnp.concatenate([y_a, y_b], axis=-1).astype(BF16)
    y = jnp.dot(ycat, wout_ref[...], preferred_element_type=F32)
    x1 = xc + g1 * y
    x1_ref[...] = x1
    _ffn_pre(x1, mod_ref, gffn_ref, wr_hi_ref, wr_lo_ref, rb_ref,
             h_ref, gates_ref, sel_ref, cnt_ref)


def _mod_index(nbp, bps):
    def f(i):
        return jnp.where(i < nbp, 0, 1 + jnp.maximum(i - nbp, 0) // bps)
    return f


def _const_spec(shape):
    nd = len(shape)
    return pl.BlockSpec(shape, lambda i: (0,) * nd)


def _mixer(xp, xs, mod, g_mix, w_in, pool_w, pool_scale, conv_w, w_out, g_ffn, wr_hi, wr_lo, rbias,
           nbp, bps, sp, ss):
    d = xp.shape[1]
    t = xp.shape[0] + xs.shape[0]
    nblk = t // TM
    midx = _mod_index(nbp, bps)
    hpb = TM // HALO
    nh = xs.shape[0] // HALO
    in_specs = [
        pl.BlockSpec((TM, d), lambda i: (jnp.minimum(i, nbp - 1), 0)),
        pl.BlockSpec((TM, d), lambda i: (jnp.maximum(i - nbp, 0), 0)),
        pl.BlockSpec((HALO, d), lambda i: (jnp.maximum((i - nbp) * hpb - 1, 0), 0)),
        pl.BlockSpec((HALO, d), lambda i: (jnp.clip((i - nbp + 1) * hpb, 0, nh - 1), 0)),
        pl.BlockSpec((1, 6, d), lambda i: (midx(i), 0, 0)),
        _const_spec(g_mix.shape), _const_spec(w_in.shape), _const_spec(pool_w.shape),
        _const_spec(pool_scale.shape), _const_spec(conv_w.shape), _const_spec(w_out.shape),
        _const_spec(g_ffn.shape), _const_spec(wr_hi.shape), _const_spec(wr_lo.shape),
        _const_spec(rbias.shape),
    ]
    out_specs, out_shape = _pre_out_specs(t, d)
    return pl.pallas_call(
        functools.partial(_mixer_kernel, nbp, bps, sp, ss),
        grid=(nblk,), in_specs=in_specs, out_specs=out_specs, out_shape=out_shape,
        compiler_params=_cparams(("parallel",)), name="pool_conv_mixer",
    )(xp, xs, xs, xs, mod, g_mix, w_in, pool_w, pool_scale, conv_w, w_out, g_ffn, wr_hi, wr_lo,
      rbias)


def _slots_kernel(sel_ref, gates_ref, base_ref, pos_ref, w_ref):
    sel = sel_ref[...]
    tm = sel.shape[1]
    r = lax.broadcasted_iota(I32, (tm, tm), 0)
    c = lax.broadcasted_iota(I32, (tm, tm), 1)
    before = (r < c).astype(F32).astype(BF16)
    rank_tok = jnp.dot(sel, before, preferred_element_type=F32)
    er = lax.broadcasted_iota(I32, (N_EXPERTS, N_EXPERTS), 0)
    ec = lax.broadcasted_iota(I32, (N_EXPERTS, N_EXPERTS), 1)
    lower = (ec < er).astype(F32).astype(BF16)
    rank_exp = jnp.dot(lower, sel, preferred_element_type=F32)
    slot = base_ref[0] + rank_tok
    chosen = sel > 0
    gates = gates_ref[...]
    sub = lax.broadcasted_iota(I32, (TOPK, tm), 0)
    pos8 = jnp.zeros((TOPK, tm), F32)
    w8 = jnp.zeros((TOPK, tm), F32)
    for k in range(TOPK):
        mk = chosen & (rank_exp == float(k))
        pk = jnp.sum(jnp.where(mk, slot, 0.0), axis=0, keepdims=True)
        wk = jnp.sum(jnp.where(mk, gates, 0.0), axis=0, keepdims=True)
        pos8 = jnp.where(sub == k, pk, pos8)
        w8 = jnp.where(sub == k, wk, w8)
    pos_ref[...] = pos8.astype(I32)
    w8 = jnp.concatenate([w8, jnp.zeros((128 - TOPK, tm), F32)], axis=0)
    w_ref[...] = w8.T


def _slots(sel_t, gates_t, base):
    ne, t = sel_t.shape
    return pl.pallas_call(
        _slots_kernel, grid=(t // TM,),
        in_specs=[
            pl.BlockSpec((ne, TM), lambda i: (0, i)),
            pl.BlockSpec((ne, TM), lambda i: (0, i)),
            pl.BlockSpec((1, ne, 1), lambda i: (i, 0, 0)),
        ],
        out_specs=[pl.BlockSpec((TOPK, TM), lambda i: (0, i)),
                   pl.BlockSpec((TM, 128), lambda i: (i, 0))],
        out_shape=[jax.ShapeDtypeStruct((TOPK, t), I32), jax.ShapeDtypeStruct((t, 128), F32)],
        compiler_params=_cparams(("parallel",)), name="moe_slots",
    )(sel_t, gates_t, base)


def _sc_gather(table, idx):
    m = idx.shape[0]
    d = table.shape[1]
    assert m % (SC_WINDOW * SC_WORKERS) == 0
    mesh = plsc.VectorSubcoreMesh(core_axis_name="core", subcore_axis_name="subcore")

    @pl.kernel(out_type=jax.ShapeDtypeStruct((m, d), table.dtype), mesh=mesh)
    def gather_rows(x_hbm, i_hbm, o_hbm):
        def body(i_vmem, o_vmem):
            pltpu.sync_copy(x_hbm.at[i_vmem.at[0]], o_vmem)

        pltpu.emit_pipeline(
            body, grid=(m // SC_WINDOW,),
            in_specs=[pl.BlockSpec((1, SC_WINDOW), index_map=lambda i: (i, 0))],
            out_specs=[pl.BlockSpec((SC_WINDOW, d), index_map=lambda i: (i, 0))],
            core_axis_name=("core", "subcore"),
            dimension_semantics=(pltpu.PARALLEL,),
        )(i_hbm, o_hbm)

    return gather_rows(table, idx.reshape(m // SC_WINDOW, SC_WINDOW))


def _sc_scatter(rows, idx, n_out):
    nk, t = idx.shape
    d = rows.shape[1]
    assert t % (SC_WINDOW * SC_WORKERS) == 0
    mesh = plsc.VectorSubcoreMesh(core_axis_name="core", subcore_axis_name="subcore")

    @pl.kernel(out_type=jax.ShapeDtypeStruct((n_out, d), rows.dtype), mesh=mesh)
    def scatter_rows(x_hbm, i_hbm, o_hbm):
        def body(x_vmem, i_vmem):
            for k in range(nk):
                pltpu.sync_copy(x_vmem, o_hbm.at[i_vmem.at[k, 0]])

        pltpu.emit_pipeline(
            body, grid=(t // SC_WINDOW,),
            in_specs=[pl.BlockSpec((SC_WINDOW, d), index_map=lambda i: (i, 0)),
                      pl.BlockSpec((nk, 1, SC_WINDOW), index_map=lambda i: (0, i, 0))],
            out_specs=[],
            core_axis_name=("core", "subcore"),
            dimension_semantics=(pltpu.PARALLEL,),
        )(x_hbm, i_hbm)

    return scatter_rows(rows, idx.reshape(nk, t // SC_WINDOW, SC_WINDOW))


def _expert_ffn_kernel(te_ref, fresh_ref, nv_ref, x_hbm, wg_ref, wu_ref, wd_ref, y_ref,
                       xbuf, xsem, wg_s, wu_s, wd_s):
    j = pl.program_id(0)
    nv = nv_ref[0]

    def tile_copy(tile, slot):
        rows = pl.ds(pl.multiple_of(tile * TG, TG), TG)
        return pltpu.make_async_copy(x_hbm.at[rows, :], xbuf.at[slot], xsem.at[slot])

    @pl.when(j == 0)
    def _():
        tile_copy(0, 0).start()

        @pl.when(nv > 1)
        def _():
            tile_copy(1, 1).start()

    @pl.when(j < nv)
    def _():
        @pl.when(j + 2 < nv)
        def _():
            tile_copy(j + 2, lax.rem(j + 2, 3)).start()

        slot = lax.rem(j, 3)
        tile_copy(j, slot).wait()

        @pl.when(fresh_ref[j] == 1)
        def _():
            wg_s[...] = wg_ref[0, 0].astype(BF16)
            wu_s[...] = wu_ref[0, 0].astype(BF16)
            wd_s[...] = wd_ref[0, 0].astype(BF16)

        x = _unpack_pair(xbuf[slot]).astype(BF16)
        a = jnp.dot(x, wg_s[...], preferred_element_type=F32)
        b = jnp.dot(x, wu_s[...], preferred_element_type=F32)
        act = (_silu(a) * b).astype(BF16)
        y_ref[...] = _pack_pair(jnp.dot(act, wd_s[...], preferred_element_type=F32))


def _expert_ffn(xs, n_tiles, tile_expert, fresh, n_valid, layer, w_gate, w_up, w_down):
    dh = xs.shape[1]
    last = lambda j, te, fr, nv: jnp.minimum(j, nv[0] - 1)
    wspec = lambda w: pl.BlockSpec((1, 1) + w.shape[2:],
                                   lambda j, te, fr, nv: (layer, te[j], 0, 0))
    grid_spec = pltpu.PrefetchScalarGridSpec(
        num_scalar_prefetch=3, grid=(n_tiles,),
        in_specs=[pl.BlockSpec(memory_space=pl.ANY), wspec(w_gate), wspec(w_up), wspec(w_down)],
        out_specs=pl.BlockSpec((TG, dh), lambda j, te, fr, nv: (last(j, te, fr, nv), 0)),
        scratch_shapes=[pltpu.VMEM((3, TG, dh), xs.dtype), pltpu.SemaphoreType.DMA((3,)),
                        pltpu.VMEM(w_gate.shape[2:], BF16), pltpu.VMEM(w_up.shape[2:], BF16),
                        pltpu.VMEM(w_down.shape[2:], BF16)],
    )
    return pl.pallas_call(
        _expert_ffn_kernel, grid_spec=grid_spec,
        out_shape=jax.ShapeDtypeStruct((n_tiles * TG, dh), I32),
        compiler_params=_cparams(("arbitrary",)), name="expert_ffn",
    )(tile_expert, fresh, n_valid, xs, w_gate, w_up, w_down)


def _combine_kernel(nbp, yg_ref, w_ref, h_ref, x_ref, mod_ref, sgu_ref, sd_ref, *o_refs):
    f = sd_ref.shape[0]
    h = _unpack_pair(h_ref[...]).astype(BF16)
    hs = jnp.dot(h, sgu_ref[...], preferred_element_type=F32)
    act = (_silu(hs[:, :f]) * hs[:, f:]).astype(BF16)
    acc = jnp.dot(act, sd_ref[...], preferred_element_type=F32)
    w = w_ref[...]
    for k in range(TOPK):
        acc = acc + w[:, k:k + 1] * _unpack_pair(yg_ref[k])
    out = x_ref[...] + mod_ref[0, 5:6, :] * acc
    if len(o_refs) == 1:
        o_refs[0][...] = out
    else:
        i = pl.program_id(0)

        @pl.when(i < nbp)
        def _():
            o_refs[0][...] = out

        @pl.when(i >= nbp)
        def _():
            o_refs[1][...] = out


def _combine(yg, w8, h, x, mod, sgu, sd, nbp, bps, split_out):
    t, d = x.shape
    tc = 2 * TM
    assert (nbp * TM) % tc == 0 and (bps * TM) % tc == 0
    nbp, bps = nbp * TM // tc, bps * TM // tc
    midx = _mod_index(nbp, bps)
    if split_out:
        out_specs = [pl.BlockSpec((tc, d), lambda i: (jnp.minimum(i, nbp - 1), 0)),
                     pl.BlockSpec((tc, d), lambda i: (jnp.maximum(i - nbp, 0), 0))]
        out_shape = [jax.ShapeDtypeStruct((nbp * tc, d), F32),
                     jax.ShapeDtypeStruct((t - nbp * tc, d), F32)]
    else:
        out_specs = pl.BlockSpec((tc, d), lambda i: (i, 0))
        out_shape = jax.ShapeDtypeStruct((t, d), F32)
    in_specs = [
        pl.BlockSpec((TOPK, tc, d // 2), lambda i: (0, i, 0)),
        pl.BlockSpec((tc, 128), lambda i: (i, 0)),
        pl.BlockSpec((tc, d // 2), lambda i: (i, 0)),
        pl.BlockSpec((tc, d), lambda i: (i, 0)),
        pl.BlockSpec((1, 6, d), lambda i: (midx(i), 0, 0)),
        _const_spec(sgu.shape), _const_spec(sd.shape),
    ]
    return pl.pallas_call(
        functools.partial(_combine_kernel, nbp), grid=(t // tc,), in_specs=in_specs,
        out_specs=out_specs, out_shape=out_shape,
        compiler_params=_cparams(("arbitrary",)), name="moe_combine",
    )(yg, w8, h, x, mod, sgu, sd)


def _sparse_moe(x, h, gates_t, sel_t, cnt, mod, layer, w_gate, w_up, w_down, sgu, sd, nbp, bps,
                split_out):
    t, d = x.shape
    ne = N_EXPERTS
    n_pad = ne * TG
    n_tiles = (t * TOPK + n_pad) // TG
    n_slots = n_tiles * TG
    counts = cnt[:, 0, :].astype(I32)
    per_expert = jnp.sum(counts, axis=0)
    padded = (per_expert + TG - 1) // TG * TG
    ends = jnp.cumsum(padded)
    starts = ends - padded
    block_off = jnp.cumsum(counts, axis=0) - counts
    base = (starts[None, :] + block_off).astype(F32)[:, :, None]
    n_valid = (ends[-1] // TG).astype(I32).reshape(1)
    tile_start = jnp.minimum(jnp.arange(n_tiles, dtype=I32), n_valid[0] - 1) * TG
    tile_expert = jnp.sum((ends[None, :] <= tile_start[:, None]).astype(I32), axis=1)
    fresh = jnp.concatenate([jnp.ones((1,), I32),
                             (tile_expert[1:] != tile_expert[:-1]).astype(I32)])

    pos8, w8 = _slots(sel_t, gates_t, base)
    n_extra = -(-n_pad // t)
    cand = (starts + per_expert)[:, None] + jnp.arange(TG, dtype=I32)[None, :]
    spare = n_slots + jnp.arange(n_pad, dtype=I32)
    fill = jnp.where(cand < ends[:, None], cand, spare.reshape(ne, TG)).reshape(-1)
    rest = n_slots + jnp.arange(n_pad, n_extra * t, dtype=I32) % n_pad
    dest = jnp.concatenate([pos8, jnp.concatenate([fill, rest]).reshape(n_extra, t)], axis=0)
    xs = _sc_scatter(h, dest, n_slots + n_pad)
    ys = _expert_ffn(xs, n_tiles, tile_expert, fresh, n_valid, layer, w_gate, w_up, w_down)
    yg = _sc_gather(ys, pos8.reshape(-1)).reshape(TOPK, t, d // 2)
    return _combine(yg, w8, h, x, mod, sgu, sd, nbp, bps, split_out)


def _head_rms(x, g_row, hm_ref, hmt_ref):
    ss = jnp.dot((x * x).astype(BF16), hm_ref[...], preferred_element_type=F32)
    r = lax.rsqrt(ss * (1.0 / HEAD_DIM) + EPS)
    r_hi, r_lo = _split_bf16(r)
    hmt = hmt_ref[...]
    rb = (jnp.dot(r_hi, hmt, preferred_element_type=F32)
          + jnp.dot(r_lo, hmt, preferred_element_type=F32))
    return (x * rb) * g_row


def _qkv_kernel(emit_f32, x_ref, mod_ref, gmix_ref, w_ref, qg_ref, kg_ref, hm_ref, hmt_ref, *outs):
    d = x_ref.shape[1]
    sh1 = mod_ref[0, 0:1, :]
    sc1 = mod_ref[0, 1:2, :]
    h = _adaln(x_ref[...], gmix_ref[...], sh1, sc1).astype(BF16)
    qkv = jnp.dot(h, w_ref[...], preferred_element_type=F32)
    q = _head_rms(qkv[:, :d], qg_ref[...], hm_ref, hmt_ref)
    k = _head_rms(qkv[:, d:2 * d], kg_ref[...], hm_ref, hmt_ref)
    v = qkv[:, 2 * d:]
    outs[0][...] = q.astype(BF16)
    outs[1][...] = k.astype(BF16)
    outs[2][...] = v.astype(BF16)
    if emit_f32:
        outs[3][...] = k
        outs[4][...] = v


def _qkv(x, mod, g_mix, w_qkv, qg, kg, hm, hmt, blk0, nblk, midx, emit_f32):
    t, d = x.shape
    in_specs = [
        pl.BlockSpec((TM, d), lambda i: (i + blk0, 0)),
        pl.BlockSpec((1, 6, d), lambda i: (midx(i + blk0), 0, 0)),
        _const_spec(g_mix.shape), _const_spec(w_qkv.shape), _const_spec(qg.shape),
        _const_spec(kg.shape), _const_spec(hm.shape), _const_spec(hmt.shape),
    ]
    n_out = 5 if emit_f32 else 3
    out_specs = [pl.BlockSpec((TM, d), lambda i: (i, 0)) for _ in range(n_out)]
    out_shape = [jax.ShapeDtypeStruct((nblk * TM, d), BF16 if o < 3 else F32) for o in range(n_out)]
    return pl.pallas_call(
        functools.partial(_qkv_kernel, emit_f32),
        grid=(nblk,), in_specs=in_specs, out_specs=out_specs, out_shape=out_shape,
        compiler_params=_cparams(("parallel",)), name="qkv_f32" if emit_f32 else "qkv",
    )(x, mod, g_mix, w_qkv, qg, kg, hm, hmt)


def _head_masks():
    lane = lax.broadcasted_iota(I32, (1, 2 * HEAD_DIM), 1)
    return lane < HEAD_DIM


def _ctx_attn_kernel(q_ref, k_ref, v_ref, o_ref):
    lo = _head_masks()
    pw = 2 * HEAD_DIM
    for hp in range(q_ref.shape[1] // pw):
        cols = slice(hp * pw, (hp + 1) * pw)
        q = q_ref[:, cols]
        k = k_ref[:, cols]
        v = v_ref[:, cols]
        outs = []
        for hh in range(2):
            msk = lo if hh == 0 else jnp.logical_not(lo)
            qm = jnp.where(msk, q, jnp.zeros_like(q)) * jnp.asarray(HEAD_DIM ** -0.5, BF16)
            s = lax.dot_general(qm, k, NT_DIMS, preferred_element_type=F32)
            m = jnp.max(s, axis=-1, keepdims=True)
            p = jnp.exp(s - m)
            l = jnp.sum(p, axis=-1, keepdims=True)
            o = jnp.dot(p.astype(BF16), v, preferred_element_type=F32)
            outs.append(o / l)
        o_ref[:, cols] = jnp.where(lo, outs[0], outs[1]).astype(BF16)


def _ctx_attn(q, k, v, nb, s):
    t, d = q.shape
    spec = pl.BlockSpec((s, d), lambda b: (b, 0))
    return pl.pallas_call(
        _ctx_attn_kernel, grid=(nb,), in_specs=[spec, spec, spec], out_specs=spec,
        out_shape=jax.ShapeDtypeStruct((nb * s, d), BF16),
        compiler_params=_cparams(("parallel",)), name="context_attention",
    )(q, k, v)


NA_QROWS = 8
NA_GROWS = 4
NA_KROWS = 12
NA_HPAIRS = 2


def _na_kernel(rows, q_ref, k_ref, v_ref, kc_ref, vc_ref, bias_ref, o_ref):
    rb = pl.program_id(2)
    ngrp = NA_QROWS // NA_GROWS
    nq = NA_GROWS * GRID_W
    nk = NA_KROWS * GRID_W
    lo = _head_masks()
    pw = 2 * HEAD_DIM
    for gi in range(ngrp):
        g = rb * ngrp + gi
        kr0 = jnp.clip(g * NA_GROWS - WIN_H // 2, 0, rows - NA_KROWS)
        start = pl.multiple_of(kr0 * GRID_W, 256)
        cls = jnp.where(g == 0, 0, jnp.where(g == rows // NA_GROWS - 1, 2, 1))
        qrows = slice(gi * nq, (gi + 1) * nq)
        for pp in range(NA_HPAIRS):
            cols = slice(pp * pw, (pp + 1) * pw)
            q = q_ref[0, qrows, cols]
            kw = k_ref[0, pl.ds(start, nk), cols]
            vw = v_ref[0, pl.ds(start, nk), cols]
            kc = kc_ref[0, :, cols]
            vc = vc_ref[0, :, cols]
            outs = []
            for hh in range(2):
                msk = lo if hh == 0 else jnp.logical_not(lo)
                qm = jnp.where(msk, q, jnp.zeros_like(q)) * jnp.asarray(HEAD_DIM ** -0.5, BF16)
                s = (lax.dot_general(qm, kw, NT_DIMS, preferred_element_type=F32)
                     + bias_ref[cls, 2 * pp + hh])
                sc = lax.dot_general(qm, kc, NT_DIMS, preferred_element_type=F32)
                m = jnp.maximum(jnp.max(s, axis=-1, keepdims=True),
                                jnp.max(sc, axis=-1, keepdims=True))
                p = jnp.exp(s - m)
                pc = jnp.exp(sc - m)
                l = jnp.sum(p, axis=-1, keepdims=True) + jnp.sum(pc, axis=-1, keepdims=True)
                o = (jnp.dot(p.astype(BF16), vw, preferred_element_type=F32)
                     + jnp.dot(pc.astype(BF16), vc, preferred_element_type=F32))
                outs.append(o / l)
            o_ref[0, qrows, cols] = jnp.where(lo, outs[0], outs[1]).astype(BF16)


def _na_row_classes(rows):
    out = []
    for g in (0, 1, rows // NA_GROWS - 1):
        r0 = g * NA_GROWS
        kr0 = int(np.clip(r0 - WIN_H // 2, 0, rows - NA_KROWS))
        table = []
        for rl in range(NA_GROWS):
            r = r0 + rl
            sr = int(np.clip(r - WIN_H // 2, 0, rows - WIN_H))
            table.append([(kr0 + kl - r + WIN_H - 1) if sr <= kr0 + kl < sr + WIN_H else None
                          for kl in range(NA_KROWS)])
        out.append(table)
    return out


def _na_bias_kernel(rows, rpb_ref, o_ref):
    h = pl.program_id(0)
    n_dr, n_dc = 2 * WIN_H - 1, 2 * WIN_W - 1
    cq = lax.broadcasted_iota(I32, (GRID_W, 2 * GRID_W), 0)
    lane = lax.broadcasted_iota(I32, (GRID_W, 2 * GRID_W), 1)
    ck = lane & (GRID_W - 1)
    q_start = jnp.clip(cq - WIN_W // 2, 0, GRID_W - WIN_W)
    col_ok = (ck >= q_start) & (ck < q_start + WIN_W)
    dc = ck - cq + (WIN_W - 1)
    neg = jnp.full((GRID_W, 2 * GRID_W), NEG, F32)
    tiles = []
    for i in range(n_dr):
        t = jnp.zeros((GRID_W, 2 * GRID_W), F32)
        for jj in range(n_dc):
            t = jnp.where(dc == jj, rpb_ref[h * (n_dr * n_dc) + i * n_dc + jj], t)
        tiles.append(jnp.where(col_ok, t, neg))
    first_half = lane < GRID_W
    for c, table in enumerate(_na_row_classes(rows)):
        for rl in range(NA_GROWS):
            for m in range(NA_KROWS // 2):
                ia, ib = table[rl][2 * m], table[rl][2 * m + 1]
                ta = neg if ia is None else tiles[ia]
                tb = neg if ib is None else tiles[ib]
                blk = ta if ia == ib else jnp.where(first_half, ta, tb)
                o_ref[c, 0, rl * GRID_W:(rl + 1) * GRID_W, m * 2 * GRID_W:(m + 1) * 2 * GRID_W] = blk


def _na_bias(rpb, rows):
    nh = rpb.shape[0]
    nq, nk = NA_GROWS * GRID_W, NA_KROWS * GRID_W
    return pl.pallas_call(
        functools.partial(_na_bias_kernel, rows), grid=(nh,),
        in_specs=[pl.BlockSpec(memory_space=pltpu.SMEM)],
        out_specs=pl.BlockSpec((3, 1, nq, nk), lambda h: (0, h, 0, 0)),
        out_shape=jax.ShapeDtypeStruct((3, nh, nq, nk), F32),
        compiler_params=_cparams(("parallel",)), name="na_bias",
    )(rpb.reshape(-1))


def _na_attn(q, k, v, kc, vc, bias_tab, rows):
    nb, s, d = q.shape
    lw = NA_HPAIRS * 2 * HEAD_DIM
    hp = d // lw
    nrb = rows // NA_QROWS
    nq = NA_QROWS * GRID_W
    lc = kc.shape[1]
    in_specs = [
        pl.BlockSpec((1, nq, lw), lambda h, b, r: (b, r, h)),
        pl.BlockSpec((1, s, lw), lambda h, b, r: (b, 0, h)),
        pl.BlockSpec((1, s, lw), lambda h, b, r: (b, 0, h)),
        pl.BlockSpec((1, lc, lw), lambda h, b, r: (b, 0, h)),
        pl.BlockSpec((1, lc, lw), lambda h, b, r: (b, 0, h)),
        pl.BlockSpec((3, 2 * NA_HPAIRS) + bias_tab.shape[2:], lambda h, b, r: (0, h, 0, 0)),
    ]
    return pl.pallas_call(
        functools.partial(_na_kernel, rows),
        grid=(hp, nb, nrb), in_specs=in_specs,
        out_specs=pl.BlockSpec((1, nq, lw), lambda h, b, r: (b, r, h)),
        out_shape=jax.ShapeDtypeStruct((nb, s, d), BF16),
        compiler_params=_cparams(("parallel", "parallel", "parallel")),
        name="neighbourhood_attention",
    )(q, k, v, kc, vc, bias_tab)


def _oproj_kernel(nbp, ap_ref, as_ref, x_ref, mod_ref, wout_ref, gffn_ref, wr_hi_ref, wr_lo_ref,
                  rb_ref, x1_ref, h_ref, gates_ref, sel_ref, cnt_ref):
    a = jnp.where(pl.program_id(0) < nbp, ap_ref[...], as_ref[...])
    y = jnp.dot(a, wout_ref[...], preferred_element_type=F32)
    x1 = x_ref[...] + mod_ref[0, 2:3, :] * y
    x1_ref[...] = x1
    _ffn_pre(x1, mod_ref, gffn_ref, wr_hi_ref, wr_lo_ref, rb_ref,
             h_ref, gates_ref, sel_ref, cnt_ref)


def _oproj(attn_p, attn_s, x, mod, w_out, g_ffn, wr_hi, wr_lo, rbias, nbp, bps):
    t, d = x.shape
    midx = _mod_index(nbp, bps)
    in_specs = [
        pl.BlockSpec((TM, d), lambda i: (jnp.minimum(i, nbp - 1), 0)),
        pl.BlockSpec((TM, d), lambda i: (jnp.maximum(i - nbp, 0), 0)),
        pl.BlockSpec((TM, d), lambda i: (i, 0)),
        pl.BlockSpec((1, 6, d), lambda i: (midx(i), 0, 0)),
        _const_spec(w_out.shape), _const_spec(g_ffn.shape), _const_spec(wr_hi.shape),
        _const_spec(wr_lo.shape), _const_spec(rbias.shape),
    ]
    out_specs, out_shape = _pre_out_specs(t, d)
    return pl.pallas_call(
        functools.partial(_oproj_kernel, nbp), grid=(t // TM,), in_specs=in_specs,
        out_specs=out_specs, out_shape=out_shape, compiler_params=_cparams(("parallel",)),
        name="attn_out_proj",
    )(attn_p, attn_s, x, mod, w_out, g_ffn, wr_hi, wr_lo, rbias)


def _router_weights(w_router, router_bias):
    d, ne = w_router.shape
    w = jnp.pad(w_router, ((0, 0), (0, 128 - ne)))
    hi = w.astype(BF16)
    lo = (w - hi.astype(F32)).astype(BF16)
    return hi, lo, router_bias.reshape(ne, 1)


def _shared_weights(s_gate, s_up, s_down):
    return jnp.concatenate([s_gate, s_up], axis=-1).astype(BF16), s_down.astype(BF16)


def _layer_weights(layer, d, norm_mix, norm_ffn, pc, na, moe):
    i = layer // 2
    w = dict(g_mix=norm_mix[layer].reshape(1, d), g_ffn=norm_ffn[layer].reshape(1, d))
    w["wr_hi"], w["wr_lo"], w["rbias"] = _router_weights(moe["router"][layer],
                                                         moe["router_bias"][layer])
    w["sgu"], w["sd"] = _shared_weights(moe["s_gate"][layer], moe["s_up"][layer],
                                        moe["s_down"][layer])
    if layer % 2 == 0:
        w.update(w_in=pc["w_in"][i].astype(BF16), pool_w=pc["pool_w"][i].astype(BF16),
                 pool_scale=pc["pool_scale"][i].reshape(1, -1), conv_w=pc["conv_w"][i],
                 w_out=pc["w_out"][i].astype(BF16))
    else:
        nh = d // HEAD_DIM
        head_of = np.arange(d) // HEAD_DIM
        w.update(hm=jnp.asarray(head_of[:, None] == np.arange(128)[None, :], BF16),
                 hmt=jnp.asarray(np.arange(128)[:, None] == head_of[None, :], BF16),
                 qg=jnp.tile(na["q_norm"][i], nh).reshape(1, d),
                 kg=jnp.tile(na["k_norm"][i], nh).reshape(1, d),
                 w_qkv=na["w_qkv"][i].astype(BF16), w_out=na["w_out"][i].astype(BF16))
    return w


def kernel(x_prompt, x_sample, cache_k, cache_v, c, c_ctx, ada_w, ada_b, norm_mix, norm_ffn,
           pc_w_in, pc_pool_w, pc_pool_scale, pc_conv_w, pc_w_out,
           na_w_qkv, na_q_norm, na_k_norm, na_rpb, na_w_out,
           moe_router, moe_router_bias, moe_w_gate, moe_w_up, moe_w_down,
           moe_shared_gate, moe_shared_up, moe_shared_down):
    nb_p, s_p, d = x_prompt.shape
    nb_s, s_s, _ = x_sample.shape
    assert s_p == TM and s_s % TM == 0 and nb_p > 0 and nb_s > 0
    tp, ts = nb_p * s_p, nb_s * s_s
    nbp, bps = tp // TM, s_s // TM
    depth = ada_w.shape[0]
    nh = d // HEAD_DIM
    rows = s_s // GRID_W

    cond = jnp.concatenate([c_ctx[None], c], axis=0)
    cond = jnp.pad(cond, ((0, -cond.shape[0] % 8), (0, 0)))
    mods = _modulation(cond, ada_w, ada_b).reshape(depth, cond.shape[0], 6, d)
    pc = dict(w_in=pc_w_in, pool_w=pc_pool_w, pool_scale=pc_pool_scale, conv_w=pc_conv_w,
              w_out=pc_w_out)
    na = dict(w_qkv=na_w_qkv, q_norm=na_q_norm, k_norm=na_k_norm, w_out=na_w_out)
    moe = dict(router=moe_router, router_bias=moe_router_bias, s_gate=moe_shared_gate,
               s_up=moe_shared_up, s_down=moe_shared_down)

    xp, xs = x_prompt.reshape(tp, d), x_sample.reshape(ts, d)
    x = None
    new_k, new_v = [], []
    for layer in range(depth):
        i = layer // 2
        w = _layer_weights(layer, d, norm_mix, norm_ffn, pc, na, moe)
        mod = mods[layer]
        if layer % 2 == 0:
            if x is not None:
                xp, xs = x[:tp], x[tp:]
            x, h, gates_t, sel_t, cnt = _mixer(
                xp, xs, mod, w["g_mix"], w["w_in"], w["pool_w"], w["pool_scale"], w["conv_w"],
                w["w_out"], w["g_ffn"], w["wr_hi"], w["wr_lo"], w["rbias"], nbp, bps, s_p, s_s)
        else:
            midx = _mod_index(nbp, bps)
            qkv_args = (x, mod, w["g_mix"], w["w_qkv"], w["qg"], w["kg"], w["hm"], w["hmt"])
            qp, kp, vp, kp32, vp32 = _qkv(*qkv_args, 0, nbp, midx, True)
            qs, ks, vs = _qkv(*qkv_args, nbp, ts // TM, midx, False)
            new_k.append(kp32.reshape(nb_p, s_p, nh, HEAD_DIM))
            new_v.append(vp32.reshape(nb_p, s_p, nh, HEAD_DIM))
            a_p = _ctx_attn(qp, kp, vp, nb_p, s_p)
            lc = cache_k.shape[2]
            kc = cache_k[:, i].reshape(nb_s, lc, d).astype(BF16)
            vc = cache_v[:, i].reshape(nb_s, lc, d).astype(BF16)
            a_s = _na_attn(qs.reshape(nb_s, s_s, d), ks.reshape(nb_s, s_s, d),
                           vs.reshape(nb_s, s_s, d), kc, vc, _na_bias(na_rpb[i], rows), rows)
            x, h, gates_t, sel_t, cnt = _oproj(a_p, a_s.reshape(ts, d), x, mod, w["w_out"],
                                               w["g_ffn"], w["wr_hi"], w["wr_lo"], w["rbias"],
                                               nbp, bps)
        x = _sparse_moe(x, h, gates_t, sel_t, cnt, mod, layer, moe_w_gate, moe_w_up, moe_w_down,
                        w["sgu"], w["sd"], nbp, bps, split_out=layer == depth - 1)
    y_p, y_s = x
    return (y_p.reshape(nb_p, s_p, d), y_s.reshape(nb_s, s_s, d),
            jnp.stack(new_k, axis=1), jnp.stack(new_v, axis=1))
```

```python
import functools

import numpy as np
import jax
import jax.numpy as jnp
from jax import lax
from jax.experimental import pallas as pl
from jax.experimental.pallas import tpu as pltpu
from jax.experimental.pallas import tpu_sc as plsc

F32 = jnp.float32
BF16 = jnp.bfloat16
I32 = jnp.int32
U32 = jnp.uint32

TM = 256
HALO = 8
POOL_WINDOWS = (2, 4, 8, 16)
N_EXPERTS = 64
N_GROUPS = 8
GROUP_SIZE = N_EXPERTS // N_GROUPS
TOPK_GROUPS = 4
TOPK = 8
ROUTED_SCALE = 2.5
EPS = 1e-6
GRID_W = 64
WIN_H = 8
WIN_W = 16
HEAD_DIM = 64
NEG = float(np.finfo(np.float32).min)
VMEM_LIMIT = 56 * 1024 * 1024
NT_DIMS = (((1,), (1,)), ((), ()))
TG = 1024
SC_WINDOW = 64
SC_WORKERS = 32


def _cparams(sem):
    return pltpu.CompilerParams(dimension_semantics=sem, vmem_limit_bytes=VMEM_LIMIT)


def _silu(x):
    return x * jax.nn.sigmoid(x)


def _split_bf16(x):
    hi = x.astype(BF16)
    lo = (x - hi.astype(F32)).astype(BF16)
    return hi, lo


def _adaln(x, g, shift, scale):
    ms = jnp.mean(x * x, axis=-1, keepdims=True)
    return (x * lax.rsqrt(ms + EPS)) * g * (1.0 + scale) + shift


def _pack_pair(x):
    w = x.shape[1] // 2
    lo = lax.bitcast_convert_type(x[:, :w].astype(BF16).astype(F32), U32) >> 16
    hi = lax.bitcast_convert_type(x[:, w:].astype(BF16).astype(F32), U32)
    return lax.bitcast_convert_type(lo | hi, I32)


def _unpack_pair(p):
    u = lax.bitcast_convert_type(p, U32)
    lo = lax.bitcast_convert_type(u << 16, F32)
    hi = lax.bitcast_convert_type(u & jnp.uint32(0xFFFF0000), F32)
    return jnp.concatenate([lo, hi], axis=-1)


def _mod_kernel(cond_ref, w_ref, b_ref, o_ref):
    c = cond_ref[...]
    a = _silu(c)
    o_ref[0] = jnp.dot(a, w_ref[0], preferred_element_type=F32,
                       precision=lax.Precision.HIGHEST) + b_ref[0]


def _modulation(cond, ada_w, ada_b):
    depth, d, n = ada_w.shape
    rows = cond.shape[0]
    tn = 1536
    return pl.pallas_call(
        _mod_kernel,
        grid=(depth, n // tn),
        in_specs=[
            pl.BlockSpec((rows, d), lambda l, j: (0, 0)),
            pl.BlockSpec((1, d, tn), lambda l, j: (l, 0, j)),
            pl.BlockSpec((1, 1, tn), lambda l, j: (l, 0, j)),
        ],
        out_specs=pl.BlockSpec((1, rows, tn), lambda l, j: (l, 0, j)),
        out_shape=jax.ShapeDtypeStruct((depth, rows, n), F32),
        compiler_params=_cparams(("arbitrary", "arbitrary")),
        name="modulation",
    )(cond, ada_w, ada_b.reshape(depth, 1, n))


def _route(logits_t, bias_col):
    tm = logits_t.shape[1]
    scores = jax.nn.sigmoid(logits_t)
    biased = scores + bias_col
    sub = lax.broadcasted_iota(I32, (GROUP_SIZE, tm), 0).astype(F32)
    ninf = jnp.float32(-jnp.inf)
    groups, gscore = [], []
    for g in range(N_GROUPS):
        v = biased[g * GROUP_SIZE:(g + 1) * GROUP_SIZE]
        m1 = jnp.max(v, axis=0, keepdims=True)
        first = jnp.min(jnp.where(v == m1, sub, float(GROUP_SIZE)), axis=0, keepdims=True)
        m2 = jnp.max(jnp.where(sub == first, ninf, v), axis=0, keepdims=True)
        groups.append(v)
        gscore.append(m1 + m2)
    masked = []
    for g in range(N_GROUPS):
        rank = jnp.zeros((1, tm), I32)
        for g2 in range(N_GROUPS):
            if g2 == g:
                continue
            ahead = gscore[g2] > gscore[g]
            if g2 < g:
                ahead = ahead | (gscore[g2] == gscore[g])
            rank = rank + ahead.astype(I32)
        masked.append(jnp.where(rank < TOPK_GROUPS, groups[g], ninf))
    masked = jnp.concatenate(masked, axis=0)
    eidx = lax.broadcasted_iota(I32, (N_EXPERTS, tm), 0).astype(F32)
    sel = jnp.zeros((N_EXPERTS, tm), jnp.bool_)
    for _ in range(TOPK):
        best = jnp.max(masked, axis=0, keepdims=True)
        first = jnp.min(jnp.where(masked == best, eidx, float(N_EXPERTS)), axis=0, keepdims=True)
        hit = eidx == first
        sel = sel | hit
        masked = jnp.where(hit, ninf, masked)
    w = jnp.where(sel, scores, 0.0)
    wsum = jnp.sum(w, axis=0, keepdims=True)
    return w / wsum * ROUTED_SCALE, sel


def _ffn_pre(x1, mod_ref, gffn_ref, wr_hi_ref, wr_lo_ref, rb_ref,
             h_ref, gates_ref, sel_ref, cnt_ref):
    sh2 = mod_ref[0, 3:4, :]
    sc2 = mod_ref[0, 4:5, :]
    h = _adaln(x1, gffn_ref[...], sh2, sc2)
    h_hi, h_lo = _split_bf16(h)
    h_ref[...] = _pack_pair(h_hi)
    wr_hi = wr_hi_ref[...]
    logits = (jnp.dot(h_hi, wr_hi, preferred_element_type=F32)
              + jnp.dot(h_hi, wr_lo_ref[...], preferred_element_type=F32)
              + jnp.dot(h_lo, wr_hi, preferred_element_type=F32))
    gates_t, sel = _route(logits.T[:N_EXPERTS], rb_ref[...])
    gates_ref[...] = gates_t
    sel_b = sel.astype(F32).astype(BF16)
    sel_ref[...] = sel_b
    ones = jnp.ones((8, sel_b.shape[1]), BF16)
    cnt_ref[0] = lax.dot_general(ones, sel_b, NT_DIMS, preferred_element_type=F32)


def _pre_out_specs(t, d):
    specs = [
        pl.BlockSpec((TM, d), lambda i: (i, 0)),
        pl.BlockSpec((TM, d // 2), lambda i: (i, 0)),
        pl.BlockSpec((N_EXPERTS, TM), lambda i: (0, i)),
        pl.BlockSpec((N_EXPERTS, TM), lambda i: (0, i)),
        pl.BlockSpec((1, 8, N_EXPERTS), lambda i: (i, 0, 0)),
    ]
    shapes = [
        jax.ShapeDtypeStruct((t, d), F32),
        jax.ShapeDtypeStruct((t, d // 2), I32),
        jax.ShapeDtypeStruct((N_EXPERTS, t), F32),
        jax.ShapeDtypeStruct((N_EXPERTS, t), BF16),
        jax.ShapeDtypeStruct((t // TM, 8, N_EXPERTS), F32),
    ]
    return specs, shapes


def _mixer_kernel(nbp, bps, sp, ss,
                  xpc_ref, xsc_ref, xprev_ref, xnext_ref, mod_ref, gmix_ref, win_ref, pw_ref, ps_ref,
                  cw_ref,
                  wout_ref, gffn_ref, wr_hi_ref, wr_lo_ref, rb_ref,
                  x1_ref, h_ref, gates_ref, sel_ref, cnt_ref):
    i = pl.program_id(0)
    is_p = i < nbp
    j = lax.rem(jnp.maximum(i - nbp, 0), bps)
    first = is_p | (j == 0)
    last = is_p | (j == bps - 1)
    base = jnp.where(is_p, 0, j * TM)
    slen = jnp.where(is_p, sp, ss)

    sh1 = mod_ref[0, 0:1, :]
    sc1 = mod_ref[0, 1:2, :]
    g1 = mod_ref[0, 2:3, :]
    xc = jnp.where(is_p, xpc_ref[...], xsc_ref[...])
    x_ext = jnp.concatenate([xprev_ref[...], xc, xnext_ref[...]], axis=0)
    h_ext = _adaln(x_ext, gmix_ref[...], sh1, sc1).astype(BF16)
    u = jnp.dot(h_ext, win_ref[...], preferred_element_type=F32)
    next_ = TM + 2 * HALO
    row = lax.broadcasted_iota(I32, (next_, 1), 0)
    keep = ((row >= HALO) | jnp.logical_not(first)) & ((row < HALO + TM) | jnp.logical_not(last))
    u = jnp.where(keep, u, 0.0)

    dm = u.shape[1] // 4
    ua = u[:, :dm]
    gate_b = u[HALO:HALO + TM, dm:2 * dm]
    z = u[:, 2 * dm:3 * dm] * u[:, 3 * dm:]

    def up(a, k):
        return pltpu.roll(a, next_ - k, 0)

    pos = base + lax.broadcasted_iota(I32, (TM, 1), 0)
    pg = dm // len(POOL_WINDOWS)
    ya = []
    for g, w in enumerate(POOL_WINDOWS):
        e = ua[:, g * pg:(g + 1) * pg]
        acc = e
        span = 1
        while span < w:
            acc = acc + up(acc, span)
            span *= 2
        off = HALO - w // 2
        wsum = (up(acc, off) if off else acc)[:TM]
        lo = jnp.maximum(pos - w // 2, 0)
        hi = jnp.minimum(pos + (w - w // 2 - 1), slen - 1)
        cnt = (hi - lo + 1).astype(F32)
        diff = wsum / cnt - e[HALO:HALO + TM]
        ya.append(jnp.dot(diff.astype(BF16), pw_ref[g], preferred_element_type=F32))
    y_a = jnp.concatenate(ya, axis=-1) * ps_ref[...]
    zc = (cw_ref[0:1, :] * up(z, HALO - 1)[:TM] + cw_ref[1:2, :] * z[HALO:HALO + TM]
          + cw_ref[2:3, :] * up(z, HALO + 1)[:TM])
    y_b = gate_b * zc
    ycat = jnp.concatenate([y_a, y_b], axis=-1).astype(BF16)
    y = jnp.dot(ycat, wout_ref[...], preferred_element_type=F32)
    x1 = xc + g1 * y
    x1_ref[...] = x1
    _ffn_pre(x1, mod_ref, gffn_ref, wr_hi_ref, wr_lo_ref, rb_ref,
             h_ref, gates_ref, sel_ref, cnt_ref)


def _mod_index(nbp, bps):
    def f(i):
        return jnp.where(i < nbp, 0, 1 + jnp.maximum(i - nbp, 0) // bps)
    return f


def _const_spec(shape):
    nd = len(shape)
    return pl.BlockSpec(shape, lambda i: (0,) * nd)


def _mixer(xp, xs, mod, g_mix, w_in, pool_w, pool_scale, conv_w, w_out, g_ffn, wr_hi, wr_lo, rbias,
           nbp, bps, sp, ss):
    d = xp.shape[1]
    t = xp.shape[0] + xs.shape[0]
    nblk = t // TM
    midx = _mod_index(nbp, bps)
    hpb = TM // HALO
    nh = xs.shape[0] // HALO
    in_specs = [
        pl.BlockSpec((TM, d), lambda i: (jnp.minimum(i, nbp - 1), 0)),
        pl.BlockSpec((TM, d), lambda i: (jnp.maximum(i - nbp, 0), 0)),
        pl.BlockSpec((HALO, d), lambda i: (jnp.maximum((i - nbp) * hpb - 1, 0), 0)),
        pl.BlockSpec((HALO, d), lambda i: (jnp.clip((i - nbp + 1) * hpb, 0, nh - 1), 0)),
        pl.BlockSpec((1, 6, d), lambda i: (midx(i), 0, 0)),
        _const_spec(g_mix.shape), _const_spec(w_in.shape), _const_spec(pool_w.shape),
        _const_spec(pool_scale.shape), _const_spec(conv_w.shape), _const_spec(w_out.shape),
        _const_spec(g_ffn.shape), _const_spec(wr_hi.shape), _const_spec(wr_lo.shape),
        _const_spec(rbias.shape),
    ]
    out_specs, out_shape = _pre_out_specs(t, d)
    return pl.pallas_call(
        functools.partial(_mixer_kernel, nbp, bps, sp, ss),
        grid=(nblk,), in_specs=in_specs, out_specs=out_specs, out_shape=out_shape,
        compiler_params=_cparams(("parallel",)), name="pool_conv_mixer",
    )(xp, xs, xs, xs, mod, g_mix, w_in, pool_w, pool_scale, conv_w, w_out, g_ffn, wr_hi, wr_lo,
      rbias)


def _slots_kernel(sel_ref, gates_ref, base_ref, pos_ref, w_ref):
    sel = sel_ref[...]
    tm = sel.shape[1]
    r = lax.broadcasted_iota(I32, (tm, tm), 0)
    c = lax.broadcasted_iota(I32, (tm, tm), 1)
    before = (r < c).astype(F32).astype(BF16)
    rank_tok = jnp.dot(sel, before, preferred_element_type=F32)
    er = lax.broadcasted_iota(I32, (N_EXPERTS, N_EXPERTS), 0)
    ec = lax.broadcasted_iota(I32, (N_EXPERTS, N_EXPERTS), 1)
    lower = (ec < er).astype(F32).astype(BF16)
    rank_exp = jnp.dot(lower, sel, preferred_element_type=F32)
    slot = base_ref[0] + rank_tok
    chosen = sel > 0
    gates = gates_ref[...]
    sub = lax.broadcasted_iota(I32, (TOPK, tm), 0)
    pos8 = jnp.zeros((TOPK, tm), F32)
    w8 = jnp.zeros((TOPK, tm), F32)
    for k in range(TOPK):
        mk = chosen & (rank_exp == float(k))
        pk = jnp.sum(jnp.where(mk, slot, 0.0), axis=0, keepdims=True)
        wk = jnp.sum(jnp.where(mk, gates, 0.0), axis=0, keepdims=True)
        pos8 = jnp.where(sub == k, pk, pos8)
        w8 = jnp.where(sub == k, wk, w8)
    pos_ref[...] = pos8.astype(I32)
    w8 = jnp.concatenate([w8, jnp.zeros((128 - TOPK, tm), F32)], axis=0)
    w_ref[...] = w8.T


def _slots(sel_t, gates_t, base):
    ne, t = sel_t.shape
    return pl.pallas_call(
        _slots_kernel, grid=(t // TM,),
        in_specs=[
            pl.BlockSpec((ne, TM), lambda i: (0, i)),
            pl.BlockSpec((ne, TM), lambda i: (0, i)),
            pl.BlockSpec((1, ne, 1), lambda i: (i, 0, 0)),
        ],
        out_specs=[pl.BlockSpec((TOPK, TM), lambda i: (0, i)),
                   pl.BlockSpec((TM, 128), lambda i: (i, 0))],
        out_shape=[jax.ShapeDtypeStruct((TOPK, t), I32), jax.ShapeDtypeStruct((t, 128), F32)],
        compiler_params=_cparams(("parallel",)), name="moe_slots",
    )(sel_t, gates_t, base)


def _sc_gather(table, idx):
    m = idx.shape[0]
    d = table.shape[1]
    assert m % (SC_WINDOW * SC_WORKERS) == 0
    mesh = plsc.VectorSubcoreMesh(core_axis_name="core", subcore_axis_name="subcore")

    @pl.kernel(out_type=jax.ShapeDtypeStruct((m, d), table.dtype), mesh=mesh)
    def gather_rows(x_hbm, i_hbm, o_hbm):
        def body(i_vmem, o_vmem):
            pltpu.sync_copy(x_hbm.at[i_vmem.at[0]], o_vmem)

        pltpu.emit_pipeline(
            body, grid=(m // SC_WINDOW,),
            in_specs=[pl.BlockSpec((1, SC_WINDOW), index_map=lambda i: (i, 0))],
            out_specs=[pl.BlockSpec((SC_WINDOW, d), index_map=lambda i: (i, 0))],
            core_axis_name=("core", "subcore"),
            dimension_semantics=(pltpu.PARALLEL,),
        )(i_hbm, o_hbm)

    return gather_rows(table, idx.reshape(m // SC_WINDOW, SC_WINDOW))


def _sc_scatter(rows, idx, n_out):
    nk, t = idx.shape
    d = rows.shape[1]
    assert t % (SC_WINDOW * SC_WORKERS) == 0
    mesh = plsc.VectorSubcoreMesh(core_axis_name="core", subcore_axis_name="subcore")

    @pl.kernel(out_type=jax.ShapeDtypeStruct((n_out, d), rows.dtype), mesh=mesh)
    def scatter_rows(x_hbm, i_hbm, o_hbm):
        def body(x_vmem, i_vmem):
            for k in range(nk):
                pltpu.sync_copy(x_vmem, o_hbm.at[i_vmem.at[k, 0]])

        pltpu.emit_pipeline(
            body, grid=(t // SC_WINDOW,),
            in_specs=[pl.BlockSpec((SC_WINDOW, d), index_map=lambda i: (i, 0)),
                      pl.BlockSpec((nk, 1, SC_WINDOW), index_map=lambda i: (0, i, 0))],
            out_specs=[],
            core_axis_name=("core", "subcore"),
            dimension_semantics=(pltpu.PARALLEL,),
        )(x_hbm, i_hbm)

    return scatter_rows(rows, idx.reshape(nk, t // SC_WINDOW, SC_WINDOW))


def _expert_ffn_kernel(te_ref, fresh_ref, nv_ref, x_hbm, wg_ref, wu_ref, wd_ref, y_ref,
                       xbuf, xsem, wg_s, wu_s, wd_s):
    j = pl.program_id(0)
    nv = nv_ref[0]

    def tile_copy(tile, slot):
        rows = pl.ds(pl.multiple_of(tile * TG, TG), TG)
        return pltpu.make_async_copy(x_hbm.at[rows, :], xbuf.at[slot], xsem.at[slot])

    @pl.when(j == 0)
    def _():
        tile_copy(0, 0).start()

        @pl.when(nv > 1)
        def _():
            tile_copy(1, 1).start()

    @pl.when(j < nv)
    def _():
        @pl.when(j + 2 < nv)
        def _():
            tile_copy(j + 2, lax.rem(j + 2, 3)).start()

        slot = lax.rem(j, 3)
        tile_copy(j, slot).wait()

        @pl.when(fresh_ref[j] == 1)
        def _():
            wg_s[...] = wg_ref[0, 0].astype(BF16)
            wu_s[...] = wu_ref[0, 0].astype(BF16)
            wd_s[...] = wd_ref[0, 0].astype(BF16)

        x = _unpack_pair(xbuf[slot]).astype(BF16)
        a = jnp.dot(x, wg_s[...], preferred_element_type=F32)
        b = jnp.dot(x, wu_s[...], preferred_element_type=F32)
        act = (_silu(a) * b).astype(BF16)
        y_ref[...] = _pack_pair(jnp.dot(act, wd_s[...], preferred_element_type=F32))


def _expert_ffn(xs, n_tiles, tile_expert, fresh, n_valid, layer, w_gate, w_up, w_down):
    dh = xs.shape[1]
    last = lambda j, te, fr, nv: jnp.minimum(j, nv[0] - 1)
    wspec = lambda w: pl.BlockSpec((1, 1) + w.shape[2:],
                                   lambda j, te, fr, nv: (layer, te[j], 0, 0))
    grid_spec = pltpu.PrefetchScalarGridSpec(
        num_scalar_prefetch=3, grid=(n_tiles,),
        in_specs=[pl.BlockSpec(memory_space=pl.ANY), wspec(w_gate), wspec(w_up), wspec(w_down)],
        out_specs=pl.BlockSpec((TG, dh), lambda j, te, fr, nv: (last(j, te, fr, nv), 0)),
        scratch_shapes=[pltpu.VMEM((3, TG, dh), xs.dtype), pltpu.SemaphoreType.DMA((3,)),
                        pltpu.VMEM(w_gate.shape[2:], BF16), pltpu.VMEM(w_up.shape[2:], BF16),
                        pltpu.VMEM(w_down.shape[2:], BF16)],
    )
    return pl.pallas_call(
        _expert_ffn_kernel, grid_spec=grid_spec,
        out_shape=jax.ShapeDtypeStruct((n_tiles * TG, dh), I32),
        compiler_params=_cparams(("arbitrary",)), name="expert_ffn",
    )(tile_expert, fresh, n_valid, xs, w_gate, w_up, w_down)


def _combine_value(yg_ref, w_ref, h_ref, x_ref, mod_ref, sgu_ref, sd_ref):
    f = sd_ref.shape[0]
    h = _unpack_pair(h_ref[...]).astype(BF16)
    hs = jnp.dot(h, sgu_ref[...], preferred_element_type=F32)
    act = (_silu(hs[:, :f]) * hs[:, f:]).astype(BF16)
    acc = jnp.dot(act, sd_ref[...], preferred_element_type=F32)
    w = w_ref[...]
    for k in range(TOPK):
        acc = acc + w[:, k:k + 1] * _unpack_pair(yg_ref[k])
    return x_ref[...] + mod_ref[0, 5:6, :] * acc


def _combine_kernel(nbp, yg_ref, w_ref, h_ref, x_ref, mod_ref, sgu_ref, sd_ref, *o_refs):
    out = _combine_value(yg_ref, w_ref, h_ref, x_ref, mod_ref, sgu_ref, sd_ref)
    if len(o_refs) == 1:
        o_refs[0][...] = out
    else:
        i = pl.program_id(0)

        @pl.when(i < nbp)
        def _():
            o_refs[0][...] = out

        @pl.when(i >= nbp)
        def _():
            o_refs[1][...] = out


def _combine(yg, w8, h, x, mod, sgu, sd, nbp, bps, split_out):
    t, d = x.shape
    tc = 2 * TM
    assert (nbp * TM) % tc == 0 and (bps * TM) % tc == 0
    nbp, bps = nbp * TM // tc, bps * TM // tc
    midx = _mod_index(nbp, bps)
    if split_out:
        out_specs = [pl.BlockSpec((tc, d), lambda i: (jnp.minimum(i, nbp - 1), 0)),
                     pl.BlockSpec((tc, d), lambda i: (jnp.maximum(i - nbp, 0), 0))]
        out_shape = [jax.ShapeDtypeStruct((nbp * tc, d), F32),
                     jax.ShapeDtypeStruct((t - nbp * tc, d), F32)]
    else:
        out_specs = pl.BlockSpec((tc, d), lambda i: (i, 0))
        out_shape = jax.ShapeDtypeStruct((t, d), F32)
    in_specs = [
        pl.BlockSpec((TOPK, tc, d // 2), lambda i: (0, i, 0)),
        pl.BlockSpec((tc, 128), lambda i: (i, 0)),
        pl.BlockSpec((tc, d // 2), lambda i: (i, 0)),
        pl.BlockSpec((tc, d), lambda i: (i, 0)),
        pl.BlockSpec((1, 6, d), lambda i: (midx(i), 0, 0)),
        _const_spec(sgu.shape), _const_spec(sd.shape),
    ]
    return pl.pallas_call(
        functools.partial(_combine_kernel, nbp), grid=(t // tc,), in_specs=in_specs,
        out_specs=out_specs, out_shape=out_shape,
        compiler_params=_cparams(("arbitrary",)), name="moe_combine",
    )(yg, w8, h, x, mod, sgu, sd)


def _combine_qkv_kernel(nbp, yg_ref, w_ref, h_ref, x_ref, mod_ref, sgu_ref, sd_ref,
                        mod1_ref, gmix_ref, wqkv_ref, qg_ref, kg_ref, hm_ref, hmt_ref,
                        x_out, q_out, k_out, v_out, k32_out, v32_out):
    out = _combine_value(yg_ref, w_ref, h_ref, x_ref, mod_ref, sgu_ref, sd_ref)
    x_out[...] = out
    q, k, v = _qkv_values(out, mod1_ref, gmix_ref, wqkv_ref, qg_ref, kg_ref, hm_ref, hmt_ref)
    q_out[...] = q.astype(BF16)
    k_out[...] = k.astype(BF16)
    v_out[...] = v.astype(BF16)

    @pl.when(pl.program_id(0) < nbp)
    def _():
        k32_out[...] = k
        v32_out[...] = v


def _combine_qkv(yg, w8, h, x, mod, sgu, sd, mod1, g_mix, w_qkv, qg, kg, hm, hmt, nbp, bps):
    t, d = x.shape
    midx = _mod_index(nbp, bps)
    row = lambda i: (i, 0)
    in_specs = [
        pl.BlockSpec((TOPK, TM, d // 2), lambda i: (0, i, 0)),
        pl.BlockSpec((TM, 128), row),
        pl.BlockSpec((TM, d // 2), row),
        pl.BlockSpec((TM, d), row),
        pl.BlockSpec((1, 6, d), lambda i: (midx(i), 0, 0)),
        _const_spec(sgu.shape), _const_spec(sd.shape),
        pl.BlockSpec((1, 6, d), lambda i: (midx(i), 0, 0)),
        _const_spec(g_mix.shape), _const_spec(w_qkv.shape), _const_spec(qg.shape),
        _const_spec(kg.shape), _const_spec(hm.shape), _const_spec(hmt.shape),
    ]
    pinned = pl.BlockSpec((TM, d), lambda i: (jnp.minimum(i, nbp - 1), 0))
    out_specs = [pl.BlockSpec((TM, d), row)] * 4 + [pinned, pinned]
    out_shape = ([jax.ShapeDtypeStruct((t, d), F32)] + [jax.ShapeDtypeStruct((t, d), BF16)] * 3
                 + [jax.ShapeDtypeStruct((nbp * TM, d), F32)] * 2)
    return pl.pallas_call(
        functools.partial(_combine_qkv_kernel, nbp), grid=(t // TM,), in_specs=in_specs,
        out_specs=out_specs, out_shape=out_shape,
        compiler_params=_cparams(("arbitrary",)), name="moe_combine_qkv",
    )(yg, w8, h, x, mod, sgu, sd, mod1, g_mix, w_qkv, qg, kg, hm, hmt)


def _sparse_moe(x, h, gates_t, sel_t, cnt, layer, w_gate, w_up, w_down, finish):
    t, d = x.shape
    ne = N_EXPERTS
    n_pad = ne * TG
    n_tiles = (t * TOPK + n_pad) // TG
    n_slots = n_tiles * TG
    counts = cnt[:, 0, :].astype(I32)
    per_expert = jnp.sum(counts, axis=0)
    padded = (per_expert + TG - 1) // TG * TG
    ends = jnp.cumsum(padded)
    starts = ends - padded
    block_off = jnp.cumsum(counts, axis=0) - counts
    base = (starts[None, :] + block_off).astype(F32)[:, :, None]
    n_valid = (ends[-1] // TG).astype(I32).reshape(1)
    tile_start = jnp.minimum(jnp.arange(n_tiles, dtype=I32), n_valid[0] - 1) * TG
    tile_expert = jnp.sum((ends[None, :] <= tile_start[:, None]).astype(I32), axis=1)
    fresh = jnp.concatenate([jnp.ones((1,), I32),
                             (tile_expert[1:] != tile_expert[:-1]).astype(I32)])

    pos8, w8 = _slots(sel_t, gates_t, base)
    n_extra = -(-n_pad // t)
    cand = (starts + per_expert)[:, None] + jnp.arange(TG, dtype=I32)[None, :]
    spare = n_slots + jnp.arange(n_pad, dtype=I32)
    fill = jnp.where(cand < ends[:, None], cand, spare.reshape(ne, TG)).reshape(-1)
    rest = n_slots + jnp.arange(n_pad, n_extra * t, dtype=I32) % n_pad
    dest = jnp.concatenate([pos8, jnp.concatenate([fill, rest]).reshape(n_extra, t)], axis=0)
    xs = _sc_scatter(h, dest, n_slots + n_pad)
    ys = _expert_ffn(xs, n_tiles, tile_expert, fresh, n_valid, layer, w_gate, w_up, w_down)
    yg = _sc_gather(ys, pos8.reshape(-1)).reshape(TOPK, t, d // 2)
    return finish(yg, w8)


def _head_rms(x, g_row, hm_ref, hmt_ref):
    ss = jnp.dot((x * x).astype(BF16), hm_ref[...], preferred_element_type=F32)
    r = lax.rsqrt(ss * (1.0 / HEAD_DIM) + EPS)
    r_hi, r_lo = _split_bf16(r)
    hmt = hmt_ref[...]
    rb = (jnp.dot(r_hi, hmt, preferred_element_type=F32)
          + jnp.dot(r_lo, hmt, preferred_element_type=F32))
    return (x * rb) * g_row


def _qkv_values(x, mod_ref, gmix_ref, w_ref, qg_ref, kg_ref, hm_ref, hmt_ref):
    d = x.shape[1]
    sh1 = mod_ref[0, 0:1, :]
    sc1 = mod_ref[0, 1:2, :]
    h = _adaln(x, gmix_ref[...], sh1, sc1).astype(BF16)
    qkv = jnp.dot(h, w_ref[...], preferred_element_type=F32)
    q = _head_rms(qkv[:, :d], qg_ref[...], hm_ref, hmt_ref)
    k = _head_rms(qkv[:, d:2 * d], kg_ref[...], hm_ref, hmt_ref)
    return q, k, qkv[:, 2 * d:]


def _qkv_kernel(emit_f32, x_ref, mod_ref, gmix_ref, w_ref, qg_ref, kg_ref, hm_ref, hmt_ref, *outs):
    q, k, v = _qkv_values(x_ref[...], mod_ref, gmix_ref, w_ref, qg_ref, kg_ref, hm_ref, hmt_ref)
    outs[0][...] = q.astype(BF16)
    outs[1][...] = k.astype(BF16)
    outs[2][...] = v.astype(BF16)
    if emit_f32:
        outs[3][...] = k
        outs[4][...] = v


def _qkv(x, mod, g_mix, w_qkv, qg, kg, hm, hmt, blk0, nblk, midx, emit_f32):
    t, d = x.shape
    in_specs = [
        pl.BlockSpec((TM, d), lambda i: (i + blk0, 0)),
        pl.BlockSpec((1, 6, d), lambda i: (midx(i + blk0), 0, 0)),
        _const_spec(g_mix.shape), _const_spec(w_qkv.shape), _const_spec(qg.shape),
        _const_spec(kg.shape), _const_spec(hm.shape), _const_spec(hmt.shape),
    ]
    n_out = 5 if emit_f32 else 3
    out_specs = [pl.BlockSpec((TM, d), lambda i: (i, 0)) for _ in range(n_out)]
    out_shape = [jax.ShapeDtypeStruct((nblk * TM, d), BF16 if o < 3 else F32) for o in range(n_out)]
    return pl.pallas_call(
        functools.partial(_qkv_kernel, emit_f32),
        grid=(nblk,), in_specs=in_specs, out_specs=out_specs, out_shape=out_shape,
        compiler_params=_cparams(("parallel",)), name="qkv_f32" if emit_f32 else "qkv",
    )(x, mod, g_mix, w_qkv, qg, kg, hm, hmt)


def _head_masks():
    lane = lax.broadcasted_iota(I32, (1, 2 * HEAD_DIM), 1)
    return lane < HEAD_DIM


def _ctx_attn_kernel(q_ref, k_ref, v_ref, o_ref):
    lo = _head_masks()
    pw = 2 * HEAD_DIM
    for hp in range(q_ref.shape[1] // pw):
        cols = slice(hp * pw, (hp + 1) * pw)
        q = q_ref[:, cols]
        k = k_ref[:, cols]
        v = v_ref[:, cols]
        outs = []
        for hh in range(2):
            msk = lo if hh == 0 else jnp.logical_not(lo)
            qm = jnp.where(msk, q, jnp.zeros_like(q)) * jnp.asarray(HEAD_DIM ** -0.5, BF16)
            s = lax.dot_general(qm, k, NT_DIMS, preferred_element_type=F32)
            m = jnp.max(s, axis=-1, keepdims=True)
            p = jnp.exp(s - m)
            l = jnp.sum(p, axis=-1, keepdims=True)
            o = jnp.dot(p.astype(BF16), v, preferred_element_type=F32)
            outs.append(o / l)
        o_ref[:, cols] = jnp.where(lo, outs[0], outs[1]).astype(BF16)


def _ctx_attn(q, k, v, nb, s):
    t, d = q.shape
    spec = pl.BlockSpec((s, d), lambda b: (b, 0))
    return pl.pallas_call(
        _ctx_attn_kernel, grid=(nb,), in_specs=[spec, spec, spec], out_specs=spec,
        out_shape=jax.ShapeDtypeStruct((nb * s, d), BF16),
        compiler_params=_cparams(("parallel",)), name="context_attention",
    )(q, k, v)


NA_QROWS = 8
NA_GROWS = 4
NA_KROWS = 12
NA_HPAIRS = 2


def _na_kernel(rows, q_ref, k_ref, v_ref, kc_ref, vc_ref, bias_ref, o_ref):
    rb = pl.program_id(2)
    ngrp = NA_QROWS // NA_GROWS
    nq = NA_GROWS * GRID_W
    nk = NA_KROWS * GRID_W
    lo = _head_masks()
    pw = 2 * HEAD_DIM
    for gi in range(ngrp):
        g = rb * ngrp + gi
        kr0 = jnp.clip(g * NA_GROWS - WIN_H // 2, 0, rows - NA_KROWS)
        start = pl.multiple_of(kr0 * GRID_W, 256)
        cls = jnp.where(g == 0, 0, jnp.where(g == rows // NA_GROWS - 1, 2, 1))
        qrows = slice(gi * nq, (gi + 1) * nq)
        for pp in range(NA_HPAIRS):
            cols = slice(pp * pw, (pp + 1) * pw)
            q = q_ref[0, qrows, cols]
            kw = k_ref[0, pl.ds(start, nk), cols]
            vw = v_ref[0, pl.ds(start, nk), cols]
            kc = kc_ref[0, :, cols]
            vc = vc_ref[0, :, cols]
            outs = []
            for hh in range(2):
                msk = lo if hh == 0 else jnp.logical_not(lo)
                qm = jnp.where(msk, q, jnp.zeros_like(q)) * jnp.asarray(HEAD_DIM ** -0.5, BF16)
                s = (lax.dot_general(qm, kw, NT_DIMS, preferred_element_type=F32)
                     + bias_ref[cls, 2 * pp + hh])
                sc = lax.dot_general(qm, kc, NT_DIMS, preferred_element_type=F32)
                m = jnp.maximum(jnp.max(s, axis=-1, keepdims=True),
                                jnp.max(sc, axis=-1, keepdims=True))
                p = jnp.exp(s - m)
                pc = jnp.exp(sc - m)
                l = jnp.sum(p, axis=-1, keepdims=True) + jnp.sum(pc, axis=-1, keepdims=True)
                o = (jnp.dot(p.astype(BF16), vw, preferred_element_type=F32)
                     + jnp.dot(pc.astype(BF16), vc, preferred_element_type=F32))
                outs.append(o / l)
            o_ref[0, qrows, cols] = jnp.where(lo, outs[0], outs[1]).astype(BF16)


def _na_row_classes(rows):
    out = []
    for g in (0, 1, rows // NA_GROWS - 1):
        r0 = g * NA_GROWS
        kr0 = int(np.clip(r0 - WIN_H // 2, 0, rows - NA_KROWS))
        table = []
        for rl in range(NA_GROWS):
            r = r0 + rl
            sr = int(np.clip(r - WIN_H // 2, 0, rows - WIN_H))
            table.append([(kr0 + kl - r + WIN_H - 1) if sr <= kr0 + kl < sr + WIN_H else None
                          for kl in range(NA_KROWS)])
        out.append(table)
    return out


def _na_bias_kernel(rows, rpb_ref, o_ref):
    h = pl.program_id(0)
    n_dr, n_dc = 2 * WIN_H - 1, 2 * WIN_W - 1
    cq = lax.broadcasted_iota(I32, (GRID_W, 2 * GRID_W), 0)
    lane = lax.broadcasted_iota(I32, (GRID_W, 2 * GRID_W), 1)
    ck = lane & (GRID_W - 1)
    q_start = jnp.clip(cq - WIN_W // 2, 0, GRID_W - WIN_W)
    col_ok = (ck >= q_start) & (ck < q_start + WIN_W)
    dc = ck - cq + (WIN_W - 1)
    neg = jnp.full((GRID_W, 2 * GRID_W), NEG, F32)
    tiles = []
    for i in range(n_dr):
        t = jnp.zeros((GRID_W, 2 * GRID_W), F32)
        for jj in range(n_dc):
            t = jnp.where(dc == jj, rpb_ref[h * (n_dr * n_dc) + i * n_dc + jj], t)
        tiles.append(jnp.where(col_ok, t, neg))
    first_half = lane < GRID_W
    for c, table in enumerate(_na_row_classes(rows)):
        for rl in range(NA_GROWS):
            for m in range(NA_KROWS // 2):
                ia, ib = table[rl][2 * m], table[rl][2 * m + 1]
                ta = neg if ia is None else tiles[ia]
                tb = neg if ib is None else tiles[ib]
                blk = ta if ia == ib else jnp.where(first_half, ta, tb)
                o_ref[c, 0, rl * GRID_W:(rl + 1) * GRID_W, m * 2 * GRID_W:(m + 1) * 2 * GRID_W] = blk


def _na_bias(rpb, rows):
    nh = rpb.shape[0]
    nq, nk = NA_GROWS * GRID_W, NA_KROWS * GRID_W
    return pl.pallas_call(
        functools.partial(_na_bias_kernel, rows), grid=(nh,),
        in_specs=[pl.BlockSpec(memory_space=pltpu.SMEM)],
        out_specs=pl.BlockSpec((3, 1, nq, nk), lambda h: (0, h, 0, 0)),
        out_shape=jax.ShapeDtypeStruct((3, nh, nq, nk), F32),
        compiler_params=_cparams(("parallel",)), name="na_bias",
    )(rpb.reshape(-1))


def _na_attn(q, k, v, kc, vc, bias_tab, rows, b_off=0):
    s, d = q.shape[1:]
    nb = kc.shape[0]
    lw = NA_HPAIRS * 2 * HEAD_DIM
    hp = d // lw
    nrb = rows // NA_QROWS
    nq = NA_QROWS * GRID_W
    lc = kc.shape[1]
    in_specs = [
        pl.BlockSpec((1, nq, lw), lambda h, b, r: (b + b_off, r, h)),
        pl.BlockSpec((1, s, lw), lambda h, b, r: (b + b_off, 0, h)),
        pl.BlockSpec((1, s, lw), lambda h, b, r: (b + b_off, 0, h)),
        pl.BlockSpec((1, lc, lw), lambda h, b, r: (b, 0, h)),
        pl.BlockSpec((1, lc, lw), lambda h, b, r: (b, 0, h)),
        pl.BlockSpec((3, 2 * NA_HPAIRS) + bias_tab.shape[2:], lambda h, b, r: (0, h, 0, 0)),
    ]
    return pl.pallas_call(
        functools.partial(_na_kernel, rows),
        grid=(hp, nb, nrb), in_specs=in_specs,
        out_specs=pl.BlockSpec((1, nq, lw), lambda h, b, r: (b, r, h)),
        out_shape=jax.ShapeDtypeStruct((nb, s, d), BF16),
        compiler_params=_cparams(("parallel", "parallel", "parallel")),
        name="neighbourhood_attention",
    )(q, k, v, kc, vc, bias_tab)


def _oproj_kernel(nbp, ap_ref, as_ref, x_ref, mod_ref, wout_ref, gffn_ref, wr_hi_ref, wr_lo_ref,
                  rb_ref, x1_ref, h_ref, gates_ref, sel_ref, cnt_ref):
    a = jnp.where(pl.program_id(0) < nbp, ap_ref[...], as_ref[...])
    y = jnp.dot(a, wout_ref[...], preferred_element_type=F32)
    x1 = x_ref[...] + mod_ref[0, 2:3, :] * y
    x1_ref[...] = x1
    _ffn_pre(x1, mod_ref, gffn_ref, wr_hi_ref, wr_lo_ref, rb_ref,
             h_ref, gates_ref, sel_ref, cnt_ref)


def _oproj(attn_p, attn_s, x, mod, w_out, g_ffn, wr_hi, wr_lo, rbias, nbp, bps):
    t, d = x.shape
    midx = _mod_index(nbp, bps)
    in_specs = [
        pl.BlockSpec((TM, d), lambda i: (jnp.minimum(i, nbp - 1), 0)),
        pl.BlockSpec((TM, d), lambda i: (jnp.maximum(i - nbp, 0), 0)),
        pl.BlockSpec((TM, d), lambda i: (i, 0)),
        pl.BlockSpec((1, 6, d), lambda i: (midx(i), 0, 0)),
        _const_spec(w_out.shape), _const_spec(g_ffn.shape), _const_spec(wr_hi.shape),
        _const_spec(wr_lo.shape), _const_spec(rbias.shape),
    ]
    out_specs, out_shape = _pre_out_specs(t, d)
    return pl.pallas_call(
        functools.partial(_oproj_kernel, nbp), grid=(t // TM,), in_specs=in_specs,
        out_specs=out_specs, out_shape=out_shape, compiler_params=_cparams(("parallel",)),
        name="attn_out_proj",
    )(attn_p, attn_s, x, mod, w_out, g_ffn, wr_hi, wr_lo, rbias)


def _router_weights(w_router, router_bias):
    d, ne = w_router.shape
    w = jnp.pad(w_router, ((0, 0), (0, 128 - ne)))
    hi = w.astype(BF16)
    lo = (w - hi.astype(F32)).astype(BF16)
    return hi, lo, router_bias.reshape(ne, 1)


def _shared_weights(s_gate, s_up, s_down):
    return jnp.concatenate([s_gate, s_up], axis=-1).astype(BF16), s_down.astype(BF16)


def _layer_weights(layer, d, norm_mix, norm_ffn, pc, na, moe):
    i = layer // 2
    w = dict(g_mix=norm_mix[layer].reshape(1, d), g_ffn=norm_ffn[layer].reshape(1, d))
    w["wr_hi"], w["wr_lo"], w["rbias"] = _router_weights(moe["router"][layer],
                                                         moe["router_bias"][layer])
    w["sgu"], w["sd"] = _shared_weights(moe["s_gate"][layer], moe["s_up"][layer],
                                        moe["s_down"][layer])
    if layer % 2 == 0:
        w.update(w_in=pc["w_in"][i].astype(BF16), pool_w=pc["pool_w"][i].astype(BF16),
                 pool_scale=pc["pool_scale"][i].reshape(1, -1), conv_w=pc["conv_w"][i],
                 w_out=pc["w_out"][i].astype(BF16))
    else:
        nh = d // HEAD_DIM
        head_of = np.arange(d) // HEAD_DIM
        w.update(hm=jnp.asarray(head_of[:, None] == np.arange(128)[None, :], BF16),
                 hmt=jnp.asarray(np.arange(128)[:, None] == head_of[None, :], BF16),
                 qg=jnp.tile(na["q_norm"][i], nh).reshape(1, d),
                 kg=jnp.tile(na["k_norm"][i], nh).reshape(1, d),
                 w_qkv=na["w_qkv"][i].astype(BF16), w_out=na["w_out"][i].astype(BF16))
    return w


def kernel(x_prompt, x_sample, cache_k, cache_v, c, c_ctx, ada_w, ada_b, norm_mix, norm_ffn,
           pc_w_in, pc_pool_w, pc_pool_scale, pc_conv_w, pc_w_out,
           na_w_qkv, na_q_norm, na_k_norm, na_rpb, na_w_out,
           moe_router, moe_router_bias, moe_w_gate, moe_w_up, moe_w_down,
           moe_shared_gate, moe_shared_up, moe_shared_down):
    nb_p, s_p, d = x_prompt.shape
    nb_s, s_s, _ = x_sample.shape
    assert s_p == TM and s_s % TM == 0 and nb_p > 0 and nb_s > 0
    tp, ts = nb_p * s_p, nb_s * s_s
    nbp, bps = tp // TM, s_s // TM
    depth = ada_w.shape[0]
    nh = d // HEAD_DIM
    rows = s_s // GRID_W

    cond = jnp.concatenate([c_ctx[None], c], axis=0)
    cond = jnp.pad(cond, ((0, -cond.shape[0] % 8), (0, 0)))
    mods = _modulation(cond, ada_w, ada_b).reshape(depth, cond.shape[0], 6, d)
    pc = dict(w_in=pc_w_in, pool_w=pc_pool_w, pool_scale=pc_pool_scale, conv_w=pc_conv_w,
              w_out=pc_w_out)
    na = dict(w_qkv=na_w_qkv, q_norm=na_q_norm, k_norm=na_k_norm, w_out=na_w_out)
    moe = dict(router=moe_router, router_bias=moe_router_bias, s_gate=moe_shared_gate,
               s_up=moe_shared_up, s_down=moe_shared_down)

    xp, xs = x_prompt.reshape(tp, d), x_sample.reshape(ts, d)
    x = None
    ready_qkv = None
    new_k, new_v = [], []
    weights = [_layer_weights(l, d, norm_mix, norm_ffn, pc, na, moe) for l in range(depth)]
    for layer in range(depth):
        i = layer // 2
        w = weights[layer]
        mod = mods[layer]
        if layer % 2 == 0:
            if x is not None:
                xp, xs = x[:tp], x[tp:]
            x, h, gates_t, sel_t, cnt = _mixer(
                xp, xs, mod, w["g_mix"], w["w_in"], w["pool_w"], w["pool_scale"], w["conv_w"],
                w["w_out"], w["g_ffn"], w["wr_hi"], w["wr_lo"], w["rbias"], nbp, bps, s_p, s_s)
        else:
            if ready_qkv is not None and tp % s_s == 0:
                q_all, k_all, v_all, kp32, vp32 = ready_qkv
                qp, kp, vp = q_all, k_all, v_all
                seqs = (tp + ts) // s_s
                qs, ks, vs = (a.reshape(seqs, s_s, d) for a in (q_all, k_all, v_all))
                b_off = tp // s_s
            else:
                midx = _mod_index(nbp, bps)
                qkv_args = (x, mod, w["g_mix"], w["w_qkv"], w["qg"], w["kg"], w["hm"], w["hmt"])
                qp, kp, vp, kp32, vp32 = _qkv(*qkv_args, 0, nbp, midx, True)
                qs, ks, vs = (a.reshape(nb_s, s_s, d)
                              for a in _qkv(*qkv_args, nbp, ts // TM, midx, False))
                b_off = 0
            new_k.append(kp32.reshape(nb_p, s_p, nh, HEAD_DIM))
            new_v.append(vp32.reshape(nb_p, s_p, nh, HEAD_DIM))
            a_p = _ctx_attn(qp, kp, vp, nb_p, s_p)
            lc = cache_k.shape[2]
            kc = cache_k[:, i].reshape(nb_s, lc, d).astype(BF16)
            vc = cache_v[:, i].reshape(nb_s, lc, d).astype(BF16)
            a_s = _na_attn(qs, ks, vs, kc, vc, _na_bias(na_rpb[i], rows), rows, b_off)
            x, h, gates_t, sel_t, cnt = _oproj(a_p, a_s.reshape(ts, d), x, mod, w["w_out"],
                                               w["g_ffn"], w["wr_hi"], w["wr_lo"], w["rbias"],
                                               nbp, bps)
        ready_qkv = None
        if layer + 1 < depth and (layer + 1) % 2 == 1:
            w1 = weights[layer + 1]
            finish = lambda yg, w8: _combine_qkv(
                yg, w8, h, x, mod, w["sgu"], w["sd"], mods[layer + 1], w1["g_mix"], w1["w_qkv"],
                w1["qg"], w1["kg"], w1["hm"], w1["hmt"], nbp, bps)
            x, *ready_qkv = _sparse_moe(x, h, gates_t, sel_t, cnt, layer, moe_w_gate, moe_w_up,
                                        moe_w_down, finish)
        else:
            finish = lambda yg, w8: _combine(yg, w8, h, x, mod, w["sgu"], w["sd"], nbp, bps,
                                             layer == depth - 1)
            x = _sparse_moe(x, h, gates_t, sel_t, cnt, layer, moe_w_gate, moe_w_up, moe_w_down,
                            finish)
    y_p, y_s = x
    return (y_p.reshape(nb_p, s_p, d), y_s.reshape(nb_s, s_s, d),
            jnp.stack(new_k, axis=1), jnp.stack(new_v, axis=1))
```

```python
import functools

import numpy as np
import jax
import jax.numpy as jnp
from jax import lax
from jax.experimental import pallas as pl
from jax.experimental.pallas import tpu as pltpu
from jax.experimental.pallas import tpu_sc as plsc

F32 = jnp.float32
BF16 = jnp.bfloat16
I32 = jnp.int32
U32 = jnp.uint32

TM = 256
HALO = 8
POOL_WINDOWS = (2, 4, 8, 16)
N_EXPERTS = 64
N_GROUPS = 8
GROUP_SIZE = N_EXPERTS // N_GROUPS
TOPK_GROUPS = 4
TOPK = 8
ROUTED_SCALE = 2.5
EPS = 1e-6
GRID_W = 64
WIN_H = 8
WIN_W = 16
HEAD_DIM = 64
NEG = float(np.finfo(np.float32).min)
VMEM_LIMIT = 56 * 1024 * 1024
NT_DIMS = (((1,), (1,)), ((), ()))
TG = 1024
SC_WINDOW = 64
SC_WORKERS = 32


def _cparams(sem):
    return pltpu.CompilerParams(dimension_semantics=sem, vmem_limit_bytes=VMEM_LIMIT)


def _silu(x):
    return x * jax.nn.sigmoid(x)


def _split_bf16(x):
    hi = x.astype(BF16)
    lo = (x - hi.astype(F32)).astype(BF16)
    return hi, lo


def _adaln(x, g, shift, scale):
    ms = jnp.mean(x * x, axis=-1, keepdims=True)
    return (x * lax.rsqrt(ms + EPS)) * g * (1.0 + scale) + shift


def _pack_pair(x):
    w = x.shape[1] // 2
    lo = lax.bitcast_convert_type(x[:, :w].astype(BF16).astype(F32), U32) >> 16
    hi = lax.bitcast_convert_type(x[:, w:].astype(BF16).astype(F32), U32)
    return lax.bitcast_convert_type(lo | hi, I32)


def _unpack_pair(p):
    u = lax.bitcast_convert_type(p, U32)
    lo = lax.bitcast_convert_type(u << 16, F32)
    hi = lax.bitcast_convert_type(u & jnp.uint32(0xFFFF0000), F32)
    return jnp.concatenate([lo, hi], axis=-1)


def _mod_kernel(cond_ref, w_ref, b_ref, o_ref):
    c = cond_ref[...]
    a = _silu(c)
    o_ref[0] = jnp.dot(a, w_ref[0], preferred_element_type=F32,
                       precision=lax.Precision.HIGHEST) + b_ref[0]


def _modulation(cond, ada_w, ada_b):
    depth, d, n = ada_w.shape
    rows = cond.shape[0]
    tn = 1536
    return pl.pallas_call(
        _mod_kernel,
        grid=(depth, n // tn),
        in_specs=[
            pl.BlockSpec((rows, d), lambda l, j: (0, 0)),
            pl.BlockSpec((1, d, tn), lambda l, j: (l, 0, j)),
            pl.BlockSpec((1, 1, tn), lambda l, j: (l, 0, j)),
        ],
        out_specs=pl.BlockSpec((1, rows, tn), lambda l, j: (l, 0, j)),
        out_shape=jax.ShapeDtypeStruct((depth, rows, n), F32),
        compiler_params=_cparams(("arbitrary", "arbitrary")),
        name="modulation",
    )(cond, ada_w, ada_b.reshape(depth, 1, n))


def _route(logits_t, bias_col):
    tm = logits_t.shape[1]
    scores = jax.nn.sigmoid(logits_t)
    biased = scores + bias_col
    sub = lax.broadcasted_iota(I32, (GROUP_SIZE, tm), 0).astype(F32)
    ninf = jnp.float32(-jnp.inf)
    groups, gscore = [], []
    for g in range(N_GROUPS):
        v = biased[g * GROUP_SIZE:(g + 1) * GROUP_SIZE]
        m1 = jnp.max(v, axis=0, keepdims=True)
        first = jnp.min(jnp.where(v == m1, sub, float(GROUP_SIZE)), axis=0, keepdims=True)
        m2 = jnp.max(jnp.where(sub == first, ninf, v), axis=0, keepdims=True)
        groups.append(v)
        gscore.append(m1 + m2)
    masked = []
    for g in range(N_GROUPS):
        rank = jnp.zeros((1, tm), I32)
        for g2 in range(N_GROUPS):
            if g2 == g:
                continue
            ahead = gscore[g2] > gscore[g]
            if g2 < g:
                ahead = ahead | (gscore[g2] == gscore[g])
            rank = rank + ahead.astype(I32)
        masked.append(jnp.where(rank < TOPK_GROUPS, groups[g], ninf))
    masked = jnp.concatenate(masked, axis=0)
    eidx = lax.broadcasted_iota(I32, (N_EXPERTS, tm), 0).astype(F32)
    sel = jnp.zeros((N_EXPERTS, tm), jnp.bool_)
    for _ in range(TOPK):
        best = jnp.max(masked, axis=0, keepdims=True)
        first = jnp.min(jnp.where(masked == best, eidx, float(N_EXPERTS)), axis=0, keepdims=True)
        hit = eidx == first
        sel = sel | hit
        masked = jnp.where(hit, ninf, masked)
    w = jnp.where(sel, scores, 0.0)
    wsum = jnp.sum(w, axis=0, keepdims=True)
    return w / wsum * ROUTED_SCALE, sel


def _ffn_pre(x1, mod_ref, gffn_ref, wr_hi_ref, wr_lo_ref, rb_ref,
             h_ref, gates_ref, sel_ref, cnt_ref):
    sh2 = mod_ref[0, 3:4, :]
    sc2 = mod_ref[0, 4:5, :]
    h = _adaln(x1, gffn_ref[...], sh2, sc2)
    h_hi, h_lo = _split_bf16(h)
    h_ref[...] = _pack_pair(h_hi)
    wr_hi = wr_hi_ref[...]
    logits = (jnp.dot(h_hi, wr_hi, preferred_element_type=F32)
              + jnp.dot(h_hi, wr_lo_ref[...], preferred_element_type=F32)
              + jnp.dot(h_lo, wr_hi, preferred_element_type=F32))
    gates_t, sel = _route(logits.T[:N_EXPERTS], rb_ref[...])
    gates_ref[...] = gates_t
    sel_b = sel.astype(F32).astype(BF16)
    sel_ref[...] = sel_b
    ones = jnp.ones((8, sel_b.shape[1]), BF16)
    cnt_ref[0] = lax.dot_general(ones, sel_b, NT_DIMS, preferred_element_type=F32)


def _pre_out_specs(t, d):
    specs = [
        pl.BlockSpec((TM, d), lambda i: (i, 0)),
        pl.BlockSpec((TM, d // 2), lambda i: (i, 0)),
        pl.BlockSpec((N_EXPERTS, TM), lambda i: (0, i)),
        pl.BlockSpec((N_EXPERTS, TM), lambda i: (0, i)),
        pl.BlockSpec((1, 8, N_EXPERTS), lambda i: (i, 0, 0)),
    ]
    shapes = [
        jax.ShapeDtypeStruct((t, d), F32),
        jax.ShapeDtypeStruct((t, d // 2), I32),
        jax.ShapeDtypeStruct((N_EXPERTS, t), F32),
        jax.ShapeDtypeStruct((N_EXPERTS, t), BF16),
        jax.ShapeDtypeStruct((t // TM, 8, N_EXPERTS), F32),
    ]
    return specs, shapes


def _mixer_kernel(nbp, bps, sp, ss,
                  xpc_ref, xsc_ref, xprev_ref, xnext_ref, mod_ref, gmix_ref, win_ref, pw_ref, ps_ref,
                  cw_ref,
                  wout_ref, gffn_ref, wr_hi_ref, wr_lo_ref, rb_ref,
                  x1_ref, h_ref, gates_ref, sel_ref, cnt_ref):
    i = pl.program_id(0)
    is_p = i < nbp
    j = lax.rem(jnp.maximum(i - nbp, 0), bps)
    first = is_p | (j == 0)
    last = is_p | (j == bps - 1)
    base = jnp.where(is_p, 0, j * TM)
    slen = jnp.where(is_p, sp, ss)

    sh1 = mod_ref[0, 0:1, :]
    sc1 = mod_ref[0, 1:2, :]
    g1 = mod_ref[0, 2:3, :]
    xc = jnp.where(is_p, xpc_ref[...], xsc_ref[...])
    x_ext = jnp.concatenate([xprev_ref[...], xc, xnext_ref[...]], axis=0)
    h_ext = _adaln(x_ext, gmix_ref[...], sh1, sc1).astype(BF16)
    u = jnp.dot(h_ext, win_ref[...], preferred_element_type=F32)
    next_ = TM + 2 * HALO
    row = lax.broadcasted_iota(I32, (next_, 1), 0)
    keep = ((row >= HALO) | jnp.logical_not(first)) & ((row < HALO + TM) | jnp.logical_not(last))
    u = jnp.where(keep, u, 0.0)

    dm = u.shape[1] // 4
    ua = u[:, :dm]
    gate_b = u[HALO:HALO + TM, dm:2 * dm]
    z = u[:, 2 * dm:3 * dm] * u[:, 3 * dm:]

    def up(a, k):
        return pltpu.roll(a, next_ - k, 0)

    pos = base + lax.broadcasted_iota(I32, (TM, 1), 0)
    pg = dm // len(POOL_WINDOWS)
    ya = []
    for g, w in enumerate(POOL_WINDOWS):
        e = ua[:, g * pg:(g + 1) * pg]
        acc = e
        span = 1
        while span < w:
            acc = acc + up(acc, span)
            span *= 2
        off = HALO - w // 2
        wsum = (up(acc, off) if off else acc)[:TM]
        lo = jnp.maximum(pos - w // 2, 0)
        hi = jnp.minimum(pos + (w - w // 2 - 1), slen - 1)
        cnt = (hi - lo + 1).astype(F32)
        diff = wsum / cnt - e[HALO:HALO + TM]
        ya.append(jnp.dot(diff.astype(BF16), pw_ref[g], preferred_element_type=F32))
    y_a = jnp.concatenate(ya, axis=-1) * ps_ref[...]
    zc = (cw_ref[0:1, :] * up(z, HALO - 1)[:TM] + cw_ref[1:2, :] * z[HALO:HALO + TM]
          + cw_ref[2:3, :] * up(z, HALO + 1)[:TM])
    y_b = gate_b * zc
    ycat = jnp.concatenate([y_a, y_b], axis=-1).astype(BF16)
    y = jnp.dot(ycat, wout_ref[...], preferred_element_type=F32)
    x1 = xc + g1 * y
    x1_ref[...] = x1
    _ffn_pre(x1, mod_ref, gffn_ref, wr_hi_ref, wr_lo_ref, rb_ref,
             h_ref, gates_ref, sel_ref, cnt_ref)


def _mod_index(nbp, bps):
    def f(i):
        return jnp.where(i < nbp, 0, 1 + jnp.maximum(i - nbp, 0) // bps)
    return f


def _const_spec(shape):
    nd = len(shape)
    return pl.BlockSpec(shape, lambda i: (0,) * nd)


def _mixer(xp, xs, mod, g_mix, w_in, pool_w, pool_scale, conv_w, w_out, g_ffn, wr_hi, wr_lo, rbias,
           nbp, bps, sp, ss):
    d = xp.shape[1]
    t = xp.shape[0] + xs.shape[0]
    nblk = t // TM
    midx = _mod_index(nbp, bps)
    hpb = TM // HALO
    nh = xs.shape[0] // HALO
    in_specs = [
        pl.BlockSpec((TM, d), lambda i: (jnp.minimum(i, nbp - 1), 0)),
        pl.BlockSpec((TM, d), lambda i: (jnp.maximum(i - nbp, 0), 0)),
        pl.BlockSpec((HALO, d), lambda i: (jnp.maximum((i - nbp) * hpb - 1, 0), 0)),
        pl.BlockSpec((HALO, d), lambda i: (jnp.clip((i - nbp + 1) * hpb, 0, nh - 1), 0)),
        pl.BlockSpec((1, 6, d), lambda i: (midx(i), 0, 0)),
        _const_spec(g_mix.shape), _const_spec(w_in.shape), _const_spec(pool_w.shape),
        _const_spec(pool_scale.shape), _const_spec(conv_w.shape), _const_spec(w_out.shape),
        _const_spec(g_ffn.shape), _const_spec(wr_hi.shape), _const_spec(wr_lo.shape),
        _const_spec(rbias.shape),
    ]
    out_specs, out_shape = _pre_out_specs(t, d)
    return pl.pallas_call(
        functools.partial(_mixer_kernel, nbp, bps, sp, ss),
        grid=(nblk,), in_specs=in_specs, out_specs=out_specs, out_shape=out_shape,
        compiler_params=_cparams(("parallel",)), name="pool_conv_mixer",
    )(xp, xs, xs, xs, mod, g_mix, w_in, pool_w, pool_scale, conv_w, w_out, g_ffn, wr_hi, wr_lo,
      rbias)


SLOT_BLOCKS = 4


def _slots_kernel(sel_ref, gates_ref, base_ref, pos_ref, w_ref):
    r = lax.broadcasted_iota(I32, (TM, TM), 0)
    c = lax.broadcasted_iota(I32, (TM, TM), 1)
    before = (r < c).astype(F32).astype(BF16)
    er = lax.broadcasted_iota(I32, (N_EXPERTS, N_EXPERTS), 0)
    ec = lax.broadcasted_iota(I32, (N_EXPERTS, N_EXPERTS), 1)
    lower = (ec < er).astype(F32).astype(BF16)
    sub = lax.broadcasted_iota(I32, (TOPK, TM), 0)
    for blk in range(sel_ref.shape[1] // TM):
        cols = slice(blk * TM, (blk + 1) * TM)
        sel = sel_ref[:, cols]
        rank_tok = jnp.dot(sel, before, preferred_element_type=F32)
        rank_exp = jnp.dot(lower, sel, preferred_element_type=F32)
        slot = base_ref[blk] + rank_tok
        chosen = sel > 0
        gates = gates_ref[:, cols]
        pos8 = jnp.zeros((TOPK, TM), F32)
        w8 = jnp.zeros((TOPK, TM), F32)
        for k in range(TOPK):
            mk = chosen & (rank_exp == float(k))
            pk = jnp.sum(jnp.where(mk, slot, 0.0), axis=0, keepdims=True)
            wk = jnp.sum(jnp.where(mk, gates, 0.0), axis=0, keepdims=True)
            pos8 = jnp.where(sub == k, pk, pos8)
            w8 = jnp.where(sub == k, wk, w8)
        pos_ref[:, cols] = pos8.astype(I32)
        w8 = jnp.concatenate([w8, jnp.zeros((128 - TOPK, TM), F32)], axis=0)
        w_ref[cols, :] = w8.T


def _slots(sel_t, gates_t, base):
    ne, t = sel_t.shape
    nblk = next(n for n in range(SLOT_BLOCKS, 0, -1) if (t // TM) % n == 0)
    ts = nblk * TM
    return pl.pallas_call(
        _slots_kernel, grid=(t // ts,),
        in_specs=[
            pl.BlockSpec((ne, ts), lambda i: (0, i)),
            pl.BlockSpec((ne, ts), lambda i: (0, i)),
            pl.BlockSpec((nblk, ne, 1), lambda i: (i, 0, 0)),
        ],
        out_specs=[pl.BlockSpec((TOPK, ts), lambda i: (0, i)),
                   pl.BlockSpec((ts, 128), lambda i: (i, 0))],
        out_shape=[jax.ShapeDtypeStruct((TOPK, t), I32), jax.ShapeDtypeStruct((t, 128), F32)],
        compiler_params=_cparams(("parallel",)), name="moe_slots",
    )(sel_t, gates_t, base)


def _sc_gather(table, idx):
    m = idx.shape[0]
    d = table.shape[1]
    assert m % (SC_WINDOW * SC_WORKERS) == 0
    mesh = plsc.VectorSubcoreMesh(core_axis_name="core", subcore_axis_name="subcore")

    @pl.kernel(out_type=jax.ShapeDtypeStruct((m, d), table.dtype), mesh=mesh)
    def gather_rows(x_hbm, i_hbm, o_hbm):
        def body(i_vmem, o_vmem):
            pltpu.sync_copy(x_hbm.at[i_vmem.at[0]], o_vmem)

        pltpu.emit_pipeline(
            body, grid=(m // SC_WINDOW,),
            in_specs=[pl.BlockSpec((1, SC_WINDOW), index_map=lambda i: (i, 0))],
            out_specs=[pl.BlockSpec((SC_WINDOW, d), index_map=lambda i: (i, 0))],
            core_axis_name=("core", "subcore"),
            dimension_semantics=(pltpu.PARALLEL,),
        )(i_hbm, o_hbm)

    return gather_rows(table, idx.reshape(m // SC_WINDOW, SC_WINDOW))


def _sc_scatter(rows, idx, n_out):
    nk, t = idx.shape
    d = rows.shape[1]
    assert t % (SC_WINDOW * SC_WORKERS) == 0
    mesh = plsc.VectorSubcoreMesh(core_axis_name="core", subcore_axis_name="subcore")

    @pl.kernel(out_type=jax.ShapeDtypeStruct((n_out, d), rows.dtype), mesh=mesh)
    def scatter_rows(x_hbm, i_hbm, o_hbm):
        def body(x_vmem, i_vmem):
            for k in range(nk):
                pltpu.sync_copy(x_vmem, o_hbm.at[i_vmem.at[k, 0]])

        pltpu.emit_pipeline(
            body, grid=(t // SC_WINDOW,),
            in_specs=[pl.BlockSpec((SC_WINDOW, d), index_map=lambda i: (i, 0)),
                      pl.BlockSpec((nk, 1, SC_WINDOW), index_map=lambda i: (0, i, 0))],
            out_specs=[],
            core_axis_name=("core", "subcore"),
            dimension_semantics=(pltpu.PARALLEL,),
        )(x_hbm, i_hbm)

    return scatter_rows(rows, idx.reshape(nk, t // SC_WINDOW, SC_WINDOW))


def _expert_ffn_kernel(te_ref, fresh_ref, nv_ref, x_hbm, wg_ref, wu_ref, wd_ref, y_ref,
                       xbuf, xsem, wg_s, wu_s, wd_s):
    j = pl.program_id(0)
    nv = nv_ref[0]

    def tile_copy(tile, slot):
        rows = pl.ds(pl.multiple_of(tile * TG, TG), TG)
        return pltpu.make_async_copy(x_hbm.at[rows, :], xbuf.at[slot], xsem.at[slot])

    @pl.when(j == 0)
    def _():
        tile_copy(0, 0).start()

        @pl.when(nv > 1)
        def _():
            tile_copy(1, 1).start()

    @pl.when(j < nv)
    def _():
        @pl.when(j + 2 < nv)
        def _():
            tile_copy(j + 2, lax.rem(j + 2, 3)).start()

        slot = lax.rem(j, 3)
        tile_copy(j, slot).wait()

        @pl.when(fresh_ref[j] == 1)
        def _():
            wg_s[...] = wg_ref[0, 0].astype(BF16)
            wu_s[...] = wu_ref[0, 0].astype(BF16)
            wd_s[...] = wd_ref[0, 0].astype(BF16)

        x = _unpack_pair(xbuf[slot]).astype(BF16)
        a = jnp.dot(x, wg_s[...], preferred_element_type=F32)
        b = jnp.dot(x, wu_s[...], preferred_element_type=F32)
        act = (_silu(a) * b).astype(BF16)
        y_ref[...] = _pack_pair(jnp.dot(act, wd_s[...], preferred_element_type=F32))


def _expert_ffn(xs, n_tiles, tile_expert, fresh, n_valid, layer, w_gate, w_up, w_down):
    dh = xs.shape[1]
    last = lambda j, te, fr, nv: jnp.minimum(j, nv[0] - 1)
    wspec = lambda w: pl.BlockSpec((1, 1) + w.shape[2:],
                                   lambda j, te, fr, nv: (layer, te[j], 0, 0))
    grid_spec = pltpu.PrefetchScalarGridSpec(
        num_scalar_prefetch=3, grid=(n_tiles,),
        in_specs=[pl.BlockSpec(memory_space=pl.ANY), wspec(w_gate), wspec(w_up), wspec(w_down)],
        out_specs=pl.BlockSpec((TG, dh), lambda j, te, fr, nv: (last(j, te, fr, nv), 0)),
        scratch_shapes=[pltpu.VMEM((3, TG, dh), xs.dtype), pltpu.SemaphoreType.DMA((3,)),
                        pltpu.VMEM(w_gate.shape[2:], BF16), pltpu.VMEM(w_up.shape[2:], BF16),
                        pltpu.VMEM(w_down.shape[2:], BF16)],
    )
    return pl.pallas_call(
        _expert_ffn_kernel, grid_spec=grid_spec,
        out_shape=jax.ShapeDtypeStruct((n_tiles * TG, dh), I32),
        compiler_params=_cparams(("arbitrary",)), name="expert_ffn",
    )(tile_expert, fresh, n_valid, xs, w_gate, w_up, w_down)


def _combine_value(yg_ref, w_ref, h_ref, x_ref, mod_ref, sgu_ref, sd_ref):
    f = sd_ref.shape[0]
    h = _unpack_pair(h_ref[...]).astype(BF16)
    hs = jnp.dot(h, sgu_ref[...], preferred_element_type=F32)
    act = (_silu(hs[:, :f]) * hs[:, f:]).astype(BF16)
    acc = jnp.dot(act, sd_ref[...], preferred_element_type=F32)
    w = w_ref[...]
    for k in range(TOPK):
        acc = acc + w[:, k:k + 1] * _unpack_pair(yg_ref[k])
    return x_ref[...] + mod_ref[0, 5:6, :] * acc


def _combine_kernel(nbp, yg_ref, w_ref, h_ref, x_ref, mod_ref, sgu_ref, sd_ref, *o_refs):
    out = _combine_value(yg_ref, w_ref, h_ref, x_ref, mod_ref, sgu_ref, sd_ref)
    if len(o_refs) == 1:
        o_refs[0][...] = out
    else:
        i = pl.program_id(0)

        @pl.when(i < nbp)
        def _():
            o_refs[0][...] = out

        @pl.when(i >= nbp)
        def _():
            o_refs[1][...] = out


def _combine(yg, w8, h, x, mod, sgu, sd, nbp, bps, split_out):
    t, d = x.shape
    tc = 2 * TM
    assert (nbp * TM) % tc == 0 and (bps * TM) % tc == 0
    nbp, bps = nbp * TM // tc, bps * TM // tc
    midx = _mod_index(nbp, bps)
    if split_out:
        out_specs = [pl.BlockSpec((tc, d), lambda i: (jnp.minimum(i, nbp - 1), 0)),
                     pl.BlockSpec((tc, d), lambda i: (jnp.maximum(i - nbp, 0), 0))]
        out_shape = [jax.ShapeDtypeStruct((nbp * tc, d), F32),
                     jax.ShapeDtypeStruct((t - nbp * tc, d), F32)]
    else:
        out_specs = pl.BlockSpec((tc, d), lambda i: (i, 0))
        out_shape = jax.ShapeDtypeStruct((t, d), F32)
    in_specs = [
        pl.BlockSpec((TOPK, tc, d // 2), lambda i: (0, i, 0)),
        pl.BlockSpec((tc, 128), lambda i: (i, 0)),
        pl.BlockSpec((tc, d // 2), lambda i: (i, 0)),
        pl.BlockSpec((tc, d), lambda i: (i, 0)),
        pl.BlockSpec((1, 6, d), lambda i: (midx(i), 0, 0)),
        _const_spec(sgu.shape), _const_spec(sd.shape),
    ]
    return pl.pallas_call(
        functools.partial(_combine_kernel, nbp), grid=(t // tc,), in_specs=in_specs,
        out_specs=out_specs, out_shape=out_shape,
        compiler_params=_cparams(("arbitrary",)), name="moe_combine",
    )(yg, w8, h, x, mod, sgu, sd)


def _combine_qkv_kernel(nbp, yg_ref, w_ref, h_ref, x_ref, mod_ref, sgu_ref, sd_ref,
                        mod1_ref, gmix_ref, wqkv_ref, qg_ref, kg_ref, hm_ref, hmt_ref,
                        x_out, q_out, k_out, v_out, k32_out, v32_out):
    out = _combine_value(yg_ref, w_ref, h_ref, x_ref, mod_ref, sgu_ref, sd_ref)
    x_out[...] = out
    q, k, v = _qkv_values(out, mod1_ref, gmix_ref, wqkv_ref, qg_ref, kg_ref, hm_ref, hmt_ref)
    q_out[...] = q.astype(BF16)
    k_out[...] = k.astype(BF16)
    v_out[...] = v.astype(BF16)

    @pl.when(pl.program_id(0) < nbp)
    def _():
        k32_out[...] = k
        v32_out[...] = v


def _combine_qkv(yg, w8, h, x, mod, sgu, sd, mod1, g_mix, w_qkv, qg, kg, hm, hmt, nbp, bps):
    t, d = x.shape
    midx = _mod_index(nbp, bps)
    row = lambda i: (i, 0)
    in_specs = [
        pl.BlockSpec((TOPK, TM, d // 2), lambda i: (0, i, 0)),
        pl.BlockSpec((TM, 128), row),
        pl.BlockSpec((TM, d // 2), row),
        pl.BlockSpec((TM, d), row),
        pl.BlockSpec((1, 6, d), lambda i: (midx(i), 0, 0)),
        _const_spec(sgu.shape), _const_spec(sd.shape),
        pl.BlockSpec((1, 6, d), lambda i: (midx(i), 0, 0)),
        _const_spec(g_mix.shape), _const_spec(w_qkv.shape), _const_spec(qg.shape),
        _const_spec(kg.shape), _const_spec(hm.shape), _const_spec(hmt.shape),
    ]
    pinned = pl.BlockSpec((TM, d), lambda i: (jnp.minimum(i, nbp - 1), 0))
    out_specs = [pl.BlockSpec((TM, d), row)] * 4 + [pinned, pinned]
    out_shape = ([jax.ShapeDtypeStruct((t, d), F32)] + [jax.ShapeDtypeStruct((t, d), BF16)] * 3
                 + [jax.ShapeDtypeStruct((nbp * TM, d), F32)] * 2)
    return pl.pallas_call(
        functools.partial(_combine_qkv_kernel, nbp), grid=(t // TM,), in_specs=in_specs,
        out_specs=out_specs, out_shape=out_shape,
        compiler_params=_cparams(("arbitrary",)), name="moe_combine_qkv",
    )(yg, w8, h, x, mod, sgu, sd, mod1, g_mix, w_qkv, qg, kg, hm, hmt)


def _sparse_moe(x, h, gates_t, sel_t, cnt, layer, w_gate, w_up, w_down, finish):
    t, d = x.shape
    ne = N_EXPERTS
    n_pad = ne * TG
    n_tiles = (t * TOPK + n_pad) // TG
    n_slots = n_tiles * TG
    counts = cnt[:, 0, :].astype(I32)
    per_expert = jnp.sum(counts, axis=0)
    padded = (per_expert + TG - 1) // TG * TG
    ends = jnp.cumsum(padded)
    starts = ends - padded
    block_off = jnp.cumsum(counts, axis=0) - counts
    base = (starts[None, :] + block_off).astype(F32)[:, :, None]
    n_valid = (ends[-1] // TG).astype(I32).reshape(1)
    tile_start = jnp.minimum(jnp.arange(n_tiles, dtype=I32), n_valid[0] - 1) * TG
    tile_expert = jnp.sum((ends[None, :] <= tile_start[:, None]).astype(I32), axis=1)
    fresh = jnp.concatenate([jnp.ones((1,), I32),
                             (tile_expert[1:] != tile_expert[:-1]).astype(I32)])

    pos8, w8 = _slots(sel_t, gates_t, base)
    n_extra = -(-n_pad // t)
    cand = (starts + per_expert)[:, None] + jnp.arange(TG, dtype=I32)[None, :]
    spare = n_slots + jnp.arange(n_pad, dtype=I32)
    fill = jnp.where(cand < ends[:, None], cand, spare.reshape(ne, TG)).reshape(-1)
    rest = n_slots + jnp.arange(n_pad, n_extra * t, dtype=I32) % n_pad
    dest = jnp.concatenate([pos8, jnp.concatenate([fill, rest]).reshape(n_extra, t)], axis=0)
    xs = _sc_scatter(h, dest, n_slots + n_pad)
    ys = _expert_ffn(xs, n_tiles, tile_expert, fresh, n_valid, layer, w_gate, w_up, w_down)
    yg = _sc_gather(ys, pos8.reshape(-1)).reshape(TOPK, t, d // 2)
    return finish(yg, w8)


def _head_rms(x, g_row, hm_ref, hmt_ref):
    ss = jnp.dot((x * x).astype(BF16), hm_ref[...], preferred_element_type=F32)
    r = lax.rsqrt(ss * (1.0 / HEAD_DIM) + EPS)
    r_hi, r_lo = _split_bf16(r)
    hmt = hmt_ref[...]
    rb = (jnp.dot(r_hi, hmt, preferred_element_type=F32)
          + jnp.dot(r_lo, hmt, preferred_element_type=F32))
    return (x * rb) * g_row


def _qkv_values(x, mod_ref, gmix_ref, w_ref, qg_ref, kg_ref, hm_ref, hmt_ref):
    d = x.shape[1]
    sh1 = mod_ref[0, 0:1, :]
    sc1 = mod_ref[0, 1:2, :]
    h = _adaln(x, gmix_ref[...], sh1, sc1).astype(BF16)
    qkv = jnp.dot(h, w_ref[...], preferred_element_type=F32)
    q = _head_rms(qkv[:, :d], qg_ref[...], hm_ref, hmt_ref)
    k = _head_rms(qkv[:, d:2 * d], kg_ref[...], hm_ref, hmt_ref)
    return q, k, qkv[:, 2 * d:]


def _qkv_kernel(emit_f32, x_ref, mod_ref, gmix_ref, w_ref, qg_ref, kg_ref, hm_ref, hmt_ref, *outs):
    q, k, v = _qkv_values(x_ref[...], mod_ref, gmix_ref, w_ref, qg_ref, kg_ref, hm_ref, hmt_ref)
    outs[0][...] = q.astype(BF16)
    outs[1][...] = k.astype(BF16)
    outs[2][...] = v.astype(BF16)
    if emit_f32:
        outs[3][...] = k
        outs[4][...] = v


def _qkv(x, mod, g_mix, w_qkv, qg, kg, hm, hmt, blk0, nblk, midx, emit_f32):
    t, d = x.shape
    in_specs = [
        pl.BlockSpec((TM, d), lambda i: (i + blk0, 0)),
        pl.BlockSpec((1, 6, d), lambda i: (midx(i + blk0), 0, 0)),
        _const_spec(g_mix.shape), _const_spec(w_qkv.shape), _const_spec(qg.shape),
        _const_spec(kg.shape), _const_spec(hm.shape), _const_spec(hmt.shape),
    ]
    n_out = 5 if emit_f32 else 3
    out_specs = [pl.BlockSpec((TM, d), lambda i: (i, 0)) for _ in range(n_out)]
    out_shape = [jax.ShapeDtypeStruct((nblk * TM, d), BF16 if o < 3 else F32) for o in range(n_out)]
    return pl.pallas_call(
        functools.partial(_qkv_kernel, emit_f32),
        grid=(nblk,), in_specs=in_specs, out_specs=out_specs, out_shape=out_shape,
        compiler_params=_cparams(("parallel",)), name="qkv_f32" if emit_f32 else "qkv",
    )(x, mod, g_mix, w_qkv, qg, kg, hm, hmt)


def _head_masks():
    lane = lax.broadcasted_iota(I32, (1, 2 * HEAD_DIM), 1)
    return lane < HEAD_DIM


def _ctx_attn_kernel(q_ref, k_ref, v_ref, o_ref):
    lo = _head_masks()
    pw = 2 * HEAD_DIM
    for hp in range(q_ref.shape[1] // pw):
        cols = slice(hp * pw, (hp + 1) * pw)
        q = q_ref[:, cols]
        k = k_ref[:, cols]
        v = v_ref[:, cols]
        outs = []
        for hh in range(2):
            msk = lo if hh == 0 else jnp.logical_not(lo)
            qm = jnp.where(msk, q, jnp.zeros_like(q)) * jnp.asarray(HEAD_DIM ** -0.5, BF16)
            s = lax.dot_general(qm, k, NT_DIMS, preferred_element_type=F32)
            m = jnp.max(s, axis=-1, keepdims=True)
            p = jnp.exp(s - m)
            l = jnp.sum(p, axis=-1, keepdims=True)
            o = jnp.dot(p.astype(BF16), v, preferred_element_type=F32)
            outs.append(o / l)
        o_ref[:, cols] = jnp.where(lo, outs[0], outs[1]).astype(BF16)


def _ctx_attn(q, k, v, nb, s):
    t, d = q.shape
    spec = pl.BlockSpec((s, d), lambda b: (b, 0))
    return pl.pallas_call(
        _ctx_attn_kernel, grid=(nb,), in_specs=[spec, spec, spec], out_specs=spec,
        out_shape=jax.ShapeDtypeStruct((nb * s, d), BF16),
        compiler_params=_cparams(("parallel",)), name="context_attention",
    )(q, k, v)


NA_QROWS = 8
NA_GROWS = 4
NA_KROWS = 12
NA_HPAIRS = 2


def _na_kernel(rows, q_ref, k_ref, v_ref, kc_ref, vc_ref, bias_ref, o_ref):
    rb = pl.program_id(2)
    ngrp = NA_QROWS // NA_GROWS
    nq = NA_GROWS * GRID_W
    nk = NA_KROWS * GRID_W
    lo = _head_masks()
    pw = 2 * HEAD_DIM
    for gi in range(ngrp):
        g = rb * ngrp + gi
        kr0 = jnp.clip(g * NA_GROWS - WIN_H // 2, 0, rows - NA_KROWS)
        start = pl.multiple_of(kr0 * GRID_W, 256)
        cls = jnp.where(g == 0, 0, jnp.where(g == rows // NA_GROWS - 1, 2, 1))
        qrows = slice(gi * nq, (gi + 1) * nq)
        for pp in range(NA_HPAIRS):
            cols = slice(pp * pw, (pp + 1) * pw)
            q = q_ref[0, qrows, cols]
            kw = k_ref[0, pl.ds(start, nk), cols]
            vw = v_ref[0, pl.ds(start, nk), cols]
            kc = kc_ref[0, :, cols]
            vc = vc_ref[0, :, cols]
            outs = []
            for hh in range(2):
                msk = lo if hh == 0 else jnp.logical_not(lo)
                qm = jnp.where(msk, q, jnp.zeros_like(q)) * jnp.asarray(HEAD_DIM ** -0.5, BF16)
                s = (lax.dot_general(qm, kw, NT_DIMS, preferred_element_type=F32)
                     + bias_ref[cls, 2 * pp + hh])
                sc = lax.dot_general(qm, kc, NT_DIMS, preferred_element_type=F32)
                m = jnp.maximum(jnp.max(s, axis=-1, keepdims=True),
                                jnp.max(sc, axis=-1, keepdims=True))
                p = jnp.exp(s - m)
                pc = jnp.exp(sc - m)
                l = jnp.sum(p, axis=-1, keepdims=True) + jnp.sum(pc, axis=-1, keepdims=True)
                o = (jnp.dot(p.astype(BF16), vw, preferred_element_type=F32)
                     + jnp.dot(pc.astype(BF16), vc, preferred_element_type=F32))
                outs.append(o / l)
            o_ref[0, qrows, cols] = jnp.where(lo, outs[0], outs[1]).astype(BF16)


def _na_row_classes(rows):
    out = []
    for g in (0, 1, rows // NA_GROWS - 1):
        r0 = g * NA_GROWS
        kr0 = int(np.clip(r0 - WIN_H // 2, 0, rows - NA_KROWS))
        table = []
        for rl in range(NA_GROWS):
            r = r0 + rl
            sr = int(np.clip(r - WIN_H // 2, 0, rows - WIN_H))
            table.append([(kr0 + kl - r + WIN_H - 1) if sr <= kr0 + kl < sr + WIN_H else None
                          for kl in range(NA_KROWS)])
        out.append(table)
    return out


def _na_bias_kernel(rows, rpb_ref, o_ref):
    h = pl.program_id(0)
    n_dr, n_dc = 2 * WIN_H - 1, 2 * WIN_W - 1
    cq = lax.broadcasted_iota(I32, (GRID_W, 2 * GRID_W), 0)
    lane = lax.broadcasted_iota(I32, (GRID_W, 2 * GRID_W), 1)
    ck = lane & (GRID_W - 1)
    q_start = jnp.clip(cq - WIN_W // 2, 0, GRID_W - WIN_W)
    col_ok = (ck >= q_start) & (ck < q_start + WIN_W)
    dc = ck - cq + (WIN_W - 1)
    neg = jnp.full((GRID_W, 2 * GRID_W), NEG, F32)
    tiles = []
    for i in range(n_dr):
        t = jnp.zeros((GRID_W, 2 * GRID_W), F32)
        for jj in range(n_dc):
            t = jnp.where(dc == jj, rpb_ref[h * (n_dr * n_dc) + i * n_dc + jj], t)
        tiles.append(jnp.where(col_ok, t, neg))
    first_half = lane < GRID_W
    for c, table in enumerate(_na_row_classes(rows)):
        for rl in range(NA_GROWS):
            for m in range(NA_KROWS // 2):
                ia, ib = table[rl][2 * m], table[rl][2 * m + 1]
                ta = neg if ia is None else tiles[ia]
                tb = neg if ib is None else tiles[ib]
                blk = ta if ia == ib else jnp.where(first_half, ta, tb)
                o_ref[c, 0, rl * GRID_W:(rl + 1) * GRID_W, m * 2 * GRID_W:(m + 1) * 2 * GRID_W] = blk


def _na_bias(rpb, rows):
    nh = rpb.shape[0]
    nq, nk = NA_GROWS * GRID_W, NA_KROWS * GRID_W
    return pl.pallas_call(
        functools.partial(_na_bias_kernel, rows), grid=(nh,),
        in_specs=[pl.BlockSpec(memory_space=pltpu.SMEM)],
        out_specs=pl.BlockSpec((3, 1, nq, nk), lambda h: (0, h, 0, 0)),
        out_shape=jax.ShapeDtypeStruct((3, nh, nq, nk), F32),
        compiler_params=_cparams(("parallel",)), name="na_bias",
    )(rpb.reshape(-1))


def _na_attn(q, k, v, kc, vc, bias_tab, rows, b_off=0):
    s, d = q.shape[1:]
    nb = kc.shape[0]
    lw = NA_HPAIRS * 2 * HEAD_DIM
    hp = d // lw
    nrb = rows // NA_QROWS
    nq = NA_QROWS * GRID_W
    lc = kc.shape[1]
    in_specs = [
        pl.BlockSpec((1, nq, lw), lambda h, b, r: (b + b_off, r, h)),
        pl.BlockSpec((1, s, lw), lambda h, b, r: (b + b_off, 0, h)),
        pl.BlockSpec((1, s, lw), lambda h, b, r: (b + b_off, 0, h)),
        pl.BlockSpec((1, lc, lw), lambda h, b, r: (b, 0, h)),
        pl.BlockSpec((1, lc, lw), lambda h, b, r: (b, 0, h)),
        pl.BlockSpec((3, 2 * NA_HPAIRS) + bias_tab.shape[2:], lambda h, b, r: (0, h, 0, 0)),
    ]
    return pl.pallas_call(
        functools.partial(_na_kernel, rows),
        grid=(hp, nb, nrb), in_specs=in_specs,
        out_specs=pl.BlockSpec((1, nq, lw), lambda h, b, r: (b, r, h)),
        out_shape=jax.ShapeDtypeStruct((nb, s, d), BF16),
        compiler_params=_cparams(("parallel", "parallel", "parallel")),
        name="neighbourhood_attention",
    )(q, k, v, kc, vc, bias_tab)


def _oproj_kernel(nbp, ap_ref, as_ref, x_ref, mod_ref, wout_ref, gffn_ref, wr_hi_ref, wr_lo_ref,
                  rb_ref, x1_ref, h_ref, gates_ref, sel_ref, cnt_ref):
    a = jnp.where(pl.program_id(0) < nbp, ap_ref[...], as_ref[...])
    y = jnp.dot(a, wout_ref[...], preferred_element_type=F32)
    x1 = x_ref[...] + mod_ref[0, 2:3, :] * y
    x1_ref[...] = x1
    _ffn_pre(x1, mod_ref, gffn_ref, wr_hi_ref, wr_lo_ref, rb_ref,
             h_ref, gates_ref, sel_ref, cnt_ref)


def _oproj(attn_p, attn_s, x, mod, w_out, g_ffn, wr_hi, wr_lo, rbias, nbp, bps):
    t, d = x.shape
    midx = _mod_index(nbp, bps)
    in_specs = [
        pl.BlockSpec((TM, d), lambda i: (jnp.minimum(i, nbp - 1), 0)),
        pl.BlockSpec((TM, d), lambda i: (jnp.maximum(i - nbp, 0), 0)),
        pl.BlockSpec((TM, d), lambda i: (i, 0)),
        pl.BlockSpec((1, 6, d), lambda i: (midx(i), 0, 0)),
        _const_spec(w_out.shape), _const_spec(g_ffn.shape), _const_spec(wr_hi.shape),
        _const_spec(wr_lo.shape), _const_spec(rbias.shape),
    ]
    out_specs, out_shape = _pre_out_specs(t, d)
    return pl.pallas_call(
        functools.partial(_oproj_kernel, nbp), grid=(t // TM,), in_specs=in_specs,
        out_specs=out_specs, out_shape=out_shape, compiler_params=_cparams(("parallel",)),
        name="attn_out_proj",
    )(attn_p, attn_s, x, mod, w_out, g_ffn, wr_hi, wr_lo, rbias)


def _router_weights(w_router, router_bias):
    d, ne = w_router.shape
    w = jnp.pad(w_router, ((0, 0), (0, 128 - ne)))
    hi = w.astype(BF16)
    lo = (w - hi.astype(F32)).astype(BF16)
    return hi, lo, router_bias.reshape(ne, 1)


def _shared_weights(s_gate, s_up, s_down):
    return jnp.concatenate([s_gate, s_up], axis=-1).astype(BF16), s_down.astype(BF16)


def _layer_weights(layer, d, norm_mix, norm_ffn, pc, na, moe):
    i = layer // 2
    w = dict(g_mix=norm_mix[layer].reshape(1, d), g_ffn=norm_ffn[layer].reshape(1, d))
    w["wr_hi"], w["wr_lo"], w["rbias"] = _router_weights(moe["router"][layer],
                                                         moe["router_bias"][layer])
    w["sgu"], w["sd"] = _shared_weights(moe["s_gate"][layer], moe["s_up"][layer],
                                        moe["s_down"][layer])
    if layer % 2 == 0:
        w.update(w_in=pc["w_in"][i].astype(BF16), pool_w=pc["pool_w"][i].astype(BF16),
                 pool_scale=pc["pool_scale"][i].reshape(1, -1), conv_w=pc["conv_w"][i],
                 w_out=pc["w_out"][i].astype(BF16))
    else:
        nh = d // HEAD_DIM
        head_of = np.arange(d) // HEAD_DIM
        w.update(hm=jnp.asarray(head_of[:, None] == np.arange(128)[None, :], BF16),
                 hmt=jnp.asarray(np.arange(128)[:, None] == head_of[None, :], BF16),
                 qg=jnp.tile(na["q_norm"][i], nh).reshape(1, d),
                 kg=jnp.tile(na["k_norm"][i], nh).reshape(1, d),
                 w_qkv=na["w_qkv"][i].astype(BF16), w_out=na["w_out"][i].astype(BF16))
    return w


def kernel(x_prompt, x_sample, cache_k, cache_v, c, c_ctx, ada_w, ada_b, norm_mix, norm_ffn,
           pc_w_in, pc_pool_w, pc_pool_scale, pc_conv_w, pc_w_out,
           na_w_qkv, na_q_norm, na_k_norm, na_rpb, na_w_out,
           moe_router, moe_router_bias, moe_w_gate, moe_w_up, moe_w_down,
           moe_shared_gate, moe_shared_up, moe_shared_down):
    nb_p, s_p, d = x_prompt.shape
    nb_s, s_s, _ = x_sample.shape
    assert s_p == TM and s_s % TM == 0 and nb_p > 0 and nb_s > 0
    tp, ts = nb_p * s_p, nb_s * s_s
    nbp, bps = tp // TM, s_s // TM
    depth = ada_w.shape[0]
    nh = d // HEAD_DIM
    rows = s_s // GRID_W

    cond = jnp.concatenate([c_ctx[None], c], axis=0)
    cond = jnp.pad(cond, ((0, -cond.shape[0] % 8), (0, 0)))
    mods = _modulation(cond, ada_w, ada_b).reshape(depth, cond.shape[0], 6, d)
    pc = dict(w_in=pc_w_in, pool_w=pc_pool_w, pool_scale=pc_pool_scale, conv_w=pc_conv_w,
              w_out=pc_w_out)
    na = dict(w_qkv=na_w_qkv, q_norm=na_q_norm, k_norm=na_k_norm, w_out=na_w_out)
    moe = dict(router=moe_router, router_bias=moe_router_bias, s_gate=moe_shared_gate,
               s_up=moe_shared_up, s_down=moe_shared_down)

    xp, xs = x_prompt.reshape(tp, d), x_sample.reshape(ts, d)
    x = None
    ready_qkv = None
    new_k, new_v = [], []
    weights = [_layer_weights(l, d, norm_mix, norm_ffn, pc, na, moe) for l in range(depth)]
    for layer in range(depth):
        i = layer // 2
        w = weights[layer]
        mod = mods[layer]
        if layer % 2 == 0:
            if x is not None:
                xp, xs = x[:tp], x[tp:]
            x, h, gates_t, sel_t, cnt = _mixer(
                xp, xs, mod, w["g_mix"], w["w_in"], w["pool_w"], w["pool_scale"], w["conv_w"],
                w["w_out"], w["g_ffn"], w["wr_hi"], w["wr_lo"], w["rbias"], nbp, bps, s_p, s_s)
        else:
            if ready_qkv is not None and tp % s_s == 0:
                q_all, k_all, v_all, kp32, vp32 = ready_qkv
                qp, kp, vp = q_all, k_all, v_all
                seqs = (tp + ts) // s_s
                qs, ks, vs = (a.reshape(seqs, s_s, d) for a in (q_all, k_all, v_all))
                b_off = tp // s_s
            else:
                midx = _mod_index(nbp, bps)
                qkv_args = (x, mod, w["g_mix"], w["w_qkv"], w["qg"], w["kg"], w["hm"], w["hmt"])
                qp, kp, vp, kp32, vp32 = _qkv(*qkv_args, 0, nbp, midx, True)
                qs, ks, vs = (a.reshape(nb_s, s_s, d)
                              for a in _qkv(*qkv_args, nbp, ts // TM, midx, False))
                b_off = 0
            new_k.append(kp32.reshape(nb_p, s_p, nh, HEAD_DIM))
            new_v.append(vp32.reshape(nb_p, s_p, nh, HEAD_DIM))
            a_p = _ctx_attn(qp, kp, vp, nb_p, s_p)
            lc = cache_k.shape[2]
            kc = cache_k[:, i].reshape(nb_s, lc, d).astype(BF16)
            vc = cache_v[:, i].reshape(nb_s, lc, d).astype(BF16)
            a_s = _na_attn(qs, ks, vs, kc, vc, _na_bias(na_rpb[i], rows), rows, b_off)
            x, h, gates_t, sel_t, cnt = _oproj(a_p, a_s.reshape(ts, d), x, mod, w["w_out"],
                                               w["g_ffn"], w["wr_hi"], w["wr_lo"], w["rbias"],
                                               nbp, bps)
        ready_qkv = None
        if layer + 1 < depth and (layer + 1) % 2 == 1:
            w1 = weights[layer + 1]
            finish = lambda yg, w8: _combine_qkv(
                yg, w8, h, x, mod, w["sgu"], w["sd"], mods[layer + 1], w1["g_mix"], w1["w_qkv"],
                w1["qg"], w1["kg"], w1["hm"], w1["hmt"], nbp, bps)
            x, *ready_qkv = _sparse_moe(x, h, gates_t, sel_t, cnt, layer, moe_w_gate, moe_w_up,
                                        moe_w_down, finish)
        else:
            finish = lambda yg, w8: _combine(yg, w8, h, x, mod, w["sgu"], w["sd"], nbp, bps,
                                             layer == depth - 1)
            x = _sparse_moe(x, h, gates_t, sel_t, cnt, layer, moe_w_gate, moe_w_up, moe_w_down,
                            finish)
    y_p, y_s = x
    return (y_p.reshape(nb_p, s_p, d), y_s.reshape(nb_s, s_s, d),
            jnp.stack(new_k, axis=1), jnp.stack(new_v, axis=1))
```

```python
import functools

import numpy as np
import jax
import jax.numpy as jnp
from jax import lax
from jax.experimental import pallas as pl
from jax.experimental.pallas import tpu as pltpu
from jax.experimental.pallas import tpu_sc as plsc

F32 = jnp.float32
BF16 = jnp.bfloat16
I32 = jnp.int32
U32 = jnp.uint32

TM = 256
HALO = 8
POOL_WINDOWS = (2, 4, 8, 16)
N_EXPERTS = 64
N_GROUPS = 8
GROUP_SIZE = N_EXPERTS // N_GROUPS
TOPK_GROUPS = 4
TOPK = 8
ROUTED_SCALE = 2.5
EPS = 1e-6
GRID_W = 64
WIN_H = 8
WIN_W = 16
HEAD_DIM = 64
NEG = float(np.finfo(np.float32).min)
VMEM_LIMIT = 56 * 1024 * 1024
NT_DIMS = (((1,), (1,)), ((), ()))
TG = 1024
SC_WINDOW = 64
SC_WORKERS = 32


def _cparams(sem):
    return pltpu.CompilerParams(dimension_semantics=sem, vmem_limit_bytes=VMEM_LIMIT)


def _silu(x):
    return x * jax.nn.sigmoid(x)


def _split_bf16(x):
    hi = x.astype(BF16)
    lo = (x - hi.astype(F32)).astype(BF16)
    return hi, lo


def _adaln(x, g, shift, scale):
    ms = jnp.mean(x * x, axis=-1, keepdims=True)
    return (x * lax.rsqrt(ms + EPS)) * g * (1.0 + scale) + shift


def _pack_pair(x):
    w = x.shape[1] // 2
    lo = lax.bitcast_convert_type(x[:, :w].astype(BF16).astype(F32), U32) >> 16
    hi = lax.bitcast_convert_type(x[:, w:].astype(BF16).astype(F32), U32)
    return lax.bitcast_convert_type(lo | hi, I32)


def _unpack_pair(p):
    u = lax.bitcast_convert_type(p, U32)
    lo = lax.bitcast_convert_type(u << 16, F32)
    hi = lax.bitcast_convert_type(u & jnp.uint32(0xFFFF0000), F32)
    return jnp.concatenate([lo, hi], axis=-1)


def _mod_kernel(cond_ref, w_ref, b_ref, o_ref):
    c = cond_ref[...]
    a = _silu(c)
    o_ref[0] = jnp.dot(a, w_ref[0], preferred_element_type=F32,
                       precision=lax.Precision.HIGHEST) + b_ref[0]


def _modulation(cond, ada_w, ada_b):
    depth, d, n = ada_w.shape
    rows = cond.shape[0]
    tn = 1536
    return pl.pallas_call(
        _mod_kernel,
        grid=(depth, n // tn),
        in_specs=[
            pl.BlockSpec((rows, d), lambda l, j: (0, 0)),
            pl.BlockSpec((1, d, tn), lambda l, j: (l, 0, j)),
            pl.BlockSpec((1, 1, tn), lambda l, j: (l, 0, j)),
        ],
        out_specs=pl.BlockSpec((1, rows, tn), lambda l, j: (l, 0, j)),
        out_shape=jax.ShapeDtypeStruct((depth, rows, n), F32),
        compiler_params=_cparams(("arbitrary", "arbitrary")),
        name="modulation",
    )(cond, ada_w, ada_b.reshape(depth, 1, n))


def _route(logits_t, bias_col):
    tm = logits_t.shape[1]
    scores = jax.nn.sigmoid(logits_t)
    biased = scores + bias_col
    sub = lax.broadcasted_iota(I32, (GROUP_SIZE, tm), 0).astype(F32)
    ninf = jnp.float32(-jnp.inf)
    groups, gscore = [], []
    for g in range(N_GROUPS):
        v = biased[g * GROUP_SIZE:(g + 1) * GROUP_SIZE]
        m1 = jnp.max(v, axis=0, keepdims=True)
        first = jnp.min(jnp.where(v == m1, sub, float(GROUP_SIZE)), axis=0, keepdims=True)
        m2 = jnp.max(jnp.where(sub == first, ninf, v), axis=0, keepdims=True)
        groups.append(v)
        gscore.append(m1 + m2)
    masked = []
    for g in range(N_GROUPS):
        rank = jnp.zeros((1, tm), I32)
        for g2 in range(N_GROUPS):
            if g2 == g:
                continue
            ahead = gscore[g2] > gscore[g]
            if g2 < g:
                ahead = ahead | (gscore[g2] == gscore[g])
            rank = rank + ahead.astype(I32)
        masked.append(jnp.where(rank < TOPK_GROUPS, groups[g], ninf))
    masked = jnp.concatenate(masked, axis=0)
    eidx = lax.broadcasted_iota(I32, (N_EXPERTS, tm), 0).astype(F32)
    sel = jnp.zeros((N_EXPERTS, tm), jnp.bool_)
    for _ in range(TOPK):
        best = jnp.max(masked, axis=0, keepdims=True)
        first = jnp.min(jnp.where(masked == best, eidx, float(N_EXPERTS)), axis=0, keepdims=True)
        hit = eidx == first
        sel = sel | hit
        masked = jnp.where(hit, ninf, masked)
    w = jnp.where(sel, scores, 0.0)
    wsum = jnp.sum(w, axis=0, keepdims=True)
    return w / wsum * ROUTED_SCALE, sel


def _ffn_pre(x1, mod_ref, gffn_ref, wr_hi_ref, wr_lo_ref, rb_ref,
             h_ref, gates_ref, sel_ref, cnt_ref):
    sh2 = mod_ref[0, 3:4, :]
    sc2 = mod_ref[0, 4:5, :]
    h = _adaln(x1, gffn_ref[...], sh2, sc2)
    h_hi, h_lo = _split_bf16(h)
    h_ref[...] = _pack_pair(h_hi)
    wr_hi = wr_hi_ref[...]
    logits = (jnp.dot(h_hi, wr_hi, preferred_element_type=F32)
              + jnp.dot(h_hi, wr_lo_ref[...], preferred_element_type=F32)
              + jnp.dot(h_lo, wr_hi, preferred_element_type=F32))
    gates_t, sel = _route(logits.T[:N_EXPERTS], rb_ref[...])
    gates_ref[...] = gates_t
    sel_b = sel.astype(F32).astype(BF16)
    sel_ref[...] = sel_b
    ones = jnp.ones((8, sel_b.shape[1]), BF16)
    cnt_ref[0] = lax.dot_general(ones, sel_b, NT_DIMS, preferred_element_type=F32)


def _pre_out_specs(t, d):
    specs = [
        pl.BlockSpec((TM, d), lambda i: (i, 0)),
        pl.BlockSpec((TM, d // 2), lambda i: (i, 0)),
        pl.BlockSpec((N_EXPERTS, TM), lambda i: (0, i)),
        pl.BlockSpec((N_EXPERTS, TM), lambda i: (0, i)),
        pl.BlockSpec((1, 8, N_EXPERTS), lambda i: (i, 0, 0)),
    ]
    shapes = [
        jax.ShapeDtypeStruct((t, d), F32),
        jax.ShapeDtypeStruct((t, d // 2), I32),
        jax.ShapeDtypeStruct((N_EXPERTS, t), F32),
        jax.ShapeDtypeStruct((N_EXPERTS, t), BF16),
        jax.ShapeDtypeStruct((t // TM, 8, N_EXPERTS), F32),
    ]
    return specs, shapes


def _mixer_kernel(nbp, bps, sp, ss,
                  xpc_ref, xsc_ref, xprev_ref, xnext_ref, mod_ref, gmix_ref, win_ref, pw_ref, ps_ref,
                  cw_ref,
                  wout_ref, gffn_ref, wr_hi_ref, wr_lo_ref, rb_ref,
                  x1_ref, h_ref, gates_ref, sel_ref, cnt_ref):
    i = pl.program_id(0)
    is_p = i < nbp
    j = lax.rem(jnp.maximum(i - nbp, 0), bps)
    first = is_p | (j == 0)
    last = is_p | (j == bps - 1)
    base = jnp.where(is_p, 0, j * TM)
    slen = jnp.where(is_p, sp, ss)

    sh1 = mod_ref[0, 0:1, :]
    sc1 = mod_ref[0, 1:2, :]
    g1 = mod_ref[0, 2:3, :]
    xc = jnp.where(is_p, xpc_ref[...], xsc_ref[...])
    x_ext = jnp.concatenate([xprev_ref[...], xc, xnext_ref[...]], axis=0)
    h_ext = _adaln(x_ext, gmix_ref[...], sh1, sc1).astype(BF16)
    u = jnp.dot(h_ext, win_ref[...], preferred_element_type=F32)
    next_ = TM + 2 * HALO
    row = lax.broadcasted_iota(I32, (next_, 1), 0)
    keep = ((row >= HALO) | jnp.logical_not(first)) & ((row < HALO + TM) | jnp.logical_not(last))
    u = jnp.where(keep, u, 0.0)

    dm = u.shape[1] // 4
    ua = u[:, :dm]
    gate_b = u[HALO:HALO + TM, dm:2 * dm]
    z = u[:, 2 * dm:3 * dm] * u[:, 3 * dm:]

    def up(a, k):
        return pltpu.roll(a, next_ - k, 0)

    pos = base + lax.broadcasted_iota(I32, (TM, 1), 0)
    pg = dm // len(POOL_WINDOWS)
    ya = []
    for g, w in enumerate(POOL_WINDOWS):
        e = ua[:, g * pg:(g + 1) * pg]
        acc = e
        span = 1
        while span < w:
            acc = acc + up(acc, span)
            span *= 2
        off = HALO - w // 2
        wsum = (up(acc, off) if off else acc)[:TM]
        lo = jnp.maximum(pos - w // 2, 0)
        hi = jnp.minimum(pos + (w - w // 2 - 1), slen - 1)
        cnt = (hi - lo + 1).astype(F32)
        diff = wsum / cnt - e[HALO:HALO + TM]
        ya.append(jnp.dot(diff.astype(BF16), pw_ref[g], preferred_element_type=F32))
    y_a = jnp.concatenate(ya, axis=-1) * ps_ref[...]
    zc = (cw_ref[0:1, :] * up(z, HALO - 1)[:TM] + cw_ref[1:2, :] * z[HALO:HALO + TM]
          + cw_ref[2:3, :] * up(z, HALO + 1)[:TM])
    y_b = gate_b * zc
    ycat = jnp.concatenate([y_a, y_b], axis=-1).astype(BF16)
    y = jnp.dot(ycat, wout_ref[...], preferred_element_type=F32)
    x1 = xc + g1 * y
    x1_ref[...] = x1
    _ffn_pre(x1, mod_ref, gffn_ref, wr_hi_ref, wr_lo_ref, rb_ref,
             h_ref, gates_ref, sel_ref, cnt_ref)


def _mod_index(nbp, bps):
    def f(i):
        return jnp.where(i < nbp, 0, 1 + jnp.maximum(i - nbp, 0) // bps)
    return f


def _const_spec(shape):
    nd = len(shape)
    return pl.BlockSpec(shape, lambda i: (0,) * nd)


def _mixer(xp, xs, mod, g_mix, w_in, pool_w, pool_scale, conv_w, w_out, g_ffn, wr_hi, wr_lo, rbias,
           nbp, bps, sp, ss):
    d = xp.shape[1]
    t = xp.shape[0] + xs.shape[0]
    nblk = t // TM
    midx = _mod_index(nbp, bps)
    hpb = TM // HALO
    nh = xs.shape[0] // HALO
    in_specs = [
        pl.BlockSpec((TM, d), lambda i: (jnp.minimum(i, nbp - 1), 0)),
        pl.BlockSpec((TM, d), lambda i: (jnp.maximum(i - nbp, 0), 0)),
        pl.BlockSpec((HALO, d), lambda i: (jnp.maximum((i - nbp) * hpb - 1, 0), 0)),
        pl.BlockSpec((HALO, d), lambda i: (jnp.clip((i - nbp + 1) * hpb, 0, nh - 1), 0)),
        pl.BlockSpec((1, 6, d), lambda i: (midx(i), 0, 0)),
        _const_spec(g_mix.shape), _const_spec(w_in.shape), _const_spec(pool_w.shape),
        _const_spec(pool_scale.shape), _const_spec(conv_w.shape), _const_spec(w_out.shape),
        _const_spec(g_ffn.shape), _const_spec(wr_hi.shape), _const_spec(wr_lo.shape),
        _const_spec(rbias.shape),
    ]
    out_specs, out_shape = _pre_out_specs(t, d)
    return pl.pallas_call(
        functools.partial(_mixer_kernel, nbp, bps, sp, ss),
        grid=(nblk,), in_specs=in_specs, out_specs=out_specs, out_shape=out_shape,
        compiler_params=_cparams(("parallel",)), name="pool_conv_mixer",
    )(xp, xs, xs, xs, mod, g_mix, w_in, pool_w, pool_scale, conv_w, w_out, g_ffn, wr_hi, wr_lo,
      rbias)


SLOT_BLOCKS = 4


def _slots_kernel(sel_ref, gates_ref, base_ref, pos_ref, w_ref):
    r = lax.broadcasted_iota(I32, (TM, TM), 0)
    c = lax.broadcasted_iota(I32, (TM, TM), 1)
    before = (r < c).astype(F32).astype(BF16)
    er = lax.broadcasted_iota(I32, (N_EXPERTS, N_EXPERTS), 0)
    ec = lax.broadcasted_iota(I32, (N_EXPERTS, N_EXPERTS), 1)
    lower = (ec < er).astype(F32).astype(BF16)
    sub = lax.broadcasted_iota(I32, (TOPK, TM), 0)
    for blk in range(sel_ref.shape[1] // TM):
        cols = slice(blk * TM, (blk + 1) * TM)
        sel = sel_ref[:, cols]
        rank_tok = jnp.dot(sel, before, preferred_element_type=F32)
        rank_exp = jnp.dot(lower, sel, preferred_element_type=F32)
        slot = base_ref[blk] + rank_tok
        chosen = sel > 0
        gates = gates_ref[:, cols]
        pos8 = jnp.zeros((TOPK, TM), F32)
        w8 = jnp.zeros((TOPK, TM), F32)
        for k in range(TOPK):
            mk = chosen & (rank_exp == float(k))
            pk = jnp.sum(jnp.where(mk, slot, 0.0), axis=0, keepdims=True)
            wk = jnp.sum(jnp.where(mk, gates, 0.0), axis=0, keepdims=True)
            pos8 = jnp.where(sub == k, pk, pos8)
            w8 = jnp.where(sub == k, wk, w8)
        pos_ref[:, cols] = pos8.astype(I32)
        w8 = jnp.concatenate([w8, jnp.zeros((128 - TOPK, TM), F32)], axis=0)
        w_ref[cols, :] = w8.T


def _slots(sel_t, gates_t, base):
    ne, t = sel_t.shape
    nblk = next(n for n in range(SLOT_BLOCKS, 0, -1) if (t // TM) % n == 0)
    ts = nblk * TM
    return pl.pallas_call(
        _slots_kernel, grid=(t // ts,),
        in_specs=[
            pl.BlockSpec((ne, ts), lambda i: (0, i)),
            pl.BlockSpec((ne, ts), lambda i: (0, i)),
            pl.BlockSpec((nblk, ne, 1), lambda i: (i, 0, 0)),
        ],
        out_specs=[pl.BlockSpec((TOPK, ts), lambda i: (0, i)),
                   pl.BlockSpec((ts, 128), lambda i: (i, 0))],
        out_shape=[jax.ShapeDtypeStruct((TOPK, t), I32), jax.ShapeDtypeStruct((t, 128), F32)],
        compiler_params=_cparams(("parallel",)), name="moe_slots",
    )(sel_t, gates_t, base)


def _sc_gather(table, idx):
    m = idx.shape[0]
    d = table.shape[1]
    assert m % (SC_WINDOW * SC_WORKERS) == 0
    mesh = plsc.VectorSubcoreMesh(core_axis_name="core", subcore_axis_name="subcore")

    @pl.kernel(out_type=jax.ShapeDtypeStruct((m, d), table.dtype), mesh=mesh)
    def gather_rows(x_hbm, i_hbm, o_hbm):
        def body(i_vmem, o_vmem):
            pltpu.sync_copy(x_hbm.at[i_vmem.at[0]], o_vmem)

        pltpu.emit_pipeline(
            body, grid=(m // SC_WINDOW,),
            in_specs=[pl.BlockSpec((1, SC_WINDOW), index_map=lambda i: (i, 0))],
            out_specs=[pl.BlockSpec((SC_WINDOW, d), index_map=lambda i: (i, 0))],
            core_axis_name=("core", "subcore"),
            dimension_semantics=(pltpu.PARALLEL,),
        )(i_hbm, o_hbm)

    return gather_rows(table, idx.reshape(m // SC_WINDOW, SC_WINDOW))


def _sc_scatter(rows, idx, n_out):
    nk, t = idx.shape
    d = rows.shape[1]
    assert t % (SC_WINDOW * SC_WORKERS) == 0
    mesh = plsc.VectorSubcoreMesh(core_axis_name="core", subcore_axis_name="subcore")

    @pl.kernel(out_type=jax.ShapeDtypeStruct((n_out, d), rows.dtype), mesh=mesh)
    def scatter_rows(x_hbm, i_hbm, o_hbm):
        def body(x_vmem, i_vmem):
            for k in range(nk):
                pltpu.sync_copy(x_vmem, o_hbm.at[i_vmem.at[k, 0]])

        pltpu.emit_pipeline(
            body, grid=(t // SC_WINDOW,),
            in_specs=[pl.BlockSpec((SC_WINDOW, d), index_map=lambda i: (i, 0)),
                      pl.BlockSpec((nk, 1, SC_WINDOW), index_map=lambda i: (0, i, 0))],
            out_specs=[],
            core_axis_name=("core", "subcore"),
            dimension_semantics=(pltpu.PARALLEL,),
        )(x_hbm, i_hbm)

    return scatter_rows(rows, idx.reshape(nk, t // SC_WINDOW, SC_WINDOW))


def _expert_ffn_kernel(te_ref, fresh_ref, nv_ref, x_hbm, wg_ref, wu_ref, wd_ref, y_ref,
                       xbuf, xsem, wg_s, wu_s, wd_s):
    j = pl.program_id(0)
    nv = nv_ref[0]

    def tile_copy(tile, slot):
        rows = pl.ds(pl.multiple_of(tile * TG, TG), TG)
        return pltpu.make_async_copy(x_hbm.at[rows, :], xbuf.at[slot], xsem.at[slot])

    @pl.when(j == 0)
    def _():
        tile_copy(0, 0).start()

        @pl.when(nv > 1)
        def _():
            tile_copy(1, 1).start()

    @pl.when(j < nv)
    def _():
        @pl.when(j + 2 < nv)
        def _():
            tile_copy(j + 2, lax.rem(j + 2, 3)).start()

        slot = lax.rem(j, 3)
        tile_copy(j, slot).wait()

        @pl.when(fresh_ref[j] == 1)
        def _():
            wg_s[...] = wg_ref[0, 0].astype(BF16)
            wu_s[...] = wu_ref[0, 0].astype(BF16)
            wd_s[...] = wd_ref[0, 0].astype(BF16)

        x = _unpack_pair(xbuf[slot]).astype(BF16)
        a = jnp.dot(x, wg_s[...], preferred_element_type=F32)
        b = jnp.dot(x, wu_s[...], preferred_element_type=F32)
        act = (_silu(a) * b).astype(BF16)
        y_ref[...] = _pack_pair(jnp.dot(act, wd_s[...], preferred_element_type=F32))


def _expert_ffn(xs, n_tiles, tile_expert, fresh, n_valid, layer, w_gate, w_up, w_down):
    dh = xs.shape[1]
    last = lambda j, te, fr, nv: jnp.minimum(j, nv[0] - 1)
    wspec = lambda w: pl.BlockSpec((1, 1) + w.shape[2:],
                                   lambda j, te, fr, nv: (layer, te[j], 0, 0))
    grid_spec = pltpu.PrefetchScalarGridSpec(
        num_scalar_prefetch=3, grid=(n_tiles,),
        in_specs=[pl.BlockSpec(memory_space=pl.ANY), wspec(w_gate), wspec(w_up), wspec(w_down)],
        out_specs=pl.BlockSpec((TG, dh), lambda j, te, fr, nv: (last(j, te, fr, nv), 0)),
        scratch_shapes=[pltpu.VMEM((3, TG, dh), xs.dtype), pltpu.SemaphoreType.DMA((3,)),
                        pltpu.VMEM(w_gate.shape[2:], BF16), pltpu.VMEM(w_up.shape[2:], BF16),
                        pltpu.VMEM(w_down.shape[2:], BF16)],
    )
    return pl.pallas_call(
        _expert_ffn_kernel, grid_spec=grid_spec,
        out_shape=jax.ShapeDtypeStruct((n_tiles * TG, dh), I32),
        compiler_params=_cparams(("arbitrary",)), name="expert_ffn",
    )(tile_expert, fresh, n_valid, xs, w_gate, w_up, w_down)


def _combine_value(yg_ref, w_ref, h_ref, x_ref, mod_ref, sgu_ref, sd_ref):
    f = sd_ref.shape[0]
    h = _unpack_pair(h_ref[...]).astype(BF16)
    hs = jnp.dot(h, sgu_ref[...], preferred_element_type=F32)
    act = (_silu(hs[:, :f]) * hs[:, f:]).astype(BF16)
    acc = jnp.dot(act, sd_ref[...], preferred_element_type=F32)
    w = w_ref[...]
    for k in range(TOPK):
        acc = acc + w[:, k:k + 1] * _unpack_pair(yg_ref[k])
    return x_ref[...] + mod_ref[0, 5:6, :] * acc


def _combine_kernel(nbp, yg_ref, w_ref, h_ref, x_ref, mod_ref, sgu_ref, sd_ref, *o_refs):
    out = _combine_value(yg_ref, w_ref, h_ref, x_ref, mod_ref, sgu_ref, sd_ref)
    if len(o_refs) == 1:
        o_refs[0][...] = out
    else:
        i = pl.program_id(0)

        @pl.when(i < nbp)
        def _():
            o_refs[0][...] = out

        @pl.when(i >= nbp)
        def _():
            o_refs[1][...] = out


def _combine(yg, w8, h, x, mod, sgu, sd, nbp, bps, split_out):
    t, d = x.shape
    tc = 2 * TM
    assert (nbp * TM) % tc == 0 and (bps * TM) % tc == 0
    nbp, bps = nbp * TM // tc, bps * TM // tc
    midx = _mod_index(nbp, bps)
    if split_out:
        out_specs = [pl.BlockSpec((tc, d), lambda i: (jnp.minimum(i, nbp - 1), 0)),
                     pl.BlockSpec((tc, d), lambda i: (jnp.maximum(i - nbp, 0), 0))]
        out_shape = [jax.ShapeDtypeStruct((nbp * tc, d), F32),
                     jax.ShapeDtypeStruct((t - nbp * tc, d), F32)]
    else:
        out_specs = pl.BlockSpec((tc, d), lambda i: (i, 0))
        out_shape = jax.ShapeDtypeStruct((t, d), F32)
    in_specs = [
        pl.BlockSpec((TOPK, tc, d // 2), lambda i: (0, i, 0)),
        pl.BlockSpec((tc, 128), lambda i: (i, 0)),
        pl.BlockSpec((tc, d // 2), lambda i: (i, 0)),
        pl.BlockSpec((tc, d), lambda i: (i, 0)),
        pl.BlockSpec((1, 6, d), lambda i: (midx(i), 0, 0)),
        _const_spec(sgu.shape), _const_spec(sd.shape),
    ]
    return pl.pallas_call(
        functools.partial(_combine_kernel, nbp), grid=(t // tc,), in_specs=in_specs,
        out_specs=out_specs, out_shape=out_shape,
        compiler_params=_cparams(("arbitrary",)), name="moe_combine",
    )(yg, w8, h, x, mod, sgu, sd)


def _combine_qkv_kernel(nbp, yg_ref, w_ref, h_ref, x_ref, mod_ref, sgu_ref, sd_ref,
                        mod1_ref, gmix_ref, wqkv_ref, qg_ref, kg_ref, hm_ref, hmt_ref,
                        x_out, q_out, k_out, v_out, k32_out, v32_out):
    out = _combine_value(yg_ref, w_ref, h_ref, x_ref, mod_ref, sgu_ref, sd_ref)
    x_out[...] = out
    q, k, v = _qkv_values(out, mod1_ref, gmix_ref, wqkv_ref, qg_ref, kg_ref, hm_ref, hmt_ref)
    q_out[...] = q.astype(BF16)
    k_out[...] = k.astype(BF16)
    v_out[...] = v.astype(BF16)

    @pl.when(pl.program_id(0) < nbp)
    def _():
        k32_out[...] = k
        v32_out[...] = v


def _combine_qkv(yg, w8, h, x, mod, sgu, sd, mod1, g_mix, w_qkv, qg, kg, hm, hmt, nbp, bps):
    t, d = x.shape
    midx = _mod_index(nbp, bps)
    row = lambda i: (i, 0)
    in_specs = [
        pl.BlockSpec((TOPK, TM, d // 2), lambda i: (0, i, 0)),
        pl.BlockSpec((TM, 128), row),
        pl.BlockSpec((TM, d // 2), row),
        pl.BlockSpec((TM, d), row),
        pl.BlockSpec((1, 6, d), lambda i: (midx(i), 0, 0)),
        _const_spec(sgu.shape), _const_spec(sd.shape),
        pl.BlockSpec((1, 6, d), lambda i: (midx(i), 0, 0)),
        _const_spec(g_mix.shape), _const_spec(w_qkv.shape), _const_spec(qg.shape),
        _const_spec(kg.shape), _const_spec(hm.shape), _const_spec(hmt.shape),
    ]
    pinned = pl.BlockSpec((TM, d), lambda i: (jnp.minimum(i, nbp - 1), 0))
    out_specs = [pl.BlockSpec((TM, d), row)] * 4 + [pinned, pinned]
    out_shape = ([jax.ShapeDtypeStruct((t, d), F32)] + [jax.ShapeDtypeStruct((t, d), BF16)] * 3
                 + [jax.ShapeDtypeStruct((nbp * TM, d), F32)] * 2)
    return pl.pallas_call(
        functools.partial(_combine_qkv_kernel, nbp), grid=(t // TM,), in_specs=in_specs,
        out_specs=out_specs, out_shape=out_shape,
        compiler_params=_cparams(("arbitrary",)), name="moe_combine_qkv",
    )(yg, w8, h, x, mod, sgu, sd, mod1, g_mix, w_qkv, qg, kg, hm, hmt)


def _sparse_moe(x, h, gates_t, sel_t, cnt, layer, w_gate, w_up, w_down, finish):
    t, d = x.shape
    ne = N_EXPERTS
    n_pad = ne * TG
    n_tiles = (t * TOPK + n_pad) // TG
    n_slots = n_tiles * TG
    counts = cnt[:, 0, :].astype(I32)
    per_expert = jnp.sum(counts, axis=0)
    padded = (per_expert + TG - 1) // TG * TG
    ends = jnp.cumsum(padded)
    starts = ends - padded
    block_off = jnp.cumsum(counts, axis=0) - counts
    base = (starts[None, :] + block_off).astype(F32)[:, :, None]
    n_valid = (ends[-1] // TG).astype(I32).reshape(1)
    tile_start = jnp.minimum(jnp.arange(n_tiles, dtype=I32), n_valid[0] - 1) * TG
    tile_expert = jnp.sum((ends[None, :] <= tile_start[:, None]).astype(I32), axis=1)
    fresh = jnp.concatenate([jnp.ones((1,), I32),
                             (tile_expert[1:] != tile_expert[:-1]).astype(I32)])

    pos8, w8 = _slots(sel_t, gates_t, base)
    n_extra = -(-n_pad // t)
    cand = (starts + per_expert)[:, None] + jnp.arange(TG, dtype=I32)[None, :]
    spare = n_slots + jnp.arange(n_pad, dtype=I32)
    fill = jnp.where(cand < ends[:, None], cand, spare.reshape(ne, TG)).reshape(-1)
    rest = n_slots + jnp.arange(n_pad, n_extra * t, dtype=I32) % n_pad
    dest = jnp.concatenate([pos8, jnp.concatenate([fill, rest]).reshape(n_extra, t)], axis=0)
    xs = _sc_scatter(h, dest, n_slots + n_pad)
    ys = _expert_ffn(xs, n_tiles, tile_expert, fresh, n_valid, layer, w_gate, w_up, w_down)
    yg = _sc_gather(ys, pos8.reshape(-1)).reshape(TOPK, t, d // 2)
    return finish(yg, w8)


def _head_rms(x, g_row, hm_ref, hmt_ref):
    ss = jnp.dot((x * x).astype(BF16), hm_ref[...], preferred_element_type=F32)
    r = lax.rsqrt(ss * (1.0 / HEAD_DIM) + EPS)
    r_hi, r_lo = _split_bf16(r)
    hmt = hmt_ref[...]
    rb = (jnp.dot(r_hi, hmt, preferred_element_type=F32)
          + jnp.dot(r_lo, hmt, preferred_element_type=F32))
    return (x * rb) * g_row


def _qkv_values(x, mod_ref, gmix_ref, w_ref, qg_ref, kg_ref, hm_ref, hmt_ref):
    d = x.shape[1]
    sh1 = mod_ref[0, 0:1, :]
    sc1 = mod_ref[0, 1:2, :]
    h = _adaln(x, gmix_ref[...], sh1, sc1).astype(BF16)
    qkv = jnp.dot(h, w_ref[...], preferred_element_type=F32)
    q = _head_rms(qkv[:, :d], qg_ref[...], hm_ref, hmt_ref)
    k = _head_rms(qkv[:, d:2 * d], kg_ref[...], hm_ref, hmt_ref)
    return q, k, qkv[:, 2 * d:]


def _qkv_kernel(emit_f32, x_ref, mod_ref, gmix_ref, w_ref, qg_ref, kg_ref, hm_ref, hmt_ref, *outs):
    q, k, v = _qkv_values(x_ref[...], mod_ref, gmix_ref, w_ref, qg_ref, kg_ref, hm_ref, hmt_ref)
    outs[0][...] = q.astype(BF16)
    outs[1][...] = k.astype(BF16)
    outs[2][...] = v.astype(BF16)
    if emit_f32:
        outs[3][...] = k
        outs[4][...] = v


def _qkv(x, mod, g_mix, w_qkv, qg, kg, hm, hmt, blk0, nblk, midx, emit_f32):
    t, d = x.shape
    in_specs = [
        pl.BlockSpec((TM, d), lambda i: (i + blk0, 0)),
        pl.BlockSpec((1, 6, d), lambda i: (midx(i + blk0), 0, 0)),
        _const_spec(g_mix.shape), _const_spec(w_qkv.shape), _const_spec(qg.shape),
        _const_spec(kg.shape), _const_spec(hm.shape), _const_spec(hmt.shape),
    ]
    n_out = 5 if emit_f32 else 3
    out_specs = [pl.BlockSpec((TM, d), lambda i: (i, 0)) for _ in range(n_out)]
    out_shape = [jax.ShapeDtypeStruct((nblk * TM, d), BF16 if o < 3 else F32) for o in range(n_out)]
    return pl.pallas_call(
        functools.partial(_qkv_kernel, emit_f32),
        grid=(nblk,), in_specs=in_specs, out_specs=out_specs, out_shape=out_shape,
        compiler_params=_cparams(("parallel",)), name="qkv_f32" if emit_f32 else "qkv",
    )(x, mod, g_mix, w_qkv, qg, kg, hm, hmt)


def _head_masks():
    lane = lax.broadcasted_iota(I32, (1, 2 * HEAD_DIM), 1)
    return lane < HEAD_DIM


def _ctx_attn_kernel(q_ref, k_ref, v_ref, o_ref):
    lo = _head_masks()
    pw = 2 * HEAD_DIM
    for hp in range(q_ref.shape[1] // pw):
        cols = slice(hp * pw, (hp + 1) * pw)
        q = q_ref[:, cols]
        k = k_ref[:, cols]
        v = v_ref[:, cols]
        outs = []
        for hh in range(2):
            msk = lo if hh == 0 else jnp.logical_not(lo)
            qm = jnp.where(msk, q, jnp.zeros_like(q)) * jnp.asarray(HEAD_DIM ** -0.5, BF16)
            s = lax.dot_general(qm, k, NT_DIMS, preferred_element_type=F32)
            m = jnp.max(s, axis=-1, keepdims=True)
            p = jnp.exp(s - m)
            l = jnp.sum(p, axis=-1, keepdims=True)
            o = jnp.dot(p.astype(BF16), v, preferred_element_type=F32)
            outs.append(o / l)
        o_ref[:, cols] = jnp.where(lo, outs[0], outs[1]).astype(BF16)


def _ctx_attn(q, k, v, nb, s):
    t, d = q.shape
    spec = pl.BlockSpec((s, d), lambda b: (b, 0))
    return pl.pallas_call(
        _ctx_attn_kernel, grid=(nb,), in_specs=[spec, spec, spec], out_specs=spec,
        out_shape=jax.ShapeDtypeStruct((nb * s, d), BF16),
        compiler_params=_cparams(("parallel",)), name="context_attention",
    )(q, k, v)


NA_QROWS = 16
NA_GROWS = 4
NA_KROWS = 12
NA_HPAIRS = 2


def _na_kernel(rows, q_ref, k_ref, v_ref, kc_ref, vc_ref, bias_ref, o_ref):
    rb = pl.program_id(2)
    ngrp = NA_QROWS // NA_GROWS
    nq = NA_GROWS * GRID_W
    nk = NA_KROWS * GRID_W
    lo = _head_masks()
    pw = 2 * HEAD_DIM
    for gi in range(ngrp):
        g = rb * ngrp + gi
        kr0 = jnp.clip(g * NA_GROWS - WIN_H // 2, 0, rows - NA_KROWS)
        start = pl.multiple_of(kr0 * GRID_W, 256)
        cls = jnp.where(g == 0, 0, jnp.where(g == rows // NA_GROWS - 1, 2, 1))
        qrows = slice(gi * nq, (gi + 1) * nq)
        for pp in range(NA_HPAIRS):
            cols = slice(pp * pw, (pp + 1) * pw)
            q = q_ref[0, qrows, cols]
            kw = k_ref[0, pl.ds(start, nk), cols]
            vw = v_ref[0, pl.ds(start, nk), cols]
            kc = kc_ref[0, :, cols]
            vc = vc_ref[0, :, cols]
            outs = []
            for hh in range(2):
                msk = lo if hh == 0 else jnp.logical_not(lo)
                qm = jnp.where(msk, q, jnp.zeros_like(q)) * jnp.asarray(HEAD_DIM ** -0.5, BF16)
                s = (lax.dot_general(qm, kw, NT_DIMS, preferred_element_type=F32)
                     + bias_ref[cls, 2 * pp + hh])
                sc = lax.dot_general(qm, kc, NT_DIMS, preferred_element_type=F32)
                m = jnp.maximum(jnp.max(s, axis=-1, keepdims=True),
                                jnp.max(sc, axis=-1, keepdims=True))
                p = jnp.exp(s - m)
                pc = jnp.exp(sc - m)
                l = jnp.sum(p, axis=-1, keepdims=True) + jnp.sum(pc, axis=-1, keepdims=True)
                o = (jnp.dot(p.astype(BF16), vw, preferred_element_type=F32)
                     + jnp.dot(pc.astype(BF16), vc, preferred_element_type=F32))
                outs.append(o / l)
            o_ref[0, qrows, cols] = jnp.where(lo, outs[0], outs[1]).astype(BF16)


def _na_row_classes(rows):
    out = []
    for g in (0, 1, rows // NA_GROWS - 1):
        r0 = g * NA_GROWS
        kr0 = int(np.clip(r0 - WIN_H // 2, 0, rows - NA_KROWS))
        table = []
        for rl in range(NA_GROWS):
            r = r0 + rl
            sr = int(np.clip(r - WIN_H // 2, 0, rows - WIN_H))
            table.append([(kr0 + kl - r + WIN_H - 1) if sr <= kr0 + kl < sr + WIN_H else None
                          for kl in range(NA_KROWS)])
        out.append(table)
    return out


def _na_bias_kernel(rows, rpb_ref, o_ref):
    h = pl.program_id(0)
    n_dr, n_dc = 2 * WIN_H - 1, 2 * WIN_W - 1
    cq = lax.broadcasted_iota(I32, (GRID_W, 2 * GRID_W), 0)
    lane = lax.broadcasted_iota(I32, (GRID_W, 2 * GRID_W), 1)
    ck = lane & (GRID_W - 1)
    q_start = jnp.clip(cq - WIN_W // 2, 0, GRID_W - WIN_W)
    col_ok = (ck >= q_start) & (ck < q_start + WIN_W)
    dc = ck - cq + (WIN_W - 1)
    neg = jnp.full((GRID_W, 2 * GRID_W), NEG, F32)
    tiles = []
    for i in range(n_dr):
        t = jnp.zeros((GRID_W, 2 * GRID_W), F32)
        for jj in range(n_dc):
            t = jnp.where(dc == jj, rpb_ref[h * (n_dr * n_dc) + i * n_dc + jj], t)
        tiles.append(jnp.where(col_ok, t, neg))
    first_half = lane < GRID_W
    for c, table in enumerate(_na_row_classes(rows)):
        for rl in range(NA_GROWS):
            for m in range(NA_KROWS // 2):
                ia, ib = table[rl][2 * m], table[rl][2 * m + 1]
                ta = neg if ia is None else tiles[ia]
                tb = neg if ib is None else tiles[ib]
                blk = ta if ia == ib else jnp.where(first_half, ta, tb)
                o_ref[c, 0, rl * GRID_W:(rl + 1) * GRID_W, m * 2 * GRID_W:(m + 1) * 2 * GRID_W] = blk


def _na_bias(rpb, rows):
    nh = rpb.shape[0]
    nq, nk = NA_GROWS * GRID_W, NA_KROWS * GRID_W
    return pl.pallas_call(
        functools.partial(_na_bias_kernel, rows), grid=(nh,),
        in_specs=[pl.BlockSpec(memory_space=pltpu.SMEM)],
        out_specs=pl.BlockSpec((3, 1, nq, nk), lambda h: (0, h, 0, 0)),
        out_shape=jax.ShapeDtypeStruct((3, nh, nq, nk), F32),
        compiler_params=_cparams(("parallel",)), name="na_bias",
    )(rpb.reshape(-1))


def _na_attn(q, k, v, kc, vc, bias_tab, rows, b_off=0):
    s, d = q.shape[1:]
    nb = kc.shape[0]
    lw = NA_HPAIRS * 2 * HEAD_DIM
    hp = d // lw
    nrb = rows // NA_QROWS
    nq = NA_QROWS * GRID_W
    lc = kc.shape[1]
    in_specs = [
        pl.BlockSpec((1, nq, lw), lambda h, b, r: (b + b_off, r, h)),
        pl.BlockSpec((1, s, lw), lambda h, b, r: (b + b_off, 0, h)),
        pl.BlockSpec((1, s, lw), lambda h, b, r: (b + b_off, 0, h)),
        pl.BlockSpec((1, lc, lw), lambda h, b, r: (b, 0, h)),
        pl.BlockSpec((1, lc, lw), lambda h, b, r: (b, 0, h)),
        pl.BlockSpec((3, 2 * NA_HPAIRS) + bias_tab.shape[2:], lambda h, b, r: (0, h, 0, 0)),
    ]
    return pl.pallas_call(
        functools.partial(_na_kernel, rows),
        grid=(hp, nb, nrb), in_specs=in_specs,
        out_specs=pl.BlockSpec((1, nq, lw), lambda h, b, r: (b, r, h)),
        out_shape=jax.ShapeDtypeStruct((nb, s, d), BF16),
        compiler_params=_cparams(("parallel", "parallel", "parallel")),
        name="neighbourhood_attention",
    )(q, k, v, kc, vc, bias_tab)


def _oproj_kernel(nbp, ap_ref, as_ref, x_ref, mod_ref, wout_ref, gffn_ref, wr_hi_ref, wr_lo_ref,
                  rb_ref, x1_ref, h_ref, gates_ref, sel_ref, cnt_ref):
    a = jnp.where(pl.program_id(0) < nbp, ap_ref[...], as_ref[...])
    y = jnp.dot(a, wout_ref[...], preferred_element_type=F32)
    x1 = x_ref[...] + mod_ref[0, 2:3, :] * y
    x1_ref[...] = x1
    _ffn_pre(x1, mod_ref, gffn_ref, wr_hi_ref, wr_lo_ref, rb_ref,
             h_ref, gates_ref, sel_ref, cnt_ref)


def _oproj(attn_p, attn_s, x, mod, w_out, g_ffn, wr_hi, wr_lo, rbias, nbp, bps):
    t, d = x.shape
    midx = _mod_index(nbp, bps)
    in_specs = [
        pl.BlockSpec((TM, d), lambda i: (jnp.minimum(i, nbp - 1), 0)),
        pl.BlockSpec((TM, d), lambda i: (jnp.maximum(i - nbp, 0), 0)),
        pl.BlockSpec((TM, d), lambda i: (i, 0)),
        pl.BlockSpec((1, 6, d), lambda i: (midx(i), 0, 0)),
        _const_spec(w_out.shape), _const_spec(g_ffn.shape), _const_spec(wr_hi.shape),
        _const_spec(wr_lo.shape), _const_spec(rbias.shape),
    ]
    out_specs, out_shape = _pre_out_specs(t, d)
    return pl.pallas_call(
        functools.partial(_oproj_kernel, nbp), grid=(t // TM,), in_specs=in_specs,
        out_specs=out_specs, out_shape=out_shape, compiler_params=_cparams(("parallel",)),
        name="attn_out_proj",
    )(attn_p, attn_s, x, mod, w_out, g_ffn, wr_hi, wr_lo, rbias)


def _router_weights(w_router, router_bias):
    d, ne = w_router.shape
    w = jnp.pad(w_router, ((0, 0), (0, 128 - ne)))
    hi = w.astype(BF16)
    lo = (w - hi.astype(F32)).astype(BF16)
    return hi, lo, router_bias.reshape(ne, 1)


def _shared_weights(s_gate, s_up, s_down):
    return jnp.concatenate([s_gate, s_up], axis=-1).astype(BF16), s_down.astype(BF16)


def _layer_weights(layer, d, norm_mix, norm_ffn, pc, na, moe):
    i = layer // 2
    w = dict(g_mix=norm_mix[layer].reshape(1, d), g_ffn=norm_ffn[layer].reshape(1, d))
    w["wr_hi"], w["wr_lo"], w["rbias"] = _router_weights(moe["router"][layer],
                                                         moe["router_bias"][layer])
    w["sgu"], w["sd"] = _shared_weights(moe["s_gate"][layer], moe["s_up"][layer],
                                        moe["s_down"][layer])
    if layer % 2 == 0:
        w.update(w_in=pc["w_in"][i].astype(BF16), pool_w=pc["pool_w"][i].astype(BF16),
                 pool_scale=pc["pool_scale"][i].reshape(1, -1), conv_w=pc["conv_w"][i],
                 w_out=pc["w_out"][i].astype(BF16))
    else:
        nh = d // HEAD_DIM
        head_of = np.arange(d) // HEAD_DIM
        w.update(hm=jnp.asarray(head_of[:, None] == np.arange(128)[None, :], BF16),
                 hmt=jnp.asarray(np.arange(128)[:, None] == head_of[None, :], BF16),
                 qg=jnp.tile(na["q_norm"][i], nh).reshape(1, d),
                 kg=jnp.tile(na["k_norm"][i], nh).reshape(1, d),
                 w_qkv=na["w_qkv"][i].astype(BF16), w_out=na["w_out"][i].astype(BF16))
    return w


def kernel(x_prompt, x_sample, cache_k, cache_v, c, c_ctx, ada_w, ada_b, norm_mix, norm_ffn,
           pc_w_in, pc_pool_w, pc_pool_scale, pc_conv_w, pc_w_out,
           na_w_qkv, na_q_norm, na_k_norm, na_rpb, na_w_out,
           moe_router, moe_router_bias, moe_w_gate, moe_w_up, moe_w_down,
           moe_shared_gate, moe_shared_up, moe_shared_down):
    nb_p, s_p, d = x_prompt.shape
    nb_s, s_s, _ = x_sample.shape
    assert s_p == TM and s_s % TM == 0 and nb_p > 0 and nb_s > 0
    tp, ts = nb_p * s_p, nb_s * s_s
    nbp, bps = tp // TM, s_s // TM
    depth = ada_w.shape[0]
    nh = d // HEAD_DIM
    rows = s_s // GRID_W

    cond = jnp.concatenate([c_ctx[None], c], axis=0)
    cond = jnp.pad(cond, ((0, -cond.shape[0] % 8), (0, 0)))
    mods = _modulation(cond, ada_w, ada_b).reshape(depth, cond.shape[0], 6, d)
    pc = dict(w_in=pc_w_in, pool_w=pc_pool_w, pool_scale=pc_pool_scale, conv_w=pc_conv_w,
              w_out=pc_w_out)
    na = dict(w_qkv=na_w_qkv, q_norm=na_q_norm, k_norm=na_k_norm, w_out=na_w_out)
    moe = dict(router=moe_router, router_bias=moe_router_bias, s_gate=moe_shared_gate,
               s_up=moe_shared_up, s_down=moe_shared_down)

    xp, xs = x_prompt.reshape(tp, d), x_sample.reshape(ts, d)
    x = None
    ready_qkv = None
    new_k, new_v = [], []
    weights = [_layer_weights(l, d, norm_mix, norm_ffn, pc, na, moe) for l in range(depth)]
    for layer in range(depth):
        i = layer // 2
        w = weights[layer]
        mod = mods[layer]
        if layer % 2 == 0:
            if x is not None:
                xp, xs = x[:tp], x[tp:]
            x, h, gates_t, sel_t, cnt = _mixer(
                xp, xs, mod, w["g_mix"], w["w_in"], w["pool_w"], w["pool_scale"], w["conv_w"],
                w["w_out"], w["g_ffn"], w["wr_hi"], w["wr_lo"], w["rbias"], nbp, bps, s_p, s_s)
        else:
            if ready_qkv is not None and tp % s_s == 0:
                q_all, k_all, v_all, kp32, vp32 = ready_qkv
                qp, kp, vp = q_all, k_all, v_all
                seqs = (tp + ts) // s_s
                qs, ks, vs = (a.reshape(seqs, s_s, d) for a in (q_all, k_all, v_all))
                b_off = tp // s_s
            else:
                midx = _mod_index(nbp, bps)
                qkv_args = (x, mod, w["g_mix"], w["w_qkv"], w["qg"], w["kg"], w["hm"], w["hmt"])
                qp, kp, vp, kp32, vp32 = _qkv(*qkv_args, 0, nbp, midx, True)
                qs, ks, vs = (a.reshape(nb_s, s_s, d)
                              for a in _qkv(*qkv_args, nbp, ts // TM, midx, False))
                b_off = 0
            new_k.append(kp32.reshape(nb_p, s_p, nh, HEAD_DIM))
            new_v.append(vp32.reshape(nb_p, s_p, nh, HEAD_DIM))
            a_p = _ctx_attn(qp, kp, vp, nb_p, s_p)
            lc = cache_k.shape[2]
            kc = cache_k[:, i].reshape(nb_s, lc, d).astype(BF16)
            vc = cache_v[:, i].reshape(nb_s, lc, d).astype(BF16)
            a_s = _na_attn(qs, ks, vs, kc, vc, _na_bias(na_rpb[i], rows), rows, b_off)
            x, h, gates_t, sel_t, cnt = _oproj(a_p, a_s.reshape(ts, d), x, mod, w["w_out"],
                                               w["g_ffn"], w["wr_hi"], w["wr_lo"], w["rbias"],
                                               nbp, bps)
        ready_qkv = None
        if layer + 1 < depth and (layer + 1) % 2 == 1:
            w1 = weights[layer + 1]
            finish = lambda yg, w8: _combine_qkv(
                yg, w8, h, x, mod, w["sgu"], w["sd"], mods[layer + 1], w1["g_mix"], w1["w_qkv"],
                w1["qg"], w1["kg"], w1["hm"], w1["hmt"], nbp, bps)
            x, *ready_qkv = _sparse_moe(x, h, gates_t, sel_t, cnt, layer, moe_w_gate, moe_w_up,
                                        moe_w_down, finish)
        else:
            finish = lambda yg, w8: _combine(yg, w8, h, x, mod, w["sgu"], w["sd"], nbp, bps,
                                             layer == depth - 1)
            x = _sparse_moe(x, h, gates_t, sel_t, cnt, layer, moe_w_gate, moe_w_up, moe_w_down,
                            finish)
    y_p, y_s = x
    return (y_p.reshape(nb_p, s_p, d), y_s.reshape(nb_s, s_s, d),
            jnp.stack(new_k, axis=1), jnp.stack(new_v, axis=1))
```

```python
import functools

import numpy as np
import jax
import jax.numpy as jnp
from jax import lax
from jax.experimental import pallas as pl
from jax.experimental.pallas import tpu as pltpu
from jax.experimental.pallas import tpu_sc as plsc

F32 = jnp.float32
BF16 = jnp.bfloat16
I32 = jnp.int32
U32 = jnp.uint32

TM = 256
HALO = 8
POOL_WINDOWS = (2, 4, 8, 16)
N_EXPERTS = 64
N_GROUPS = 8
GROUP_SIZE = N_EXPERTS // N_GROUPS
TOPK_GROUPS = 4
TOPK = 8
ROUTED_SCALE = 2.5
EPS = 1e-6
GRID_W = 64
WIN_H = 8
WIN_W = 16
HEAD_DIM = 64
NEG = float(np.finfo(np.float32).min)
VMEM_LIMIT = 56 * 1024 * 1024
NT_DIMS = (((1,), (1,)), ((), ()))
TG = 1024
SC_WINDOW = 64
SC_WORKERS = 32


def _cparams(sem):
    return pltpu.CompilerParams(dimension_semantics=sem, vmem_limit_bytes=VMEM_LIMIT)


def _silu(x):
    return x * jax.nn.sigmoid(x)


def _split_bf16(x):
    hi = x.astype(BF16)
    lo = (x - hi.astype(F32)).astype(BF16)
    return hi, lo


def _adaln(x, g, shift, scale):
    ms = jnp.mean(x * x, axis=-1, keepdims=True)
    return (x * lax.rsqrt(ms + EPS)) * g * (1.0 + scale) + shift


def _pack_pair(x):
    w = x.shape[1] // 2
    lo = lax.bitcast_convert_type(x[:, :w].astype(BF16).astype(F32), U32) >> 16
    hi = lax.bitcast_convert_type(x[:, w:].astype(BF16).astype(F32), U32)
    return lax.bitcast_convert_type(lo | hi, I32)


def _unpack_pair(p):
    u = lax.bitcast_convert_type(p, U32)
    lo = lax.bitcast_convert_type(u << 16, F32)
    hi = lax.bitcast_convert_type(u & jnp.uint32(0xFFFF0000), F32)
    return jnp.concatenate([lo, hi], axis=-1)


def _mod_kernel(cond_ref, w_ref, b_ref, o_ref):
    c = cond_ref[...]
    a = _silu(c)
    o_ref[0] = jnp.dot(a, w_ref[0], preferred_element_type=F32,
                       precision=lax.Precision.HIGHEST) + b_ref[0]


def _modulation(cond, ada_w, ada_b):
    depth, d, n = ada_w.shape
    rows = cond.shape[0]
    tn = 1536
    return pl.pallas_call(
        _mod_kernel,
        grid=(depth, n // tn),
        in_specs=[
            pl.BlockSpec((rows, d), lambda l, j: (0, 0)),
            pl.BlockSpec((1, d, tn), lambda l, j: (l, 0, j)),
            pl.BlockSpec((1, 1, tn), lambda l, j: (l, 0, j)),
        ],
        out_specs=pl.BlockSpec((1, rows, tn), lambda l, j: (l, 0, j)),
        out_shape=jax.ShapeDtypeStruct((depth, rows, n), F32),
        compiler_params=_cparams(("arbitrary", "arbitrary")),
        name="modulation",
    )(cond, ada_w, ada_b.reshape(depth, 1, n))


def _route(logits_t, bias_col):
    tm = logits_t.shape[1]
    scores = jax.nn.sigmoid(logits_t)
    biased = scores + bias_col
    sub = lax.broadcasted_iota(I32, (GROUP_SIZE, tm), 0).astype(F32)
    ninf = jnp.float32(-jnp.inf)
    groups, gscore = [], []
    for g in range(N_GROUPS):
        v = biased[g * GROUP_SIZE:(g + 1) * GROUP_SIZE]
        m1 = jnp.max(v, axis=0, keepdims=True)
        first = jnp.min(jnp.where(v == m1, sub, float(GROUP_SIZE)), axis=0, keepdims=True)
        m2 = jnp.max(jnp.where(sub == first, ninf, v), axis=0, keepdims=True)
        groups.append(v)
        gscore.append(m1 + m2)
    masked = []
    for g in range(N_GROUPS):
        rank = jnp.zeros((1, tm), I32)
        for g2 in range(N_GROUPS):
            if g2 == g:
                continue
            ahead = gscore[g2] > gscore[g]
            if g2 < g:
                ahead = ahead | (gscore[g2] == gscore[g])
            rank = rank + ahead.astype(I32)
        masked.append(jnp.where(rank < TOPK_GROUPS, groups[g], ninf))
    masked = jnp.concatenate(masked, axis=0)
    eidx = lax.broadcasted_iota(I32, (N_EXPERTS, tm), 0).astype(F32)
    sel = jnp.zeros((N_EXPERTS, tm), jnp.bool_)
    for _ in range(TOPK):
        best = jnp.max(masked, axis=0, keepdims=True)
        first = jnp.min(jnp.where(masked == best, eidx, float(N_EXPERTS)), axis=0, keepdims=True)
        hit = eidx == first
        sel = sel | hit
        masked = jnp.where(hit, ninf, masked)
    w = jnp.where(sel, scores, 0.0)
    wsum = jnp.sum(w, axis=0, keepdims=True)
    return w / wsum * ROUTED_SCALE, sel


def _ffn_pre(x1, mod_ref, gffn_ref, wr_hi_ref, wr_lo_ref, rb_ref,
             rows, sub, h_ref, gates_ref, sel_ref, cnt_ref):
    sh2 = mod_ref[0, 3:4, :]
    sc2 = mod_ref[0, 4:5, :]
    h = _adaln(x1, gffn_ref[...], sh2, sc2)
    h_hi, h_lo = _split_bf16(h)
    h_ref[rows, :] = _pack_pair(h_hi)
    wr_hi = wr_hi_ref[...]
    logits = (jnp.dot(h_hi, wr_hi, preferred_element_type=F32)
              + jnp.dot(h_hi, wr_lo_ref[...], preferred_element_type=F32)
              + jnp.dot(h_lo, wr_hi, preferred_element_type=F32))
    gates_t, sel = _route(logits.T[:N_EXPERTS], rb_ref[...])
    gates_ref[:, rows] = gates_t
    sel_b = sel.astype(F32).astype(BF16)
    sel_ref[:, rows] = sel_b
    ones = jnp.ones((8, sel_b.shape[1]), BF16)
    cnt_ref[sub] = lax.dot_general(ones, sel_b, NT_DIMS, preferred_element_type=F32)


PRE_BLOCKS = 2


def _pre_out_specs(t, d, nsub):
    tm = nsub * TM
    specs = [
        pl.BlockSpec((tm, d), lambda i: (i, 0)),
        pl.BlockSpec((tm, d // 2), lambda i: (i, 0)),
        pl.BlockSpec((N_EXPERTS, tm), lambda i: (0, i)),
        pl.BlockSpec((N_EXPERTS, tm), lambda i: (0, i)),
        pl.BlockSpec((nsub, 8, N_EXPERTS), lambda i: (i, 0, 0)),
    ]
    shapes = [
        jax.ShapeDtypeStruct((t, d), F32),
        jax.ShapeDtypeStruct((t, d // 2), I32),
        jax.ShapeDtypeStruct((N_EXPERTS, t), F32),
        jax.ShapeDtypeStruct((N_EXPERTS, t), BF16),
        jax.ShapeDtypeStruct((t // TM, 8, N_EXPERTS), F32),
    ]
    return specs, shapes


def _mixer_kernel(nbp, bps, sp, ss,
                  xpc_ref, xsc_ref, xprev_ref, xnext_ref, mod_ref, gmix_ref, win_ref, pw_ref, ps_ref,
                  cw_ref,
                  wout_ref, gffn_ref, wr_hi_ref, wr_lo_ref, rb_ref,
                  x1_ref, h_ref, gates_ref, sel_ref, cnt_ref):
    i = pl.program_id(0)
    nsub = xpc_ref.shape[0] // TM
    is_p = i * nsub < nbp
    sh1 = mod_ref[0, 0:1, :]
    sc1 = mod_ref[0, 1:2, :]
    g1 = mod_ref[0, 2:3, :]
    next_ = TM + 2 * HALO

    def up(a, k):
        return pltpu.roll(a, next_ - k, 0)

    x_all = jnp.where(is_p, xpc_ref[...], xsc_ref[...])
    for sub in range(nsub):
        rows = slice(sub * TM, (sub + 1) * TM)
        j = lax.rem(jnp.maximum(i * nsub + sub - nbp, 0), bps)
        first = is_p | (j == 0)
        last = is_p | (j == bps - 1)
        base = jnp.where(is_p, 0, j * TM)
        slen = jnp.where(is_p, sp, ss)
        xc = x_all[rows]
        x_prev = xprev_ref[...] if sub == 0 else x_all[sub * TM - HALO:sub * TM]
        x_next = (xnext_ref[...] if sub == nsub - 1
                  else x_all[(sub + 1) * TM:(sub + 1) * TM + HALO])
        x_ext = jnp.concatenate([x_prev, xc, x_next], axis=0)
        h_ext = _adaln(x_ext, gmix_ref[...], sh1, sc1).astype(BF16)
        u = jnp.dot(h_ext, win_ref[...], preferred_element_type=F32)
        row = lax.broadcasted_iota(I32, (next_, 1), 0)
        keep = (((row >= HALO) | jnp.logical_not(first))
                & ((row < HALO + TM) | jnp.logical_not(last)))
        u = jnp.where(keep, u, 0.0)

        dm = u.shape[1] // 4
        ua = u[:, :dm]
        gate_b = u[HALO:HALO + TM, dm:2 * dm]
        z = u[:, 2 * dm:3 * dm] * u[:, 3 * dm:]

        pos = base + lax.broadcasted_iota(I32, (TM, 1), 0)
        pg = dm // len(POOL_WINDOWS)
        ya = []
        for g, w in enumerate(POOL_WINDOWS):
            e = ua[:, g * pg:(g + 1) * pg]
            acc = e
            span = 1
            while span < w:
                acc = acc + up(acc, span)
                span *= 2
            off = HALO - w // 2
            wsum = (up(acc, off) if off else acc)[:TM]
            lo = jnp.maximum(pos - w // 2, 0)
            hi = jnp.minimum(pos + (w - w // 2 - 1), slen - 1)
            cnt = (hi - lo + 1).astype(F32)
            diff = wsum / cnt - e[HALO:HALO + TM]
            ya.append(jnp.dot(diff.astype(BF16), pw_ref[g], preferred_element_type=F32))
        y_a = jnp.concatenate(ya, axis=-1) * ps_ref[...]
        zc = (cw_ref[0:1, :] * up(z, HALO - 1)[:TM] + cw_ref[1:2, :] * z[HALO:HALO + TM]
              + cw_ref[2:3, :] * up(z, HALO + 1)[:TM])
        y_b = gate_b * zc
        ycat = jnp.concatenate([y_a, y_b], axis=-1).astype(BF16)
        y = jnp.dot(ycat, wout_ref[...], preferred_element_type=F32)
        x1 = xc + g1 * y
        x1_ref[rows, :] = x1
        _ffn_pre(x1, mod_ref, gffn_ref, wr_hi_ref, wr_lo_ref, rb_ref,
                 rows, sub, h_ref, gates_ref, sel_ref, cnt_ref)


def _mod_index(nbp, bps):
    def f(i):
        return jnp.where(i < nbp, 0, 1 + jnp.maximum(i - nbp, 0) // bps)
    return f


def _const_spec(shape):
    nd = len(shape)
    return pl.BlockSpec(shape, lambda i: (0,) * nd)


def _mixer(xp, xs, mod, g_mix, w_in, pool_w, pool_scale, conv_w, w_out, g_ffn, wr_hi, wr_lo, rbias,
           nbp, bps, sp, ss):
    d = xp.shape[1]
    t = xp.shape[0] + xs.shape[0]
    nsub = PRE_BLOCKS if nbp % PRE_BLOCKS == 0 and bps % PRE_BLOCKS == 0 else 1
    tm = nsub * TM
    nblk = t // tm
    nbp2 = nbp // nsub
    midx = _mod_index(nbp2, bps // nsub)
    hpb = tm // HALO
    nh = xs.shape[0] // HALO
    in_specs = [
        pl.BlockSpec((tm, d), lambda i: (jnp.minimum(i, nbp2 - 1), 0)),
        pl.BlockSpec((tm, d), lambda i: (jnp.maximum(i - nbp2, 0), 0)),
        pl.BlockSpec((HALO, d), lambda i: (jnp.maximum((i - nbp2) * hpb - 1, 0), 0)),
        pl.BlockSpec((HALO, d), lambda i: (jnp.clip((i - nbp2 + 1) * hpb, 0, nh - 1), 0)),
        pl.BlockSpec((1, 6, d), lambda i: (midx(i), 0, 0)),
        _const_spec(g_mix.shape), _const_spec(w_in.shape), _const_spec(pool_w.shape),
        _const_spec(pool_scale.shape), _const_spec(conv_w.shape), _const_spec(w_out.shape),
        _const_spec(g_ffn.shape), _const_spec(wr_hi.shape), _const_spec(wr_lo.shape),
        _const_spec(rbias.shape),
    ]
    out_specs, out_shape = _pre_out_specs(t, d, nsub)
    return pl.pallas_call(
        functools.partial(_mixer_kernel, nbp, bps, sp, ss),
        grid=(nblk,), in_specs=in_specs, out_specs=out_specs, out_shape=out_shape,
        compiler_params=_cparams(("parallel",)), name="pool_conv_mixer",
    )(xp, xs, xs, xs, mod, g_mix, w_in, pool_w, pool_scale, conv_w, w_out, g_ffn, wr_hi, wr_lo,
      rbias)


SLOT_BLOCKS = 4


def _slots_kernel(sel_ref, gates_ref, base_ref, pos_ref, w_ref):
    r = lax.broadcasted_iota(I32, (TM, TM), 0)
    c = lax.broadcasted_iota(I32, (TM, TM), 1)
    before = (r < c).astype(F32).astype(BF16)
    er = lax.broadcasted_iota(I32, (N_EXPERTS, N_EXPERTS), 0)
    ec = lax.broadcasted_iota(I32, (N_EXPERTS, N_EXPERTS), 1)
    lower = (ec < er).astype(F32).astype(BF16)
    sub = lax.broadcasted_iota(I32, (TOPK, TM), 0)
    for blk in range(sel_ref.shape[1] // TM):
        cols = slice(blk * TM, (blk + 1) * TM)
        sel = sel_ref[:, cols]
        rank_tok = jnp.dot(sel, before, preferred_element_type=F32)
        rank_exp = jnp.dot(lower, sel, preferred_element_type=F32)
        slot = base_ref[blk] + rank_tok
        chosen = sel > 0
        gates = gates_ref[:, cols]
        pos8 = jnp.zeros((TOPK, TM), F32)
        w8 = jnp.zeros((TOPK, TM), F32)
        for k in range(TOPK):
            mk = chosen & (rank_exp == float(k))
            pk = jnp.sum(jnp.where(mk, slot, 0.0), axis=0, keepdims=True)
            wk = jnp.sum(jnp.where(mk, gates, 0.0), axis=0, keepdims=True)
            pos8 = jnp.where(sub == k, pk, pos8)
            w8 = jnp.where(sub == k, wk, w8)
        pos_ref[:, cols] = pos8.astype(I32)
        w8 = jnp.concatenate([w8, jnp.zeros((128 - TOPK, TM), F32)], axis=0)
        w_ref[cols, :] = w8.T


def _slots(sel_t, gates_t, base):
    ne, t = sel_t.shape
    nblk = next(n for n in range(SLOT_BLOCKS, 0, -1) if (t // TM) % n == 0)
    ts = nblk * TM
    return pl.pallas_call(
        _slots_kernel, grid=(t // ts,),
        in_specs=[
            pl.BlockSpec((ne, ts), lambda i: (0, i)),
            pl.BlockSpec((ne, ts), lambda i: (0, i)),
            pl.BlockSpec((nblk, ne, 1), lambda i: (i, 0, 0)),
        ],
        out_specs=[pl.BlockSpec((TOPK, ts), lambda i: (0, i)),
                   pl.BlockSpec((ts, 128), lambda i: (i, 0))],
        out_shape=[jax.ShapeDtypeStruct((TOPK, t), I32), jax.ShapeDtypeStruct((t, 128), F32)],
        compiler_params=_cparams(("parallel",)), name="moe_slots",
    )(sel_t, gates_t, base)


def _sc_gather(table, idx):
    m = idx.shape[0]
    d = table.shape[1]
    assert m % (SC_WINDOW * SC_WORKERS) == 0
    mesh = plsc.VectorSubcoreMesh(core_axis_name="core", subcore_axis_name="subcore")

    @pl.kernel(out_type=jax.ShapeDtypeStruct((m, d), table.dtype), mesh=mesh)
    def gather_rows(x_hbm, i_hbm, o_hbm):
        def body(i_vmem, o_vmem):
            pltpu.sync_copy(x_hbm.at[i_vmem.at[0]], o_vmem)

        pltpu.emit_pipeline(
            body, grid=(m // SC_WINDOW,),
            in_specs=[pl.BlockSpec((1, SC_WINDOW), index_map=lambda i: (i, 0))],
            out_specs=[pl.BlockSpec((SC_WINDOW, d), index_map=lambda i: (i, 0))],
            core_axis_name=("core", "subcore"),
            dimension_semantics=(pltpu.PARALLEL,),
        )(i_hbm, o_hbm)

    return gather_rows(table, idx.reshape(m // SC_WINDOW, SC_WINDOW))


def _sc_scatter(rows, idx, n_out):
    nk, t = idx.shape
    d = rows.shape[1]
    assert t % (SC_WINDOW * SC_WORKERS) == 0
    mesh = plsc.VectorSubcoreMesh(core_axis_name="core", subcore_axis_name="subcore")

    @pl.kernel(out_type=jax.ShapeDtypeStruct((n_out, d), rows.dtype), mesh=mesh)
    def scatter_rows(x_hbm, i_hbm, o_hbm):
        def body(x_vmem, i_vmem):
            for k in range(nk):
                pltpu.sync_copy(x_vmem, o_hbm.at[i_vmem.at[k, 0]])

        pltpu.emit_pipeline(
            body, grid=(t // SC_WINDOW,),
            in_specs=[pl.BlockSpec((SC_WINDOW, d), index_map=lambda i: (i, 0)),
                      pl.BlockSpec((nk, 1, SC_WINDOW), index_map=lambda i: (0, i, 0))],
            out_specs=[],
            core_axis_name=("core", "subcore"),
            dimension_semantics=(pltpu.PARALLEL,),
        )(x_hbm, i_hbm)

    return scatter_rows(rows, idx.reshape(nk, t // SC_WINDOW, SC_WINDOW))


def _expert_ffn_kernel(te_ref, fresh_ref, nv_ref, x_hbm, wg_ref, wu_ref, wd_ref, y_ref,
                       xbuf, xsem, wg_s, wu_s, wd_s):
    j = pl.program_id(0)
    nv = nv_ref[0]

    def tile_copy(tile, slot):
        rows = pl.ds(pl.multiple_of(tile * TG, TG), TG)
        return pltpu.make_async_copy(x_hbm.at[rows, :], xbuf.at[slot], xsem.at[slot])

    @pl.when(j == 0)
    def _():
        tile_copy(0, 0).start()

        @pl.when(nv > 1)
        def _():
            tile_copy(1, 1).start()

    @pl.when(j < nv)
    def _():
        @pl.when(j + 2 < nv)
        def _():
            tile_copy(j + 2, lax.rem(j + 2, 3)).start()

        slot = lax.rem(j, 3)
        tile_copy(j, slot).wait()

        @pl.when(fresh_ref[j] == 1)
        def _():
            wg_s[...] = wg_ref[0, 0].astype(BF16)
            wu_s[...] = wu_ref[0, 0].astype(BF16)
            wd_s[...] = wd_ref[0, 0].astype(BF16)

        x = _unpack_pair(xbuf[slot]).astype(BF16)
        a = jnp.dot(x, wg_s[...], preferred_element_type=F32)
        b = jnp.dot(x, wu_s[...], preferred_element_type=F32)
        act = (_silu(a) * b).astype(BF16)
        y_ref[...] = _pack_pair(jnp.dot(act, wd_s[...], preferred_element_type=F32))


def _expert_ffn(xs, n_tiles, tile_expert, fresh, n_valid, layer, w_gate, w_up, w_down):
    dh = xs.shape[1]
    last = lambda j, te, fr, nv: jnp.minimum(j, nv[0] - 1)
    wspec = lambda w: pl.BlockSpec((1, 1) + w.shape[2:],
                                   lambda j, te, fr, nv: (layer, te[j], 0, 0))
    grid_spec = pltpu.PrefetchScalarGridSpec(
        num_scalar_prefetch=3, grid=(n_tiles,),
        in_specs=[pl.BlockSpec(memory_space=pl.ANY), wspec(w_gate), wspec(w_up), wspec(w_down)],
        out_specs=pl.BlockSpec((TG, dh), lambda j, te, fr, nv: (last(j, te, fr, nv), 0)),
        scratch_shapes=[pltpu.VMEM((3, TG, dh), xs.dtype), pltpu.SemaphoreType.DMA((3,)),
                        pltpu.VMEM(w_gate.shape[2:], BF16), pltpu.VMEM(w_up.shape[2:], BF16),
                        pltpu.VMEM(w_down.shape[2:], BF16)],
    )
    return pl.pallas_call(
        _expert_ffn_kernel, grid_spec=grid_spec,
        out_shape=jax.ShapeDtypeStruct((n_tiles * TG, dh), I32),
        compiler_params=_cparams(("arbitrary",)), name="expert_ffn",
    )(tile_expert, fresh, n_valid, xs, w_gate, w_up, w_down)


def _combine_value(yg_ref, w_ref, h_ref, x_ref, mod_ref, sgu_ref, sd_ref):
    f = sd_ref.shape[0]
    h = _unpack_pair(h_ref[...]).astype(BF16)
    hs = jnp.dot(h, sgu_ref[...], preferred_element_type=F32)
    act = (_silu(hs[:, :f]) * hs[:, f:]).astype(BF16)
    acc = jnp.dot(act, sd_ref[...], preferred_element_type=F32)
    w = w_ref[...]
    for k in range(TOPK):
        acc = acc + w[:, k:k + 1] * _unpack_pair(yg_ref[k])
    return x_ref[...] + mod_ref[0, 5:6, :] * acc


def _combine_kernel(nbp, yg_ref, w_ref, h_ref, x_ref, mod_ref, sgu_ref, sd_ref, *o_refs):
    out = _combine_value(yg_ref, w_ref, h_ref, x_ref, mod_ref, sgu_ref, sd_ref)
    if len(o_refs) == 1:
        o_refs[0][...] = out
    else:
        i = pl.program_id(0)

        @pl.when(i < nbp)
        def _():
            o_refs[0][...] = out

        @pl.when(i >= nbp)
        def _():
            o_refs[1][...] = out


def _combine(yg, w8, h, x, mod, sgu, sd, nbp, bps, split_out):
    t, d = x.shape
    tc = 2 * TM
    assert (nbp * TM) % tc == 0 and (bps * TM) % tc == 0
    nbp, bps = nbp * TM // tc, bps * TM // tc
    midx = _mod_index(nbp, bps)
    if split_out:
        out_specs = [pl.BlockSpec((tc, d), lambda i: (jnp.minimum(i, nbp - 1), 0)),
                     pl.BlockSpec((tc, d), lambda i: (jnp.maximum(i - nbp, 0), 0))]
        out_shape = [jax.ShapeDtypeStruct((nbp * tc, d), F32),
                     jax.ShapeDtypeStruct((t - nbp * tc, d), F32)]
    else:
        out_specs = pl.BlockSpec((tc, d), lambda i: (i, 0))
        out_shape = jax.ShapeDtypeStruct((t, d), F32)
    in_specs = [
        pl.BlockSpec((TOPK, tc, d // 2), lambda i: (0, i, 0)),
        pl.BlockSpec((tc, 128), lambda i: (i, 0)),
        pl.BlockSpec((tc, d // 2), lambda i: (i, 0)),
        pl.BlockSpec((tc, d), lambda i: (i, 0)),
        pl.BlockSpec((1, 6, d), lambda i: (midx(i), 0, 0)),
        _const_spec(sgu.shape), _const_spec(sd.shape),
    ]
    return pl.pallas_call(
        functools.partial(_combine_kernel, nbp), grid=(t // tc,), in_specs=in_specs,
        out_specs=out_specs, out_shape=out_shape,
        compiler_params=_cparams(("arbitrary",)), name="moe_combine",
    )(yg, w8, h, x, mod, sgu, sd)


def _combine_qkv_kernel(nbp, yg_ref, w_ref, h_ref, x_ref, mod_ref, sgu_ref, sd_ref,
                        mod1_ref, gmix_ref, wqkv_ref, qg_ref, kg_ref, hm_ref, hmt_ref,
                        x_out, q_out, k_out, v_out, k32_out, v32_out):
    out = _combine_value(yg_ref, w_ref, h_ref, x_ref, mod_ref, sgu_ref, sd_ref)
    x_out[...] = out
    q, k, v = _qkv_values(out, mod1_ref, gmix_ref, wqkv_ref, qg_ref, kg_ref, hm_ref, hmt_ref)
    q_out[...] = q.astype(BF16)
    k_out[...] = k.astype(BF16)
    v_out[...] = v.astype(BF16)

    @pl.when(pl.program_id(0) < nbp)
    def _():
        k32_out[...] = k
        v32_out[...] = v


def _combine_qkv(yg, w8, h, x, mod, sgu, sd, mod1, g_mix, w_qkv, qg, kg, hm, hmt, nbp, bps):
    t, d = x.shape
    midx = _mod_index(nbp, bps)
    row = lambda i: (i, 0)
    in_specs = [
        pl.BlockSpec((TOPK, TM, d // 2), lambda i: (0, i, 0)),
        pl.BlockSpec((TM, 128), row),
        pl.BlockSpec((TM, d // 2), row),
        pl.BlockSpec((TM, d), row),
        pl.BlockSpec((1, 6, d), lambda i: (midx(i), 0, 0)),
        _const_spec(sgu.shape), _const_spec(sd.shape),
        pl.BlockSpec((1, 6, d), lambda i: (midx(i), 0, 0)),
        _const_spec(g_mix.shape), _const_spec(w_qkv.shape), _const_spec(qg.shape),
        _const_spec(kg.shape), _const_spec(hm.shape), _const_spec(hmt.shape),
    ]
    pinned = pl.BlockSpec((TM, d), lambda i: (jnp.minimum(i, nbp - 1), 0))
    out_specs = [pl.BlockSpec((TM, d), row)] * 4 + [pinned, pinned]
    out_shape = ([jax.ShapeDtypeStruct((t, d), F32)] + [jax.ShapeDtypeStruct((t, d), BF16)] * 3
                 + [jax.ShapeDtypeStruct((nbp * TM, d), F32)] * 2)
    return pl.pallas_call(
        functools.partial(_combine_qkv_kernel, nbp), grid=(t // TM,), in_specs=in_specs,
        out_specs=out_specs, out_shape=out_shape,
        compiler_params=_cparams(("arbitrary",)), name="moe_combine_qkv",
    )(yg, w8, h, x, mod, sgu, sd, mod1, g_mix, w_qkv, qg, kg, hm, hmt)


def _sparse_moe(x, h, gates_t, sel_t, cnt, layer, w_gate, w_up, w_down, finish):
    t, d = x.shape
    ne = N_EXPERTS
    n_pad = ne * TG
    n_tiles = (t * TOPK + n_pad) // TG
    n_slots = n_tiles * TG
    counts = cnt[:, 0, :].astype(I32)
    per_expert = jnp.sum(counts, axis=0)
    padded = (per_expert + TG - 1) // TG * TG
    ends = jnp.cumsum(padded)
    starts = ends - padded
    block_off = jnp.cumsum(counts, axis=0) - counts
    base = (starts[None, :] + block_off).astype(F32)[:, :, None]
    n_valid = (ends[-1] // TG).astype(I32).reshape(1)
    tile_start = jnp.minimum(jnp.arange(n_tiles, dtype=I32), n_valid[0] - 1) * TG
    tile_expert = jnp.sum((ends[None, :] <= tile_start[:, None]).astype(I32), axis=1)
    fresh = jnp.concatenate([jnp.ones((1,), I32),
                             (tile_expert[1:] != tile_expert[:-1]).astype(I32)])

    pos8, w8 = _slots(sel_t, gates_t, base)
    n_extra = -(-n_pad // t)
    cand = (starts + per_expert)[:, None] + jnp.arange(TG, dtype=I32)[None, :]
    spare = n_slots + jnp.arange(n_pad, dtype=I32)
    fill = jnp.where(cand < ends[:, None], cand, spare.reshape(ne, TG)).reshape(-1)
    rest = n_slots + jnp.arange(n_pad, n_extra * t, dtype=I32) % n_pad
    dest = jnp.concatenate([pos8, jnp.concatenate([fill, rest]).reshape(n_extra, t)], axis=0)
    xs = _sc_scatter(h, dest, n_slots + n_pad)
    ys = _expert_ffn(xs, n_tiles, tile_expert, fresh, n_valid, layer, w_gate, w_up, w_down)
    yg = _sc_gather(ys, pos8.reshape(-1)).reshape(TOPK, t, d // 2)
    return finish(yg, w8)


def _head_rms(x, g_row, hm_ref, hmt_ref):
    ss = jnp.dot((x * x).astype(BF16), hm_ref[...], preferred_element_type=F32)
    r = lax.rsqrt(ss * (1.0 / HEAD_DIM) + EPS)
    r_hi, r_lo = _split_bf16(r)
    hmt = hmt_ref[...]
    rb = (jnp.dot(r_hi, hmt, preferred_element_type=F32)
          + jnp.dot(r_lo, hmt, preferred_element_type=F32))
    return (x * rb) * g_row


def _qkv_values(x, mod_ref, gmix_ref, w_ref, qg_ref, kg_ref, hm_ref, hmt_ref):
    d = x.shape[1]
    sh1 = mod_ref[0, 0:1, :]
    sc1 = mod_ref[0, 1:2, :]
    h = _adaln(x, gmix_ref[...], sh1, sc1).astype(BF16)
    qkv = jnp.dot(h, w_ref[...], preferred_element_type=F32)
    q = _head_rms(qkv[:, :d], qg_ref[...], hm_ref, hmt_ref)
    k = _head_rms(qkv[:, d:2 * d], kg_ref[...], hm_ref, hmt_ref)
    return q, k, qkv[:, 2 * d:]


def _qkv_kernel(emit_f32, x_ref, mod_ref, gmix_ref, w_ref, qg_ref, kg_ref, hm_ref, hmt_ref, *outs):
    q, k, v = _qkv_values(x_ref[...], mod_ref, gmix_ref, w_ref, qg_ref, kg_ref, hm_ref, hmt_ref)
    outs[0][...] = q.astype(BF16)
    outs[1][...] = k.astype(BF16)
    outs[2][...] = v.astype(BF16)
    if emit_f32:
        outs[3][...] = k
        outs[4][...] = v


def _qkv(x, mod, g_mix, w_qkv, qg, kg, hm, hmt, blk0, nblk, midx, emit_f32):
    t, d = x.shape
    in_specs = [
        pl.BlockSpec((TM, d), lambda i: (i + blk0, 0)),
        pl.BlockSpec((1, 6, d), lambda i: (midx(i + blk0), 0, 0)),
        _const_spec(g_mix.shape), _const_spec(w_qkv.shape), _const_spec(qg.shape),
        _const_spec(kg.shape), _const_spec(hm.shape), _const_spec(hmt.shape),
    ]
    n_out = 5 if emit_f32 else 3
    out_specs = [pl.BlockSpec((TM, d), lambda i: (i, 0)) for _ in range(n_out)]
    out_shape = [jax.ShapeDtypeStruct((nblk * TM, d), BF16 if o < 3 else F32) for o in range(n_out)]
    return pl.pallas_call(
        functools.partial(_qkv_kernel, emit_f32),
        grid=(nblk,), in_specs=in_specs, out_specs=out_specs, out_shape=out_shape,
        compiler_params=_cparams(("parallel",)), name="qkv_f32" if emit_f32 else "qkv",
    )(x, mod, g_mix, w_qkv, qg, kg, hm, hmt)


def _head_masks():
    lane = lax.broadcasted_iota(I32, (1, 2 * HEAD_DIM), 1)
    return lane < HEAD_DIM


def _ctx_attn_kernel(q_ref, k_ref, v_ref, o_ref):
    lo = _head_masks()
    pw = 2 * HEAD_DIM
    for hp in range(q_ref.shape[1] // pw):
        cols = slice(hp * pw, (hp + 1) * pw)
        q = q_ref[:, cols]
        k = k_ref[:, cols]
        v = v_ref[:, cols]
        outs = []
        for hh in range(2):
            msk = lo if hh == 0 else jnp.logical_not(lo)
            qm = jnp.where(msk, q, jnp.zeros_like(q)) * jnp.asarray(HEAD_DIM ** -0.5, BF16)
            s = lax.dot_general(qm, k, NT_DIMS, preferred_element_type=F32)
            m = jnp.max(s, axis=-1, keepdims=True)
            p = jnp.exp(s - m)
            l = jnp.sum(p, axis=-1, keepdims=True)
            o = jnp.dot(p.astype(BF16), v, preferred_element_type=F32)
            outs.append(o / l)
        o_ref[:, cols] = jnp.where(lo, outs[0], outs[1]).astype(BF16)


def _ctx_attn(q, k, v, nb, s):
    t, d = q.shape
    spec = pl.BlockSpec((s, d), lambda b: (b, 0))
    return pl.pallas_call(
        _ctx_attn_kernel, grid=(nb,), in_specs=[spec, spec, spec], out_specs=spec,
        out_shape=jax.ShapeDtypeStruct((nb * s, d), BF16),
        compiler_params=_cparams(("parallel",)), name="context_attention",
    )(q, k, v)


NA_QROWS = 16
NA_GROWS = 4
NA_KROWS = 12
NA_HPAIRS = 2


def _na_kernel(rows, q_ref, k_ref, v_ref, kc_ref, vc_ref, bias_ref, o_ref):
    rb = pl.program_id(2)
    ngrp = NA_QROWS // NA_GROWS
    nq = NA_GROWS * GRID_W
    nk = NA_KROWS * GRID_W
    lo = _head_masks()
    pw = 2 * HEAD_DIM
    for gi in range(ngrp):
        g = rb * ngrp + gi
        kr0 = jnp.clip(g * NA_GROWS - WIN_H // 2, 0, rows - NA_KROWS)
        start = pl.multiple_of(kr0 * GRID_W, 256)
        cls = jnp.where(g == 0, 0, jnp.where(g == rows // NA_GROWS - 1, 2, 1))
        qrows = slice(gi * nq, (gi + 1) * nq)
        for pp in range(NA_HPAIRS):
            cols = slice(pp * pw, (pp + 1) * pw)
            q = q_ref[0, qrows, cols]
            kw = k_ref[0, pl.ds(start, nk), cols]
            vw = v_ref[0, pl.ds(start, nk), cols]
            kc = kc_ref[0, :, cols]
            vc = vc_ref[0, :, cols]
            outs = []
            for hh in range(2):
                msk = lo if hh == 0 else jnp.logical_not(lo)
                qm = jnp.where(msk, q, jnp.zeros_like(q)) * jnp.asarray(HEAD_DIM ** -0.5, BF16)
                s = (lax.dot_general(qm, kw, NT_DIMS, preferred_element_type=F32)
                     + bias_ref[cls, 2 * pp + hh])
                sc = lax.dot_general(qm, kc, NT_DIMS, preferred_element_type=F32)
                m = jnp.maximum(jnp.max(s, axis=-1, keepdims=True),
                                jnp.max(sc, axis=-1, keepdims=True))
                p = jnp.exp(s - m)
                pc = jnp.exp(sc - m)
                l = jnp.sum(p, axis=-1, keepdims=True) + jnp.sum(pc, axis=-1, keepdims=True)
                o = (jnp.dot(p.astype(BF16), vw, preferred_element_type=F32)
                     + jnp.dot(pc.astype(BF16), vc, preferred_element_type=F32))
                outs.append(o / l)
            o_ref[0, qrows, cols] = jnp.where(lo, outs[0], outs[1]).astype(BF16)


def _na_row_classes(rows):
    out = []
    for g in (0, 1, rows // NA_GROWS - 1):
        r0 = g * NA_GROWS
        kr0 = int(np.clip(r0 - WIN_H // 2, 0, rows - NA_KROWS))
        table = []
        for rl in range(NA_GROWS):
            r = r0 + rl
            sr = int(np.clip(r - WIN_H // 2, 0, rows - WIN_H))
            table.append([(kr0 + kl - r + WIN_H - 1) if sr <= kr0 + kl < sr + WIN_H else None
                          for kl in range(NA_KROWS)])
        out.append(table)
    return out


def _na_bias_kernel(rows, rpb_ref, o_ref):
    h = pl.program_id(0)
    n_dr, n_dc = 2 * WIN_H - 1, 2 * WIN_W - 1
    cq = lax.broadcasted_iota(I32, (GRID_W, 2 * GRID_W), 0)
    lane = lax.broadcasted_iota(I32, (GRID_W, 2 * GRID_W), 1)
    ck = lane & (GRID_W - 1)
    q_start = jnp.clip(cq - WIN_W // 2, 0, GRID_W - WIN_W)
    col_ok = (ck >= q_start) & (ck < q_start + WIN_W)
    dc = ck - cq + (WIN_W - 1)
    neg = jnp.full((GRID_W, 2 * GRID_W), NEG, F32)
    tiles = []
    for i in range(n_dr):
        t = jnp.zeros((GRID_W, 2 * GRID_W), F32)
        for jj in range(n_dc):
            t = jnp.where(dc == jj, rpb_ref[h * (n_dr * n_dc) + i * n_dc + jj], t)
        tiles.append(jnp.where(col_ok, t, neg))
    first_half = lane < GRID_W
    for c, table in enumerate(_na_row_classes(rows)):
        for rl in range(NA_GROWS):
            for m in range(NA_KROWS // 2):
                ia, ib = table[rl][2 * m], table[rl][2 * m + 1]
                ta = neg if ia is None else tiles[ia]
                tb = neg if ib is None else tiles[ib]
                blk = ta if ia == ib else jnp.where(first_half, ta, tb)
                o_ref[c, 0, rl * GRID_W:(rl + 1) * GRID_W, m * 2 * GRID_W:(m + 1) * 2 * GRID_W] = blk


def _na_bias(rpb, rows):
    nh = rpb.shape[0]
    nq, nk = NA_GROWS * GRID_W, NA_KROWS * GRID_W
    return pl.pallas_call(
        functools.partial(_na_bias_kernel, rows), grid=(nh,),
        in_specs=[pl.BlockSpec(memory_space=pltpu.SMEM)],
        out_specs=pl.BlockSpec((3, 1, nq, nk), lambda h: (0, h, 0, 0)),
        out_shape=jax.ShapeDtypeStruct((3, nh, nq, nk), F32),
        compiler_params=_cparams(("parallel",)), name="na_bias",
    )(rpb.reshape(-1))


def _na_attn(q, k, v, kc, vc, bias_tab, rows, b_off=0):
    s, d = q.shape[1:]
    nb = kc.shape[0]
    lw = NA_HPAIRS * 2 * HEAD_DIM
    hp = d // lw
    nrb = rows // NA_QROWS
    nq = NA_QROWS * GRID_W
    lc = kc.shape[1]
    in_specs = [
        pl.BlockSpec((1, nq, lw), lambda h, b, r: (b + b_off, r, h)),
        pl.BlockSpec((1, s, lw), lambda h, b, r: (b + b_off, 0, h)),
        pl.BlockSpec((1, s, lw), lambda h, b, r: (b + b_off, 0, h)),
        pl.BlockSpec((1, lc, lw), lambda h, b, r: (b, 0, h)),
        pl.BlockSpec((1, lc, lw), lambda h, b, r: (b, 0, h)),
        pl.BlockSpec((3, 2 * NA_HPAIRS) + bias_tab.shape[2:], lambda h, b, r: (0, h, 0, 0)),
    ]
    return pl.pallas_call(
        functools.partial(_na_kernel, rows),
        grid=(hp, nb, nrb), in_specs=in_specs,
        out_specs=pl.BlockSpec((1, nq, lw), lambda h, b, r: (b, r, h)),
        out_shape=jax.ShapeDtypeStruct((nb, s, d), BF16),
        compiler_params=_cparams(("parallel", "parallel", "parallel")),
        name="neighbourhood_attention",
    )(q, k, v, kc, vc, bias_tab)


def _oproj_kernel(nbp, ap_ref, as_ref, x_ref, mod_ref, wout_ref, gffn_ref, wr_hi_ref, wr_lo_ref,
                  rb_ref, x1_ref, h_ref, gates_ref, sel_ref, cnt_ref):
    is_p = pl.program_id(0) < nbp
    for sub in range(x_ref.shape[0] // TM):
        rows = slice(sub * TM, (sub + 1) * TM)
        a = jnp.where(is_p, ap_ref[rows, :], as_ref[rows, :])
        y = jnp.dot(a, wout_ref[...], preferred_element_type=F32)
        x1 = x_ref[rows, :] + mod_ref[0, 2:3, :] * y
        x1_ref[rows, :] = x1
        _ffn_pre(x1, mod_ref, gffn_ref, wr_hi_ref, wr_lo_ref, rb_ref,
                 rows, sub, h_ref, gates_ref, sel_ref, cnt_ref)


def _oproj(attn_p, attn_s, x, mod, w_out, g_ffn, wr_hi, wr_lo, rbias, nbp, bps):
    t, d = x.shape
    nsub = PRE_BLOCKS if nbp % PRE_BLOCKS == 0 and bps % PRE_BLOCKS == 0 else 1
    ts = nsub * TM
    nbp, bps = nbp // nsub, bps // nsub
    midx = _mod_index(nbp, bps)
    in_specs = [
        pl.BlockSpec((ts, d), lambda i: (jnp.minimum(i, nbp - 1), 0)),
        pl.BlockSpec((ts, d), lambda i: (jnp.maximum(i - nbp, 0), 0)),
        pl.BlockSpec((ts, d), lambda i: (i, 0)),
        pl.BlockSpec((1, 6, d), lambda i: (midx(i), 0, 0)),
        _const_spec(w_out.shape), _const_spec(g_ffn.shape), _const_spec(wr_hi.shape),
        _const_spec(wr_lo.shape), _const_spec(rbias.shape),
    ]
    out_specs, out_shape = _pre_out_specs(t, d, nsub)
    return pl.pallas_call(
        functools.partial(_oproj_kernel, nbp), grid=(t // ts,), in_specs=in_specs,
        out_specs=out_specs, out_shape=out_shape, compiler_params=_cparams(("parallel",)),
        name="attn_out_proj",
    )(attn_p, attn_s, x, mod, w_out, g_ffn, wr_hi, wr_lo, rbias)


def _router_weights(w_router, router_bias):
    d, ne = w_router.shape
    w = jnp.pad(w_router, ((0, 0), (0, 128 - ne)))
    hi = w.astype(BF16)
    lo = (w - hi.astype(F32)).astype(BF16)
    return hi, lo, router_bias.reshape(ne, 1)


def _shared_weights(s_gate, s_up, s_down):
    return jnp.concatenate([s_gate, s_up], axis=-1).astype(BF16), s_down.astype(BF16)


def _layer_weights(layer, d, norm_mix, norm_ffn, pc, na, moe):
    i = layer // 2
    w = dict(g_mix=norm_mix[layer].reshape(1, d), g_ffn=norm_ffn[layer].reshape(1, d))
    w["wr_hi"], w["wr_lo"], w["rbias"] = _router_weights(moe["router"][layer],
                                                         moe["router_bias"][layer])
    w["sgu"], w["sd"] = _shared_weights(moe["s_gate"][layer], moe["s_up"][layer],
                                        moe["s_down"][layer])
    if layer % 2 == 0:
        w.update(w_in=pc["w_in"][i].astype(BF16), pool_w=pc["pool_w"][i].astype(BF16),
                 pool_scale=pc["pool_scale"][i].reshape(1, -1), conv_w=pc["conv_w"][i],
                 w_out=pc["w_out"][i].astype(BF16))
    else:
        nh = d // HEAD_DIM
        head_of = np.arange(d) // HEAD_DIM
        w.update(hm=jnp.asarray(head_of[:, None] == np.arange(128)[None, :], BF16),
                 hmt=jnp.asarray(np.arange(128)[:, None] == head_of[None, :], BF16),
                 qg=jnp.tile(na["q_norm"][i], nh).reshape(1, d),
                 kg=jnp.tile(na["k_norm"][i], nh).reshape(1, d),
                 w_qkv=na["w_qkv"][i].astype(BF16), w_out=na["w_out"][i].astype(BF16))
    return w


def kernel(x_prompt, x_sample, cache_k, cache_v, c, c_ctx, ada_w, ada_b, norm_mix, norm_ffn,
           pc_w_in, pc_pool_w, pc_pool_scale, pc_conv_w, pc_w_out,
           na_w_qkv, na_q_norm, na_k_norm, na_rpb, na_w_out,
           moe_router, moe_router_bias, moe_w_gate, moe_w_up, moe_w_down,
           moe_shared_gate, moe_shared_up, moe_shared_down):
    nb_p, s_p, d = x_prompt.shape
    nb_s, s_s, _ = x_sample.shape
    assert s_p == TM and s_s % TM == 0 and nb_p > 0 and nb_s > 0
    tp, ts = nb_p * s_p, nb_s * s_s
    nbp, bps = tp // TM, s_s // TM
    depth = ada_w.shape[0]
    nh = d // HEAD_DIM
    rows = s_s // GRID_W

    cond = jnp.concatenate([c_ctx[None], c], axis=0)
    cond = jnp.pad(cond, ((0, -cond.shape[0] % 8), (0, 0)))
    mods = _modulation(cond, ada_w, ada_b).reshape(depth, cond.shape[0], 6, d)
    pc = dict(w_in=pc_w_in, pool_w=pc_pool_w, pool_scale=pc_pool_scale, conv_w=pc_conv_w,
              w_out=pc_w_out)
    na = dict(w_qkv=na_w_qkv, q_norm=na_q_norm, k_norm=na_k_norm, w_out=na_w_out)
    moe = dict(router=moe_router, router_bias=moe_router_bias, s_gate=moe_shared_gate,
               s_up=moe_shared_up, s_down=moe_shared_down)

    xp, xs = x_prompt.reshape(tp, d), x_sample.reshape(ts, d)
    x = None
    ready_qkv = None
    new_k, new_v = [], []
    weights = [_layer_weights(l, d, norm_mix, norm_ffn, pc, na, moe) for l in range(depth)]
    for layer in range(depth):
        i = layer // 2
        w = weights[layer]
        mod = mods[layer]
        if layer % 2 == 0:
            if x is not None:
                xp, xs = x[:tp], x[tp:]
            x, h, gates_t, sel_t, cnt = _mixer(
                xp, xs, mod, w["g_mix"], w["w_in"], w["pool_w"], w["pool_scale"], w["conv_w"],
                w["w_out"], w["g_ffn"], w["wr_hi"], w["wr_lo"], w["rbias"], nbp, bps, s_p, s_s)
        else:
            if ready_qkv is not None and tp % s_s == 0:
                q_all, k_all, v_all, kp32, vp32 = ready_qkv
                qp, kp, vp = q_all, k_all, v_all
                seqs = (tp + ts) // s_s
                qs, ks, vs = (a.reshape(seqs, s_s, d) for a in (q_all, k_all, v_all))
                b_off = tp // s_s
            else:
                midx = _mod_index(nbp, bps)
                qkv_args = (x, mod, w["g_mix"], w["w_qkv"], w["qg"], w["kg"], w["hm"], w["hmt"])
                qp, kp, vp, kp32, vp32 = _qkv(*qkv_args, 0, nbp, midx, True)
                qs, ks, vs = (a.reshape(nb_s, s_s, d)
                              for a in _qkv(*qkv_args, nbp, ts // TM, midx, False))
                b_off = 0
            new_k.append(kp32.reshape(nb_p, s_p, nh, HEAD_DIM))
            new_v.append(vp32.reshape(nb_p, s_p, nh, HEAD_DIM))
            a_p = _ctx_attn(qp, kp, vp, nb_p, s_p)
            lc = cache_k.shape[2]
            kc = cache_k[:, i].reshape(nb_s, lc, d).astype(BF16)
            vc = cache_v[:, i].reshape(nb_s, lc, d).astype(BF16)
            a_s = _na_attn(qs, ks, vs, kc, vc, _na_bias(na_rpb[i], rows), rows, b_off)
            x, h, gates_t, sel_t, cnt = _oproj(a_p, a_s.reshape(ts, d), x, mod, w["w_out"],
                                               w["g_ffn"], w["wr_hi"], w["wr_lo"], w["rbias"],
                                               nbp, bps)
        ready_qkv = None
        if layer + 1 < depth and (layer + 1) % 2 == 1:
            w1 = weights[layer + 1]
            finish = lambda yg, w8: _combine_qkv(
                yg, w8, h, x, mod, w["sgu"], w["sd"], mods[layer + 1], w1["g_mix"], w1["w_qkv"],
                w1["qg"], w1["kg"], w1["hm"], w1["hmt"], nbp, bps)
            x, *ready_qkv = _sparse_moe(x, h, gates_t, sel_t, cnt, layer, moe_w_gate, moe_w_up,
                                        moe_w_down, finish)
        else:
            finish = lambda yg, w8: _combine(yg, w8, h, x, mod, w["sgu"], w["sd"], nbp, bps,
                                             layer == depth - 1)
            x = _sparse_moe(x, h, gates_t, sel_t, cnt, layer, moe_w_gate, moe_w_up, moe_w_down,
                            finish)
    y_p, y_s = x
    return (y_p.reshape(nb_p, s_p, d), y_s.reshape(nb_s, s_s, d),
            jnp.stack(new_k, axis=1), jnp.stack(new_v, axis=1))
```

```python
import functools

import numpy as np
import jax
import jax.numpy as jnp
from jax import lax
from jax.experimental import pallas as pl
from jax.experimental.pallas import tpu as pltpu
from jax.experimental.pallas import tpu_sc as plsc

F32 = jnp.float32
BF16 = jnp.bfloat16
I32 = jnp.int32
U32 = jnp.uint32

TM = 256
HALO = 8
POOL_WINDOWS = (2, 4, 8, 16)
N_EXPERTS = 64
N_GROUPS = 8
GROUP_SIZE = N_EXPERTS // N_GROUPS
TOPK_GROUPS = 4
TOPK = 8
ROUTED_SCALE = 2.5
EPS = 1e-6
GRID_W = 64
WIN_H = 8
WIN_W = 16
HEAD_DIM = 64
NEG = float(np.finfo(np.float32).min)
VMEM_LIMIT = 56 * 1024 * 1024
NT_DIMS = (((1,), (1,)), ((), ()))
TG = 1024
SC_WINDOW = 64
SC_WORKERS = 32


def _cparams(sem):
    return pltpu.CompilerParams(dimension_semantics=sem, vmem_limit_bytes=VMEM_LIMIT)


def _silu(x):
    return x * jax.nn.sigmoid(x)


def _split_bf16(x):
    hi = x.astype(BF16)
    lo = (x - hi.astype(F32)).astype(BF16)
    return hi, lo


def _adaln(x, g, shift, scale):
    ms = jnp.mean(x * x, axis=-1, keepdims=True)
    return (x * lax.rsqrt(ms + EPS)) * g * (1.0 + scale) + shift


def _pack_pair(x):
    w = x.shape[1] // 2
    lo = lax.bitcast_convert_type(x[:, :w].astype(BF16).astype(F32), U32) >> 16
    hi = lax.bitcast_convert_type(x[:, w:].astype(BF16).astype(F32), U32)
    return lax.bitcast_convert_type(lo | hi, I32)


def _unpack_pair(p):
    u = lax.bitcast_convert_type(p, U32)
    lo = lax.bitcast_convert_type(u << 16, F32)
    hi = lax.bitcast_convert_type(u & jnp.uint32(0xFFFF0000), F32)
    return jnp.concatenate([lo, hi], axis=-1)


def _mod_kernel(cond_ref, w_ref, b_ref, o_ref):
    c = cond_ref[...]
    a = _silu(c)
    o_ref[0] = jnp.dot(a, w_ref[0], preferred_element_type=F32,
                       precision=lax.Precision.HIGHEST) + b_ref[0]


def _modulation(cond, ada_w, ada_b):
    depth, d, n = ada_w.shape
    rows = cond.shape[0]
    tn = 1536
    return pl.pallas_call(
        _mod_kernel,
        grid=(depth, n // tn),
        in_specs=[
            pl.BlockSpec((rows, d), lambda l, j: (0, 0)),
            pl.BlockSpec((1, d, tn), lambda l, j: (l, 0, j)),
            pl.BlockSpec((1, 1, tn), lambda l, j: (l, 0, j)),
        ],
        out_specs=pl.BlockSpec((1, rows, tn), lambda l, j: (l, 0, j)),
        out_shape=jax.ShapeDtypeStruct((depth, rows, n), F32),
        compiler_params=_cparams(("arbitrary", "arbitrary")),
        name="modulation",
    )(cond, ada_w, ada_b.reshape(depth, 1, n))


def _route(logits_t, bias_col):
    tm = logits_t.shape[1]
    scores = jax.nn.sigmoid(logits_t)
    biased = scores + bias_col
    sub = lax.broadcasted_iota(I32, (GROUP_SIZE, tm), 0).astype(F32)
    ninf = jnp.float32(-jnp.inf)
    groups, gscore = [], []
    for g in range(N_GROUPS):
        v = biased[g * GROUP_SIZE:(g + 1) * GROUP_SIZE]
        m1 = jnp.max(v, axis=0, keepdims=True)
        first = jnp.min(jnp.where(v == m1, sub, float(GROUP_SIZE)), axis=0, keepdims=True)
        m2 = jnp.max(jnp.where(sub == first, ninf, v), axis=0, keepdims=True)
        groups.append(v)
        gscore.append(m1 + m2)
    masked = []
    for g in range(N_GROUPS):
        rank = jnp.zeros((1, tm), I32)
        for g2 in range(N_GROUPS):
            if g2 == g:
                continue
            ahead = gscore[g2] > gscore[g]
            if g2 < g:
                ahead = ahead | (gscore[g2] == gscore[g])
            rank = rank + ahead.astype(I32)
        masked.append(jnp.where(rank < TOPK_GROUPS, groups[g], ninf))
    masked = jnp.concatenate(masked, axis=0)
    eidx = lax.broadcasted_iota(I32, (N_EXPERTS, tm), 0).astype(F32)
    sel = jnp.zeros((N_EXPERTS, tm), jnp.bool_)
    for _ in range(TOPK):
        best = jnp.max(masked, axis=0, keepdims=True)
        first = jnp.min(jnp.where(masked == best, eidx, float(N_EXPERTS)), axis=0, keepdims=True)
        hit = eidx == first
        sel = sel | hit
        masked = jnp.where(hit, ninf, masked)
    w = jnp.where(sel, scores, 0.0)
    wsum = jnp.sum(w, axis=0, keepdims=True)
    return w / wsum * ROUTED_SCALE, sel


def _ffn_pre(x1, mod_ref, gffn_ref, wr_hi_ref, wr_lo_ref, rb_ref,
             rows, sub, h_ref, gates_ref, sel_ref, cnt_ref):
    sh2 = mod_ref[0, 3:4, :]
    sc2 = mod_ref[0, 4:5, :]
    h = _adaln(x1, gffn_ref[...], sh2, sc2)
    h_hi, h_lo = _split_bf16(h)
    h_ref[rows, :] = _pack_pair(h_hi)
    wr_hi = wr_hi_ref[...]
    logits = (jnp.dot(h_hi, wr_hi, preferred_element_type=F32)
              + jnp.dot(h_hi, wr_lo_ref[...], preferred_element_type=F32)
              + jnp.dot(h_lo, wr_hi, preferred_element_type=F32))
    gates_t, sel = _route(logits.T[:N_EXPERTS], rb_ref[...])
    gates_ref[:, rows] = gates_t
    sel_b = sel.astype(F32).astype(BF16)
    sel_ref[:, rows] = sel_b
    ones = jnp.ones((8, sel_b.shape[1]), BF16)
    cnt_ref[sub] = lax.dot_general(ones, sel_b, NT_DIMS, preferred_element_type=F32)


PRE_BLOCKS = 2


def _pre_out_specs(t, d, nsub):
    tm = nsub * TM
    specs = [
        pl.BlockSpec((tm, d), lambda i: (i, 0)),
        pl.BlockSpec((tm, d // 2), lambda i: (i, 0)),
        pl.BlockSpec((N_EXPERTS, tm), lambda i: (0, i)),
        pl.BlockSpec((N_EXPERTS, tm), lambda i: (0, i)),
        pl.BlockSpec((nsub, 8, N_EXPERTS), lambda i: (i, 0, 0)),
    ]
    shapes = [
        jax.ShapeDtypeStruct((t, d), F32),
        jax.ShapeDtypeStruct((t, d // 2), I32),
        jax.ShapeDtypeStruct((N_EXPERTS, t), F32),
        jax.ShapeDtypeStruct((N_EXPERTS, t), BF16),
        jax.ShapeDtypeStruct((t // TM, 8, N_EXPERTS), F32),
    ]
    return specs, shapes


def _mixer_kernel(nbp, bps, sp, ss,
                  xpc_ref, xsc_ref, xprev_ref, xnext_ref, mod_ref, gmix_ref, win_ref, pw_ref, ps_ref,
                  cw_ref,
                  wout_ref, gffn_ref, wr_hi_ref, wr_lo_ref, rb_ref,
                  x1_ref, h_ref, gates_ref, sel_ref, cnt_ref):
    i = pl.program_id(0)
    nsub = xpc_ref.shape[0] // TM
    is_p = i * nsub < nbp
    sh1 = mod_ref[0, 0:1, :]
    sc1 = mod_ref[0, 1:2, :]
    g1 = mod_ref[0, 2:3, :]
    next_ = TM + 2 * HALO

    def up(a, k):
        return pltpu.roll(a, next_ - k, 0)

    x_all = jnp.where(is_p, xpc_ref[...], xsc_ref[...])
    for sub in range(nsub):
        rows = slice(sub * TM, (sub + 1) * TM)
        j = lax.rem(jnp.maximum(i * nsub + sub - nbp, 0), bps)
        first = is_p | (j == 0)
        last = is_p | (j == bps - 1)
        base = jnp.where(is_p, 0, j * TM)
        slen = jnp.where(is_p, sp, ss)
        xc = x_all[rows]
        x_prev = xprev_ref[...] if sub == 0 else x_all[sub * TM - HALO:sub * TM]
        x_next = (xnext_ref[...] if sub == nsub - 1
                  else x_all[(sub + 1) * TM:(sub + 1) * TM + HALO])
        x_ext = jnp.concatenate([x_prev, xc, x_next], axis=0)
        h_ext = _adaln(x_ext, gmix_ref[...], sh1, sc1).astype(BF16)
        u = jnp.dot(h_ext, win_ref[...], preferred_element_type=F32)
        row = lax.broadcasted_iota(I32, (next_, 1), 0)
        keep = (((row >= HALO) | jnp.logical_not(first))
                & ((row < HALO + TM) | jnp.logical_not(last)))
        u = jnp.where(keep, u, 0.0)

        dm = u.shape[1] // 4
        ua = u[:, :dm]
        gate_b = u[HALO:HALO + TM, dm:2 * dm]
        z = u[:, 2 * dm:3 * dm] * u[:, 3 * dm:]

        pos = base + lax.broadcasted_iota(I32, (TM, 1), 0)
        pg = dm // len(POOL_WINDOWS)
        ya = []
        for g, w in enumerate(POOL_WINDOWS):
            e = ua[:, g * pg:(g + 1) * pg]
            acc = e
            span = 1
            while span < w:
                acc = acc + up(acc, span)
                span *= 2
            off = HALO - w // 2
            wsum = (up(acc, off) if off else acc)[:TM]
            lo = jnp.maximum(pos - w // 2, 0)
            hi = jnp.minimum(pos + (w - w // 2 - 1), slen - 1)
            cnt = (hi - lo + 1).astype(F32)
            diff = wsum / cnt - e[HALO:HALO + TM]
            ya.append(jnp.dot(diff.astype(BF16), pw_ref[g], preferred_element_type=F32))
        y_a = jnp.concatenate(ya, axis=-1) * ps_ref[...]
        zc = (cw_ref[0:1, :] * up(z, HALO - 1)[:TM] + cw_ref[1:2, :] * z[HALO:HALO + TM]
              + cw_ref[2:3, :] * up(z, HALO + 1)[:TM])
        y_b = gate_b * zc
        ycat = jnp.concatenate([y_a, y_b], axis=-1).astype(BF16)
        y = jnp.dot(ycat, wout_ref[...], preferred_element_type=F32)
        x1 = xc + g1 * y
        x1_ref[rows, :] = x1
        _ffn_pre(x1, mod_ref, gffn_ref, wr_hi_ref, wr_lo_ref, rb_ref,
                 rows, sub, h_ref, gates_ref, sel_ref, cnt_ref)


def _mod_index(nbp, bps):
    def f(i):
        return jnp.where(i < nbp, 0, 1 + jnp.maximum(i - nbp, 0) // bps)
    return f


def _const_spec(shape):
    nd = len(shape)
    return pl.BlockSpec(shape, lambda i: (0,) * nd)


def _mixer(xp, xs, mod, g_mix, w_in, pool_w, pool_scale, conv_w, w_out, g_ffn, wr_hi, wr_lo, rbias,
           nbp, bps, sp, ss):
    d = xp.shape[1]
    t = xp.shape[0] + xs.shape[0]
    nsub = PRE_BLOCKS if nbp % PRE_BLOCKS == 0 and bps % PRE_BLOCKS == 0 else 1
    tm = nsub * TM
    nblk = t // tm
    nbp2 = nbp // nsub
    midx = _mod_index(nbp2, bps // nsub)
    hpb = tm // HALO
    nh = xs.shape[0] // HALO
    in_specs = [
        pl.BlockSpec((tm, d), lambda i: (jnp.minimum(i, nbp2 - 1), 0)),
        pl.BlockSpec((tm, d), lambda i: (jnp.maximum(i - nbp2, 0), 0)),
        pl.BlockSpec((HALO, d), lambda i: (jnp.maximum((i - nbp2) * hpb - 1, 0), 0)),
        pl.BlockSpec((HALO, d), lambda i: (jnp.clip((i - nbp2 + 1) * hpb, 0, nh - 1), 0)),
        pl.BlockSpec((1, 6, d), lambda i: (midx(i), 0, 0)),
        _const_spec(g_mix.shape), _const_spec(w_in.shape), _const_spec(pool_w.shape),
        _const_spec(pool_scale.shape), _const_spec(conv_w.shape), _const_spec(w_out.shape),
        _const_spec(g_ffn.shape), _const_spec(wr_hi.shape), _const_spec(wr_lo.shape),
        _const_spec(rbias.shape),
    ]
    out_specs, out_shape = _pre_out_specs(t, d, nsub)
    return pl.pallas_call(
        functools.partial(_mixer_kernel, nbp, bps, sp, ss),
        grid=(nblk,), in_specs=in_specs, out_specs=out_specs, out_shape=out_shape,
        compiler_params=_cparams(("parallel",)), name="pool_conv_mixer",
    )(xp, xs, xs, xs, mod, g_mix, w_in, pool_w, pool_scale, conv_w, w_out, g_ffn, wr_hi, wr_lo,
      rbias)


SLOT_BLOCKS = 4


def _slots_kernel(sel_ref, gates_ref, base_ref, pos_ref, w_ref):
    r = lax.broadcasted_iota(I32, (TM, TM), 0)
    c = lax.broadcasted_iota(I32, (TM, TM), 1)
    before = (r < c).astype(F32).astype(BF16)
    er = lax.broadcasted_iota(I32, (N_EXPERTS, N_EXPERTS), 0)
    ec = lax.broadcasted_iota(I32, (N_EXPERTS, N_EXPERTS), 1)
    lower = (ec < er).astype(F32).astype(BF16)
    sub = lax.broadcasted_iota(I32, (TOPK, TM), 0)
    for blk in range(sel_ref.shape[1] // TM):
        cols = slice(blk * TM, (blk + 1) * TM)
        sel = sel_ref[:, cols]
        rank_tok = jnp.dot(sel, before, preferred_element_type=F32)
        rank_exp = jnp.dot(lower, sel, preferred_element_type=F32)
        slot = base_ref[blk] + rank_tok
        chosen = sel > 0
        gates = gates_ref[:, cols]
        pos8 = jnp.zeros((TOPK, TM), F32)
        w8 = jnp.zeros((TOPK, TM), F32)
        for k in range(TOPK):
            mk = chosen & (rank_exp == float(k))
            pk = jnp.sum(jnp.where(mk, slot, 0.0), axis=0, keepdims=True)
            wk = jnp.sum(jnp.where(mk, gates, 0.0), axis=0, keepdims=True)
            pos8 = jnp.where(sub == k, pk, pos8)
            w8 = jnp.where(sub == k, wk, w8)
        pos_ref[:, cols] = pos8.astype(I32)
        w8 = jnp.concatenate([w8, jnp.zeros((128 - TOPK, TM), F32)], axis=0)
        w_ref[cols, :] = w8.T


def _slots(sel_t, gates_t, base):
    ne, t = sel_t.shape
    nblk = next(n for n in range(SLOT_BLOCKS, 0, -1) if (t // TM) % n == 0)
    ts = nblk * TM
    return pl.pallas_call(
        _slots_kernel, grid=(t // ts,),
        in_specs=[
            pl.BlockSpec((ne, ts), lambda i: (0, i)),
            pl.BlockSpec((ne, ts), lambda i: (0, i)),
            pl.BlockSpec((nblk, ne, 1), lambda i: (i, 0, 0)),
        ],
        out_specs=[pl.BlockSpec((TOPK, ts), lambda i: (0, i)),
                   pl.BlockSpec((ts, 128), lambda i: (i, 0))],
        out_shape=[jax.ShapeDtypeStruct((TOPK, t), I32), jax.ShapeDtypeStruct((t, 128), F32)],
        compiler_params=_cparams(("parallel",)), name="moe_slots",
    )(sel_t, gates_t, base)


def _sc_gather(table, idx):
    m = idx.shape[0]
    d = table.shape[1]
    assert m % (SC_WINDOW * SC_WORKERS) == 0
    mesh = plsc.VectorSubcoreMesh(core_axis_name="core", subcore_axis_name="subcore")

    @pl.kernel(out_type=jax.ShapeDtypeStruct((m, d), table.dtype), mesh=mesh)
    def gather_rows(x_hbm, i_hbm, o_hbm):
        def body(i_vmem, o_vmem):
            pltpu.sync_copy(x_hbm.at[i_vmem.at[0]], o_vmem)

        pltpu.emit_pipeline(
            body, grid=(m // SC_WINDOW,),
            in_specs=[pl.BlockSpec((1, SC_WINDOW), index_map=lambda i: (i, 0))],
            out_specs=[pl.BlockSpec((SC_WINDOW, d), index_map=lambda i: (i, 0))],
            core_axis_name=("core", "subcore"),
            dimension_semantics=(pltpu.PARALLEL,),
        )(i_hbm, o_hbm)

    return gather_rows(table, idx.reshape(m // SC_WINDOW, SC_WINDOW))


def _sc_scatter(rows, idx, n_out):
    nk, t = idx.shape
    d = rows.shape[1]
    assert t % (SC_WINDOW * SC_WORKERS) == 0
    mesh = plsc.VectorSubcoreMesh(core_axis_name="core", subcore_axis_name="subcore")

    @pl.kernel(out_type=jax.ShapeDtypeStruct((n_out, d), rows.dtype), mesh=mesh)
    def scatter_rows(x_hbm, i_hbm, o_hbm):
        def body(x_vmem, i_vmem):
            for k in range(nk):
                pltpu.sync_copy(x_vmem, o_hbm.at[i_vmem.at[k, 0]])

        pltpu.emit_pipeline(
            body, grid=(t // SC_WINDOW,),
            in_specs=[pl.BlockSpec((SC_WINDOW, d), index_map=lambda i: (i, 0)),
                      pl.BlockSpec((nk, 1, SC_WINDOW), index_map=lambda i: (0, i, 0))],
            out_specs=[],
            core_axis_name=("core", "subcore"),
            dimension_semantics=(pltpu.PARALLEL,),
        )(x_hbm, i_hbm)

    return scatter_rows(rows, idx.reshape(nk, t // SC_WINDOW, SC_WINDOW))


def _expert_ffn_kernel(te_ref, fresh_ref, nv_ref, x_hbm, wg_ref, wu_ref, wd_ref, y_hbm,
                       xbuf, xsem, ybuf, ysem, wg_s, wu_s, wd_s):
    j = pl.program_id(0)
    nv = nv_ref[0]
    last_step = pl.num_programs(0) - 1

    def out_copy(tile, slot):
        rows = pl.ds(pl.multiple_of(tile * TG, TG), TG)
        return pltpu.make_async_copy(ybuf.at[slot], y_hbm.at[rows, :], ysem.at[slot])

    @pl.when((j >= 2) & (j - 2 < nv))
    def _():
        out_copy(j - 2, lax.rem(j, 2)).wait()

    def tile_copy(tile, slot):
        rows = pl.ds(pl.multiple_of(tile * TG, TG), TG)
        return pltpu.make_async_copy(x_hbm.at[rows, :], xbuf.at[slot], xsem.at[slot])

    @pl.when(j == 0)
    def _():
        tile_copy(0, 0).start()

        @pl.when(nv > 1)
        def _():
            tile_copy(1, 1).start()

    @pl.when(j < nv)
    def _():
        @pl.when(j + 2 < nv)
        def _():
            tile_copy(j + 2, lax.rem(j + 2, 3)).start()

        slot = lax.rem(j, 3)
        tile_copy(j, slot).wait()

        @pl.when(fresh_ref[j] == 1)
        def _():
            wg_s[...] = wg_ref[0, 0].astype(BF16)
            wu_s[...] = wu_ref[0, 0].astype(BF16)
            wd_s[...] = wd_ref[0, 0].astype(BF16)

        x = _unpack_pair(xbuf[slot]).astype(BF16)
        a = jnp.dot(x, wg_s[...], preferred_element_type=F32)
        b = jnp.dot(x, wu_s[...], preferred_element_type=F32)
        act = (_silu(a) * b).astype(BF16)
        oslot = lax.rem(j, 2)
        ybuf[oslot] = _pack_pair(jnp.dot(act, wd_s[...], preferred_element_type=F32))
        out_copy(j, oslot).start()

    @pl.when(j == last_step)
    def _():
        @pl.when((j >= 1) & (j - 1 < nv))
        def _():
            out_copy(j - 1, lax.rem(j + 1, 2)).wait()

        @pl.when(j < nv)
        def _():
            out_copy(j, lax.rem(j, 2)).wait()


def _expert_ffn(xs, n_tiles, tile_expert, fresh, n_valid, layer, w_gate, w_up, w_down):
    dh = xs.shape[1]
    wspec = lambda w: pl.BlockSpec((1, 1) + w.shape[2:],
                                   lambda j, te, fr, nv: (layer, te[j], 0, 0))
    grid_spec = pltpu.PrefetchScalarGridSpec(
        num_scalar_prefetch=3, grid=(n_tiles,),
        in_specs=[pl.BlockSpec(memory_space=pl.ANY), wspec(w_gate), wspec(w_up), wspec(w_down)],
        out_specs=pl.BlockSpec(memory_space=pl.ANY),
        scratch_shapes=[pltpu.VMEM((3, TG, dh), xs.dtype), pltpu.SemaphoreType.DMA((3,)),
                        pltpu.VMEM((2, TG, dh), I32), pltpu.SemaphoreType.DMA((2,)),
                        pltpu.VMEM(w_gate.shape[2:], BF16), pltpu.VMEM(w_up.shape[2:], BF16),
                        pltpu.VMEM(w_down.shape[2:], BF16)],
    )
    return pl.pallas_call(
        _expert_ffn_kernel, grid_spec=grid_spec,
        out_shape=jax.ShapeDtypeStruct((n_tiles * TG, dh), I32),
        compiler_params=_cparams(("arbitrary",)), name="expert_ffn",
    )(tile_expert, fresh, n_valid, xs, w_gate, w_up, w_down)


def _combine_value(yg_ref, w_ref, h_ref, x_ref, mod_ref, sgu_ref, sd_ref):
    f = sd_ref.shape[0]
    h = _unpack_pair(h_ref[...]).astype(BF16)
    hs = jnp.dot(h, sgu_ref[...], preferred_element_type=F32)
    act = (_silu(hs[:, :f]) * hs[:, f:]).astype(BF16)
    acc = jnp.dot(act, sd_ref[...], preferred_element_type=F32)
    w = w_ref[...]
    for k in range(TOPK):
        acc = acc + w[:, k:k + 1] * _unpack_pair(yg_ref[k])
    return x_ref[...] + mod_ref[0, 5:6, :] * acc


def _combine_kernel(nbp, yg_ref, w_ref, h_ref, x_ref, mod_ref, sgu_ref, sd_ref, *o_refs):
    out = _combine_value(yg_ref, w_ref, h_ref, x_ref, mod_ref, sgu_ref, sd_ref)
    if len(o_refs) == 1:
        o_refs[0][...] = out
    else:
        i = pl.program_id(0)

        @pl.when(i < nbp)
        def _():
            o_refs[0][...] = out

        @pl.when(i >= nbp)
        def _():
            o_refs[1][...] = out


def _combine(yg, w8, h, x, mod, sgu, sd, nbp, bps, split_out):
    t, d = x.shape
    tc = 2 * TM
    assert (nbp * TM) % tc == 0 and (bps * TM) % tc == 0
    nbp, bps = nbp * TM // tc, bps * TM // tc
    midx = _mod_index(nbp, bps)
    if split_out:
        out_specs = [pl.BlockSpec((tc, d), lambda i: (jnp.minimum(i, nbp - 1), 0)),
                     pl.BlockSpec((tc, d), lambda i: (jnp.maximum(i - nbp, 0), 0))]
        out_shape = [jax.ShapeDtypeStruct((nbp * tc, d), F32),
                     jax.ShapeDtypeStruct((t - nbp * tc, d), F32)]
    else:
        out_specs = pl.BlockSpec((tc, d), lambda i: (i, 0))
        out_shape = jax.ShapeDtypeStruct((t, d), F32)
    in_specs = [
        pl.BlockSpec((TOPK, tc, d // 2), lambda i: (0, i, 0)),
        pl.BlockSpec((tc, 128), lambda i: (i, 0)),
        pl.BlockSpec((tc, d // 2), lambda i: (i, 0)),
        pl.BlockSpec((tc, d), lambda i: (i, 0)),
        pl.BlockSpec((1, 6, d), lambda i: (midx(i), 0, 0)),
        _const_spec(sgu.shape), _const_spec(sd.shape),
    ]
    return pl.pallas_call(
        functools.partial(_combine_kernel, nbp), grid=(t // tc,), in_specs=in_specs,
        out_specs=out_specs, out_shape=out_shape,
        compiler_params=_cparams(("arbitrary",)), name="moe_combine",
    )(yg, w8, h, x, mod, sgu, sd)


def _combine_qkv_kernel(nbp, yg_ref, w_ref, h_ref, x_ref, mod_ref, sgu_ref, sd_ref,
                        mod1_ref, gmix_ref, wqkv_ref, qg_ref, kg_ref, hm_ref, hmt_ref,
                        x_out, q_out, k_out, v_out, k32_out, v32_out):
    out = _combine_value(yg_ref, w_ref, h_ref, x_ref, mod_ref, sgu_ref, sd_ref)
    x_out[...] = out
    q, k, v = _qkv_values(out, mod1_ref, gmix_ref, wqkv_ref, qg_ref, kg_ref, hm_ref, hmt_ref)
    q_out[...] = q.astype(BF16)
    k_out[...] = k.astype(BF16)
    v_out[...] = v.astype(BF16)

    @pl.when(pl.program_id(0) < nbp)
    def _():
        k32_out[...] = k
        v32_out[...] = v


def _combine_qkv(yg, w8, h, x, mod, sgu, sd, mod1, g_mix, w_qkv, qg, kg, hm, hmt, nbp, bps):
    t, d = x.shape
    midx = _mod_index(nbp, bps)
    row = lambda i: (i, 0)
    in_specs = [
        pl.BlockSpec((TOPK, TM, d // 2), lambda i: (0, i, 0)),
        pl.BlockSpec((TM, 128), row),
        pl.BlockSpec((TM, d // 2), row),
        pl.BlockSpec((TM, d), row),
        pl.BlockSpec((1, 6, d), lambda i: (midx(i), 0, 0)),
        _const_spec(sgu.shape), _const_spec(sd.shape),
        pl.BlockSpec((1, 6, d), lambda i: (midx(i), 0, 0)),
        _const_spec(g_mix.shape), _const_spec(w_qkv.shape), _const_spec(qg.shape),
        _const_spec(kg.shape), _const_spec(hm.shape), _const_spec(hmt.shape),
    ]
    pinned = pl.BlockSpec((TM, d), lambda i: (jnp.minimum(i, nbp - 1), 0))
    out_specs = [pl.BlockSpec((TM, d), row)] * 4 + [pinned, pinned]
    out_shape = ([jax.ShapeDtypeStruct((t, d), F32)] + [jax.ShapeDtypeStruct((t, d), BF16)] * 3
                 + [jax.ShapeDtypeStruct((nbp * TM, d), F32)] * 2)
    return pl.pallas_call(
        functools.partial(_combine_qkv_kernel, nbp), grid=(t // TM,), in_specs=in_specs,
        out_specs=out_specs, out_shape=out_shape,
        compiler_params=_cparams(("arbitrary",)), name="moe_combine_qkv",
    )(yg, w8, h, x, mod, sgu, sd, mod1, g_mix, w_qkv, qg, kg, hm, hmt)


def _sparse_moe(x, h, gates_t, sel_t, cnt, layer, w_gate, w_up, w_down, finish):
    t, d = x.shape
    ne = N_EXPERTS
    n_pad = ne * TG
    n_tiles = (t * TOPK + n_pad) // TG
    n_slots = n_tiles * TG
    counts = cnt[:, 0, :].astype(I32)
    per_expert = jnp.sum(counts, axis=0)
    padded = (per_expert + TG - 1) // TG * TG
    ends = jnp.cumsum(padded)
    starts = ends - padded
    block_off = jnp.cumsum(counts, axis=0) - counts
    base = (starts[None, :] + block_off).astype(F32)[:, :, None]
    n_valid = (ends[-1] // TG).astype(I32).reshape(1)
    tile_start = jnp.minimum(jnp.arange(n_tiles, dtype=I32), n_valid[0] - 1) * TG
    tile_expert = jnp.sum((ends[None, :] <= tile_start[:, None]).astype(I32), axis=1)
    fresh = jnp.concatenate([jnp.ones((1,), I32),
                             (tile_expert[1:] != tile_expert[:-1]).astype(I32)])

    pos8, w8 = _slots(sel_t, gates_t, base)
    n_extra = -(-n_pad // t)
    cand = (starts + per_expert)[:, None] + jnp.arange(TG, dtype=I32)[None, :]
    spare = n_slots + jnp.arange(n_pad, dtype=I32)
    fill = jnp.where(cand < ends[:, None], cand, spare.reshape(ne, TG)).reshape(-1)
    rest = n_slots + jnp.arange(n_pad, n_extra * t, dtype=I32) % n_pad
    dest = jnp.concatenate([pos8, jnp.concatenate([fill, rest]).reshape(n_extra, t)], axis=0)
    xs = _sc_scatter(h, dest, n_slots + n_pad)
    ys = _expert_ffn(xs, n_tiles, tile_expert, fresh, n_valid, layer, w_gate, w_up, w_down)
    yg = _sc_gather(ys, pos8.reshape(-1)).reshape(TOPK, t, d // 2)
    return finish(yg, w8)


def _head_rms(x, g_row, hm_ref, hmt_ref):
    ss = jnp.dot((x * x).astype(BF16), hm_ref[...], preferred_element_type=F32)
    r = lax.rsqrt(ss * (1.0 / HEAD_DIM) + EPS)
    r_hi, r_lo = _split_bf16(r)
    hmt = hmt_ref[...]
    rb = (jnp.dot(r_hi, hmt, preferred_element_type=F32)
          + jnp.dot(r_lo, hmt, preferred_element_type=F32))
    return (x * rb) * g_row


def _qkv_values(x, mod_ref, gmix_ref, w_ref, qg_ref, kg_ref, hm_ref, hmt_ref):
    d = x.shape[1]
    sh1 = mod_ref[0, 0:1, :]
    sc1 = mod_ref[0, 1:2, :]
    h = _adaln(x, gmix_ref[...], sh1, sc1).astype(BF16)
    qkv = jnp.dot(h, w_ref[...], preferred_element_type=F32)
    q = _head_rms(qkv[:, :d], qg_ref[...], hm_ref, hmt_ref)
    k = _head_rms(qkv[:, d:2 * d], kg_ref[...], hm_ref, hmt_ref)
    return q, k, qkv[:, 2 * d:]


def _qkv_kernel(emit_f32, x_ref, mod_ref, gmix_ref, w_ref, qg_ref, kg_ref, hm_ref, hmt_ref, *outs):
    q, k, v = _qkv_values(x_ref[...], mod_ref, gmix_ref, w_ref, qg_ref, kg_ref, hm_ref, hmt_ref)
    outs[0][...] = q.astype(BF16)
    outs[1][...] = k.astype(BF16)
    outs[2][...] = v.astype(BF16)
    if emit_f32:
        outs[3][...] = k
        outs[4][...] = v


def _qkv(x, mod, g_mix, w_qkv, qg, kg, hm, hmt, blk0, nblk, midx, emit_f32):
    t, d = x.shape
    in_specs = [
        pl.BlockSpec((TM, d), lambda i: (i + blk0, 0)),
        pl.BlockSpec((1, 6, d), lambda i: (midx(i + blk0), 0, 0)),
        _const_spec(g_mix.shape), _const_spec(w_qkv.shape), _const_spec(qg.shape),
        _const_spec(kg.shape), _const_spec(hm.shape), _const_spec(hmt.shape),
    ]
    n_out = 5 if emit_f32 else 3
    out_specs = [pl.BlockSpec((TM, d), lambda i: (i, 0)) for _ in range(n_out)]
    out_shape = [jax.ShapeDtypeStruct((nblk * TM, d), BF16 if o < 3 else F32) for o in range(n_out)]
    return pl.pallas_call(
        functools.partial(_qkv_kernel, emit_f32),
        grid=(nblk,), in_specs=in_specs, out_specs=out_specs, out_shape=out_shape,
        compiler_params=_cparams(("parallel",)), name="qkv_f32" if emit_f32 else "qkv",
    )(x, mod, g_mix, w_qkv, qg, kg, hm, hmt)


def _head_masks():
    lane = lax.broadcasted_iota(I32, (1, 2 * HEAD_DIM), 1)
    return lane < HEAD_DIM


def _ctx_attn_kernel(q_ref, k_ref, v_ref, o_ref):
    lo = _head_masks()
    pw = 2 * HEAD_DIM
    for hp in range(q_ref.shape[1] // pw):
        cols = slice(hp * pw, (hp + 1) * pw)
        q = q_ref[:, cols]
        k = k_ref[:, cols]
        v = v_ref[:, cols]
        outs = []
        for hh in range(2):
            msk = lo if hh == 0 else jnp.logical_not(lo)
            qm = jnp.where(msk, q, jnp.zeros_like(q)) * jnp.asarray(HEAD_DIM ** -0.5, BF16)
            s = lax.dot_general(qm, k, NT_DIMS, preferred_element_type=F32)
            m = jnp.max(s, axis=-1, keepdims=True)
            p = jnp.exp(s - m)
            l = jnp.sum(p, axis=-1, keepdims=True)
            o = jnp.dot(p.astype(BF16), v, preferred_element_type=F32)
            outs.append(o / l)
        o_ref[:, cols] = jnp.where(lo, outs[0], outs[1]).astype(BF16)


def _ctx_attn(q, k, v, nb, s):
    t, d = q.shape
    spec = pl.BlockSpec((s, d), lambda b: (b, 0))
    return pl.pallas_call(
        _ctx_attn_kernel, grid=(nb,), in_specs=[spec, spec, spec], out_specs=spec,
        out_shape=jax.ShapeDtypeStruct((nb * s, d), BF16),
        compiler_params=_cparams(("parallel",)), name="context_attention",
    )(q, k, v)


NA_QROWS = 16
NA_GROWS = 4
NA_KROWS = 12
NA_HPAIRS = 2


def _na_kernel(rows, q_ref, k_ref, v_ref, kc_ref, vc_ref, bias_ref, o_ref):
    rb = pl.program_id(2)
    ngrp = NA_QROWS // NA_GROWS
    nq = NA_GROWS * GRID_W
    nk = NA_KROWS * GRID_W
    lo = _head_masks()
    pw = 2 * HEAD_DIM
    for gi in range(ngrp):
        g = rb * ngrp + gi
        kr0 = jnp.clip(g * NA_GROWS - WIN_H // 2, 0, rows - NA_KROWS)
        start = pl.multiple_of(kr0 * GRID_W, 256)
        cls = jnp.where(g == 0, 0, jnp.where(g == rows // NA_GROWS - 1, 2, 1))
        qrows = slice(gi * nq, (gi + 1) * nq)
        for pp in range(NA_HPAIRS):
            cols = slice(pp * pw, (pp + 1) * pw)
            q = q_ref[0, qrows, cols]
            kw = k_ref[0, pl.ds(start, nk), cols]
            vw = v_ref[0, pl.ds(start, nk), cols]
            kc = kc_ref[0, :, cols]
            vc = vc_ref[0, :, cols]
            outs = []
            for hh in range(2):
                msk = lo if hh == 0 else jnp.logical_not(lo)
                qm = jnp.where(msk, q, jnp.zeros_like(q)) * jnp.asarray(HEAD_DIM ** -0.5, BF16)
                s = (lax.dot_general(qm, kw, NT_DIMS, preferred_element_type=F32)
                     + bias_ref[cls, 2 * pp + hh])
                sc = lax.dot_general(qm, kc, NT_DIMS, preferred_element_type=F32)
                m = jnp.maximum(jnp.max(s, axis=-1, keepdims=True),
                                jnp.max(sc, axis=-1, keepdims=True))
                p = jnp.exp(s - m)
                pc = jnp.exp(sc - m)
                l = jnp.sum(p, axis=-1, keepdims=True) + jnp.sum(pc, axis=-1, keepdims=True)
                o = (jnp.dot(p.astype(BF16), vw, preferred_element_type=F32)
                     + jnp.dot(pc.astype(BF16), vc, preferred_element_type=F32))
                outs.append(o / l)
            o_ref[0, qrows, cols] = jnp.where(lo, outs[0], outs[1]).astype(BF16)


def _na_row_classes(rows):
    out = []
    for g in (0, 1, rows // NA_GROWS - 1):
        r0 = g * NA_GROWS
        kr0 = int(np.clip(r0 - WIN_H // 2, 0, rows - NA_KROWS))
        table = []
        for rl in range(NA_GROWS):
            r = r0 + rl
            sr = int(np.clip(r - WIN_H // 2, 0, rows - WIN_H))
            table.append([(kr0 + kl - r + WIN_H - 1) if sr <= kr0 + kl < sr + WIN_H else None
                          for kl in range(NA_KROWS)])
        out.append(table)
    return out


def _na_bias_kernel(rows, rpb_ref, o_ref):
    h = pl.program_id(0)
    n_dr, n_dc = 2 * WIN_H - 1, 2 * WIN_W - 1
    cq = lax.broadcasted_iota(I32, (GRID_W, 2 * GRID_W), 0)
    lane = lax.broadcasted_iota(I32, (GRID_W, 2 * GRID_W), 1)
    ck = lane & (GRID_W - 1)
    q_start = jnp.clip(cq - WIN_W // 2, 0, GRID_W - WIN_W)
    col_ok = (ck >= q_start) & (ck < q_start + WIN_W)
    dc = ck - cq + (WIN_W - 1)
    neg = jnp.full((GRID_W, 2 * GRID_W), NEG, F32)
    tiles = []
    for i in range(n_dr):
        t = jnp.zeros((GRID_W, 2 * GRID_W), F32)
        for jj in range(n_dc):
            t = jnp.where(dc == jj, rpb_ref[h * (n_dr * n_dc) + i * n_dc + jj], t)
        tiles.append(jnp.where(col_ok, t, neg))
    first_half = lane < GRID_W
    for c, table in enumerate(_na_row_classes(rows)):
        for rl in range(NA_GROWS):
            for m in range(NA_KROWS // 2):
                ia, ib = table[rl][2 * m], table[rl][2 * m + 1]
                ta = neg if ia is None else tiles[ia]
                tb = neg if ib is None else tiles[ib]
                blk = ta if ia == ib else jnp.where(first_half, ta, tb)
                o_ref[c, 0, rl * GRID_W:(rl + 1) * GRID_W, m * 2 * GRID_W:(m + 1) * 2 * GRID_W] = blk


def _na_bias(rpb, rows):
    nh = rpb.shape[0]
    nq, nk = NA_GROWS * GRID_W, NA_KROWS * GRID_W
    return pl.pallas_call(
        functools.partial(_na_bias_kernel, rows), grid=(nh,),
        in_specs=[pl.BlockSpec(memory_space=pltpu.SMEM)],
        out_specs=pl.BlockSpec((3, 1, nq, nk), lambda h: (0, h, 0, 0)),
        out_shape=jax.ShapeDtypeStruct((3, nh, nq, nk), F32),
        compiler_params=_cparams(("parallel",)), name="na_bias",
    )(rpb.reshape(-1))


def _na_attn(q, k, v, kc, vc, bias_tab, rows, b_off=0):
    s, d = q.shape[1:]
    nb = kc.shape[0]
    lw = NA_HPAIRS * 2 * HEAD_DIM
    hp = d // lw
    nrb = rows // NA_QROWS
    nq = NA_QROWS * GRID_W
    lc = kc.shape[1]
    in_specs = [
        pl.BlockSpec((1, nq, lw), lambda h, b, r: (b + b_off, r, h)),
        pl.BlockSpec((1, s, lw), lambda h, b, r: (b + b_off, 0, h)),
        pl.BlockSpec((1, s, lw), lambda h, b, r: (b + b_off, 0, h)),
        pl.BlockSpec((1, lc, lw), lambda h, b, r: (b, 0, h)),
        pl.BlockSpec((1, lc, lw), lambda h, b, r: (b, 0, h)),
        pl.BlockSpec((3, 2 * NA_HPAIRS) + bias_tab.shape[2:], lambda h, b, r: (0, h, 0, 0)),
    ]
    return pl.pallas_call(
        functools.partial(_na_kernel, rows),
        grid=(hp, nb, nrb), in_specs=in_specs,
        out_specs=pl.BlockSpec((1, nq, lw), lambda h, b, r: (b, r, h)),
        out_shape=jax.ShapeDtypeStruct((nb, s, d), BF16),
        compiler_params=_cparams(("parallel", "parallel", "parallel")),
        name="neighbourhood_attention",
    )(q, k, v, kc, vc, bias_tab)


def _oproj_kernel(nbp, ap_ref, as_ref, x_ref, mod_ref, wout_ref, gffn_ref, wr_hi_ref, wr_lo_ref,
                  rb_ref, x1_ref, h_ref, gates_ref, sel_ref, cnt_ref):
    is_p = pl.program_id(0) < nbp
    for sub in range(x_ref.shape[0] // TM):
        rows = slice(sub * TM, (sub + 1) * TM)
        a = jnp.where(is_p, ap_ref[rows, :], as_ref[rows, :])
        y = jnp.dot(a, wout_ref[...], preferred_element_type=F32)
        x1 = x_ref[rows, :] + mod_ref[0, 2:3, :] * y
        x1_ref[rows, :] = x1
        _ffn_pre(x1, mod_ref, gffn_ref, wr_hi_ref, wr_lo_ref, rb_ref,
                 rows, sub, h_ref, gates_ref, sel_ref, cnt_ref)


def _oproj(attn_p, attn_s, x, mod, w_out, g_ffn, wr_hi, wr_lo, rbias, nbp, bps):
    t, d = x.shape
    nsub = PRE_BLOCKS if nbp % PRE_BLOCKS == 0 and bps % PRE_BLOCKS == 0 else 1
    ts = nsub * TM
    nbp, bps = nbp // nsub, bps // nsub
    midx = _mod_index(nbp, bps)
    in_specs = [
        pl.BlockSpec((ts, d), lambda i: (jnp.minimum(i, nbp - 1), 0)),
        pl.BlockSpec((ts, d), lambda i: (jnp.maximum(i - nbp, 0), 0)),
        pl.BlockSpec((ts, d), lambda i: (i, 0)),
        pl.BlockSpec((1, 6, d), lambda i: (midx(i), 0, 0)),
        _const_spec(w_out.shape), _const_spec(g_ffn.shape), _const_spec(wr_hi.shape),
        _const_spec(wr_lo.shape), _const_spec(rbias.shape),
    ]
    out_specs, out_shape = _pre_out_specs(t, d, nsub)
    return pl.pallas_call(
        functools.partial(_oproj_kernel, nbp), grid=(t // ts,), in_specs=in_specs,
        out_specs=out_specs, out_shape=out_shape, compiler_params=_cparams(("parallel",)),
        name="attn_out_proj",
    )(attn_p, attn_s, x, mod, w_out, g_ffn, wr_hi, wr_lo, rbias)


def _router_weights(w_router, router_bias):
    d, ne = w_router.shape
    w = jnp.pad(w_router, ((0, 0), (0, 128 - ne)))
    hi = w.astype(BF16)
    lo = (w - hi.astype(F32)).astype(BF16)
    return hi, lo, router_bias.reshape(ne, 1)


def _shared_weights(s_gate, s_up, s_down):
    return jnp.concatenate([s_gate, s_up], axis=-1).astype(BF16), s_down.astype(BF16)


def _layer_weights(layer, d, norm_mix, norm_ffn, pc, na, moe):
    i = layer // 2
    w = dict(g_mix=norm_mix[layer].reshape(1, d), g_ffn=norm_ffn[layer].reshape(1, d))
    w["wr_hi"], w["wr_lo"], w["rbias"] = _router_weights(moe["router"][layer],
                                                         moe["router_bias"][layer])
    w["sgu"], w["sd"] = _shared_weights(moe["s_gate"][layer], moe["s_up"][layer],
                                        moe["s_down"][layer])
    if layer % 2 == 0:
        w.update(w_in=pc["w_in"][i].astype(BF16), pool_w=pc["pool_w"][i].astype(BF16),
                 pool_scale=pc["pool_scale"][i].reshape(1, -1), conv_w=pc["conv_w"][i],
                 w_out=pc["w_out"][i].astype(BF16))
    else:
        nh = d // HEAD_DIM
        head_of = np.arange(d) // HEAD_DIM
        w.update(hm=jnp.asarray(head_of[:, None] == np.arange(128)[None, :], BF16),
                 hmt=jnp.asarray(np.arange(128)[:, None] == head_of[None, :], BF16),
                 qg=jnp.tile(na["q_norm"][i], nh).reshape(1, d),
                 kg=jnp.tile(na["k_norm"][i], nh).reshape(1, d),
                 w_qkv=na["w_qkv"][i].astype(BF16), w_out=na["w_out"][i].astype(BF16))
    return w


def kernel(x_prompt, x_sample, cache_k, cache_v, c, c_ctx, ada_w, ada_b, norm_mix, norm_ffn,
           pc_w_in, pc_pool_w, pc_pool_scale, pc_conv_w, pc_w_out,
           na_w_qkv, na_q_norm, na_k_norm, na_rpb, na_w_out,
           moe_router, moe_router_bias, moe_w_gate, moe_w_up, moe_w_down,
           moe_shared_gate, moe_shared_up, moe_shared_down):
    nb_p, s_p, d = x_prompt.shape
    nb_s, s_s, _ = x_sample.shape
    assert s_p == TM and s_s % TM == 0 and nb_p > 0 and nb_s > 0
    tp, ts = nb_p * s_p, nb_s * s_s
    nbp, bps = tp // TM, s_s // TM
    depth = ada_w.shape[0]
    nh = d // HEAD_DIM
    rows = s_s // GRID_W

    cond = jnp.concatenate([c_ctx[None], c], axis=0)
    cond = jnp.pad(cond, ((0, -cond.shape[0] % 8), (0, 0)))
    mods = _modulation(cond, ada_w, ada_b).reshape(depth, cond.shape[0], 6, d)
    pc = dict(w_in=pc_w_in, pool_w=pc_pool_w, pool_scale=pc_pool_scale, conv_w=pc_conv_w,
              w_out=pc_w_out)
    na = dict(w_qkv=na_w_qkv, q_norm=na_q_norm, k_norm=na_k_norm, w_out=na_w_out)
    moe = dict(router=moe_router, router_bias=moe_router_bias, s_gate=moe_shared_gate,
               s_up=moe_shared_up, s_down=moe_shared_down)

    xp, xs = x_prompt.reshape(tp, d), x_sample.reshape(ts, d)
    x = None
    ready_qkv = None
    new_k, new_v = [], []
    weights = [_layer_weights(l, d, norm_mix, norm_ffn, pc, na, moe) for l in range(depth)]
    for layer in range(depth):
        i = layer // 2
        w = weights[layer]
        mod = mods[layer]
        if layer % 2 == 0:
            if x is not None:
                xp, xs = x[:tp], x[tp:]
            x, h, gates_t, sel_t, cnt = _mixer(
                xp, xs, mod, w["g_mix"], w["w_in"], w["pool_w"], w["pool_scale"], w["conv_w"],
                w["w_out"], w["g_ffn"], w["wr_hi"], w["wr_lo"], w["rbias"], nbp, bps, s_p, s_s)
        else:
            if ready_qkv is not None and tp % s_s == 0:
                q_all, k_all, v_all, kp32, vp32 = ready_qkv
                qp, kp, vp = q_all, k_all, v_all
                seqs = (tp + ts) // s_s
                qs, ks, vs = (a.reshape(seqs, s_s, d) for a in (q_all, k_all, v_all))
                b_off = tp // s_s
            else:
                midx = _mod_index(nbp, bps)
                qkv_args = (x, mod, w["g_mix"], w["w_qkv"], w["qg"], w["kg"], w["hm"], w["hmt"])
                qp, kp, vp, kp32, vp32 = _qkv(*qkv_args, 0, nbp, midx, True)
                qs, ks, vs = (a.reshape(nb_s, s_s, d)
                              for a in _qkv(*qkv_args, nbp, ts // TM, midx, False))
                b_off = 0
            new_k.append(kp32.reshape(nb_p, s_p, nh, HEAD_DIM))
            new_v.append(vp32.reshape(nb_p, s_p, nh, HEAD_DIM))
            a_p = _ctx_attn(qp, kp, vp, nb_p, s_p)
            lc = cache_k.shape[2]
            kc = cache_k[:, i].reshape(nb_s, lc, d).astype(BF16)
            vc = cache_v[:, i].reshape(nb_s, lc, d).astype(BF16)
            a_s = _na_attn(qs, ks, vs, kc, vc, _na_bias(na_rpb[i], rows), rows, b_off)
            x, h, gates_t, sel_t, cnt = _oproj(a_p, a_s.reshape(ts, d), x, mod, w["w_out"],
                                               w["g_ffn"], w["wr_hi"], w["wr_lo"], w["rbias"],
                                               nbp, bps)
        ready_qkv = None
        if layer + 1 < depth and (layer + 1) % 2 == 1:
            w1 = weights[layer + 1]
            finish = lambda yg, w8: _combine_qkv(
                yg, w8, h, x, mod, w["sgu"], w["sd"], mods[layer + 1], w1["g_mix"], w1["w_qkv"],
                w1["qg"], w1["kg"], w1["hm"], w1["hmt"], nbp, bps)
            x, *ready_qkv = _sparse_moe(x, h, gates_t, sel_t, cnt, layer, moe_w_gate, moe_w_up,
                                        moe_w_down, finish)
        else:
            finish = lambda yg, w8: _combine(yg, w8, h, x, mod, w["sgu"], w["sd"], nbp, bps,
                                             layer == depth - 1)
            x = _sparse_moe(x, h, gates_t, sel_t, cnt, layer, moe_w_gate, moe_w_up, moe_w_down,
                            finish)
    y_p, y_s = x
    return (y_p.reshape(nb_p, s_p, d), y_s.reshape(nb_s, s_s, d),
            jnp.stack(new_k, axis=1), jnp.stack(new_v, axis=1))
```

```python
import functools

import numpy as np
import jax
import jax.numpy as jnp
from jax import lax
from jax.experimental import pallas as pl
from jax.experimental.pallas import tpu as pltpu
from jax.experimental.pallas import tpu_sc as plsc

F32 = jnp.float32
BF16 = jnp.bfloat16
I32 = jnp.int32
U32 = jnp.uint32

TM = 256
HALO = 8
POOL_WINDOWS = (2, 4, 8, 16)
N_EXPERTS = 64
N_GROUPS = 8
GROUP_SIZE = N_EXPERTS // N_GROUPS
TOPK_GROUPS = 4
TOPK = 8
ROUTED_SCALE = 2.5
EPS = 1e-6
GRID_W = 64
WIN_H = 8
WIN_W = 16
HEAD_DIM = 64
NEG = float(np.finfo(np.float32).min)
VMEM_LIMIT = 56 * 1024 * 1024
NT_DIMS = (((1,), (1,)), ((), ()))
TG = 1024
SC_WINDOW = 64
SC_WORKERS = 32


def _cparams(sem):
    return pltpu.CompilerParams(dimension_semantics=sem, vmem_limit_bytes=VMEM_LIMIT)


def _silu(x):
    return x * jax.nn.sigmoid(x)


def _split_bf16(x):
    hi = x.astype(BF16)
    lo = (x - hi.astype(F32)).astype(BF16)
    return hi, lo


def _adaln(x, g, shift, scale):
    ms = jnp.mean(x * x, axis=-1, keepdims=True)
    return (x * lax.rsqrt(ms + EPS)) * g * (1.0 + scale) + shift


def _pack_pair(x):
    w = x.shape[1] // 2
    lo = lax.bitcast_convert_type(x[:, :w].astype(BF16).astype(F32), U32) >> 16
    hi = lax.bitcast_convert_type(x[:, w:].astype(BF16).astype(F32), U32)
    return lax.bitcast_convert_type(lo | hi, I32)


def _unpack_pair(p):
    u = lax.bitcast_convert_type(p, U32)
    lo = lax.bitcast_convert_type(u << 16, F32)
    hi = lax.bitcast_convert_type(u & jnp.uint32(0xFFFF0000), F32)
    return jnp.concatenate([lo, hi], axis=-1)


def _mod_kernel(cond_ref, w_ref, b_ref, o_ref):
    c = cond_ref[...]
    a = _silu(c)
    o_ref[0] = jnp.dot(a, w_ref[0], preferred_element_type=F32,
                       precision=lax.Precision.HIGHEST) + b_ref[0]


def _modulation(cond, ada_w, ada_b):
    depth, d, n = ada_w.shape
    rows = cond.shape[0]
    tn = 1536
    return pl.pallas_call(
        _mod_kernel,
        grid=(depth, n // tn),
        in_specs=[
            pl.BlockSpec((rows, d), lambda l, j: (0, 0)),
            pl.BlockSpec((1, d, tn), lambda l, j: (l, 0, j)),
            pl.BlockSpec((1, 1, tn), lambda l, j: (l, 0, j)),
        ],
        out_specs=pl.BlockSpec((1, rows, tn), lambda l, j: (l, 0, j)),
        out_shape=jax.ShapeDtypeStruct((depth, rows, n), F32),
        compiler_params=_cparams(("arbitrary", "arbitrary")),
        name="modulation",
    )(cond, ada_w, ada_b.reshape(depth, 1, n))


def _route(logits_t, bias_col):
    tm = logits_t.shape[1]
    scores = jax.nn.sigmoid(logits_t)
    biased = scores + bias_col
    sub = lax.broadcasted_iota(I32, (GROUP_SIZE, tm), 0).astype(F32)
    ninf = jnp.float32(-jnp.inf)
    groups, gscore = [], []
    for g in range(N_GROUPS):
        v = biased[g * GROUP_SIZE:(g + 1) * GROUP_SIZE]
        m1 = jnp.max(v, axis=0, keepdims=True)
        first = jnp.min(jnp.where(v == m1, sub, float(GROUP_SIZE)), axis=0, keepdims=True)
        m2 = jnp.max(jnp.where(sub == first, ninf, v), axis=0, keepdims=True)
        groups.append(v)
        gscore.append(m1 + m2)
    masked = []
    for g in range(N_GROUPS):
        rank = jnp.zeros((1, tm), I32)
        for g2 in range(N_GROUPS):
            if g2 == g:
                continue
            ahead = gscore[g2] > gscore[g]
            if g2 < g:
                ahead = ahead | (gscore[g2] == gscore[g])
            rank = rank + ahead.astype(I32)
        masked.append(jnp.where(rank < TOPK_GROUPS, groups[g], ninf))
    masked = jnp.concatenate(masked, axis=0)
    eidx = lax.broadcasted_iota(I32, (N_EXPERTS, tm), 0).astype(F32)
    sel = jnp.zeros((N_EXPERTS, tm), jnp.bool_)
    for _ in range(TOPK):
        best = jnp.max(masked, axis=0, keepdims=True)
        first = jnp.min(jnp.where(masked == best, eidx, float(N_EXPERTS)), axis=0, keepdims=True)
        hit = eidx == first
        sel = sel | hit
        masked = jnp.where(hit, ninf, masked)
    w = jnp.where(sel, scores, 0.0)
    wsum = jnp.sum(w, axis=0, keepdims=True)
    return w / wsum * ROUTED_SCALE, sel


def _ffn_pre(x1, mod_ref, gffn_ref, wr_hi_ref, wr_lo_ref, rb_ref,
             rows, sub, h_ref, gates_ref, sel_ref, cnt_ref):
    sh2 = mod_ref[0, 3:4, :]
    sc2 = mod_ref[0, 4:5, :]
    h = _adaln(x1, gffn_ref[...], sh2, sc2)
    h_hi, h_lo = _split_bf16(h)
    h_ref[rows, :] = _pack_pair(h_hi)
    wr_hi = wr_hi_ref[...]
    logits = (jnp.dot(h_hi, wr_hi, preferred_element_type=F32)
              + jnp.dot(h_hi, wr_lo_ref[...], preferred_element_type=F32)
              + jnp.dot(h_lo, wr_hi, preferred_element_type=F32))
    gates_t, sel = _route(logits.T[:N_EXPERTS], rb_ref[...])
    gates_ref[:, rows] = gates_t
    sel_b = sel.astype(F32).astype(BF16)
    sel_ref[:, rows] = sel_b
    ones = jnp.ones((8, sel_b.shape[1]), BF16)
    cnt_ref[sub] = lax.dot_general(ones, sel_b, NT_DIMS, preferred_element_type=F32)


PRE_BLOCKS = 2


def _pre_out_specs(t, d, nsub):
    tm = nsub * TM
    specs = [
        pl.BlockSpec((tm, d), lambda i: (i, 0)),
        pl.BlockSpec((tm, d // 2), lambda i: (i, 0)),
        pl.BlockSpec((N_EXPERTS, tm), lambda i: (0, i)),
        pl.BlockSpec((N_EXPERTS, tm), lambda i: (0, i)),
        pl.BlockSpec((nsub, 8, N_EXPERTS), lambda i: (i, 0, 0)),
    ]
    shapes = [
        jax.ShapeDtypeStruct((t, d), F32),
        jax.ShapeDtypeStruct((t, d // 2), I32),
        jax.ShapeDtypeStruct((N_EXPERTS, t), F32),
        jax.ShapeDtypeStruct((N_EXPERTS, t), BF16),
        jax.ShapeDtypeStruct((t // TM, 8, N_EXPERTS), F32),
    ]
    return specs, shapes


def _mixer_kernel(nbp, bps, sp, ss,
                  xpc_ref, xsc_ref, xprev_ref, xnext_ref, mod_ref, gmix_ref, win_ref, pw_ref, ps_ref,
                  cw_ref,
                  wout_ref, gffn_ref, wr_hi_ref, wr_lo_ref, rb_ref,
                  x1_ref, h_ref, gates_ref, sel_ref, cnt_ref):
    i = pl.program_id(0)
    nsub = xpc_ref.shape[0] // TM
    is_p = i * nsub < nbp
    sh1 = mod_ref[0, 0:1, :]
    sc1 = mod_ref[0, 1:2, :]
    g1 = mod_ref[0, 2:3, :]
    next_ = TM + 2 * HALO

    def up(a, k):
        return pltpu.roll(a, next_ - k, 0)

    x_all = jnp.where(is_p, xpc_ref[...], xsc_ref[...])
    for sub in range(nsub):
        rows = slice(sub * TM, (sub + 1) * TM)
        j = lax.rem(jnp.maximum(i * nsub + sub - nbp, 0), bps)
        first = is_p | (j == 0)
        last = is_p | (j == bps - 1)
        base = jnp.where(is_p, 0, j * TM)
        slen = jnp.where(is_p, sp, ss)
        xc = x_all[rows]
        x_prev = xprev_ref[...] if sub == 0 else x_all[sub * TM - HALO:sub * TM]
        x_next = (xnext_ref[...] if sub == nsub - 1
                  else x_all[(sub + 1) * TM:(sub + 1) * TM + HALO])
        x_ext = jnp.concatenate([x_prev, xc, x_next], axis=0)
        h_ext = _adaln(x_ext, gmix_ref[...], sh1, sc1).astype(BF16)
        u = jnp.dot(h_ext, win_ref[...], preferred_element_type=F32)
        row = lax.broadcasted_iota(I32, (next_, 1), 0)
        keep = (((row >= HALO) | jnp.logical_not(first))
                & ((row < HALO + TM) | jnp.logical_not(last)))
        u = jnp.where(keep, u, 0.0)

        dm = u.shape[1] // 4
        ua = u[:, :dm]
        gate_b = u[HALO:HALO + TM, dm:2 * dm]
        z = u[:, 2 * dm:3 * dm] * u[:, 3 * dm:]

        pos = base + lax.broadcasted_iota(I32, (TM, 1), 0)
        pg = dm // len(POOL_WINDOWS)
        ya = []
        for g, w in enumerate(POOL_WINDOWS):
            e = ua[:, g * pg:(g + 1) * pg]
            acc = e
            span = 1
            while span < w:
                acc = acc + up(acc, span)
                span *= 2
            off = HALO - w // 2
            wsum = (up(acc, off) if off else acc)[:TM]
            lo = jnp.maximum(pos - w // 2, 0)
            hi = jnp.minimum(pos + (w - w // 2 - 1), slen - 1)
            cnt = (hi - lo + 1).astype(F32)
            diff = wsum / cnt - e[HALO:HALO + TM]
            ya.append(jnp.dot(diff.astype(BF16), pw_ref[g], preferred_element_type=F32))
        y_a = jnp.concatenate(ya, axis=-1) * ps_ref[...]
        zc = (cw_ref[0:1, :] * up(z, HALO - 1)[:TM] + cw_ref[1:2, :] * z[HALO:HALO + TM]
              + cw_ref[2:3, :] * up(z, HALO + 1)[:TM])
        y_b = gate_b * zc
        ycat = jnp.concatenate([y_a, y_b], axis=-1).astype(BF16)
        y = jnp.dot(ycat, wout_ref[...], preferred_element_type=F32)
        x1 = xc + g1 * y
        x1_ref[rows, :] = x1
        _ffn_pre(x1, mod_ref, gffn_ref, wr_hi_ref, wr_lo_ref, rb_ref,
                 rows, sub, h_ref, gates_ref, sel_ref, cnt_ref)


def _mod_index(nbp, bps):
    def f(i):
        return jnp.where(i < nbp, 0, 1 + jnp.maximum(i - nbp, 0) // bps)
    return f


def _const_spec(shape):
    nd = len(shape)
    return pl.BlockSpec(shape, lambda i: (0,) * nd)


def _mixer(xp, xs, mod, g_mix, w_in, pool_w, pool_scale, conv_w, w_out, g_ffn, wr_hi, wr_lo, rbias,
           nbp, bps, sp, ss):
    d = xp.shape[1]
    t = xp.shape[0] + xs.shape[0]
    nsub = PRE_BLOCKS if nbp % PRE_BLOCKS == 0 and bps % PRE_BLOCKS == 0 else 1
    tm = nsub * TM
    nblk = t // tm
    nbp2 = nbp // nsub
    midx = _mod_index(nbp2, bps // nsub)
    hpb = tm // HALO
    nh = xs.shape[0] // HALO
    in_specs = [
        pl.BlockSpec((tm, d), lambda i: (jnp.minimum(i, nbp2 - 1), 0)),
        pl.BlockSpec((tm, d), lambda i: (jnp.maximum(i - nbp2, 0), 0)),
        pl.BlockSpec((HALO, d), lambda i: (jnp.maximum((i - nbp2) * hpb - 1, 0), 0)),
        pl.BlockSpec((HALO, d), lambda i: (jnp.clip((i - nbp2 + 1) * hpb, 0, nh - 1), 0)),
        pl.BlockSpec((1, 6, d), lambda i: (midx(i), 0, 0)),
        _const_spec(g_mix.shape), _const_spec(w_in.shape), _const_spec(pool_w.shape),
        _const_spec(pool_scale.shape), _const_spec(conv_w.shape), _const_spec(w_out.shape),
        _const_spec(g_ffn.shape), _const_spec(wr_hi.shape), _const_spec(wr_lo.shape),
        _const_spec(rbias.shape),
    ]
    out_specs, out_shape = _pre_out_specs(t, d, nsub)
    return pl.pallas_call(
        functools.partial(_mixer_kernel, nbp, bps, sp, ss),
        grid=(nblk,), in_specs=in_specs, out_specs=out_specs, out_shape=out_shape,
        compiler_params=_cparams(("parallel",)), name="pool_conv_mixer",
    )(xp, xs, xs, xs, mod, g_mix, w_in, pool_w, pool_scale, conv_w, w_out, g_ffn, wr_hi, wr_lo,
      rbias)


SLOT_BLOCKS = 8


def _slots_kernel(sel_ref, gates_ref, base_ref, pos_ref, w_ref):
    r = lax.broadcasted_iota(I32, (TM, TM), 0)
    c = lax.broadcasted_iota(I32, (TM, TM), 1)
    before = (r < c).astype(F32).astype(BF16)
    er = lax.broadcasted_iota(I32, (N_EXPERTS, N_EXPERTS), 0)
    ec = lax.broadcasted_iota(I32, (N_EXPERTS, N_EXPERTS), 1)
    lower = (ec < er).astype(F32).astype(BF16)
    sub = lax.broadcasted_iota(I32, (TOPK, TM), 0)
    for blk in range(sel_ref.shape[1] // TM):
        cols = slice(blk * TM, (blk + 1) * TM)
        sel = sel_ref[:, cols]
        rank_tok = jnp.dot(sel, before, preferred_element_type=F32)
        rank_exp = jnp.dot(lower, sel, preferred_element_type=F32)
        slot = base_ref[blk] + rank_tok
        chosen = sel > 0
        gates = gates_ref[:, cols]
        pos8 = jnp.zeros((TOPK, TM), F32)
        w8 = jnp.zeros((TOPK, TM), F32)
        for k in range(TOPK):
            mk = chosen & (rank_exp == float(k))
            pk = jnp.sum(jnp.where(mk, slot, 0.0), axis=0, keepdims=True)
            wk = jnp.sum(jnp.where(mk, gates, 0.0), axis=0, keepdims=True)
            pos8 = jnp.where(sub == k, pk, pos8)
            w8 = jnp.where(sub == k, wk, w8)
        pos_ref[:, cols] = pos8.astype(I32)
        w8 = jnp.concatenate([w8, jnp.zeros((128 - TOPK, TM), F32)], axis=0)
        w_ref[cols, :] = w8.T


def _slots(sel_t, gates_t, base):
    ne, t = sel_t.shape
    nblk = next(n for n in range(SLOT_BLOCKS, 0, -1) if (t // TM) % n == 0)
    ts = nblk * TM
    return pl.pallas_call(
        _slots_kernel, grid=(t // ts,),
        in_specs=[
            pl.BlockSpec((ne, ts), lambda i: (0, i)),
            pl.BlockSpec((ne, ts), lambda i: (0, i)),
            pl.BlockSpec((nblk, ne, 1), lambda i: (i, 0, 0)),
        ],
        out_specs=[pl.BlockSpec((TOPK, ts), lambda i: (0, i)),
                   pl.BlockSpec((ts, 128), lambda i: (i, 0))],
        out_shape=[jax.ShapeDtypeStruct((TOPK, t), I32), jax.ShapeDtypeStruct((t, 128), F32)],
        compiler_params=_cparams(("parallel",)), name="moe_slots",
    )(sel_t, gates_t, base)


def _sc_gather(table, idx):
    m = idx.shape[0]
    d = table.shape[1]
    assert m % (SC_WINDOW * SC_WORKERS) == 0
    mesh = plsc.VectorSubcoreMesh(core_axis_name="core", subcore_axis_name="subcore")

    @pl.kernel(out_type=jax.ShapeDtypeStruct((m, d), table.dtype), mesh=mesh)
    def gather_rows(x_hbm, i_hbm, o_hbm):
        def body(i_vmem, o_vmem):
            pltpu.sync_copy(x_hbm.at[i_vmem.at[0]], o_vmem)

        pltpu.emit_pipeline(
            body, grid=(m // SC_WINDOW,),
            in_specs=[pl.BlockSpec((1, SC_WINDOW), index_map=lambda i: (i, 0))],
            out_specs=[pl.BlockSpec((SC_WINDOW, d), index_map=lambda i: (i, 0))],
            core_axis_name=("core", "subcore"),
            dimension_semantics=(pltpu.PARALLEL,),
        )(i_hbm, o_hbm)

    return gather_rows(table, idx.reshape(m // SC_WINDOW, SC_WINDOW))


def _sc_scatter(rows, idx, n_out):
    nk, t = idx.shape
    d = rows.shape[1]
    assert t % (SC_WINDOW * SC_WORKERS) == 0
    mesh = plsc.VectorSubcoreMesh(core_axis_name="core", subcore_axis_name="subcore")

    @pl.kernel(out_type=jax.ShapeDtypeStruct((n_out, d), rows.dtype), mesh=mesh)
    def scatter_rows(x_hbm, i_hbm, o_hbm):
        def body(x_vmem, i_vmem):
            for k in range(nk):
                pltpu.sync_copy(x_vmem, o_hbm.at[i_vmem.at[k, 0]])

        pltpu.emit_pipeline(
            body, grid=(t // SC_WINDOW,),
            in_specs=[pl.BlockSpec((SC_WINDOW, d), index_map=lambda i: (i, 0)),
                      pl.BlockSpec((nk, 1, SC_WINDOW), index_map=lambda i: (0, i, 0))],
            out_specs=[],
            core_axis_name=("core", "subcore"),
            dimension_semantics=(pltpu.PARALLEL,),
        )(x_hbm, i_hbm)

    return scatter_rows(rows, idx.reshape(nk, t // SC_WINDOW, SC_WINDOW))


def _expert_ffn_kernel(te_ref, fresh_ref, nv_ref, x_hbm, wg_ref, wu_ref, wd_ref, y_hbm,
                       xbuf, xsem, ybuf, ysem, wg_s, wu_s, wd_s):
    j = pl.program_id(0)
    nv = nv_ref[0]
    last_step = pl.num_programs(0) - 1

    def out_copy(tile, slot):
        rows = pl.ds(pl.multiple_of(tile * TG, TG), TG)
        return pltpu.make_async_copy(ybuf.at[slot], y_hbm.at[rows, :], ysem.at[slot])

    @pl.when((j >= 2) & (j - 2 < nv))
    def _():
        out_copy(j - 2, lax.rem(j, 2)).wait()

    def tile_copy(tile, slot):
        rows = pl.ds(pl.multiple_of(tile * TG, TG), TG)
        return pltpu.make_async_copy(x_hbm.at[rows, :], xbuf.at[slot], xsem.at[slot])

    @pl.when(j == 0)
    def _():
        tile_copy(0, 0).start()

        @pl.when(nv > 1)
        def _():
            tile_copy(1, 1).start()

    @pl.when(j < nv)
    def _():
        @pl.when(j + 2 < nv)
        def _():
            tile_copy(j + 2, lax.rem(j + 2, 3)).start()

        slot = lax.rem(j, 3)
        tile_copy(j, slot).wait()

        @pl.when(fresh_ref[j] == 1)
        def _():
            wg_s[...] = wg_ref[0, 0].astype(BF16)
            wu_s[...] = wu_ref[0, 0].astype(BF16)
            wd_s[...] = wd_ref[0, 0].astype(BF16)

        x = _unpack_pair(xbuf[slot]).astype(BF16)
        a = jnp.dot(x, wg_s[...], preferred_element_type=F32)
        b = jnp.dot(x, wu_s[...], preferred_element_type=F32)
        act = (_silu(a) * b).astype(BF16)
        oslot = lax.rem(j, 2)
        ybuf[oslot] = _pack_pair(jnp.dot(act, wd_s[...], preferred_element_type=F32))
        out_copy(j, oslot).start()

    @pl.when(j == last_step)
    def _():
        @pl.when((j >= 1) & (j - 1 < nv))
        def _():
            out_copy(j - 1, lax.rem(j + 1, 2)).wait()

        @pl.when(j < nv)
        def _():
            out_copy(j, lax.rem(j, 2)).wait()


def _expert_ffn(xs, n_tiles, tile_expert, fresh, n_valid, layer, w_gate, w_up, w_down):
    dh = xs.shape[1]
    wspec = lambda w: pl.BlockSpec((1, 1) + w.shape[2:],
                                   lambda j, te, fr, nv: (layer, te[j], 0, 0))
    grid_spec = pltpu.PrefetchScalarGridSpec(
        num_scalar_prefetch=3, grid=(n_tiles,),
        in_specs=[pl.BlockSpec(memory_space=pl.ANY), wspec(w_gate), wspec(w_up), wspec(w_down)],
        out_specs=pl.BlockSpec(memory_space=pl.ANY),
        scratch_shapes=[pltpu.VMEM((3, TG, dh), xs.dtype), pltpu.SemaphoreType.DMA((3,)),
                        pltpu.VMEM((2, TG, dh), I32), pltpu.SemaphoreType.DMA((2,)),
                        pltpu.VMEM(w_gate.shape[2:], BF16), pltpu.VMEM(w_up.shape[2:], BF16),
                        pltpu.VMEM(w_down.shape[2:], BF16)],
    )
    return pl.pallas_call(
        _expert_ffn_kernel, grid_spec=grid_spec,
        out_shape=jax.ShapeDtypeStruct((n_tiles * TG, dh), I32),
        compiler_params=_cparams(("arbitrary",)), name="expert_ffn",
    )(tile_expert, fresh, n_valid, xs, w_gate, w_up, w_down)


def _combine_value(yg_ref, w_ref, h_ref, x_ref, mod_ref, sgu_ref, sd_ref):
    f = sd_ref.shape[0]
    h = _unpack_pair(h_ref[...]).astype(BF16)
    hs = jnp.dot(h, sgu_ref[...], preferred_element_type=F32)
    act = (_silu(hs[:, :f]) * hs[:, f:]).astype(BF16)
    acc = jnp.dot(act, sd_ref[...], preferred_element_type=F32)
    w = w_ref[...]
    for k in range(TOPK):
        acc = acc + w[:, k:k + 1] * _unpack_pair(yg_ref[k])
    return x_ref[...] + mod_ref[0, 5:6, :] * acc


def _combine_kernel(nbp, yg_ref, w_ref, h_ref, x_ref, mod_ref, sgu_ref, sd_ref, *o_refs):
    out = _combine_value(yg_ref, w_ref, h_ref, x_ref, mod_ref, sgu_ref, sd_ref)
    if len(o_refs) == 1:
        o_refs[0][...] = out
    else:
        i = pl.program_id(0)

        @pl.when(i < nbp)
        def _():
            o_refs[0][...] = out

        @pl.when(i >= nbp)
        def _():
            o_refs[1][...] = out


def _combine(yg, w8, h, x, mod, sgu, sd, nbp, bps, split_out):
    t, d = x.shape
    tc = 2 * TM
    assert (nbp * TM) % tc == 0 and (bps * TM) % tc == 0
    nbp, bps = nbp * TM // tc, bps * TM // tc
    midx = _mod_index(nbp, bps)
    if split_out:
        out_specs = [pl.BlockSpec((tc, d), lambda i: (jnp.minimum(i, nbp - 1), 0)),
                     pl.BlockSpec((tc, d), lambda i: (jnp.maximum(i - nbp, 0), 0))]
        out_shape = [jax.ShapeDtypeStruct((nbp * tc, d), F32),
                     jax.ShapeDtypeStruct((t - nbp * tc, d), F32)]
    else:
        out_specs = pl.BlockSpec((tc, d), lambda i: (i, 0))
        out_shape = jax.ShapeDtypeStruct((t, d), F32)
    in_specs = [
        pl.BlockSpec((TOPK, tc, d // 2), lambda i: (0, i, 0)),
        pl.BlockSpec((tc, 128), lambda i: (i, 0)),
        pl.BlockSpec((tc, d // 2), lambda i: (i, 0)),
        pl.BlockSpec((tc, d), lambda i: (i, 0)),
        pl.BlockSpec((1, 6, d), lambda i: (midx(i), 0, 0)),
        _const_spec(sgu.shape), _const_spec(sd.shape),
    ]
    return pl.pallas_call(
        functools.partial(_combine_kernel, nbp), grid=(t // tc,), in_specs=in_specs,
        out_specs=out_specs, out_shape=out_shape,
        compiler_params=_cparams(("arbitrary",)), name="moe_combine",
    )(yg, w8, h, x, mod, sgu, sd)


def _combine_qkv_kernel(nbp, yg_ref, w_ref, h_ref, x_ref, mod_ref, sgu_ref, sd_ref,
                        mod1_ref, gmix_ref, wqkv_ref, qg_ref, kg_ref, hm_ref, hmt_ref,
                        x_out, q_out, k_out, v_out, k32_out, v32_out):
    out = _combine_value(yg_ref, w_ref, h_ref, x_ref, mod_ref, sgu_ref, sd_ref)
    x_out[...] = out
    q, k, v = _qkv_values(out, mod1_ref, gmix_ref, wqkv_ref, qg_ref, kg_ref, hm_ref, hmt_ref)
    q_out[...] = q.astype(BF16)
    k_out[...] = k.astype(BF16)
    v_out[...] = v.astype(BF16)

    @pl.when(pl.program_id(0) < nbp)
    def _():
        k32_out[...] = k
        v32_out[...] = v


def _combine_qkv(yg, w8, h, x, mod, sgu, sd, mod1, g_mix, w_qkv, qg, kg, hm, hmt, nbp, bps):
    t, d = x.shape
    midx = _mod_index(nbp, bps)
    row = lambda i: (i, 0)
    in_specs = [
        pl.BlockSpec((TOPK, TM, d // 2), lambda i: (0, i, 0)),
        pl.BlockSpec((TM, 128), row),
        pl.BlockSpec((TM, d // 2), row),
        pl.BlockSpec((TM, d), row),
        pl.BlockSpec((1, 6, d), lambda i: (midx(i), 0, 0)),
        _const_spec(sgu.shape), _const_spec(sd.shape),
        pl.BlockSpec((1, 6, d), lambda i: (midx(i), 0, 0)),
        _const_spec(g_mix.shape), _const_spec(w_qkv.shape), _const_spec(qg.shape),
        _const_spec(kg.shape), _const_spec(hm.shape), _const_spec(hmt.shape),
    ]
    pinned = pl.BlockSpec((TM, d), lambda i: (jnp.minimum(i, nbp - 1), 0))
    out_specs = [pl.BlockSpec((TM, d), row)] * 4 + [pinned, pinned]
    out_shape = ([jax.ShapeDtypeStruct((t, d), F32)] + [jax.ShapeDtypeStruct((t, d), BF16)] * 3
                 + [jax.ShapeDtypeStruct((nbp * TM, d), F32)] * 2)
    return pl.pallas_call(
        functools.partial(_combine_qkv_kernel, nbp), grid=(t // TM,), in_specs=in_specs,
        out_specs=out_specs, out_shape=out_shape,
        compiler_params=_cparams(("arbitrary",)), name="moe_combine_qkv",
    )(yg, w8, h, x, mod, sgu, sd, mod1, g_mix, w_qkv, qg, kg, hm, hmt)


def _sparse_moe(x, h, gates_t, sel_t, cnt, layer, w_gate, w_up, w_down, finish):
    t, d = x.shape
    ne = N_EXPERTS
    n_pad = ne * TG
    n_tiles = (t * TOPK + n_pad) // TG
    n_slots = n_tiles * TG
    counts = cnt[:, 0, :].astype(I32)
    per_expert = jnp.sum(counts, axis=0)
    padded = (per_expert + TG - 1) // TG * TG
    ends = jnp.cumsum(padded)
    starts = ends - padded
    block_off = jnp.cumsum(counts, axis=0) - counts
    base = (starts[None, :] + block_off).astype(F32)[:, :, None]
    n_valid = (ends[-1] // TG).astype(I32).reshape(1)
    tile_start = jnp.minimum(jnp.arange(n_tiles, dtype=I32), n_valid[0] - 1) * TG
    tile_expert = jnp.sum((ends[None, :] <= tile_start[:, None]).astype(I32), axis=1)
    fresh = jnp.concatenate([jnp.ones((1,), I32),
                             (tile_expert[1:] != tile_expert[:-1]).astype(I32)])

    pos8, w8 = _slots(sel_t, gates_t, base)
    n_extra = -(-n_pad // t)
    cand = (starts + per_expert)[:, None] + jnp.arange(TG, dtype=I32)[None, :]
    spare = n_slots + jnp.arange(n_pad, dtype=I32)
    fill = jnp.where(cand < ends[:, None], cand, spare.reshape(ne, TG)).reshape(-1)
    rest = n_slots + jnp.arange(n_pad, n_extra * t, dtype=I32) % n_pad
    dest = jnp.concatenate([pos8, jnp.concatenate([fill, rest]).reshape(n_extra, t)], axis=0)
    xs = _sc_scatter(h, dest, n_slots + n_pad)
    ys = _expert_ffn(xs, n_tiles, tile_expert, fresh, n_valid, layer, w_gate, w_up, w_down)
    yg = _sc_gather(ys, pos8.reshape(-1)).reshape(TOPK, t, d // 2)
    return finish(yg, w8)


def _head_rms(x, g_row, hm_ref, hmt_ref):
    ss = jnp.dot((x * x).astype(BF16), hm_ref[...], preferred_element_type=F32)
    r = lax.rsqrt(ss * (1.0 / HEAD_DIM) + EPS)
    r_hi, r_lo = _split_bf16(r)
    hmt = hmt_ref[...]
    rb = (jnp.dot(r_hi, hmt, preferred_element_type=F32)
          + jnp.dot(r_lo, hmt, preferred_element_type=F32))
    return (x * rb) * g_row


def _qkv_values(x, mod_ref, gmix_ref, w_ref, qg_ref, kg_ref, hm_ref, hmt_ref):
    d = x.shape[1]
    sh1 = mod_ref[0, 0:1, :]
    sc1 = mod_ref[0, 1:2, :]
    h = _adaln(x, gmix_ref[...], sh1, sc1).astype(BF16)
    qkv = jnp.dot(h, w_ref[...], preferred_element_type=F32)
    q = _head_rms(qkv[:, :d], qg_ref[...], hm_ref, hmt_ref)
    k = _head_rms(qkv[:, d:2 * d], kg_ref[...], hm_ref, hmt_ref)
    return q, k, qkv[:, 2 * d:]


def _qkv_kernel(emit_f32, x_ref, mod_ref, gmix_ref, w_ref, qg_ref, kg_ref, hm_ref, hmt_ref, *outs):
    q, k, v = _qkv_values(x_ref[...], mod_ref, gmix_ref, w_ref, qg_ref, kg_ref, hm_ref, hmt_ref)
    outs[0][...] = q.astype(BF16)
    outs[1][...] = k.astype(BF16)
    outs[2][...] = v.astype(BF16)
    if emit_f32:
        outs[3][...] = k
        outs[4][...] = v


def _qkv(x, mod, g_mix, w_qkv, qg, kg, hm, hmt, blk0, nblk, midx, emit_f32):
    t, d = x.shape
    in_specs = [
        pl.BlockSpec((TM, d), lambda i: (i + blk0, 0)),
        pl.BlockSpec((1, 6, d), lambda i: (midx(i + blk0), 0, 0)),
        _const_spec(g_mix.shape), _const_spec(w_qkv.shape), _const_spec(qg.shape),
        _const_spec(kg.shape), _const_spec(hm.shape), _const_spec(hmt.shape),
    ]
    n_out = 5 if emit_f32 else 3
    out_specs = [pl.BlockSpec((TM, d), lambda i: (i, 0)) for _ in range(n_out)]
    out_shape = [jax.ShapeDtypeStruct((nblk * TM, d), BF16 if o < 3 else F32) for o in range(n_out)]
    return pl.pallas_call(
        functools.partial(_qkv_kernel, emit_f32),
        grid=(nblk,), in_specs=in_specs, out_specs=out_specs, out_shape=out_shape,
        compiler_params=_cparams(("parallel",)), name="qkv_f32" if emit_f32 else "qkv",
    )(x, mod, g_mix, w_qkv, qg, kg, hm, hmt)


def _head_masks():
    lane = lax.broadcasted_iota(I32, (1, 2 * HEAD_DIM), 1)
    return lane < HEAD_DIM


def _ctx_attn_kernel(q_ref, k_ref, v_ref, o_ref):
    lo = _head_masks()
    pw = 2 * HEAD_DIM
    for hp in range(q_ref.shape[1] // pw):
        cols = slice(hp * pw, (hp + 1) * pw)
        q = q_ref[:, cols]
        k = k_ref[:, cols]
        v = v_ref[:, cols]
        outs = []
        for hh in range(2):
            msk = lo if hh == 0 else jnp.logical_not(lo)
            qm = jnp.where(msk, q, jnp.zeros_like(q)) * jnp.asarray(HEAD_DIM ** -0.5, BF16)
            s = lax.dot_general(qm, k, NT_DIMS, preferred_element_type=F32)
            m = jnp.max(s, axis=-1, keepdims=True)
            p = jnp.exp(s - m)
            l = jnp.sum(p, axis=-1, keepdims=True)
            o = jnp.dot(p.astype(BF16), v, preferred_element_type=F32)
            outs.append(o / l)
        o_ref[:, cols] = jnp.where(lo, outs[0], outs[1]).astype(BF16)


def _ctx_attn(q, k, v, nb, s):
    t, d = q.shape
    spec = pl.BlockSpec((s, d), lambda b: (b, 0))
    return pl.pallas_call(
        _ctx_attn_kernel, grid=(nb,), in_specs=[spec, spec, spec], out_specs=spec,
        out_shape=jax.ShapeDtypeStruct((nb * s, d), BF16),
        compiler_params=_cparams(("parallel",)), name="context_attention",
    )(q, k, v)


NA_QROWS = 16
NA_GROWS = 4
NA_KROWS = 12
NA_HPAIRS = 2


def _na_kernel(rows, q_ref, k_ref, v_ref, kc_ref, vc_ref, bias_ref, o_ref):
    rb = pl.program_id(2)
    ngrp = NA_QROWS // NA_GROWS
    nq = NA_GROWS * GRID_W
    nk = NA_KROWS * GRID_W
    lo = _head_masks()
    pw = 2 * HEAD_DIM
    for gi in range(ngrp):
        g = rb * ngrp + gi
        kr0 = jnp.clip(g * NA_GROWS - WIN_H // 2, 0, rows - NA_KROWS)
        start = pl.multiple_of(kr0 * GRID_W, 256)
        cls = jnp.where(g == 0, 0, jnp.where(g == rows // NA_GROWS - 1, 2, 1))
        qrows = slice(gi * nq, (gi + 1) * nq)
        for pp in range(NA_HPAIRS):
            cols = slice(pp * pw, (pp + 1) * pw)
            q = q_ref[0, qrows, cols]
            kw = k_ref[0, pl.ds(start, nk), cols]
            vw = v_ref[0, pl.ds(start, nk), cols]
            kc = kc_ref[0, :, cols]
            vc = vc_ref[0, :, cols]
            outs = []
            for hh in range(2):
                msk = lo if hh == 0 else jnp.logical_not(lo)
                qm = jnp.where(msk, q, jnp.zeros_like(q)) * jnp.asarray(HEAD_DIM ** -0.5, BF16)
                s = (lax.dot_general(qm, kw, NT_DIMS, preferred_element_type=F32)
                     + bias_ref[cls, 2 * pp + hh])
                sc = lax.dot_general(qm, kc, NT_DIMS, preferred_element_type=F32)
                m = jnp.maximum(jnp.max(s, axis=-1, keepdims=True),
                                jnp.max(sc, axis=-1, keepdims=True))
                p = jnp.exp(s - m)
                pc = jnp.exp(sc - m)
                l = jnp.sum(p, axis=-1, keepdims=True) + jnp.sum(pc, axis=-1, keepdims=True)
                o = (jnp.dot(p.astype(BF16), vw, preferred_element_type=F32)
                     + jnp.dot(pc.astype(BF16), vc, preferred_element_type=F32))
                outs.append(o / l)
            o_ref[0, qrows, cols] = jnp.where(lo, outs[0], outs[1]).astype(BF16)


def _na_row_classes(rows):
    out = []
    for g in (0, 1, rows // NA_GROWS - 1):
        r0 = g * NA_GROWS
        kr0 = int(np.clip(r0 - WIN_H // 2, 0, rows - NA_KROWS))
        table = []
        for rl in range(NA_GROWS):
            r = r0 + rl
            sr = int(np.clip(r - WIN_H // 2, 0, rows - WIN_H))
            table.append([(kr0 + kl - r + WIN_H - 1) if sr <= kr0 + kl < sr + WIN_H else None
                          for kl in range(NA_KROWS)])
        out.append(table)
    return out


def _na_bias_kernel(rows, rpb_ref, o_ref):
    h = pl.program_id(0)
    n_dr, n_dc = 2 * WIN_H - 1, 2 * WIN_W - 1
    cq = lax.broadcasted_iota(I32, (GRID_W, 2 * GRID_W), 0)
    lane = lax.broadcasted_iota(I32, (GRID_W, 2 * GRID_W), 1)
    ck = lane & (GRID_W - 1)
    q_start = jnp.clip(cq - WIN_W // 2, 0, GRID_W - WIN_W)
    col_ok = (ck >= q_start) & (ck < q_start + WIN_W)
    dc = ck - cq + (WIN_W - 1)
    neg = jnp.full((GRID_W, 2 * GRID_W), NEG, F32)
    tiles = []
    for i in range(n_dr):
        t = jnp.zeros((GRID_W, 2 * GRID_W), F32)
        for jj in range(n_dc):
            t = jnp.where(dc == jj, rpb_ref[h * (n_dr * n_dc) + i * n_dc + jj], t)
        tiles.append(jnp.where(col_ok, t, neg))
    first_half = lane < GRID_W
    for c, table in enumerate(_na_row_classes(rows)):
        for rl in range(NA_GROWS):
            for m in range(NA_KROWS // 2):
                ia, ib = table[rl][2 * m], table[rl][2 * m + 1]
                ta = neg if ia is None else tiles[ia]
                tb = neg if ib is None else tiles[ib]
                blk = ta if ia == ib else jnp.where(first_half, ta, tb)
                o_ref[c, 0, rl * GRID_W:(rl + 1) * GRID_W, m * 2 * GRID_W:(m + 1) * 2 * GRID_W] = blk


def _na_bias(rpb, rows):
    nh = rpb.shape[0]
    nq, nk = NA_GROWS * GRID_W, NA_KROWS * GRID_W
    return pl.pallas_call(
        functools.partial(_na_bias_kernel, rows), grid=(nh,),
        in_specs=[pl.BlockSpec(memory_space=pltpu.SMEM)],
        out_specs=pl.BlockSpec((3, 1, nq, nk), lambda h: (0, h, 0, 0)),
        out_shape=jax.ShapeDtypeStruct((3, nh, nq, nk), F32),
        compiler_params=_cparams(("parallel",)), name="na_bias",
    )(rpb.reshape(-1))


def _na_attn(q, k, v, kc, vc, bias_tab, rows, b_off=0):
    s, d = q.shape[1:]
    nb = kc.shape[0]
    lw = NA_HPAIRS * 2 * HEAD_DIM
    hp = d // lw
    nrb = rows // NA_QROWS
    nq = NA_QROWS * GRID_W
    lc = kc.shape[1]
    in_specs = [
        pl.BlockSpec((1, nq, lw), lambda h, b, r: (b + b_off, r, h)),
        pl.BlockSpec((1, s, lw), lambda h, b, r: (b + b_off, 0, h)),
        pl.BlockSpec((1, s, lw), lambda h, b, r: (b + b_off, 0, h)),
        pl.BlockSpec((1, lc, lw), lambda h, b, r: (b, 0, h)),
        pl.BlockSpec((1, lc, lw), lambda h, b, r: (b, 0, h)),
        pl.BlockSpec((3, 2 * NA_HPAIRS) + bias_tab.shape[2:], lambda h, b, r: (0, h, 0, 0)),
    ]
    return pl.pallas_call(
        functools.partial(_na_kernel, rows),
        grid=(hp, nb, nrb), in_specs=in_specs,
        out_specs=pl.BlockSpec((1, nq, lw), lambda h, b, r: (b, r, h)),
        out_shape=jax.ShapeDtypeStruct((nb, s, d), BF16),
        compiler_params=_cparams(("parallel", "parallel", "parallel")),
        name="neighbourhood_attention",
    )(q, k, v, kc, vc, bias_tab)


def _oproj_kernel(nbp, ap_ref, as_ref, x_ref, mod_ref, wout_ref, gffn_ref, wr_hi_ref, wr_lo_ref,
                  rb_ref, x1_ref, h_ref, gates_ref, sel_ref, cnt_ref):
    is_p = pl.program_id(0) < nbp
    for sub in range(x_ref.shape[0] // TM):
        rows = slice(sub * TM, (sub + 1) * TM)
        a = jnp.where(is_p, ap_ref[rows, :], as_ref[rows, :])
        y = jnp.dot(a, wout_ref[...], preferred_element_type=F32)
        x1 = x_ref[rows, :] + mod_ref[0, 2:3, :] * y
        x1_ref[rows, :] = x1
        _ffn_pre(x1, mod_ref, gffn_ref, wr_hi_ref, wr_lo_ref, rb_ref,
                 rows, sub, h_ref, gates_ref, sel_ref, cnt_ref)


def _oproj(attn_p, attn_s, x, mod, w_out, g_ffn, wr_hi, wr_lo, rbias, nbp, bps):
    t, d = x.shape
    nsub = PRE_BLOCKS if nbp % PRE_BLOCKS == 0 and bps % PRE_BLOCKS == 0 else 1
    ts = nsub * TM
    nbp, bps = nbp // nsub, bps // nsub
    midx = _mod_index(nbp, bps)
    in_specs = [
        pl.BlockSpec((ts, d), lambda i: (jnp.minimum(i, nbp - 1), 0)),
        pl.BlockSpec((ts, d), lambda i: (jnp.maximum(i - nbp, 0), 0)),
        pl.BlockSpec((ts, d), lambda i: (i, 0)),
        pl.BlockSpec((1, 6, d), lambda i: (midx(i), 0, 0)),
        _const_spec(w_out.shape), _const_spec(g_ffn.shape), _const_spec(wr_hi.shape),
        _const_spec(wr_lo.shape), _const_spec(rbias.shape),
    ]
    out_specs, out_shape = _pre_out_specs(t, d, nsub)
    return pl.pallas_call(
        functools.partial(_oproj_kernel, nbp), grid=(t // ts,), in_specs=in_specs,
        out_specs=out_specs, out_shape=out_shape, compiler_params=_cparams(("parallel",)),
        name="attn_out_proj",
    )(attn_p, attn_s, x, mod, w_out, g_ffn, wr_hi, wr_lo, rbias)


def _router_weights(w_router, router_bias):
    d, ne = w_router.shape
    w = jnp.pad(w_router, ((0, 0), (0, 128 - ne)))
    hi = w.astype(BF16)
    lo = (w - hi.astype(F32)).astype(BF16)
    return hi, lo, router_bias.reshape(ne, 1)


def _shared_weights(s_gate, s_up, s_down):
    return jnp.concatenate([s_gate, s_up], axis=-1).astype(BF16), s_down.astype(BF16)


def _layer_weights(layer, d, norm_mix, norm_ffn, pc, na, moe):
    i = layer // 2
    w = dict(g_mix=norm_mix[layer].reshape(1, d), g_ffn=norm_ffn[layer].reshape(1, d))
    w["wr_hi"], w["wr_lo"], w["rbias"] = _router_weights(moe["router"][layer],
                                                         moe["router_bias"][layer])
    w["sgu"], w["sd"] = _shared_weights(moe["s_gate"][layer], moe["s_up"][layer],
                                        moe["s_down"][layer])
    if layer % 2 == 0:
        w.update(w_in=pc["w_in"][i].astype(BF16), pool_w=pc["pool_w"][i].astype(BF16),
                 pool_scale=pc["pool_scale"][i].reshape(1, -1), conv_w=pc["conv_w"][i],
                 w_out=pc["w_out"][i].astype(BF16))
    else:
        nh = d // HEAD_DIM
        head_of = np.arange(d) // HEAD_DIM
        w.update(hm=jnp.asarray(head_of[:, None] == np.arange(128)[None, :], BF16),
                 hmt=jnp.asarray(np.arange(128)[:, None] == head_of[None, :], BF16),
                 qg=jnp.tile(na["q_norm"][i], nh).reshape(1, d),
                 kg=jnp.tile(na["k_norm"][i], nh).reshape(1, d),
                 w_qkv=na["w_qkv"][i].astype(BF16), w_out=na["w_out"][i].astype(BF16))
    return w


def kernel(x_prompt, x_sample, cache_k, cache_v, c, c_ctx, ada_w, ada_b, norm_mix, norm_ffn,
           pc_w_in, pc_pool_w, pc_pool_scale, pc_conv_w, pc_w_out,
           na_w_qkv, na_q_norm, na_k_norm, na_rpb, na_w_out,
           moe_router, moe_router_bias, moe_w_gate, moe_w_up, moe_w_down,
           moe_shared_gate, moe_shared_up, moe_shared_down):
    nb_p, s_p, d = x_prompt.shape
    nb_s, s_s, _ = x_sample.shape
    assert s_p == TM and s_s % TM == 0 and nb_p > 0 and nb_s > 0
    tp, ts = nb_p * s_p, nb_s * s_s
    nbp, bps = tp // TM, s_s // TM
    depth = ada_w.shape[0]
    nh = d // HEAD_DIM
    rows = s_s // GRID_W

    cond = jnp.concatenate([c_ctx[None], c], axis=0)
    cond = jnp.pad(cond, ((0, -cond.shape[0] % 8), (0, 0)))
    mods = _modulation(cond, ada_w, ada_b).reshape(depth, cond.shape[0], 6, d)
    pc = dict(w_in=pc_w_in, pool_w=pc_pool_w, pool_scale=pc_pool_scale, conv_w=pc_conv_w,
              w_out=pc_w_out)
    na = dict(w_qkv=na_w_qkv, q_norm=na_q_norm, k_norm=na_k_norm, w_out=na_w_out)
    moe = dict(router=moe_router, router_bias=moe_router_bias, s_gate=moe_shared_gate,
               s_up=moe_shared_up, s_down=moe_shared_down)

    xp, xs = x_prompt.reshape(tp, d), x_sample.reshape(ts, d)
    x = None
    ready_qkv = None
    new_k, new_v = [], []
    weights = [_layer_weights(l, d, norm_mix, norm_ffn, pc, na, moe) for l in range(depth)]
    for layer in range(depth):
        i = layer // 2
        w = weights[layer]
        mod = mods[layer]
        if layer % 2 == 0:
            if x is not None:
                xp, xs = x[:tp], x[tp:]
            x, h, gates_t, sel_t, cnt = _mixer(
                xp, xs, mod, w["g_mix"], w["w_in"], w["pool_w"], w["pool_scale"], w["conv_w"],
                w["w_out"], w["g_ffn"], w["wr_hi"], w["wr_lo"], w["rbias"], nbp, bps, s_p, s_s)
        else:
            if ready_qkv is not None and tp % s_s == 0:
                q_all, k_all, v_all, kp32, vp32 = ready_qkv
                qp, kp, vp = q_all, k_all, v_all
                seqs = (tp + ts) // s_s
                qs, ks, vs = (a.reshape(seqs, s_s, d) for a in (q_all, k_all, v_all))
                b_off = tp // s_s
            else:
                midx = _mod_index(nbp, bps)
                qkv_args = (x, mod, w["g_mix"], w["w_qkv"], w["qg"], w["kg"], w["hm"], w["hmt"])
                qp, kp, vp, kp32, vp32 = _qkv(*qkv_args, 0, nbp, midx, True)
                qs, ks, vs = (a.reshape(nb_s, s_s, d)
                              for a in _qkv(*qkv_args, nbp, ts // TM, midx, False))
                b_off = 0
            new_k.append(kp32.reshape(nb_p, s_p, nh, HEAD_DIM))
            new_v.append(vp32.reshape(nb_p, s_p, nh, HEAD_DIM))
            a_p = _ctx_attn(qp, kp, vp, nb_p, s_p)
            lc = cache_k.shape[2]
            kc = cache_k[:, i].reshape(nb_s, lc, d).astype(BF16)
            vc = cache_v[:, i].reshape(nb_s, lc, d).astype(BF16)
            a_s = _na_attn(qs, ks, vs, kc, vc, _na_bias(na_rpb[i], rows), rows, b_off)
            x, h, gates_t, sel_t, cnt = _oproj(a_p, a_s.reshape(ts, d), x, mod, w["w_out"],
                                               w["g_ffn"], w["wr_hi"], w["wr_lo"], w["rbias"],
                                               nbp, bps)
        ready_qkv = None
        if layer + 1 < depth and (layer + 1) % 2 == 1:
            w1 = weights[layer + 1]
            finish = lambda yg, w8: _combine_qkv(
                yg, w8, h, x, mod, w["sgu"], w["sd"], mods[layer + 1], w1["g_mix"], w1["w_qkv"],
                w1["qg"], w1["kg"], w1["hm"], w1["hmt"], nbp, bps)
            x, *ready_qkv = _sparse_moe(x, h, gates_t, sel_t, cnt, layer, moe_w_gate, moe_w_up,
                                        moe_w_down, finish)
        else:
            finish = lambda yg, w8: _combine(yg, w8, h, x, mod, w["sgu"], w["sd"], nbp, bps,
                                             layer == depth - 1)
            x = _sparse_moe(x, h, gates_t, sel_t, cnt, layer, moe_w_gate, moe_w_up, moe_w_down,
                            finish)
    y_p, y_s = x
    return (y_p.reshape(nb_p, s_p, d), y_s.reshape(nb_s, s_s, d),
            jnp.stack(new_k, axis=1), jnp.stack(new_v, axis=1))
```
